```python
import math
import jax, jax.numpy as jnp
from jax import lax
import numpy as np

D_MODEL = 1024
BATCH = 2
SEQ = 16384
DEPTH = 4

MEM_LEN = 256
HEAD_DIM = 64
EPS = 1e-6
A_WIDTH = D_MODEL // 2
A_HEADS = A_WIDTH // HEAD_DIM
MOBA_BLOCK = 256
MOBA_TOPK = 3
MOBA_QCHUNK = 64
B_HEADS = 4
B_VW = D_MODEL // 4
B_DV = B_VW // B_HEADS
B_DK = B_DV // 2
B_KW = B_HEADS * B_DK
GATE_RANK = 16
GATE_TEMP = 16.0
GLA_CHUNK = 64
M_WIDTH = D_MODEL // 4
M_HEADS = 4
M_DH = M_WIDTH // M_HEADS

IN_SPLITS = (A_WIDTH, A_WIDTH, A_WIDTH, A_WIDTH,
             B_KW, B_KW, B_VW, B_VW, GATE_RANK,
             M_WIDTH, M_WIDTH)
IN_PROJ = sum(IN_SPLITS)

kernel_name = "hymba_moba_gla_mem_hybrid"


def rms_norm(t, g):
    tf = t.astype(jnp.float32)
    y = tf * lax.rsqrt(jnp.mean(tf * tf, axis=-1, keepdims=True) + EPS)
    return (y * g.astype(jnp.float32)).astype(t.dtype)


def split_heads(t, n):
    b, s, w = t.shape
    return t.reshape(b, s, n, w // n).transpose(0, 2, 1, 3)


def merge_heads(t):
    b, h, s, d = t.shape
    return t.transpose(0, 2, 1, 3).reshape(b, s, h * d)


def alibi_slopes(n_heads):
    return jnp.asarray([2.0 ** (-8.0 * (i + 1) / n_heads) for i in range(n_heads)], jnp.float32)


def moba_attention(q, k, v):
    bsz, nh, s, d = q.shape
    nb = -(-s // MOBA_BLOCK)
    sp = nb * MOBA_BLOCK
    pad = ((0, 0), (0, 0), (0, sp - s), (0, 0))
    q, k, v = jnp.pad(q, pad), jnp.pad(k, pad), jnp.pad(v, pad)
    kb = k.reshape(bsz, nh, nb, MOBA_BLOCK, d)
    vb = v.reshape(bsz, nh, nb, MOBA_BLOCK, d)
    kmean = jnp.mean(kb.astype(jnp.float32), axis=3)
    slopes = alibi_slopes(nh)
    scale = d ** -0.5
    nsel = min(MOBA_TOPK, nb)
    n_qc = sp // MOBA_QCHUNK
    qc = q.reshape(bsz, nh, n_qc, MOBA_QCHUNK, d).transpose(2, 0, 1, 3, 4)
    blk_ar = jnp.arange(MOBA_BLOCK, dtype=jnp.int32)
    gather = jax.vmap(jax.vmap(lambda tb, ib: tb[ib]))

    def one_chunk(args):
        qi, c = args
        t0 = c * MOBA_QCHUNK
        qblk = t0 // MOBA_BLOCK
        pos_q = t0 + jnp.arange(MOBA_QCHUNK, dtype=jnp.int32)
        gate = jnp.einsum('bhqd,bhnd->bhqn', qi.astype(jnp.float32), kmean)
        past = jnp.arange(nb, dtype=jnp.int32) < qblk
        gate = jnp.where(past, gate, -jnp.inf)
        _, idx = lax.top_k(gate, nsel)
        valid = idx < qblk
        k_sel = gather(kb, idx)
        v_sel = gather(vb, idx)
        s_past = jnp.einsum('bhqd,bhqjsd->bhqjs', qi, k_sel,
                            preferred_element_type=jnp.float32) * scale
        pos_k_past = idx[..., None] * MOBA_BLOCK + blk_ar
        dist_past = (pos_q[None, None, :, None, None] - pos_k_past).astype(jnp.float32)
        s_past = s_past - slopes[None, :, None, None, None] * dist_past
        s_past = jnp.where(valid[..., None], s_past, -jnp.inf)
        k_own = lax.dynamic_slice_in_dim(kb, qblk, 1, axis=2)[:, :, 0]
        v_own = lax.dynamic_slice_in_dim(vb, qblk, 1, axis=2)[:, :, 0]
        pos_k_own = qblk * MOBA_BLOCK + blk_ar
        dist_own = (pos_q[:, None] - pos_k_own[None, :]).astype(jnp.float32)
        s_own = jnp.einsum('bhqd,bhsd->bhqs', qi, k_own,
                           preferred_element_type=jnp.float32) * scale
        s_own = s_own - slopes[None, :, None, None] * dist_own
        s_own = jnp.where(dist_own >= 0, s_own, -jnp.inf)
        logits = jnp.concatenate([s_past.reshape(bsz, nh, MOBA_QCHUNK, nsel * MOBA_BLOCK), s_own], -1)
        p = jax.nn.softmax(logits, axis=-1).astype(v.dtype)
        p_past = p[..., :nsel * MOBA_BLOCK].reshape(bsz, nh, MOBA_QCHUNK, nsel, MOBA_BLOCK)
        p_own = p[..., nsel * MOBA_BLOCK:]
        return (jnp.einsum('bhqjs,bhqjsd->bhqd', p_past, v_sel)
                + jnp.einsum('bhqs,bhsd->bhqd', p_own, v_own))

    out = lax.map(one_chunk, (qc, jnp.arange(n_qc, dtype=jnp.int32)))
    out = out.transpose(1, 2, 0, 3, 4).reshape(bsz, nh, sp, d)
    return out[:, :, :s]


def gla_attention(q, k, v, log_a):
    bsz, nh, s, dk = q.shape
    dv = v.shape[-1]
    nc = s // GLA_CHUNK

    def to_chunks(t):
        return t.astype(jnp.float32).reshape(bsz, nh, nc, GLA_CHUNK, t.shape[-1]).transpose(2, 0, 1, 3, 4)

    mask = jnp.tril(jnp.ones((GLA_CHUNK, GLA_CHUNK), dtype=bool))

    def step(state, inp):
        qc, kc, vc, gc = inp
        b = jnp.cumsum(gc, axis=-2)
        o_inter = jnp.einsum('bhcd,bhde->bhce', qc * jnp.exp(b), state)
        diff = b[:, :, :, None, :] - b[:, :, None, :, :]
        decay = jnp.exp(jnp.where(mask[None, None, :, :, None], diff, -jnp.inf))
        att = jnp.einsum('bhid,bhjd,bhijd->bhij', qc, kc, decay)
        o_intra = jnp.einsum('bhij,bhje->bhie', att, vc)
        b_last = b[:, :, -1:, :]
        new_state = (jnp.exp(b_last[:, :, 0, :, None]) * state
                     + jnp.einsum('bhcd,bhce->bhde', kc * jnp.exp(b_last - b), vc))
        return new_state, o_inter + o_intra

    state0 = jnp.zeros((bsz, nh, dk, dv), jnp.float32)
    _, out = lax.scan(step, state0, (to_chunks(q), to_chunks(k), to_chunks(v), to_chunks(log_a)))
    out = out.transpose(1, 2, 0, 3, 4).reshape(bsz, nh, s, dv)
    return out.astype(v.dtype)


def mem_attention(q, mk, mv):
    scale = q.shape[-1] ** -0.5
    logits = jnp.einsum('bhqd,bhmd->bhqm', q, mk, preferred_element_type=jnp.float32) * scale
    p = jax.nn.softmax(logits, axis=-1).astype(mv.dtype)
    return jnp.einsum('bhqm,bhmd->bhqd', p, mv)


def setup_inputs(seed: int = 0) -> dict:
    key = jax.random.key(seed)
    ks = jax.random.split(key, 16)
    f32 = jnp.float32

    def gain(k, shape):
        return 1.0 + 0.02 * jax.random.normal(k, shape, f32)

    return {
        "x": jax.random.normal(ks[0], (BATCH, SEQ, D_MODEL), f32),
        "mem": jax.random.normal(ks[1], (BATCH, MEM_LEN, D_MODEL), f32),
        "g_pre": gain(ks[2], (DEPTH, D_MODEL)),
        "w_in": jax.random.normal(ks[3], (DEPTH, D_MODEL, IN_PROJ), f32) * D_MODEL ** -0.5,
        "g_q_moba": gain(ks[4], (DEPTH, HEAD_DIM)),
        "g_k_moba": gain(ks[5], (DEPTH, HEAD_DIM)),
        "w_gate_up": jax.random.normal(ks[6], (DEPTH, GATE_RANK, B_KW), f32) * GATE_RANK ** -0.5,
        "b_gate_up": 0.1 * jax.random.normal(ks[7], (DEPTH, B_KW), f32),
        "g_gla_out": gain(ks[8], (DEPTH, B_DV)),
        "g_mem": gain(ks[9], (DEPTH, D_MODEL)),
        "w_mem_kv": jax.random.normal(ks[10], (DEPTH, D_MODEL, 2 * M_WIDTH), f32) * D_MODEL ** -0.5,
        "g_q_mem": gain(ks[11], (DEPTH, M_DH)),
        "g_k_mem": gain(ks[12], (DEPTH, M_DH)),
        "w_out": jax.random.normal(ks[13], (DEPTH, D_MODEL, D_MODEL), f32) * (D_MODEL * 2 * DEPTH) ** -0.5,
    }


def reference(x, mem, g_pre, w_in, g_q_moba, g_k_moba, w_gate_up, b_gate_up, g_gla_out,
              g_mem, w_mem_kv, g_q_mem, g_k_mem, w_out):
    split_idx = [int(i) for i in np.cumsum(IN_SPLITS)[:-1]]
    for l in range(DEPTH):
        h = rms_norm(x, g_pre[l])
        u = h @ w_in[l]
        (qa, ka, va, ga, qb, kb, vb, gb, rb, qm, gm) = jnp.split(u, split_idx, axis=-1)

        qa_h = rms_norm(split_heads(qa, A_HEADS), g_q_moba[l])
        ka_h = rms_norm(split_heads(ka, A_HEADS), g_k_moba[l])
        oa = merge_heads(moba_attention(qa_h, ka_h, split_heads(va, A_HEADS))) * jax.nn.silu(ga)

        z = rb @ w_gate_up[l] + b_gate_up[l]
        log_a = jax.nn.log_sigmoid(z.astype(jnp.float32)) / GATE_TEMP
        qb_h = split_heads(qb, B_HEADS) * (B_DK ** -0.5)
        ob = gla_attention(qb_h, split_heads(kb, B_HEADS), split_heads(vb, B_HEADS),
                           split_heads(log_a, B_HEADS))
        ob = merge_heads(rms_norm(ob, g_gla_out[l])) * jax.nn.silu(gb)

        mkv = rms_norm(mem, g_mem[l]) @ w_mem_kv[l]
        mk, mv = jnp.split(mkv, 2, axis=-1)
        qm_h = rms_norm(split_heads(qm, M_HEADS), g_q_mem[l])
        mk_h = rms_norm(split_heads(mk, M_HEADS), g_k_mem[l])
        om = merge_heads(mem_attention(qm_h, mk_h, split_heads(mv, M_HEADS))) * jax.nn.silu(gm)

        x = x + jnp.concatenate([oa, ob, om], axis=-1) @ w_out[l]
    return x
```

```python
import functools

import numpy as np
import jax
import jax.numpy as jnp
from jax import lax
from jax.experimental import pallas as pl
from jax.experimental.pallas import tpu as pltpu

F32 = jnp.float32
BF16 = jnp.bfloat16
HIGHEST = lax.Precision.HIGHEST

EPS = 1e-6
LOG2E = 1.4426950408889634
HEAD_DIM = 64
A_HEADS = 8
A_WIDTH = A_HEADS * HEAD_DIM
MOBA_BLOCK = 256
MOBA_TOPK = 3
G_HEADS = 4
G_DK = 32
G_DV = 64
G_KW = G_HEADS * G_DK
G_VW = G_HEADS * G_DV
GATE_RANK = 16
GATE_TEMP = 16.0
GLA_CHUNK = 64
GLA_SUB = 16
M_HEADS = 4
M_WIDTH = M_HEADS * HEAD_DIM
K_AUG = 128
V_AUG = 80
ROW_TILE = 512
VMEM_LIMIT = 48 * 1024 * 1024

NEG_BIG = -1e30
POS_BIG = 1e30

_NT = (((1,), (1,)), ((), ()))
_TN = (((0,), (0,)), ((), ()))


def _dot(a, b, **kw):
    return jnp.dot(a, b, preferred_element_type=F32, **kw)


def _dg(a, b, dims, **kw):
    return lax.dot_general(a, b, dims, preferred_element_type=F32, **kw)


def _split2(v):
    hi = v.astype(BF16)
    lo = (v - hi.astype(F32)).astype(BF16)
    return hi, lo


def _split3(v):
    hi = v.astype(BF16)
    r = v - hi.astype(F32)
    mid = r.astype(BF16)
    lo = (r - mid.astype(F32)).astype(BF16)
    return hi, mid, lo


def _group_sumsq(v, bones):
    hi, lo = _split2(v * v)
    return _dot(hi, bones) + _dot(lo, bones)


def _silu(v):
    return v / (1.0 + jnp.exp(-v))


def _block_ones(n, g):
    i = np.arange(n) // g
    return (i[:, None] == i[None, :]).astype(np.float32)


def _params(sem):
    return pltpu.CompilerParams(dimension_semantics=sem, vmem_limit_bytes=VMEM_LIMIT)


def _full(shape):
    return pl.BlockSpec(shape, lambda *_: (0,) * len(shape))


def _memkv_kernel(mem_ref, gmem_ref, w_ref, gk_ref, bones_ref, mk_ref, mvx_ref):
    m = mem_ref[...]
    ms = jnp.mean(m * m, axis=-1, keepdims=True)
    hn = (m * lax.rsqrt(ms + EPS) * gmem_ref[...]).astype(BF16)
    kv = _dot(hn, w_ref[...])
    mk = kv[:, :M_WIDTH]
    mv = kv[:, M_WIDTH:]
    ss = _group_sumsq(mk, bones_ref[...])
    mk_ref[...] = (mk * lax.rsqrt(ss * (1.0 / HEAD_DIM) + EPS) * gk_ref[...]).astype(BF16)
    lane = lax.broadcasted_iota(jnp.int32, mv.shape, 1) // HEAD_DIM
    for h in range(M_HEADS):
        hm = lane == h
        mvx_ref[h] = jnp.concatenate(
            [jnp.where(hm, mv, 0.0), jnp.where(hm, 1.0, 0.0)], axis=-1).astype(BF16)


def _memkv_call(mem, g_mem, w_mem_kv, g_k_mem):
    depth = g_mem.shape[0]
    bsz, mlen, d = mem.shape
    bones = jnp.asarray(_block_ones(M_WIDTH, HEAD_DIM), BF16)
    gk = jnp.tile(g_k_mem, (1, M_HEADS)).reshape(depth, 1, M_WIDTH)
    return pl.pallas_call(
        _memkv_kernel,
        grid=(depth, bsz),
        in_specs=[
            pl.BlockSpec((None, mlen, d), lambda l, b: (b, 0, 0)),
            pl.BlockSpec((None, 1, d), lambda l, b: (l, 0, 0)),
            pl.BlockSpec((None, d, 2 * M_WIDTH), lambda l, b: (l, 0, 0)),
            pl.BlockSpec((None, 1, M_WIDTH), lambda l, b: (l, 0, 0)),
            _full((M_WIDTH, M_WIDTH)),
        ],
        out_specs=[
            pl.BlockSpec((None, None, mlen, M_WIDTH), lambda l, b: (l, b, 0, 0)),
            pl.BlockSpec((None, None, M_HEADS, mlen, 2 * M_WIDTH), lambda l, b: (l, b, 0, 0, 0)),
        ],
        out_shape=[
            jax.ShapeDtypeStruct((depth, bsz, mlen, M_WIDTH), BF16),
            jax.ShapeDtypeStruct((depth, bsz, M_HEADS, mlen, 2 * M_WIDTH), BF16),
        ],
        compiler_params=_params(("arbitrary", "arbitrary")),
        name="memkv",
    )(mem, g_mem.reshape(depth, 1, d), w_mem_kv.astype(BF16), gk, bones)


_N_K = (0, 512)
_N_GBM = (512, 1024)
_N_QB = (1024, 1152)
_N_KB = (1152, 1280)
_N_VB = (1280, 1536)
_N_RB = (1536, 1664)
_N_QM = (1664, 1920)


def _proj_kernel(x_ref, gpre_ref, wn_ref, wt_ref, gk_ref, gqcol_ref, bones_ref, e_ref, wgu_ref, bgu_ref,
                 qT_ref, kaug_ref, vT_ref, kmean_ref, sgaT_ref, sgbm_ref, gq_ref, gk2_ref, gv_ref, gg_ref,
                 qm_ref):
    tm = x_ref.shape[0]
    nblk = tm // MOBA_BLOCK
    x = x_ref[...]
    ms = jnp.mean(x * x, axis=-1, keepdims=True)
    h = (x * lax.rsqrt(ms + EPS) * gpre_ref[...]).astype(BF16)

    def nat(cols):
        return _dot(h, wn_ref[:, cols[0]:cols[1]])

    k = nat(_N_K)
    ss = _group_sumsq(k, bones_ref[...])
    kn = k * lax.rsqrt(ss * (1.0 / HEAD_DIM) + EPS) * gk_ref[...]
    kmean_ref[...] = jnp.mean(kn.reshape(nblk, MOBA_BLOCK, A_WIDTH), axis=1)
    e = e_ref[...]
    for hh in range(A_HEADS):
        for t in range(nblk):
            rows = kn[t * MOBA_BLOCK:(t + 1) * MOBA_BLOCK, hh * HEAD_DIM:(hh + 1) * HEAD_DIM]
            kaug_ref[hh, t] = jnp.concatenate([rows, e], axis=-1).astype(BF16)

    qT = _dg(wt_ref[0:A_WIDTH, :], h, _NT)
    q3 = qT.reshape(A_HEADS, HEAD_DIM, tm)
    msq = jnp.mean(q3 * q3, axis=1, keepdims=True)
    qn = q3 * lax.rsqrt(msq + EPS) * gqcol_ref[...].reshape(1, HEAD_DIM, 1)
    qT_ref[...] = qn.reshape(A_WIDTH, tm)

    vT = _dg(wt_ref[A_WIDTH:2 * A_WIDTH, :], h, _NT)
    ones_rows = jnp.where(
        lax.broadcasted_iota(jnp.int32, (V_AUG - HEAD_DIM, MOBA_BLOCK), 0) == 0, 1.0, 0.0)
    for hh in range(A_HEADS):
        for t in range(nblk):
            blk = vT[hh * HEAD_DIM:(hh + 1) * HEAD_DIM, t * MOBA_BLOCK:(t + 1) * MOBA_BLOCK]
            vT_ref[hh, t] = jnp.concatenate([blk, ones_rows], axis=0).astype(BF16)

    sgaT_ref[...] = _silu(_dg(wt_ref[2 * A_WIDTH:3 * A_WIDTH, :], h, _NT)).astype(BF16)
    sgbm_ref[...] = _silu(nat(_N_GBM)).astype(BF16)

    gq_ref[...] = nat(_N_QB) * (G_DK ** -0.5)
    gk2_ref[...] = nat(_N_KB)
    gv_ref[...] = nat(_N_VB).astype(BF16)
    rb = nat(_N_RB)
    z = _dot(rb, wgu_ref[...], precision=HIGHEST) + bgu_ref[...]
    gg_ref[...] = (jnp.minimum(z, 0.0) - jnp.log(1.0 + jnp.exp(-jnp.abs(z)))) * (1.0 / GATE_TEMP)

    qm_ref[...] = nat(_N_QM)


def _proj_call(x, g_pre, wn, wt, gk, gqcol, bones, e, wgu, bgu):
    bsz, s, d = x.shape
    tm = ROW_TILE
    nt = s // tm
    nblk = tm // MOBA_BLOCK
    nb = s // MOBA_BLOCK
    row = lambda w: pl.BlockSpec((None, tm, w), lambda b, t: (b, t, 0))
    in_specs = [
        row(d), _full((1, d)), _full(wn.shape), _full(wt.shape), _full((1, A_WIDTH)),
        _full((HEAD_DIM, 1)), _full((A_WIDTH, A_WIDTH)), _full(e.shape), _full(wgu.shape), _full((1, G_KW)),
    ]
    out_specs = [
        pl.BlockSpec((None, A_WIDTH, tm), lambda b, t: (b, 0, t)),
        pl.BlockSpec((None, A_HEADS, nblk, MOBA_BLOCK, K_AUG), lambda b, t: (b, 0, t, 0, 0)),
        pl.BlockSpec((None, A_HEADS, nblk, V_AUG, MOBA_BLOCK), lambda b, t: (b, 0, t, 0, 0)),
        pl.BlockSpec((None, None, nblk, A_WIDTH), lambda b, t: (b, t, 0, 0)),
        pl.BlockSpec((None, A_WIDTH, tm), lambda b, t: (b, 0, t)),
        row(2 * M_WIDTH), row(G_KW), row(G_KW), row(G_VW), row(G_KW), row(M_WIDTH),
    ]
    out_shape = [
        jax.ShapeDtypeStruct((bsz, A_WIDTH, s), F32),
        jax.ShapeDtypeStruct((bsz, A_HEADS, nb, MOBA_BLOCK, K_AUG), BF16),
        jax.ShapeDtypeStruct((bsz, A_HEADS, nb, V_AUG, MOBA_BLOCK), BF16),
        jax.ShapeDtypeStruct((bsz, nt, nblk, A_WIDTH), F32),
        jax.ShapeDtypeStruct((bsz, A_WIDTH, s), BF16),
        jax.ShapeDtypeStruct((bsz, s, 2 * M_WIDTH), BF16),
        jax.ShapeDtypeStruct((bsz, s, G_KW), F32),
        jax.ShapeDtypeStruct((bsz, s, G_KW), F32),
        jax.ShapeDtypeStruct((bsz, s, G_VW), BF16),
        jax.ShapeDtypeStruct((bsz, s, G_KW), F32),
        jax.ShapeDtypeStruct((bsz, s, M_WIDTH), F32),
    ]
    return pl.pallas_call(
        _proj_kernel, grid=(bsz, nt), in_specs=in_specs, out_specs=out_specs, out_shape=out_shape,
        compiler_params=_params(("arbitrary", "arbitrary")), name="proj",
    )(x, g_pre, wn, wt, gk, gqcol, bones, e, wgu, bgu)


def _moba_kernel(c1_ref, qT_ref, kaug_ref, vT_ref, kmean_ref, qx_ref, sga_ref, o_ref, sel_ref):
    hh = pl.program_id(1)
    i = pl.program_id(2)
    c1 = c1_ref[hh]
    qT = qT_ref[...]

    gate = _dot(kmean_ref[...], qT, precision=HIGHEST)
    nb = gate.shape[0]
    rowf = lax.broadcasted_iota(jnp.int32, gate.shape, 0).astype(F32)
    g = jnp.where(rowf < i.astype(F32), gate, -jnp.inf)
    sel = jnp.zeros(gate.shape, F32)
    for _ in range(MOBA_TOPK):
        m = jnp.max(g, axis=0, keepdims=True)
        idx = jnp.min(jnp.where(g == m, rowf, float(nb)), axis=0, keepdims=True)
        hit = rowf == idx
        sel = jnp.where(hit, jnp.where(m > -jnp.inf, 1.0, sel), sel)
        g = jnp.where(hit, -jnp.inf, g)
    sel_ref[...] = sel

    q_aug = jnp.concatenate([(qT * (LOG2E * HEAD_DIM ** -0.5)).astype(BF16), qx_ref[...]], axis=0)

    s = _dot(kaug_ref[i], q_aug)
    kk = lax.broadcasted_iota(jnp.int32, s.shape, 0)
    qq = lax.broadcasted_iota(jnp.int32, s.shape, 1)
    s = jnp.where(kk <= qq, s, NEG_BIG)
    m0 = jnp.max(s, axis=0, keepdims=True)
    p = jnp.exp2(s - m0)
    acc0 = _dot(vT_ref[i], p.astype(BF16))

    def body(j, carry):
        m, acc = carry
        s = _dot(kaug_ref[j], q_aug)
        picked = sel_ref[pl.ds(j, 1), :] > 0.5
        cj = c1 * (MOBA_BLOCK * (j - i)).astype(F32)
        m_blk = jnp.max(s, axis=0, keepdims=True) + cj
        m_new = jnp.where(picked, jnp.maximum(m, m_blk), m)
        shift = jnp.where(picked, m_new - cj, POS_BIG)
        p = jnp.exp2(s - shift)
        alpha = jnp.exp2(m - m_new)
        return m_new, acc * alpha + _dot(vT_ref[j], p.astype(BF16))

    _, acc = lax.fori_loop(0, i, body, (m0, acc0))
    o = acc[:HEAD_DIM] / acc[HEAD_DIM:HEAD_DIM + 1]
    o_ref[...] = (o * sga_ref[...].astype(F32)).astype(BF16)


def _moba_call(c1, qT, kaug, vT, kmean, qx, sgaT):
    bsz, _, s = qT.shape
    nb = s // MOBA_BLOCK
    qblk = pl.BlockSpec((None, HEAD_DIM, MOBA_BLOCK), lambda b, h, i: (b, h, i))
    return pl.pallas_call(
        _moba_kernel,
        grid=(bsz, A_HEADS, nb),
        in_specs=[
            pl.BlockSpec(memory_space=pltpu.SMEM),
            qblk,
            pl.BlockSpec((None, None, nb, MOBA_BLOCK, K_AUG), lambda b, h, i: (b, h, 0, 0, 0)),
            pl.BlockSpec((None, None, nb, V_AUG, MOBA_BLOCK), lambda b, h, i: (b, h, 0, 0, 0)),
            pl.BlockSpec((None, None, nb, HEAD_DIM), lambda b, h, i: (b, h, 0, 0)),
            pl.BlockSpec((None, HEAD_DIM, MOBA_BLOCK), lambda b, h, i: (h, 0, 0)),
            qblk,
        ],
        out_specs=qblk,
        out_shape=jax.ShapeDtypeStruct((bsz, A_WIDTH, s), BF16),
        scratch_shapes=[pltpu.VMEM((nb, MOBA_BLOCK), F32)],
        compiler_params=_params(("arbitrary", "arbitrary", "arbitrary")),
        name="moba",
    )(c1, qT, kaug, vT, kmean, qx, sgaT)


def _gla_consts():
    c, sb = GLA_CHUNK, GLA_SUB
    r = np.arange(c)
    blk = r // sb
    tri = (r[None, :] <= r[:, None])
    same = blk[None, :] == blk[:, None]
    lall = np.concatenate([
        tri,
        tri & same,
        same,
        blk[None, :] == blk[:, None] - 1,
        blk[None, :] == blk[:, None] - 2,
    ], axis=0).astype(np.float32)
    diff = blk[:, None] - blk[None, :]
    band = np.stack([diff == 1, diff == 2, diff == 3, same & tri]).astype(np.float32)
    dk_head = np.arange(G_KW) // G_DK
    dv_head = np.arange(G_VW) // G_DV
    bdt = (dv_head[:, None] == dk_head[None, :]).astype(np.float32)
    return lall, band, bdt


def _gla_kernel(q_ref, k_ref, v_ref, g_ref, sgb_ref, lall_ref, band_ref, bdt_ref, bones_ref, gout_ref,
                o_ref, st_ref):
    c = GLA_CHUNK

    @pl.when(pl.program_id(1) == 0)
    def _():
        st_ref[...] = jnp.zeros_like(st_ref)

    lane_k = lax.broadcasted_iota(jnp.int32, (1, G_KW), 1) // G_DK
    lane_v = lax.broadcasted_iota(jnp.int32, (1, G_VW), 1) // G_DV
    n_chunks = q_ref.shape[0] // c

    def chunk(ci, carry):
        r0 = pl.multiple_of(ci * c, c)
        q = q_ref[pl.ds(r0, c), :]
        k = k_ref[pl.ds(r0, c), :]
        v = v_ref[pl.ds(r0, c), :]
        g = g_ref[pl.ds(r0, c), :]

        g3 = jnp.concatenate(_split3(g), axis=1)
        r = _dot(lall_ref[...], g3)
        r = r[:, :G_KW] + r[:, G_KW:2 * G_KW] + r[:, 2 * G_KW:]
        b, cc, tt, p1, p2 = (r[n * c:(n + 1) * c] for n in range(5))
        b_last = b[c - 1:c]

        qt = q * jnp.exp(cc)
        q2 = qt * jnp.exp(p1)
        q3 = q2 * jnp.exp(p2)
        kt = (k * jnp.exp(tt - cc)).astype(BF16)
        kd = (k * jnp.exp(-cc)).astype(BF16)
        qi = (q * jnp.exp(b)).astype(BF16)
        kl = (k * jnp.exp(b_last - b)).astype(BF16)

        def heads(t):
            return jnp.concatenate([jnp.where(lane_k == hd, t, 0.0) for hd in range(G_HEADS)],
                                   axis=0).astype(BF16)

        xs = [_dg(heads(t), kt, _NT) for t in (qt, q2, q3)]
        yd = _dg(heads(qt), kd, _NT)
        st = st_ref[...]
        o = _dg(qi, st.astype(BF16), _NT)
        for hd in range(G_HEADS):
            sl = slice(hd * c, (hd + 1) * c)
            att = (band_ref[0] * xs[0][sl] + band_ref[1] * xs[1][sl] + band_ref[2] * xs[2][sl]
                   + band_ref[3] * yd[sl])
            vh = jnp.where(lane_v == hd, v, jnp.zeros_like(v))
            o = o + _dot(att.astype(BF16), vh)
        st_ref[...] = st * jnp.exp(b_last) + _dg(v, kl, _TN) * bdt_ref[...]

        ss = _group_sumsq(o, bones_ref[...])
        y = o * lax.rsqrt(ss * (1.0 / G_DV) + EPS) * gout_ref[...]
        o_ref[pl.ds(r0, c), :] = (y * sgb_ref[pl.ds(r0, c), :].astype(F32)).astype(BF16)
        return carry

    lax.fori_loop(0, n_chunks, chunk, 0)


def _gla_call(gq, gk, gv, gg, sgbm, lall, band, bdt, bones, gout):
    bsz, s, _ = gq.shape
    tm = ROW_TILE
    row = lambda w: pl.BlockSpec((None, tm, w), lambda b, t: (b, t, 0))
    return pl.pallas_call(
        _gla_kernel,
        grid=(bsz, s // tm),
        in_specs=[row(G_KW), row(G_KW), row(G_VW), row(G_KW), row(G_VW),
                  _full(lall.shape), _full(band.shape), _full(bdt.shape), _full(bones.shape), _full((1, G_VW))],
        out_specs=row(G_VW),
        out_shape=jax.ShapeDtypeStruct((bsz, s, G_VW), BF16),
        scratch_shapes=[pltpu.VMEM((G_VW, G_KW), F32)],
        compiler_params=_params(("arbitrary", "arbitrary")),
        name="gla",
    )(gq, gk, gv, gg, sgbm, lall, band, bdt, bones, gout)


def _mem_kernel(qm_ref, sgm_ref, mk_ref, mvx_ref, gq_ref, bones_ref, o_ref):
    qm = qm_ref[...]
    ss = _group_sumsq(qm, bones_ref[...])
    qn = qm * lax.rsqrt(ss * (1.0 / HEAD_DIM) + EPS) * gq_ref[...]
    qs = qn * (LOG2E * HEAD_DIM ** -0.5)
    lane = lax.broadcasted_iota(jnp.int32, (1, M_WIDTH), 1) // HEAD_DIM
    mk = mk_ref[...]
    acc = jnp.zeros((qm.shape[0], 2 * M_WIDTH), F32)
    for h in range(M_HEADS):
        qh = jnp.where(lane == h, qs, 0.0).astype(BF16)
        s = _dg(qh, mk, _NT)
        m = jnp.max(s, axis=-1, keepdims=True)
        p = jnp.exp2(s - m).astype(BF16)
        acc = acc + _dot(p, mvx_ref[h])
    om = acc[:, :M_WIDTH] / acc[:, M_WIDTH:]
    o_ref[...] = (om * sgm_ref[...].astype(F32)).astype(BF16)


def _mem_call(qm, sgbm, mk, mvx, gq, bones):
    bsz, s, _ = qm.shape
    tm = ROW_TILE
    mlen = mk.shape[1]
    return pl.pallas_call(
        _mem_kernel,
        grid=(bsz, s // tm),
        in_specs=[
            pl.BlockSpec((None, tm, M_WIDTH), lambda b, t: (b, t, 0)),
            pl.BlockSpec((None, tm, M_WIDTH), lambda b, t: (b, t, 1)),
            pl.BlockSpec((None, mlen, M_WIDTH), lambda b, t: (b, 0, 0)),
            pl.BlockSpec((None, M_HEADS, mlen, 2 * M_WIDTH), lambda b, t: (b, 0, 0, 0)),
            _full((1, M_WIDTH)), _full((M_WIDTH, M_WIDTH)),
        ],
        out_specs=pl.BlockSpec((None, tm, M_WIDTH), lambda b, t: (b, t, 0)),
        out_shape=jax.ShapeDtypeStruct((bsz, s, M_WIDTH), BF16),
        compiler_params=_params(("arbitrary", "arbitrary")),
        name="memattn",
    )(qm, sgbm, mk, mvx, gq, bones)


def _out_kernel(x_ref, oaT_ref, ob_ref, om_ref, wa_ref, wb_ref, o_ref):
    y = _dg(oaT_ref[...], wa_ref[...], _TN)
    y = y + _dot(jnp.concatenate([ob_ref[...], om_ref[...]], axis=-1), wb_ref[...])
    o_ref[...] = x_ref[...] + y


def _out_call(x, oaT, ob, om, wa, wb):
    bsz, s, d = x.shape
    tm = ROW_TILE
    row = lambda w: pl.BlockSpec((None, tm, w), lambda b, t: (b, t, 0))
    return pl.pallas_call(
        _out_kernel,
        grid=(bsz, s // tm),
        in_specs=[row(d), pl.BlockSpec((None, A_WIDTH, tm), lambda b, t: (b, 0, t)), row(G_VW), row(M_WIDTH),
                  _full(wa.shape), _full(wb.shape)],
        out_specs=row(d),
        out_shape=jax.ShapeDtypeStruct((bsz, s, d), F32),
        compiler_params=_params(("arbitrary", "arbitrary")),
        name="outproj",
    )(x, oaT, ob, om, wa, wb)


def _alibi_consts():
    slopes = np.asarray([2.0 ** (-8.0 * (i + 1) / A_HEADS) for i in range(A_HEADS)], np.float32)
    c1 = (slopes * np.float32(LOG2E)).astype(np.float32)
    c1j = jnp.asarray(c1)
    pieces = list(_split3(c1j * 16.0)) + list(_split3(c1j))
    qx = jnp.zeros((A_HEADS, K_AUG - HEAD_DIM, MOBA_BLOCK), BF16)
    for n, pc in enumerate(pieces):
        qx = qx.at[:, n, :].set(jnp.broadcast_to(pc[:, None], (A_HEADS, MOBA_BLOCK)))
    pos = np.arange(MOBA_BLOCK)
    e = np.zeros((MOBA_BLOCK, K_AUG - HEAD_DIM), np.float32)
    e[:, 0:3] = (pos // 16)[:, None]
    e[:, 3:6] = (pos % 16)[:, None]
    return c1j, qx, jnp.asarray(e)


def kernel(x, mem, g_pre, w_in, g_q_moba, g_k_moba, w_gate_up, b_gate_up, g_gla_out,
           g_mem, w_mem_kv, g_q_mem, g_k_mem, w_out):
    depth = g_pre.shape[0]
    d = x.shape[-1]
    c1, qx, e = _alibi_consts()
    lall_np, band_np, bdt_np = _gla_consts()
    lall = jnp.asarray(lall_np, BF16)
    band = jnp.asarray(band_np)
    bdt = jnp.asarray(bdt_np)
    bones_a = jnp.asarray(_block_ones(A_WIDTH, HEAD_DIM), BF16)
    bones_m = jnp.asarray(_block_ones(M_WIDTH, HEAD_DIM), BF16)

    o_qa, o_ka, o_va, o_ga = 0, 512, 1024, 1536
    o_qb, o_kb, o_vb, o_gb, o_rb = 2048, 2176, 2304, 2560, 2816
    o_qm, o_gm = 2832, 3088

    mk_all, mvx_all = _memkv_call(mem, g_mem, w_mem_kv, g_k_mem)

    for l in range(depth):
        w = w_in[l]
        col = lambda o, n: w[:, o:o + n]
        rb_pad = jnp.pad(col(o_rb, GATE_RANK), ((0, 0), (0, G_KW - GATE_RANK)))
        wn = jnp.concatenate([col(o_ka, 512), col(o_gb, 256), col(o_gm, 256), col(o_qb, 128), col(o_kb, 128),
                              col(o_vb, 256), rb_pad, col(o_qm, 256)], axis=1).astype(BF16)
        wt = jnp.concatenate([col(o_qa, 512), col(o_va, 512), col(o_ga, 512)], axis=1).T.astype(BF16)
        wgu = jnp.pad(w_gate_up[l], ((0, G_KW - GATE_RANK), (0, 0)))
        gk = jnp.tile(g_k_moba[l], A_HEADS).reshape(1, A_WIDTH)

        (qT, kaug, vT, kmean_nat, sgaT, sgbm, gq, gk2, gv, gg, qm) = _proj_call(
            x, g_pre[l].reshape(1, d), wn, wt, gk, g_q_moba[l].reshape(HEAD_DIM, 1), bones_a, e,
            wgu, b_gate_up[l].reshape(1, G_KW))

        bsz, s = x.shape[0], x.shape[1]
        nb = s // MOBA_BLOCK
        kmean = kmean_nat.reshape(bsz, nb, A_HEADS, HEAD_DIM).transpose(0, 2, 1, 3)
        oaT = _moba_call(c1, qT, kaug, vT, kmean, qx, sgaT)
        ob = _gla_call(gq, gk2, gv, gg, sgbm, lall, band, bdt, bones_m,
                       jnp.tile(g_gla_out[l], G_HEADS).reshape(1, G_VW))
        om = _mem_call(qm, sgbm, mk_all[l], mvx_all[l],
                       jnp.tile(g_q_mem[l], M_HEADS).reshape(1, M_WIDTH), bones_m)
        wo = w_out[l].astype(BF16)
        x = _out_call(x, oaT, ob, om, wo[:A_WIDTH], wo[A_WIDTH:])
    return x
```

```python
import functools

import numpy as np
import jax
import jax.numpy as jnp
from jax import lax
from jax.experimental import pallas as pl
from jax.experimental.pallas import tpu as pltpu

F32 = jnp.float32
BF16 = jnp.bfloat16
HIGHEST = lax.Precision.HIGHEST

EPS = 1e-6
LOG2E = 1.4426950408889634
HEAD_DIM = 64
A_HEADS = 8
A_WIDTH = A_HEADS * HEAD_DIM
MOBA_BLOCK = 256
MOBA_TOPK = 3
MOBA_GROUP = 2
G_HEADS = 4
G_DK = 32
G_DV = 64
G_KW = G_HEADS * G_DK
G_VW = G_HEADS * G_DV
GATE_RANK = 16
GATE_TEMP = 16.0
GLA_CHUNK = 64
GLA_SUB = 16
M_HEADS = 4
M_WIDTH = M_HEADS * HEAD_DIM
K_AUG = 128
V_AUG = 80
ROW_TILE = 512
VMEM_LIMIT = 48 * 1024 * 1024

NEG_BIG = -1e30
POS_BIG = 1e30

_NT = (((1,), (1,)), ((), ()))
_TN = (((0,), (0,)), ((), ()))


def _dot(a, b, **kw):
    return jnp.dot(a, b, preferred_element_type=F32, **kw)


def _dg(a, b, dims, **kw):
    return lax.dot_general(a, b, dims, preferred_element_type=F32, **kw)


def _split2(v):
    hi = v.astype(BF16)
    lo = (v - hi.astype(F32)).astype(BF16)
    return hi, lo


def _split3(v):
    hi = v.astype(BF16)
    r = v - hi.astype(F32)
    mid = r.astype(BF16)
    lo = (r - mid.astype(F32)).astype(BF16)
    return hi, mid, lo


def _group_sumsq(v, bones):
    hi, lo = _split2(v * v)
    return _dot(hi, bones) + _dot(lo, bones)


def _silu(v):
    return v / (1.0 + jnp.exp(-v))


def _block_ones(n, g):
    i = np.arange(n) // g
    return (i[:, None] == i[None, :]).astype(np.float32)


def _params(sem):
    return pltpu.CompilerParams(dimension_semantics=sem, vmem_limit_bytes=VMEM_LIMIT)


def _full(shape):
    return pl.BlockSpec(shape, lambda *_: (0,) * len(shape))


def _memkv_kernel(mem_ref, gmem_ref, w_ref, gk_ref, bones_ref, mk_ref, mvx_ref):
    m = mem_ref[...]
    ms = jnp.mean(m * m, axis=-1, keepdims=True)
    hn = (m * lax.rsqrt(ms + EPS) * gmem_ref[...]).astype(BF16)
    kv = _dot(hn, w_ref[...])
    mk = kv[:, :M_WIDTH]
    mv = kv[:, M_WIDTH:]
    ss = _group_sumsq(mk, bones_ref[...])
    mk_ref[...] = (mk * lax.rsqrt(ss * (1.0 / HEAD_DIM) + EPS) * gk_ref[...]).astype(BF16)
    lane = lax.broadcasted_iota(jnp.int32, mv.shape, 1) // HEAD_DIM
    for h in range(M_HEADS):
        hm = lane == h
        mvx_ref[h] = jnp.concatenate(
            [jnp.where(hm, mv, 0.0), jnp.where(hm, 1.0, 0.0)], axis=-1).astype(BF16)


def _memkv_call(mem, g_mem, w_mem_kv, g_k_mem):
    depth = g_mem.shape[0]
    bsz, mlen, d = mem.shape
    bones = jnp.asarray(_block_ones(M_WIDTH, HEAD_DIM), BF16)
    gk = jnp.tile(g_k_mem, (1, M_HEADS)).reshape(depth, 1, M_WIDTH)
    return pl.pallas_call(
        _memkv_kernel,
        grid=(depth, bsz),
        in_specs=[
            pl.BlockSpec((None, mlen, d), lambda l, b: (b, 0, 0)),
            pl.BlockSpec((None, 1, d), lambda l, b: (l, 0, 0)),
            pl.BlockSpec((None, d, 2 * M_WIDTH), lambda l, b: (l, 0, 0)),
            pl.BlockSpec((None, 1, M_WIDTH), lambda l, b: (l, 0, 0)),
            _full((M_WIDTH, M_WIDTH)),
        ],
        out_specs=[
            pl.BlockSpec((None, None, mlen, M_WIDTH), lambda l, b: (l, b, 0, 0)),
            pl.BlockSpec((None, None, M_HEADS, mlen, 2 * M_WIDTH), lambda l, b: (l, b, 0, 0, 0)),
        ],
        out_shape=[
            jax.ShapeDtypeStruct((depth, bsz, mlen, M_WIDTH), BF16),
            jax.ShapeDtypeStruct((depth, bsz, M_HEADS, mlen, 2 * M_WIDTH), BF16),
        ],
        compiler_params=_params(("arbitrary", "arbitrary")),
        name="memkv",
    )(mem, g_mem.reshape(depth, 1, d), w_mem_kv.astype(BF16), gk, bones)


_N_K = (0, 512)
_N_GBM = (512, 1024)
_N_QB = (1024, 1152)
_N_KB = (1152, 1280)
_N_VB = (1280, 1536)
_N_RB = (1536, 1664)
_N_QM = (1664, 1920)


def _proj_kernel(x_ref, gpre_ref, wn_ref, wt_ref, gk_ref, gqcol_ref, bones_ref, e_ref, wgu_ref, bgu_ref,
                 qT_ref, kaug_ref, vT_ref, kmean_ref, sgaT_ref, sgbm_ref, gq_ref, gk2_ref, gv_ref, gg_ref,
                 qm_ref):
    tm = x_ref.shape[0]
    nblk = tm // MOBA_BLOCK
    x = x_ref[...]
    ms = jnp.mean(x * x, axis=-1, keepdims=True)
    h = (x * lax.rsqrt(ms + EPS) * gpre_ref[...]).astype(BF16)

    def nat(cols):
        return _dot(h, wn_ref[:, cols[0]:cols[1]])

    k = nat(_N_K)
    ss = _group_sumsq(k, bones_ref[...])
    kn = k * lax.rsqrt(ss * (1.0 / HEAD_DIM) + EPS) * gk_ref[...]
    kmean_ref[...] = jnp.mean(kn.reshape(nblk, MOBA_BLOCK, A_WIDTH), axis=1)
    e = jnp.concatenate([e_ref[...]] * nblk, axis=0)
    for hh in range(A_HEADS):
        rows = kn[:, hh * HEAD_DIM:(hh + 1) * HEAD_DIM]
        kaug_ref[hh] = jnp.concatenate([rows, e], axis=-1).astype(BF16)

    qT = _dg(wt_ref[0:A_WIDTH, :], h, _NT)
    q3 = qT.reshape(A_HEADS, HEAD_DIM, tm)
    msq = jnp.mean(q3 * q3, axis=1, keepdims=True)
    qn = q3 * lax.rsqrt(msq + EPS) * gqcol_ref[...].reshape(1, HEAD_DIM, 1)
    qT_ref[...] = qn.reshape(A_WIDTH, tm)

    vT = _dg(wt_ref[A_WIDTH:2 * A_WIDTH, :], h, _NT)
    ones_rows = jnp.where(lax.broadcasted_iota(jnp.int32, (V_AUG - HEAD_DIM, tm), 0) == 0, 1.0, 0.0)
    for hh in range(A_HEADS):
        blk = vT[hh * HEAD_DIM:(hh + 1) * HEAD_DIM, :]
        vT_ref[hh] = jnp.concatenate([blk, ones_rows], axis=0).astype(BF16)

    sgaT_ref[...] = _silu(_dg(wt_ref[2 * A_WIDTH:3 * A_WIDTH, :], h, _NT)).astype(BF16)
    sgbm_ref[...] = _silu(nat(_N_GBM)).astype(BF16)

    gq_ref[...] = nat(_N_QB) * (G_DK ** -0.5)
    gk2_ref[...] = nat(_N_KB)
    gv_ref[...] = nat(_N_VB).astype(BF16)
    rb = nat(_N_RB)
    z = _dot(rb, wgu_ref[...], precision=HIGHEST) + bgu_ref[...]
    gg_ref[...] = (jnp.minimum(z, 0.0) - jnp.log(1.0 + jnp.exp(-jnp.abs(z)))) * (1.0 / GATE_TEMP)

    qm_ref[...] = nat(_N_QM)


def _proj_call(x, g_pre, wn, wt, gk, gqcol, bones, e, wgu, bgu):
    bsz, s, d = x.shape
    tm = ROW_TILE
    nt = s // tm
    nblk = tm // MOBA_BLOCK
    nb = s // MOBA_BLOCK
    row = lambda w: pl.BlockSpec((None, tm, w), lambda b, t: (b, t, 0))
    in_specs = [
        row(d), _full((1, d)), _full(wn.shape), _full(wt.shape), _full((1, A_WIDTH)),
        _full((HEAD_DIM, 1)), _full((A_WIDTH, A_WIDTH)), _full(e.shape), _full(wgu.shape), _full((1, G_KW)),
    ]
    out_specs = [
        pl.BlockSpec((None, A_WIDTH, tm), lambda b, t: (b, 0, t)),
        pl.BlockSpec((None, A_HEADS, tm, K_AUG), lambda b, t: (b, 0, t, 0)),
        pl.BlockSpec((None, A_HEADS, None, V_AUG, tm), lambda b, t: (b, 0, t, 0, 0)),
        pl.BlockSpec((None, None, nblk, A_WIDTH), lambda b, t: (b, t, 0, 0)),
        pl.BlockSpec((None, A_WIDTH, tm), lambda b, t: (b, 0, t)),
        row(2 * M_WIDTH), row(G_KW), row(G_KW), row(G_VW), row(G_KW), row(M_WIDTH),
    ]
    out_shape = [
        jax.ShapeDtypeStruct((bsz, A_WIDTH, s), F32),
        jax.ShapeDtypeStruct((bsz, A_HEADS, s, K_AUG), BF16),
        jax.ShapeDtypeStruct((bsz, A_HEADS, nt, V_AUG, tm), BF16),
        jax.ShapeDtypeStruct((bsz, nt, nblk, A_WIDTH), F32),
        jax.ShapeDtypeStruct((bsz, A_WIDTH, s), BF16),
        jax.ShapeDtypeStruct((bsz, s, 2 * M_WIDTH), BF16),
        jax.ShapeDtypeStruct((bsz, s, G_KW), F32),
        jax.ShapeDtypeStruct((bsz, s, G_KW), F32),
        jax.ShapeDtypeStruct((bsz, s, G_VW), BF16),
        jax.ShapeDtypeStruct((bsz, s, G_KW), F32),
        jax.ShapeDtypeStruct((bsz, s, M_WIDTH), F32),
    ]
    return pl.pallas_call(
        _proj_kernel, grid=(bsz, nt), in_specs=in_specs, out_specs=out_specs, out_shape=out_shape,
        compiler_params=_params(("arbitrary", "arbitrary")), name="proj",
    )(x, g_pre, wn, wt, gk, gqcol, bones, e, wgu, bgu)


def _moba_kernel(c1_ref, qT_ref, kall_ref, kown_ref, vall_ref, vown_ref, kmean_ref, qx_ref, sga_ref, o_ref,
                 sel_ref, sa_ref, sb_ref):
    hh = pl.program_id(1)
    i = pl.program_id(2)
    c1 = c1_ref[hh]
    qT = qT_ref[...]

    gate = _dot(kmean_ref[...], qT, precision=HIGHEST)
    nb = gate.shape[0]
    rowf = lax.broadcasted_iota(jnp.int32, gate.shape, 0).astype(F32)
    g = jnp.where(rowf < i.astype(F32), gate, -jnp.inf)
    sel = jnp.zeros(gate.shape, F32)
    for _ in range(MOBA_TOPK):
        m = jnp.max(g, axis=0, keepdims=True)
        idx = jnp.min(jnp.where(g == m, rowf, float(nb)), axis=0, keepdims=True)
        hit = rowf == idx
        sel = jnp.where(hit, jnp.where(m > -jnp.inf, 1.0, sel), sel)
        g = jnp.where(hit, -jnp.inf, g)
    sel_ref[...] = sel

    q_aug = jnp.concatenate([(qT * (LOG2E * HEAD_DIM ** -0.5)).astype(BF16), qx_ref[...]], axis=0)

    gk = MOBA_GROUP * MOBA_BLOCK
    n_groups = kall_ref.shape[0] // gk

    def scores(g):
        r0 = pl.multiple_of(g * gk, gk)
        return _dot(kall_ref[pl.ds(r0, gk), :], q_aug)

    sa_ref[...] = scores(0)

    s = _dot(kown_ref[...], q_aug)
    kk = lax.broadcasted_iota(jnp.int32, s.shape, 0)
    qq = lax.broadcasted_iota(jnp.int32, s.shape, 1)
    s = jnp.where(kk <= qq, s, NEG_BIG)
    m0 = jnp.max(s, axis=0, keepdims=True)
    p = jnp.exp2(s - m0)
    acc0 = _dot(vown_ref[...], p.astype(BF16))

    def consume(s_ref, g, m, acc):
        members = []
        m_new = m
        for u in range(MOBA_GROUP):
            j = g * MOBA_GROUP + u
            s = s_ref[u * MOBA_BLOCK:(u + 1) * MOBA_BLOCK, :]
            picked = sel_ref[pl.ds(j, 1), :] > 0.5
            cj = c1 * (MOBA_BLOCK * (j - i)).astype(F32)
            m_blk = jnp.max(s, axis=0, keepdims=True) + cj
            m_new = jnp.where(picked, jnp.maximum(m_new, m_blk), m_new)
            members.append((s, picked, cj))
        ps = []
        for s, picked, cj in members:
            shift = jnp.where(picked, m_new - cj, POS_BIG)
            ps.append(jnp.exp2(s - shift).astype(BF16))
        alpha = jnp.exp2(m - m_new)
        return m_new, acc * alpha + _dot(vall_ref[g], jnp.concatenate(ps, axis=0))

    def body(t, carry):
        m, acc = carry
        g = 2 * t
        sb_ref[...] = scores(g + 1)
        m, acc = consume(sa_ref, g, m, acc)
        sa_ref[...] = scores(jnp.minimum(g + 2, n_groups - 1))
        m, acc = consume(sb_ref, g + 1, m, acc)
        return m, acc

    trips = (i + 2 * MOBA_GROUP - 1) // (2 * MOBA_GROUP)
    _, acc = lax.fori_loop(0, trips, body, (m0, acc0))
    o = acc[:HEAD_DIM] / acc[HEAD_DIM:HEAD_DIM + 1]
    o_ref[...] = (o * sga_ref[...].astype(F32)).astype(BF16)


def _moba_call(c1, qT, kaug, vT, kmean, qx, sgaT):
    bsz, _, s = qT.shape
    nb = s // MOBA_BLOCK
    gk = MOBA_GROUP * MOBA_BLOCK
    assert nb % (2 * MOBA_GROUP) == 0 and vT.shape[-1] == gk
    qblk = pl.BlockSpec((None, HEAD_DIM, MOBA_BLOCK), lambda b, h, i: (b, h, i))
    return pl.pallas_call(
        _moba_kernel,
        grid=(bsz, A_HEADS, nb),
        in_specs=[
            pl.BlockSpec(memory_space=pltpu.SMEM),
            qblk,
            pl.BlockSpec((None, None, s, K_AUG), lambda b, h, i: (b, h, 0, 0)),
            pl.BlockSpec((None, None, MOBA_BLOCK, K_AUG), lambda b, h, i: (b, h, i, 0)),
            pl.BlockSpec((None, None, s // gk, V_AUG, gk), lambda b, h, i: (b, h, 0, 0, 0)),
            pl.BlockSpec((None, None, None, V_AUG, MOBA_BLOCK),
                         lambda b, h, i: (b, h, i // MOBA_GROUP, 0, i % MOBA_GROUP)),
            pl.BlockSpec((None, None, nb, HEAD_DIM), lambda b, h, i: (b, h, 0, 0)),
            pl.BlockSpec((None, HEAD_DIM, MOBA_BLOCK), lambda b, h, i: (h, 0, 0)),
            qblk,
        ],
        out_specs=qblk,
        out_shape=jax.ShapeDtypeStruct((bsz, A_WIDTH, s), BF16),
        scratch_shapes=[pltpu.VMEM((nb, MOBA_BLOCK), F32),
                        pltpu.VMEM((gk, MOBA_BLOCK), F32), pltpu.VMEM((gk, MOBA_BLOCK), F32)],
        compiler_params=_params(("arbitrary", "arbitrary", "arbitrary")),
        name="moba",
    )(c1, qT, kaug, kaug, vT, vT, kmean, qx, sgaT)


def _gla_consts():
    c, sb = GLA_CHUNK, GLA_SUB
    r = np.arange(c)
    blk = r // sb
    tri = (r[None, :] <= r[:, None])
    same = blk[None, :] == blk[:, None]
    lall = np.concatenate([
        tri,
        tri & same,
        same,
        blk[None, :] == blk[:, None] - 1,
        blk[None, :] == blk[:, None] - 2,
    ], axis=0).astype(np.float32)
    diff = blk[:, None] - blk[None, :]
    band = np.stack([diff == 1, diff == 2, diff == 3, same & tri]).astype(np.float32)
    dk_head = np.arange(G_KW) // G_DK
    dv_head = np.arange(G_VW) // G_DV
    bdt = (dv_head[:, None] == dk_head[None, :]).astype(np.float32)
    return lall, band, bdt


def _gla_kernel(q_ref, k_ref, v_ref, g_ref, sgb_ref, lall_ref, band_ref, bdt_ref, bones_ref, gout_ref,
                o_ref, st_ref):
    c = GLA_CHUNK

    @pl.when(pl.program_id(1) == 0)
    def _():
        st_ref[...] = jnp.zeros_like(st_ref)

    lane_k = lax.broadcasted_iota(jnp.int32, (1, G_KW), 1) // G_DK
    lane_v = lax.broadcasted_iota(jnp.int32, (1, G_VW), 1) // G_DV
    n_chunks = q_ref.shape[0] // c

    def chunk(ci, carry):
        r0 = pl.multiple_of(ci * c, c)
        q = q_ref[pl.ds(r0, c), :]
        k = k_ref[pl.ds(r0, c), :]
        v = v_ref[pl.ds(r0, c), :]
        g = g_ref[pl.ds(r0, c), :]

        g3 = jnp.concatenate(_split3(g), axis=1)
        r = _dot(lall_ref[...], g3)
        r = r[:, :G_KW] + r[:, G_KW:2 * G_KW] + r[:, 2 * G_KW:]
        b, cc, tt, p1, p2 = (r[n * c:(n + 1) * c] for n in range(5))
        b_last = b[c - 1:c]

        qt = q * jnp.exp(cc)
        q2 = qt * jnp.exp(p1)
        q3 = q2 * jnp.exp(p2)
        kt = (k * jnp.exp(tt - cc)).astype(BF16)
        kd = (k * jnp.exp(-cc)).astype(BF16)
        qi = (q * jnp.exp(b)).astype(BF16)
        kl = (k * jnp.exp(b_last - b)).astype(BF16)

        def heads(t):
            return jnp.concatenate([jnp.where(lane_k == hd, t, 0.0) for hd in range(G_HEADS)],
                                   axis=0).astype(BF16)

        xs = [_dg(heads(t), kt, _NT) for t in (qt, q2, q3)]
        yd = _dg(heads(qt), kd, _NT)
        st = st_ref[...]
        o = _dg(qi, st.astype(BF16), _NT)
        for hd in range(G_HEADS):
            sl = slice(hd * c, (hd + 1) * c)
            att = (band_ref[0] * xs[0][sl] + band_ref[1] * xs[1][sl] + band_ref[2] * xs[2][sl]
                   + band_ref[3] * yd[sl])
            vh = jnp.where(lane_v == hd, v, jnp.zeros_like(v))
            o = o + _dot(att.astype(BF16), vh)
        st_ref[...] = st * jnp.exp(b_last) + _dg(v, kl, _TN) * bdt_ref[...]

        ss = _group_sumsq(o, bones_ref[...])
        y = o * lax.rsqrt(ss * (1.0 / G_DV) + EPS) * gout_ref[...]
        o_ref[pl.ds(r0, c), :] = (y * sgb_ref[pl.ds(r0, c), :].astype(F32)).astype(BF16)
        return carry

    lax.fori_loop(0, n_chunks, chunk, 0)


def _gla_call(gq, gk, gv, gg, sgbm, lall, band, bdt, bones, gout):
    bsz, s, _ = gq.shape
    tm = ROW_TILE
    row = lambda w: pl.BlockSpec((None, tm, w), lambda b, t: (b, t, 0))
    return pl.pallas_call(
        _gla_kernel,
        grid=(bsz, s // tm),
        in_specs=[row(G_KW), row(G_KW), row(G_VW), row(G_KW), row(G_VW),
                  _full(lall.shape), _full(band.shape), _full(bdt.shape), _full(bones.shape), _full((1, G_VW))],
        out_specs=row(G_VW),
        out_shape=jax.ShapeDtypeStruct((bsz, s, G_VW), BF16),
        scratch_shapes=[pltpu.VMEM((G_VW, G_KW), F32)],
        compiler_params=_params(("arbitrary", "arbitrary")),
        name="gla",
    )(gq, gk, gv, gg, sgbm, lall, band, bdt, bones, gout)


def _mem_kernel(qm_ref, sgm_ref, mk_ref, mvx_ref, gq_ref, bones_ref, o_ref):
    qm = qm_ref[...]
    ss = _group_sumsq(qm, bones_ref[...])
    qn = qm * lax.rsqrt(ss * (1.0 / HEAD_DIM) + EPS) * gq_ref[...]
    qs = qn * (LOG2E * HEAD_DIM ** -0.5)
    lane = lax.broadcasted_iota(jnp.int32, (1, M_WIDTH), 1) // HEAD_DIM
    mk = mk_ref[...]
    acc = jnp.zeros((qm.shape[0], 2 * M_WIDTH), F32)
    for h in range(M_HEADS):
        qh = jnp.where(lane == h, qs, 0.0).astype(BF16)
        s = _dg(qh, mk, _NT)
        m = jnp.max(s, axis=-1, keepdims=True)
        p = jnp.exp2(s - m).astype(BF16)
        acc = acc + _dot(p, mvx_ref[h])
    om = acc[:, :M_WIDTH] / acc[:, M_WIDTH:]
    o_ref[...] = (om * sgm_ref[...].astype(F32)).astype(BF16)


def _mem_call(qm, sgbm, mk, mvx, gq, bones):
    bsz, s, _ = qm.shape
    tm = ROW_TILE
    mlen = mk.shape[1]
    return pl.pallas_call(
        _mem_kernel,
        grid=(bsz, s // tm),
        in_specs=[
            pl.BlockSpec((None, tm, M_WIDTH), lambda b, t: (b, t, 0)),
            pl.BlockSpec((None, tm, M_WIDTH), lambda b, t: (b, t, 1)),
            pl.BlockSpec((None, mlen, M_WIDTH), lambda b, t: (b, 0, 0)),
            pl.BlockSpec((None, M_HEADS, mlen, 2 * M_WIDTH), lambda b, t: (b, 0, 0, 0)),
            _full((1, M_WIDTH)), _full((M_WIDTH, M_WIDTH)),
        ],
        out_specs=pl.BlockSpec((None, tm, M_WIDTH), lambda b, t: (b, t, 0)),
        out_shape=jax.ShapeDtypeStruct((bsz, s, M_WIDTH), BF16),
        compiler_params=_params(("arbitrary", "arbitrary")),
        name="memattn",
    )(qm, sgbm, mk, mvx, gq, bones)


def _out_kernel(x_ref, oaT_ref, ob_ref, om_ref, wa_ref, wb_ref, o_ref):
    y = _dg(oaT_ref[...], wa_ref[...], _TN)
    y = y + _dot(jnp.concatenate([ob_ref[...], om_ref[...]], axis=-1), wb_ref[...])
    o_ref[...] = x_ref[...] + y


def _out_call(x, oaT, ob, om, wa, wb):
    bsz, s, d = x.shape
    tm = ROW_TILE
    row = lambda w: pl.BlockSpec((None, tm, w), lambda b, t: (b, t, 0))
    return pl.pallas_call(
        _out_kernel,
        grid=(bsz, s // tm),
        in_specs=[row(d), pl.BlockSpec((None, A_WIDTH, tm), lambda b, t: (b, 0, t)), row(G_VW), row(M_WIDTH),
                  _full(wa.shape), _full(wb.shape)],
        out_specs=row(d),
        out_shape=jax.ShapeDtypeStruct((bsz, s, d), F32),
        compiler_params=_params(("arbitrary", "arbitrary")),
        name="outproj",
    )(x, oaT, ob, om, wa, wb)


def _alibi_consts():
    slopes = np.asarray([2.0 ** (-8.0 * (i + 1) / A_HEADS) for i in range(A_HEADS)], np.float32)
    c1 = (slopes * np.float32(LOG2E)).astype(np.float32)
    c1j = jnp.asarray(c1)
    pieces = list(_split3(c1j * 16.0)) + list(_split3(c1j))
    qx = jnp.zeros((A_HEADS, K_AUG - HEAD_DIM, MOBA_BLOCK), BF16)
    for n, pc in enumerate(pieces):
        qx = qx.at[:, n, :].set(jnp.broadcast_to(pc[:, None], (A_HEADS, MOBA_BLOCK)))
    pos = np.arange(MOBA_BLOCK)
    e = np.zeros((MOBA_BLOCK, K_AUG - HEAD_DIM), np.float32)
    e[:, 0:3] = (pos // 16)[:, None]
    e[:, 3:6] = (pos % 16)[:, None]
    return c1j, qx, jnp.asarray(e)


def kernel(x, mem, g_pre, w_in, g_q_moba, g_k_moba, w_gate_up, b_gate_up, g_gla_out,
           g_mem, w_mem_kv, g_q_mem, g_k_mem, w_out):
    depth = g_pre.shape[0]
    d = x.shape[-1]
    c1, qx, e = _alibi_consts()
    lall_np, band_np, bdt_np = _gla_consts()
    lall = jnp.asarray(lall_np, BF16)
    band = jnp.asarray(band_np)
    bdt = jnp.asarray(bdt_np)
    bones_a = jnp.asarray(_block_ones(A_WIDTH, HEAD_DIM), BF16)
    bones_m = jnp.asarray(_block_ones(M_WIDTH, HEAD_DIM), BF16)

    o_qa, o_ka, o_va, o_ga = 0, 512, 1024, 1536
    o_qb, o_kb, o_vb, o_gb, o_rb = 2048, 2176, 2304, 2560, 2816
    o_qm, o_gm = 2832, 3088

    mk_all, mvx_all = _memkv_call(mem, g_mem, w_mem_kv, g_k_mem)

    for l in range(depth):
        w = w_in[l]
        col = lambda o, n: w[:, o:o + n]
        rb_pad = jnp.pad(col(o_rb, GATE_RANK), ((0, 0), (0, G_KW - GATE_RANK)))
        wn = jnp.concatenate([col(o_ka, 512), col(o_gb, 256), col(o_gm, 256), col(o_qb, 128), col(o_kb, 128),
                              col(o_vb, 256), rb_pad, col(o_qm, 256)], axis=1).astype(BF16)
        wt = jnp.concatenate([col(o_qa, 512), col(o_va, 512), col(o_ga, 512)], axis=1).T.astype(BF16)
        wgu = jnp.pad(w_gate_up[l], ((0, G_KW - GATE_RANK), (0, 0)))
        gk = jnp.tile(g_k_moba[l], A_HEADS).reshape(1, A_WIDTH)

        (qT, kaug, vT, kmean_nat, sgaT, sgbm, gq, gk2, gv, gg, qm) = _proj_call(
            x, g_pre[l].reshape(1, d), wn, wt, gk, g_q_moba[l].reshape(HEAD_DIM, 1), bones_a, e,
            wgu, b_gate_up[l].reshape(1, G_KW))

        bsz, s = x.shape[0], x.shape[1]
        nb = s // MOBA_BLOCK
        kmean = kmean_nat.reshape(bsz, nb, A_HEADS, HEAD_DIM).transpose(0, 2, 1, 3)
        oaT = _moba_call(c1, qT, kaug, vT, kmean, qx, sgaT)
        ob = _gla_call(gq, gk2, gv, gg, sgbm, lall, band, bdt, bones_m,
                       jnp.tile(g_gla_out[l], G_HEADS).reshape(1, G_VW))
        om = _mem_call(qm, sgbm, mk_all[l], mvx_all[l],
                       jnp.tile(g_q_mem[l], M_HEADS).reshape(1, M_WIDTH), bones_m)
        wo = w_out[l].astype(BF16)
        x = _out_call(x, oaT, ob, om, wo[:A_WIDTH], wo[A_WIDTH:])
    return x
```

```python
import functools

import numpy as np
import jax
import jax.numpy as jnp
from jax import lax
from jax.experimental import pallas as pl
from jax.experimental.pallas import tpu as pltpu

F32 = jnp.float32
BF16 = jnp.bfloat16
HIGHEST = lax.Precision.HIGHEST

EPS = 1e-6
LOG2E = 1.4426950408889634
HEAD_DIM = 64
A_HEADS = 8
A_WIDTH = A_HEADS * HEAD_DIM
MOBA_BLOCK = 256
MOBA_TOPK = 3
MOBA_GROUP = 4
SEL_WIDTH = 2048
G_HEADS = 4
G_DK = 32
G_DV = 64
G_KW = G_HEADS * G_DK
G_VW = G_HEADS * G_DV
GATE_RANK = 16
GATE_TEMP = 16.0
GLA_CHUNK = 64
GLA_SUB = 16
M_HEADS = 4
M_WIDTH = M_HEADS * HEAD_DIM
K_AUG = 128
V_AUG = 80
ROW_TILE = 512
VMEM_LIMIT = 48 * 1024 * 1024
MOBA_VMEM_LIMIT = 56 * 1024 * 1024

NEG_BIG = -1e30
POS_BIG = 1e30

_NT = (((1,), (1,)), ((), ()))
_TN = (((0,), (0,)), ((), ()))


def _dot(a, b, **kw):
    return jnp.dot(a, b, preferred_element_type=F32, **kw)


def _dg(a, b, dims, **kw):
    return lax.dot_general(a, b, dims, preferred_element_type=F32, **kw)


def _split2(v):
    hi = v.astype(BF16)
    lo = (v - hi.astype(F32)).astype(BF16)
    return hi, lo


def _split3(v):
    hi = v.astype(BF16)
    r = v - hi.astype(F32)
    mid = r.astype(BF16)
    lo = (r - mid.astype(F32)).astype(BF16)
    return hi, mid, lo


def _group_sumsq(v, bones):
    hi, lo = _split2(v * v)
    return _dot(hi, bones) + _dot(lo, bones)


def _silu(v):
    return v / (1.0 + jnp.exp(-v))


def _block_ones(n, g):
    i = np.arange(n) // g
    return (i[:, None] == i[None, :]).astype(np.float32)


def _params(sem):
    return pltpu.CompilerParams(dimension_semantics=sem, vmem_limit_bytes=VMEM_LIMIT)


def _full(shape):
    return pl.BlockSpec(shape, lambda *_: (0,) * len(shape))


def _memkv_kernel(mem_ref, gmem_ref, w_ref, gk_ref, bones_ref, mk_ref, mvx_ref):
    m = mem_ref[...]
    ms = jnp.mean(m * m, axis=-1, keepdims=True)
    hn = (m * lax.rsqrt(ms + EPS) * gmem_ref[...]).astype(BF16)
    kv = _dot(hn, w_ref[...])
    mk = kv[:, :M_WIDTH]
    mv = kv[:, M_WIDTH:]
    ss = _group_sumsq(mk, bones_ref[...])
    mk_ref[...] = (mk * lax.rsqrt(ss * (1.0 / HEAD_DIM) + EPS) * gk_ref[...]).astype(BF16)
    lane = lax.broadcasted_iota(jnp.int32, mv.shape, 1) // HEAD_DIM
    for h in range(M_HEADS):
        hm = lane == h
        mvx_ref[h] = jnp.concatenate(
            [jnp.where(hm, mv, 0.0), jnp.where(hm, 1.0, 0.0)], axis=-1).astype(BF16)


def _memkv_call(mem, g_mem, w_mem_kv, g_k_mem):
    depth = g_mem.shape[0]
    bsz, mlen, d = mem.shape
    bones = jnp.asarray(_block_ones(M_WIDTH, HEAD_DIM), BF16)
    gk = jnp.tile(g_k_mem, (1, M_HEADS)).reshape(depth, 1, M_WIDTH)
    return pl.pallas_call(
        _memkv_kernel,
        grid=(depth, bsz),
        in_specs=[
            pl.BlockSpec((None, mlen, d), lambda l, b: (b, 0, 0)),
            pl.BlockSpec((None, 1, d), lambda l, b: (l, 0, 0)),
            pl.BlockSpec((None, d, 2 * M_WIDTH), lambda l, b: (l, 0, 0)),
            pl.BlockSpec((None, 1, M_WIDTH), lambda l, b: (l, 0, 0)),
            _full((M_WIDTH, M_WIDTH)),
        ],
        out_specs=[
            pl.BlockSpec((None, None, mlen, M_WIDTH), lambda l, b: (l, b, 0, 0)),
            pl.BlockSpec((None, None, M_HEADS, mlen, 2 * M_WIDTH), lambda l, b: (l, b, 0, 0, 0)),
        ],
        out_shape=[
            jax.ShapeDtypeStruct((depth, bsz, mlen, M_WIDTH), BF16),
            jax.ShapeDtypeStruct((depth, bsz, M_HEADS, mlen, 2 * M_WIDTH), BF16),
        ],
        compiler_params=_params(("arbitrary", "arbitrary")),
        name="memkv",
    )(mem, g_mem.reshape(depth, 1, d), w_mem_kv.astype(BF16), gk, bones)


_N_K = (0, 512)
_N_GBM = (512, 1024)
_N_QB = (1024, 1152)
_N_KB = (1152, 1280)
_N_VB = (1280, 1536)
_N_RB = (1536, 1664)
_N_QM = (1664, 1920)


def _proj_kernel(x_ref, gpre_ref, wn_ref, wt_ref, gk_ref, gqcol_ref, bones_ref, e_ref, qx_ref, wgu_ref, bgu_ref,
                 qT_ref, qa_ref, kaug_ref, vT_ref, kmean_ref, sga_ref, sgbm_ref, gq_ref, gk2_ref, gv_ref, gg_ref,
                 qm_ref):
    tm = x_ref.shape[0]
    nblk = tm // MOBA_BLOCK
    x = x_ref[...]
    ms = jnp.mean(x * x, axis=-1, keepdims=True)
    h = (x * lax.rsqrt(ms + EPS) * gpre_ref[...]).astype(BF16)

    def nat(cols):
        return _dot(h, wn_ref[:, cols[0]:cols[1]])

    k = nat(_N_K)
    ss = _group_sumsq(k, bones_ref[...])
    kn = k * lax.rsqrt(ss * (1.0 / HEAD_DIM) + EPS) * gk_ref[...]
    kmean_ref[...] = jnp.mean(kn.reshape(nblk, MOBA_BLOCK, A_WIDTH), axis=1)
    e = jnp.concatenate([e_ref[...]] * nblk, axis=0)
    for hh in range(A_HEADS):
        rows = kn[:, hh * HEAD_DIM:(hh + 1) * HEAD_DIM]
        kaug_ref[hh] = jnp.concatenate([rows, e], axis=-1).astype(BF16)

    qT = _dg(wt_ref[0:A_WIDTH, :], h, _NT)
    q3 = qT.reshape(A_HEADS, HEAD_DIM, tm)
    msq = jnp.mean(q3 * q3, axis=1, keepdims=True)
    qn = q3 * lax.rsqrt(msq + EPS) * gqcol_ref[...].reshape(1, HEAD_DIM, 1)
    qT_ref[...] = qn.reshape(A_WIDTH, tm)
    qs = (qn * (LOG2E * HEAD_DIM ** -0.5)).astype(BF16)
    blocks = [slice(t * MOBA_BLOCK, (t + 1) * MOBA_BLOCK) for t in range(nblk)]
    for hh in range(A_HEADS):
        for t, cols in enumerate(blocks):
            qa_ref[hh, t] = jnp.concatenate([qs[hh][:, cols], qx_ref[hh]], axis=0)

    vT = _dg(wt_ref[A_WIDTH:2 * A_WIDTH, :], h, _NT)
    ones_rows = jnp.where(
        lax.broadcasted_iota(jnp.int32, (V_AUG - HEAD_DIM, MOBA_BLOCK), 0) == 0, 1.0, 0.0)
    sga = _silu(_dg(wt_ref[2 * A_WIDTH:3 * A_WIDTH, :], h, _NT)).astype(BF16)
    for hh in range(A_HEADS):
        rows = slice(hh * HEAD_DIM, (hh + 1) * HEAD_DIM)
        for t, cols in enumerate(blocks):
            vT_ref[hh, t] = jnp.concatenate([vT[rows, cols], ones_rows], axis=0).astype(BF16)
            sga_ref[hh, t] = sga[rows, cols]
    sgbm_ref[...] = _silu(nat(_N_GBM)).astype(BF16)

    gq_ref[...] = nat(_N_QB) * (G_DK ** -0.5)
    gk2_ref[...] = nat(_N_KB)
    gv_ref[...] = nat(_N_VB).astype(BF16)
    rb = nat(_N_RB)
    z = _dot(rb, wgu_ref[...], precision=HIGHEST) + bgu_ref[...]
    gg_ref[...] = (jnp.minimum(z, 0.0) - jnp.log(1.0 + jnp.exp(-jnp.abs(z)))) * (1.0 / GATE_TEMP)

    qm_ref[...] = nat(_N_QM)


def _proj_call(x, g_pre, wn, wt, gk, gqcol, bones, e, qx, wgu, bgu):
    bsz, s, d = x.shape
    tm = ROW_TILE
    nt = s // tm
    nblk = tm // MOBA_BLOCK
    nb = s // MOBA_BLOCK
    row = lambda w: pl.BlockSpec((None, tm, w), lambda b, t: (b, t, 0))
    in_specs = [
        row(d), _full((1, d)), _full(wn.shape), _full(wt.shape), _full((1, A_WIDTH)),
        _full((HEAD_DIM, 1)), _full((A_WIDTH, A_WIDTH)), _full(e.shape), _full(qx.shape), _full(wgu.shape),
        _full((1, G_KW)),
    ]
    blocked = lambda r: pl.BlockSpec((None, A_HEADS, nblk, r, MOBA_BLOCK), lambda b, t: (b, 0, t, 0, 0))
    out_specs = [
        pl.BlockSpec((None, A_WIDTH, tm), lambda b, t: (b, 0, t)),
        blocked(K_AUG),
        pl.BlockSpec((None, A_HEADS, tm, K_AUG), lambda b, t: (b, 0, t, 0)),
        blocked(V_AUG),
        pl.BlockSpec((None, None, nblk, A_WIDTH), lambda b, t: (b, t, 0, 0)),
        blocked(HEAD_DIM),
        row(2 * M_WIDTH), row(G_KW), row(G_KW), row(G_VW), row(G_KW), row(M_WIDTH),
    ]
    out_shape = [
        jax.ShapeDtypeStruct((bsz, A_WIDTH, s), F32),
        jax.ShapeDtypeStruct((bsz, A_HEADS, nb, K_AUG, MOBA_BLOCK), BF16),
        jax.ShapeDtypeStruct((bsz, A_HEADS, s, K_AUG), BF16),
        jax.ShapeDtypeStruct((bsz, A_HEADS, nb, V_AUG, MOBA_BLOCK), BF16),
        jax.ShapeDtypeStruct((bsz, nt, nblk, A_WIDTH), F32),
        jax.ShapeDtypeStruct((bsz, A_HEADS, nb, HEAD_DIM, MOBA_BLOCK), BF16),
        jax.ShapeDtypeStruct((bsz, s, 2 * M_WIDTH), BF16),
        jax.ShapeDtypeStruct((bsz, s, G_KW), F32),
        jax.ShapeDtypeStruct((bsz, s, G_KW), F32),
        jax.ShapeDtypeStruct((bsz, s, G_VW), BF16),
        jax.ShapeDtypeStruct((bsz, s, G_KW), F32),
        jax.ShapeDtypeStruct((bsz, s, M_WIDTH), F32),
    ]
    return pl.pallas_call(
        _proj_kernel, grid=(bsz, nt), in_specs=in_specs, out_specs=out_specs, out_shape=out_shape,
        compiler_params=_params(("arbitrary", "arbitrary")), name="proj",
    )(x, g_pre, wn, wt, gk, gqcol, bones, e, qx, wgu, bgu)


def _sel_kernel(qT_ref, kmean_ref, o_ref):
    width = qT_ref.shape[1]
    gate = _dot(kmean_ref[...], qT_ref[...], precision=HIGHEST)
    nb = gate.shape[0]
    rowf = lax.broadcasted_iota(jnp.int32, gate.shape, 0).astype(F32)
    first = pl.program_id(2) * (width // MOBA_BLOCK)
    qblk = (first + lax.broadcasted_iota(jnp.int32, gate.shape, 1) // MOBA_BLOCK).astype(F32)
    g = jnp.where(rowf < qblk, gate, -jnp.inf)
    sel = jnp.zeros(gate.shape, F32)
    for _ in range(MOBA_TOPK):
        m = jnp.max(g, axis=0, keepdims=True)
        idx = jnp.min(jnp.where(g == m, rowf, float(nb)), axis=0, keepdims=True)
        hit = rowf == idx
        sel = jnp.where(hit, jnp.where(m > -jnp.inf, 1.0, sel), sel)
        g = jnp.where(hit, -jnp.inf, g)
    for u in range(width // MOBA_BLOCK):
        o_ref[u] = sel[:, u * MOBA_BLOCK:(u + 1) * MOBA_BLOCK]


def _sel_call(qT, kmean):
    bsz, _, s = qT.shape
    nb = s // MOBA_BLOCK
    width = min(SEL_WIDTH, s)
    return pl.pallas_call(
        _sel_kernel,
        grid=(bsz, A_HEADS, s // width),
        in_specs=[
            pl.BlockSpec((None, HEAD_DIM, width), lambda b, h, c: (b, h, c)),
            pl.BlockSpec((None, None, nb, HEAD_DIM), lambda b, h, c: (b, h, 0, 0)),
        ],
        out_specs=pl.BlockSpec((None, None, width // MOBA_BLOCK, nb, MOBA_BLOCK), lambda b, h, c: (b, h, c, 0, 0)),
        out_shape=jax.ShapeDtypeStruct((bsz, A_HEADS, nb, nb, MOBA_BLOCK), F32),
        compiler_params=_params(("arbitrary", "arbitrary", "arbitrary")),
        name="mobasel",
    )(qT, kmean)


def _moba_items(nb):
    qi, grp, first, valid = [], [], [], []
    for i in range(1, nb):
        for g in range((i - 1) // MOBA_GROUP + 1):
            qi.append(i); grp.append(g); first.append(int(g == 0)); valid.append(1)
    n_real = len(qi)
    n_trips = n_real + (n_real % 2)
    for _ in range(n_trips + 2 - n_real):
        qi.append(qi[-1]); grp.append(0); first.append(0); valid.append(0)
    tab = np.stack([qi, grp, first, valid]).astype(np.int32)
    return tab, n_trips


def _moba_kernel(tab_ref, c1_ref, qa_ref, kall_ref, vall_ref, sel_ref, sga_ref, o_ref,
                 m_scr, acc_scr, s0_ref, s1_ref, p0_ref, p1_ref, *, n_trips):
    hh = pl.program_id(1)
    c1 = c1_ref[hh]
    nb = qa_ref.shape[0]
    gk = MOBA_GROUP * MOBA_BLOCK
    half = gk // 2

    def finish(acc, i):
        o = acc[:HEAD_DIM] / acc[HEAD_DIM:HEAD_DIM + 1]
        o_ref[i] = (o * sga_ref[i].astype(F32)).astype(BF16)

    kk = lax.broadcasted_iota(jnp.int32, (MOBA_BLOCK, MOBA_BLOCK), 0)
    qq = lax.broadcasted_iota(jnp.int32, (MOBA_BLOCK, MOBA_BLOCK), 1)

    def own(i, carry):
        r0 = pl.multiple_of(i * MOBA_BLOCK, MOBA_BLOCK)
        s = _dot(kall_ref[pl.ds(r0, MOBA_BLOCK), :], qa_ref[i])
        s = jnp.where(kk <= qq, s, NEG_BIG)
        m0 = jnp.max(s, axis=0, keepdims=True)
        acc0 = _dot(vall_ref[i], jnp.exp2(s - m0).astype(BF16))
        m_scr[i] = jnp.broadcast_to(m0, m_scr.shape[1:])
        acc_scr[i] = acc0
        finish(acc0, i)
        return carry

    lax.fori_loop(0, nb, own, 0, unroll=2)

    def item(n):
        return tab_ref[0, n], tab_ref[1, n], tab_ref[2, n], tab_ref[3, n]

    def scores(n, s_ref):
        i, g, _, _ = item(n)
        for part in range(2):
            r0 = pl.multiple_of(g * gk + part * half, half)
            s_ref[part * half:(part + 1) * half, :] = _dot(kall_ref[pl.ds(r0, half), :], qa_ref[i])

    def softmax(n, s_ref, p_ref, m):
        i, g, first, valid = item(n)
        m = jnp.where(first == 1, m_scr[i][0:1], m)
        m_new = m
        members = []
        for u in range(MOBA_GROUP):
            j = g * MOBA_GROUP + u
            picked = jnp.where(valid == 1, sel_ref[i, pl.ds(j, 1), :], 0.0) > 0.5
            cj = c1 * (MOBA_BLOCK * (j - i)).astype(F32)
            m_blk = jnp.max(s_ref[u * MOBA_BLOCK:(u + 1) * MOBA_BLOCK, :], axis=0, keepdims=True) + cj
            m_new = jnp.where(picked, jnp.maximum(m_new, m_blk), m_new)
            members.append((picked, cj))
        for u, (picked, cj) in enumerate(members):
            shift = jnp.where(picked, m_new - cj, POS_BIG)
            rows = slice(u * MOBA_BLOCK, (u + 1) * MOBA_BLOCK)
            p_ref[rows, :] = jnp.exp2(s_ref[rows, :] - shift).astype(BF16)
        return m_new, jnp.exp2(m - m_new)

    def accumulate(n, p_ref, alpha, acc):
        i, g, first, _ = item(n)
        acc = jnp.where(first == 1, acc_scr[i], acc) * alpha
        for u in range(MOBA_GROUP):
            acc = acc + _dot(vall_ref[g * MOBA_GROUP + u], p_ref[u * MOBA_BLOCK:(u + 1) * MOBA_BLOCK, :])
        finish(acc, i)
        return acc

    scores(0, s0_ref)
    scores(1, s1_ref)
    m_init = jnp.full((1, MOBA_BLOCK), NEG_BIG, F32)
    m, alpha = softmax(0, s0_ref, p0_ref, m_init)
    acc = jnp.zeros((V_AUG, MOBA_BLOCK), F32)

    def body(t, carry):
        m, alpha, acc = carry
        n = 2 * t
        scores(n + 2, s0_ref)
        m, alpha_b = softmax(n + 1, s1_ref, p1_ref, m)
        acc = accumulate(n, p0_ref, alpha, acc)
        scores(n + 3, s1_ref)
        m, alpha_a = softmax(n + 2, s0_ref, p0_ref, m)
        acc = accumulate(n + 1, p1_ref, alpha_b, acc)
        return m, alpha_a, acc

    lax.fori_loop(0, n_trips // 2, body, (m, alpha, acc))


def _moba_call(c1, qa, kaug, vT, sel, sga):
    bsz, _, nb = qa.shape[:3]
    s = nb * MOBA_BLOCK
    gk = MOBA_GROUP * MOBA_BLOCK
    assert nb % MOBA_GROUP == 0 and nb >= 2
    tab, n_trips = _moba_items(nb)
    per_bh = lambda *tail: pl.BlockSpec((None, None) + tail, lambda b, h: (b, h) + (0,) * len(tail))
    return pl.pallas_call(
        functools.partial(_moba_kernel, n_trips=n_trips),
        grid=(bsz, A_HEADS),
        in_specs=[
            pl.BlockSpec(memory_space=pltpu.SMEM),
            pl.BlockSpec(memory_space=pltpu.SMEM),
            per_bh(nb, K_AUG, MOBA_BLOCK),
            per_bh(s, K_AUG),
            per_bh(nb, V_AUG, MOBA_BLOCK),
            per_bh(nb, nb, MOBA_BLOCK),
            per_bh(nb, HEAD_DIM, MOBA_BLOCK),
        ],
        out_specs=per_bh(nb, HEAD_DIM, MOBA_BLOCK),
        out_shape=jax.ShapeDtypeStruct((bsz, A_HEADS, nb, HEAD_DIM, MOBA_BLOCK), BF16),
        scratch_shapes=[pltpu.VMEM((nb, 8, MOBA_BLOCK), F32), pltpu.VMEM((nb, V_AUG, MOBA_BLOCK), F32),
                        pltpu.VMEM((gk, MOBA_BLOCK), F32), pltpu.VMEM((gk, MOBA_BLOCK), F32),
                        pltpu.VMEM((gk, MOBA_BLOCK), BF16), pltpu.VMEM((gk, MOBA_BLOCK), BF16)],
        compiler_params=pltpu.CompilerParams(dimension_semantics=("arbitrary", "arbitrary"),
                                             vmem_limit_bytes=MOBA_VMEM_LIMIT),
        name="moba",
    )(jnp.asarray(tab), c1, qa, kaug, vT, sel, sga)


def _gla_consts():
    c, sb = GLA_CHUNK, GLA_SUB
    r = np.arange(c)
    blk = r // sb
    tri = (r[None, :] <= r[:, None])
    same = blk[None, :] == blk[:, None]
    lall = np.concatenate([
        tri,
        tri & same,
        same,
        blk[None, :] == blk[:, None] - 1,
        blk[None, :] == blk[:, None] - 2,
    ], axis=0).astype(np.float32)
    diff = blk[:, None] - blk[None, :]
    band = np.stack([diff == 1, diff == 2, diff == 3, same & tri]).astype(np.float32)
    dk_head = np.arange(G_KW) // G_DK
    dv_head = np.arange(G_VW) // G_DV
    bdt = (dv_head[:, None] == dk_head[None, :]).astype(np.float32)
    return lall, band, bdt


def _gla_kernel(q_ref, k_ref, v_ref, g_ref, sgb_ref, lall_ref, band_ref, bdt_ref, bones_ref, gout_ref,
                o_ref, st_ref):
    c = GLA_CHUNK

    @pl.when(pl.program_id(1) == 0)
    def _():
        st_ref[...] = jnp.zeros_like(st_ref)

    lane_k = lax.broadcasted_iota(jnp.int32, (1, G_KW), 1) // G_DK
    lane_v = lax.broadcasted_iota(jnp.int32, (1, G_VW), 1) // G_DV
    n_chunks = q_ref.shape[0] // c

    def chunk(ci, carry):
        r0 = pl.multiple_of(ci * c, c)
        q = q_ref[pl.ds(r0, c), :]
        k = k_ref[pl.ds(r0, c), :]
        v = v_ref[pl.ds(r0, c), :]
        g = g_ref[pl.ds(r0, c), :]

        g3 = jnp.concatenate(_split3(g), axis=1)
        r = _dot(lall_ref[...], g3)
        r = r[:, :G_KW] + r[:, G_KW:2 * G_KW] + r[:, 2 * G_KW:]
        b, cc, tt, p1, p2 = (r[n * c:(n + 1) * c] for n in range(5))
        b_last = b[c - 1:c]

        qt = q * jnp.exp(cc)
        q2 = qt * jnp.exp(p1)
        q3 = q2 * jnp.exp(p2)
        kt = (k * jnp.exp(tt - cc)).astype(BF16)
        kd = (k * jnp.exp(-cc)).astype(BF16)
        qi = (q * jnp.exp(b)).astype(BF16)
        kl = (k * jnp.exp(b_last - b)).astype(BF16)

        def heads(t):
            return jnp.concatenate([jnp.where(lane_k == hd, t, 0.0) for hd in range(G_HEADS)],
                                   axis=0).astype(BF16)

        xs = [_dg(heads(t), kt, _NT) for t in (qt, q2, q3)]
        yd = _dg(heads(qt), kd, _NT)
        st = st_ref[...]
        o = _dg(qi, st.astype(BF16), _NT)
        for hd in range(G_HEADS):
            sl = slice(hd * c, (hd + 1) * c)
            att = (band_ref[0] * xs[0][sl] + band_ref[1] * xs[1][sl] + band_ref[2] * xs[2][sl]
                   + band_ref[3] * yd[sl])
            vh = jnp.where(lane_v == hd, v, jnp.zeros_like(v))
            o = o + _dot(att.astype(BF16), vh)
        st_ref[...] = st * jnp.exp(b_last) + _dg(v, kl, _TN) * bdt_ref[...]

        ss = _group_sumsq(o, bones_ref[...])
        y = o * lax.rsqrt(ss * (1.0 / G_DV) + EPS) * gout_ref[...]
        o_ref[pl.ds(r0, c), :] = (y * sgb_ref[pl.ds(r0, c), :].astype(F32)).astype(BF16)
        return carry

    lax.fori_loop(0, n_chunks, chunk, 0)


def _gla_call(gq, gk, gv, gg, sgbm, lall, band, bdt, bones, gout):
    bsz, s, _ = gq.shape
    tm = ROW_TILE
    row = lambda w: pl.BlockSpec((None, tm, w), lambda b, t: (b, t, 0))
    return pl.pallas_call(
        _gla_kernel,
        grid=(bsz, s // tm),
        in_specs=[row(G_KW), row(G_KW), row(G_VW), row(G_KW), row(G_VW),
                  _full(lall.shape), _full(band.shape), _full(bdt.shape), _full(bones.shape), _full((1, G_VW))],
        out_specs=row(G_VW),
        out_shape=jax.ShapeDtypeStruct((bsz, s, G_VW), BF16),
        scratch_shapes=[pltpu.VMEM((G_VW, G_KW), F32)],
        compiler_params=_params(("arbitrary", "arbitrary")),
        name="gla",
    )(gq, gk, gv, gg, sgbm, lall, band, bdt, bones, gout)


def _mem_kernel(qm_ref, sgm_ref, mk_ref, mvx_ref, gq_ref, bones_ref, o_ref):
    qm = qm_ref[...]
    ss = _group_sumsq(qm, bones_ref[...])
    qn = qm * lax.rsqrt(ss * (1.0 / HEAD_DIM) + EPS) * gq_ref[...]
    qs = qn * (LOG2E * HEAD_DIM ** -0.5)
    lane = lax.broadcasted_iota(jnp.int32, (1, M_WIDTH), 1) // HEAD_DIM
    mk = mk_ref[...]
    acc = jnp.zeros((qm.shape[0], 2 * M_WIDTH), F32)
    for h in range(M_HEADS):
        qh = jnp.where(lane == h, qs, 0.0).astype(BF16)
        s = _dg(qh, mk, _NT)
        m = jnp.max(s, axis=-1, keepdims=True)
        p = jnp.exp2(s - m).astype(BF16)
        acc = acc + _dot(p, mvx_ref[h])
    om = acc[:, :M_WIDTH] / acc[:, M_WIDTH:]
    o_ref[...] = (om * sgm_ref[...].astype(F32)).astype(BF16)


def _mem_call(qm, sgbm, mk, mvx, gq, bones):
    bsz, s, _ = qm.shape
    tm = ROW_TILE
    mlen = mk.shape[1]
    return pl.pallas_call(
        _mem_kernel,
        grid=(bsz, s // tm),
        in_specs=[
            pl.BlockSpec((None, tm, M_WIDTH), lambda b, t: (b, t, 0)),
            pl.BlockSpec((None, tm, M_WIDTH), lambda b, t: (b, t, 1)),
            pl.BlockSpec((None, mlen, M_WIDTH), lambda b, t: (b, 0, 0)),
            pl.BlockSpec((None, M_HEADS, mlen, 2 * M_WIDTH), lambda b, t: (b, 0, 0, 0)),
            _full((1, M_WIDTH)), _full((M_WIDTH, M_WIDTH)),
        ],
        out_specs=pl.BlockSpec((None, tm, M_WIDTH), lambda b, t: (b, t, 0)),
        out_shape=jax.ShapeDtypeStruct((bsz, s, M_WIDTH), BF16),
        compiler_params=_params(("arbitrary", "arbitrary")),
        name="memattn",
    )(qm, sgbm, mk, mvx, gq, bones)


def _out_kernel(x_ref, oaT_ref, ob_ref, om_ref, wa_ref, wb_ref, o_ref):
    y = _dot(jnp.concatenate([ob_ref[...], om_ref[...]], axis=-1), wb_ref[...])
    for t in range(oaT_ref.shape[1]):
        rows = slice(t * MOBA_BLOCK, (t + 1) * MOBA_BLOCK)
        oaT = oaT_ref[:, t].reshape(A_WIDTH, MOBA_BLOCK)
        o_ref[rows, :] = x_ref[rows, :] + y[rows] + _dg(oaT, wa_ref[...], _TN)


def _out_call(x, oaT, ob, om, wa, wb):
    bsz, s, d = x.shape
    tm = ROW_TILE
    row = lambda w: pl.BlockSpec((None, tm, w), lambda b, t: (b, t, 0))
    return pl.pallas_call(
        _out_kernel,
        grid=(bsz, s // tm),
        in_specs=[row(d),
                  pl.BlockSpec((None, A_HEADS, tm // MOBA_BLOCK, HEAD_DIM, MOBA_BLOCK), lambda b, t: (b, 0, t, 0, 0)),
                  row(G_VW), row(M_WIDTH),
                  _full(wa.shape), _full(wb.shape)],
        out_specs=row(d),
        out_shape=jax.ShapeDtypeStruct((bsz, s, d), F32),
        compiler_params=_params(("arbitrary", "arbitrary")),
        name="outproj",
    )(x, oaT, ob, om, wa, wb)


def _alibi_consts():
    slopes = np.asarray([2.0 ** (-8.0 * (i + 1) / A_HEADS) for i in range(A_HEADS)], np.float32)
    c1 = (slopes * np.float32(LOG2E)).astype(np.float32)
    c1j = jnp.asarray(c1)
    pieces = list(_split3(c1j * 16.0)) + list(_split3(c1j))
    qx = jnp.zeros((A_HEADS, K_AUG - HEAD_DIM, MOBA_BLOCK), BF16)
    for n, pc in enumerate(pieces):
        qx = qx.at[:, n, :].set(jnp.broadcast_to(pc[:, None], (A_HEADS, MOBA_BLOCK)))
    pos = np.arange(MOBA_BLOCK)
    e = np.zeros((MOBA_BLOCK, K_AUG - HEAD_DIM), np.float32)
    e[:, 0:3] = (pos // 16)[:, None]
    e[:, 3:6] = (pos % 16)[:, None]
    return c1j, qx, jnp.asarray(e)


def kernel(x, mem, g_pre, w_in, g_q_moba, g_k_moba, w_gate_up, b_gate_up, g_gla_out,
           g_mem, w_mem_kv, g_q_mem, g_k_mem, w_out):
    depth = g_pre.shape[0]
    d = x.shape[-1]
    c1, qx, e = _alibi_consts()
    lall_np, band_np, bdt_np = _gla_consts()
    lall = jnp.asarray(lall_np, BF16)
    band = jnp.asarray(band_np)
    bdt = jnp.asarray(bdt_np)
    bones_a = jnp.asarray(_block_ones(A_WIDTH, HEAD_DIM), BF16)
    bones_m = jnp.asarray(_block_ones(M_WIDTH, HEAD_DIM), BF16)

    o_qa, o_ka, o_va, o_ga = 0, 512, 1024, 1536
    o_qb, o_kb, o_vb, o_gb, o_rb = 2048, 2176, 2304, 2560, 2816
    o_qm, o_gm = 2832, 3088

    mk_all, mvx_all = _memkv_call(mem, g_mem, w_mem_kv, g_k_mem)

    for l in range(depth):
        w = w_in[l]
        col = lambda o, n: w[:, o:o + n]
        rb_pad = jnp.pad(col(o_rb, GATE_RANK), ((0, 0), (0, G_KW - GATE_RANK)))
        wn = jnp.concatenate([col(o_ka, 512), col(o_gb, 256), col(o_gm, 256), col(o_qb, 128), col(o_kb, 128),
                              col(o_vb, 256), rb_pad, col(o_qm, 256)], axis=1).astype(BF16)
        wt = jnp.concatenate([col(o_qa, 512), col(o_va, 512), col(o_ga, 512)], axis=1).T.astype(BF16)
        wgu = jnp.pad(w_gate_up[l], ((0, G_KW - GATE_RANK), (0, 0)))
        gk = jnp.tile(g_k_moba[l], A_HEADS).reshape(1, A_WIDTH)

        (qT, qa, kaug, vT, kmean_nat, sga, sgbm, gq, gk2, gv, gg, qm) = _proj_call(
            x, g_pre[l].reshape(1, d), wn, wt, gk, g_q_moba[l].reshape(HEAD_DIM, 1), bones_a, e, qx,
            wgu, b_gate_up[l].reshape(1, G_KW))

        bsz, s = x.shape[0], x.shape[1]
        nb = s // MOBA_BLOCK
        kmean = kmean_nat.reshape(bsz, nb, A_HEADS, HEAD_DIM).transpose(0, 2, 1, 3)
        oaT = _moba_call(c1, qa, kaug, vT, _sel_call(qT, kmean), sga)
        ob = _gla_call(gq, gk2, gv, gg, sgbm, lall, band, bdt, bones_m,
                       jnp.tile(g_gla_out[l], G_HEADS).reshape(1, G_VW))
        om = _mem_call(qm, sgbm, mk_all[l], mvx_all[l],
                       jnp.tile(g_q_mem[l], M_HEADS).reshape(1, M_WIDTH), bones_m)
        wo = w_out[l].astype(BF16)
        x = _out_call(x, oaT, ob, om, wo[:A_WIDTH], wo[A_WIDTH:])
    return x
```

```python
import functools

import numpy as np
import jax
import jax.numpy as jnp
from jax import lax
from jax.experimental import pallas as pl
from jax.experimental.pallas import tpu as pltpu

F32 = jnp.float32
BF16 = jnp.bfloat16
HIGHEST = lax.Precision.HIGHEST

EPS = 1e-6
LOG2E = 1.4426950408889634
HEAD_DIM = 64
A_HEADS = 8
A_WIDTH = A_HEADS * HEAD_DIM
MOBA_BLOCK = 256
MOBA_TOPK = 3
MOBA_GROUP = 4
MOBA_UNROLL = 4
SEL_WIDTH = 2048
G_HEADS = 4
G_DK = 32
G_DV = 64
G_KW = G_HEADS * G_DK
G_VW = G_HEADS * G_DV
GATE_RANK = 16
GATE_TEMP = 16.0
GLA_CHUNK = 64
GLA_SUB = 16
M_HEADS = 4
M_WIDTH = M_HEADS * HEAD_DIM
K_AUG = 128
V_AUG = 80
ROW_TILE = 512
VMEM_LIMIT = 48 * 1024 * 1024
MOBA_VMEM_LIMIT = 56 * 1024 * 1024

NEG_BIG = -1e30
POS_BIG = 1e30

_NT = (((1,), (1,)), ((), ()))
_TN = (((0,), (0,)), ((), ()))


def _dot(a, b, **kw):
    return jnp.dot(a, b, preferred_element_type=F32, **kw)


def _dg(a, b, dims, **kw):
    return lax.dot_general(a, b, dims, preferred_element_type=F32, **kw)


def _split2(v):
    hi = v.astype(BF16)
    lo = (v - hi.astype(F32)).astype(BF16)
    return hi, lo


def _split3(v):
    hi = v.astype(BF16)
    r = v - hi.astype(F32)
    mid = r.astype(BF16)
    lo = (r - mid.astype(F32)).astype(BF16)
    return hi, mid, lo


def _group_sumsq(v, bones):
    hi, lo = _split2(v * v)
    return _dot(hi, bones) + _dot(lo, bones)


def _silu(v):
    return v / (1.0 + jnp.exp(-v))


def _block_ones(n, g):
    i = np.arange(n) // g
    return (i[:, None] == i[None, :]).astype(np.float32)


def _params(sem):
    return pltpu.CompilerParams(dimension_semantics=sem, vmem_limit_bytes=VMEM_LIMIT)


def _full(shape):
    return pl.BlockSpec(shape, lambda *_: (0,) * len(shape))


def _memkv_kernel(mem_ref, gmem_ref, w_ref, gk_ref, bones_ref, mk_ref, mvx_ref):
    m = mem_ref[...]
    ms = jnp.mean(m * m, axis=-1, keepdims=True)
    hn = (m * lax.rsqrt(ms + EPS) * gmem_ref[...]).astype(BF16)
    kv = _dot(hn, w_ref[...])
    mk = kv[:, :M_WIDTH]
    mv = kv[:, M_WIDTH:]
    ss = _group_sumsq(mk, bones_ref[...])
    mk_ref[...] = (mk * lax.rsqrt(ss * (1.0 / HEAD_DIM) + EPS) * gk_ref[...]).astype(BF16)
    lane = lax.broadcasted_iota(jnp.int32, mv.shape, 1) // HEAD_DIM
    for h in range(M_HEADS):
        hm = lane == h
        mvx_ref[h] = jnp.concatenate(
            [jnp.where(hm, mv, 0.0), jnp.where(hm, 1.0, 0.0)], axis=-1).astype(BF16)


def _memkv_call(mem, g_mem, w_mem_kv, g_k_mem):
    depth = g_mem.shape[0]
    bsz, mlen, d = mem.shape
    bones = jnp.asarray(_block_ones(M_WIDTH, HEAD_DIM), BF16)
    gk = jnp.tile(g_k_mem, (1, M_HEADS)).reshape(depth, 1, M_WIDTH)
    return pl.pallas_call(
        _memkv_kernel,
        grid=(depth, bsz),
        in_specs=[
            pl.BlockSpec((None, mlen, d), lambda l, b: (b, 0, 0)),
            pl.BlockSpec((None, 1, d), lambda l, b: (l, 0, 0)),
            pl.BlockSpec((None, d, 2 * M_WIDTH), lambda l, b: (l, 0, 0)),
            pl.BlockSpec((None, 1, M_WIDTH), lambda l, b: (l, 0, 0)),
            _full((M_WIDTH, M_WIDTH)),
        ],
        out_specs=[
            pl.BlockSpec((None, None, mlen, M_WIDTH), lambda l, b: (l, b, 0, 0)),
            pl.BlockSpec((None, None, M_HEADS, mlen, 2 * M_WIDTH), lambda l, b: (l, b, 0, 0, 0)),
        ],
        out_shape=[
            jax.ShapeDtypeStruct((depth, bsz, mlen, M_WIDTH), BF16),
            jax.ShapeDtypeStruct((depth, bsz, M_HEADS, mlen, 2 * M_WIDTH), BF16),
        ],
        compiler_params=_params(("arbitrary", "arbitrary")),
        name="memkv",
    )(mem, g_mem.reshape(depth, 1, d), w_mem_kv.astype(BF16), gk, bones)


_N_K = (0, 512)
_N_GBM = (512, 1024)
_N_QB = (1024, 1152)
_N_KB = (1152, 1280)
_N_VB = (1280, 1536)
_N_RB = (1536, 1664)
_N_QM = (1664, 1920)


def _proj_kernel(x_ref, gpre_ref, wn_ref, wt_ref, gk_ref, gqcol_ref, bones_ref, e_ref, qx_ref, wgu_ref, bgu_ref,
                 qT_ref, qa_ref, kaug_ref, vT_ref, kmean_ref, sga_ref, sgbm_ref, gq_ref, gk2_ref, gv_ref, gg_ref,
                 qm_ref):
    tm = x_ref.shape[0]
    nblk = tm // MOBA_BLOCK
    x = x_ref[...]
    ms = jnp.mean(x * x, axis=-1, keepdims=True)
    h = (x * lax.rsqrt(ms + EPS) * gpre_ref[...]).astype(BF16)

    def nat(cols):
        return _dot(h, wn_ref[:, cols[0]:cols[1]])

    k = nat(_N_K)
    ss = _group_sumsq(k, bones_ref[...])
    kn = k * lax.rsqrt(ss * (1.0 / HEAD_DIM) + EPS) * gk_ref[...]
    kmean_ref[...] = jnp.mean(kn.reshape(nblk, MOBA_BLOCK, A_WIDTH), axis=1)
    e = jnp.concatenate([e_ref[...]] * nblk, axis=0)
    for hh in range(A_HEADS):
        rows = kn[:, hh * HEAD_DIM:(hh + 1) * HEAD_DIM]
        kaug_ref[hh] = jnp.concatenate([rows, e], axis=-1).astype(BF16)

    qT = _dg(wt_ref[0:A_WIDTH, :], h, _NT)
    q3 = qT.reshape(A_HEADS, HEAD_DIM, tm)
    msq = jnp.mean(q3 * q3, axis=1, keepdims=True)
    qn = q3 * lax.rsqrt(msq + EPS) * gqcol_ref[...].reshape(1, HEAD_DIM, 1)
    qT_ref[...] = qn.reshape(A_WIDTH, tm)
    qs = (qn * (LOG2E * HEAD_DIM ** -0.5)).astype(BF16)
    blocks = [slice(t * MOBA_BLOCK, (t + 1) * MOBA_BLOCK) for t in range(nblk)]
    for hh in range(A_HEADS):
        for t, cols in enumerate(blocks):
            qa_ref[hh, t] = jnp.concatenate([qs[hh][:, cols], qx_ref[hh]], axis=0)

    vT = _dg(wt_ref[A_WIDTH:2 * A_WIDTH, :], h, _NT)
    ones_rows = jnp.where(lax.broadcasted_iota(jnp.int32, (V_AUG - HEAD_DIM, tm), 0) == 0, 1.0, 0.0)
    sga = _silu(_dg(wt_ref[2 * A_WIDTH:3 * A_WIDTH, :], h, _NT)).astype(BF16)
    for hh in range(A_HEADS):
        rows = slice(hh * HEAD_DIM, (hh + 1) * HEAD_DIM)
        vT_ref[hh] = jnp.concatenate([vT[rows], ones_rows], axis=0).astype(BF16)
        for t, cols in enumerate(blocks):
            sga_ref[hh, t] = sga[rows, cols]
    sgbm_ref[...] = _silu(nat(_N_GBM)).astype(BF16)

    gq_ref[...] = nat(_N_QB) * (G_DK ** -0.5)
    gk2_ref[...] = nat(_N_KB)
    gv_ref[...] = nat(_N_VB).astype(BF16)
    rb = nat(_N_RB)
    z = _dot(rb, wgu_ref[...], precision=HIGHEST) + bgu_ref[...]
    gg_ref[...] = (jnp.minimum(z, 0.0) - jnp.log(1.0 + jnp.exp(-jnp.abs(z)))) * (1.0 / GATE_TEMP)

    qm_ref[...] = nat(_N_QM)


def _proj_call(x, g_pre, wn, wt, gk, gqcol, bones, e, qx, wgu, bgu):
    bsz, s, d = x.shape
    tm = ROW_TILE
    nt = s // tm
    nblk = tm // MOBA_BLOCK
    nb = s // MOBA_BLOCK
    row = lambda w: pl.BlockSpec((None, tm, w), lambda b, t: (b, t, 0))
    in_specs = [
        row(d), _full((1, d)), _full(wn.shape), _full(wt.shape), _full((1, A_WIDTH)),
        _full((HEAD_DIM, 1)), _full((A_WIDTH, A_WIDTH)), _full(e.shape), _full(qx.shape), _full(wgu.shape),
        _full((1, G_KW)),
    ]
    group = MOBA_GROUP * MOBA_BLOCK
    tpg = group // tm
    blocked = lambda r: pl.BlockSpec((None, A_HEADS, nblk, r, MOBA_BLOCK), lambda b, t: (b, 0, t, 0, 0))
    out_specs = [
        pl.BlockSpec((None, A_WIDTH, tm), lambda b, t: (b, 0, t)),
        blocked(K_AUG),
        pl.BlockSpec((None, A_HEADS, tm, K_AUG), lambda b, t: (b, 0, t, 0)),
        pl.BlockSpec((None, A_HEADS, None, V_AUG, tm), lambda b, t: (b, 0, t // tpg, 0, t % tpg)),
        pl.BlockSpec((None, None, nblk, A_WIDTH), lambda b, t: (b, t, 0, 0)),
        blocked(HEAD_DIM),
        row(2 * M_WIDTH), row(G_KW), row(G_KW), row(G_VW), row(G_KW), row(M_WIDTH),
    ]
    out_shape = [
        jax.ShapeDtypeStruct((bsz, A_WIDTH, s), F32),
        jax.ShapeDtypeStruct((bsz, A_HEADS, nb, K_AUG, MOBA_BLOCK), BF16),
        jax.ShapeDtypeStruct((bsz, A_HEADS, s, K_AUG), BF16),
        jax.ShapeDtypeStruct((bsz, A_HEADS, s // group, V_AUG, group), BF16),
        jax.ShapeDtypeStruct((bsz, nt, nblk, A_WIDTH), F32),
        jax.ShapeDtypeStruct((bsz, A_HEADS, nb, HEAD_DIM, MOBA_BLOCK), BF16),
        jax.ShapeDtypeStruct((bsz, s, 2 * M_WIDTH), BF16),
        jax.ShapeDtypeStruct((bsz, s, G_KW), F32),
        jax.ShapeDtypeStruct((bsz, s, G_KW), F32),
        jax.ShapeDtypeStruct((bsz, s, G_VW), BF16),
        jax.ShapeDtypeStruct((bsz, s, G_KW), F32),
        jax.ShapeDtypeStruct((bsz, s, M_WIDTH), F32),
    ]
    return pl.pallas_call(
        _proj_kernel, grid=(bsz, nt), in_specs=in_specs, out_specs=out_specs, out_shape=out_shape,
        compiler_params=_params(("arbitrary", "arbitrary")), name="proj",
    )(x, g_pre, wn, wt, gk, gqcol, bones, e, qx, wgu, bgu)


def _sel_kernel(qT_ref, kmean_ref, o_ref):
    width = qT_ref.shape[1]
    gate = _dot(kmean_ref[...], qT_ref[...], precision=HIGHEST)
    nb = gate.shape[0]
    rowf = lax.broadcasted_iota(jnp.int32, gate.shape, 0).astype(F32)
    first = pl.program_id(2) * (width // MOBA_BLOCK)
    qblk = (first + lax.broadcasted_iota(jnp.int32, gate.shape, 1) // MOBA_BLOCK).astype(F32)
    g = jnp.where(rowf < qblk, gate, -jnp.inf)
    sel = jnp.zeros(gate.shape, F32)
    for _ in range(MOBA_TOPK):
        m = jnp.max(g, axis=0, keepdims=True)
        idx = jnp.min(jnp.where(g == m, rowf, float(nb)), axis=0, keepdims=True)
        hit = rowf == idx
        sel = jnp.where(hit, jnp.where(m > -jnp.inf, 1.0, sel), sel)
        g = jnp.where(hit, -jnp.inf, g)
    for u in range(width // MOBA_BLOCK):
        o_ref[u] = sel[:, u * MOBA_BLOCK:(u + 1) * MOBA_BLOCK]


def _sel_call(qT, kmean):
    bsz, _, s = qT.shape
    nb = s // MOBA_BLOCK
    width = min(SEL_WIDTH, s)
    return pl.pallas_call(
        _sel_kernel,
        grid=(bsz, A_HEADS, s // width),
        in_specs=[
            pl.BlockSpec((None, HEAD_DIM, width), lambda b, h, c: (b, h, c)),
            pl.BlockSpec((None, None, nb, HEAD_DIM), lambda b, h, c: (b, h, 0, 0)),
        ],
        out_specs=pl.BlockSpec((None, None, width // MOBA_BLOCK, nb, MOBA_BLOCK), lambda b, h, c: (b, h, c, 0, 0)),
        out_shape=jax.ShapeDtypeStruct((bsz, A_HEADS, nb, nb, MOBA_BLOCK), F32),
        compiler_params=_params(("arbitrary", "arbitrary", "arbitrary")),
        name="mobasel",
    )(qT, kmean)


def _moba_items(nb):
    qi, grp, first, valid = [], [], [], []
    for i in range(1, nb):
        for g in range((i - 1) // MOBA_GROUP + 1):
            qi.append(i); grp.append(g); first.append(int(g == 0)); valid.append(1)
    n_items = -(-len(qi) // MOBA_UNROLL) * MOBA_UNROLL
    for _ in range(n_items + 2 - len(qi)):
        qi.append(qi[-1]); grp.append(0); first.append(0); valid.append(0)
    tab = np.stack([qi, grp, first, valid]).astype(np.int32)
    return tab, n_items


def _moba_kernel(tab_ref, c1_ref, qa_ref, kall_ref, vall_ref, sel_ref, sga_ref, o_ref,
                 m_scr, acc_scr, s0_ref, s1_ref, mb0_ref, mb1_ref, p0_ref, p1_ref, *, n_items):
    hh = pl.program_id(1)
    c1 = c1_ref[hh]
    nb = qa_ref.shape[0]
    gk = MOBA_GROUP * MOBA_BLOCK
    half = gk // 2

    kk = lax.broadcasted_iota(jnp.int32, (MOBA_BLOCK, MOBA_BLOCK), 0)
    qq = lax.broadcasted_iota(jnp.int32, (MOBA_BLOCK, MOBA_BLOCK), 1)

    def own_scores(i, s_ref, mb_ref):
        i = jnp.minimum(i, nb - 1)
        r0 = pl.multiple_of(i * MOBA_BLOCK, MOBA_BLOCK)
        s = _dot(kall_ref[pl.ds(r0, MOBA_BLOCK), :], qa_ref[i])
        s = jnp.where(kk <= qq, s, NEG_BIG)
        s_ref[0:MOBA_BLOCK, :] = s
        mb_ref[0] = jnp.max(s, axis=0, keepdims=True)

    def own_softmax(i, s_ref, mb_ref, p_ref):
        i = jnp.minimum(i, nb - 1)
        m0 = mb_ref[0]
        m_scr[i] = jnp.broadcast_to(m0, m_scr.shape[1:])
        p_ref[0:MOBA_BLOCK, :] = jnp.exp2(s_ref[0:MOBA_BLOCK, :] - m0).astype(BF16)

    def own_pv(t, k, p_ref):
        vown = vall_ref[t][:, k * MOBA_BLOCK:(k + 1) * MOBA_BLOCK]
        acc_scr[MOBA_GROUP * t + k] = _dot(vown, p_ref[0:MOBA_BLOCK, :])

    bufs = ((s0_ref, mb0_ref, p0_ref), (s1_ref, mb1_ref, p1_ref))
    own_scores(0, s0_ref, mb0_ref)
    own_scores(1, s1_ref, mb1_ref)
    own_softmax(0, s0_ref, mb0_ref, p0_ref)

    def own_body(t, carry):
        for k in range(MOBA_GROUP):
            i = MOBA_GROUP * t + k
            s_a, mb_a, p_a = bufs[k % 2]
            s_b, mb_b, p_b = bufs[(k + 1) % 2]
            own_scores(i + 2, s_a, mb_a)
            own_softmax(i + 1, s_b, mb_b, p_b)
            own_pv(t, k, p_a)
        return carry

    lax.fori_loop(0, nb // MOBA_GROUP, own_body, 0)

    def item(n):
        return tab_ref[0, n], tab_ref[1, n], tab_ref[2, n], tab_ref[3, n]

    def scores(n, s_ref, mb_ref):
        i, g, _, _ = item(n)
        for part in range(2):
            r0 = pl.multiple_of(g * gk + part * half, half)
            s = _dot(kall_ref[pl.ds(r0, half), :], qa_ref[i])
            s_ref[part * half:(part + 1) * half, :] = s
            for u in range(MOBA_GROUP // 2):
                blk = s[u * MOBA_BLOCK:(u + 1) * MOBA_BLOCK]
                mb_ref[part * (MOBA_GROUP // 2) + u] = jnp.max(blk, axis=0, keepdims=True)

    def softmax(n, s_ref, mb_ref, p_ref, m):
        i, g, first, valid = item(n)
        m = jnp.where(first == 1, m_scr[i][0:1], m)
        m_new = m
        members = []
        for u in range(MOBA_GROUP):
            j = g * MOBA_GROUP + u
            picked = jnp.where(valid == 1, sel_ref[i, pl.ds(j, 1), :], 0.0) > 0.5
            cj = c1 * (MOBA_BLOCK * (j - i)).astype(F32)
            m_new = jnp.where(picked, jnp.maximum(m_new, mb_ref[u] + cj), m_new)
            members.append((picked, cj))
        for u, (picked, cj) in enumerate(members):
            shift = jnp.where(picked, m_new - cj, POS_BIG)
            rows = slice(u * MOBA_BLOCK, (u + 1) * MOBA_BLOCK)
            p_ref[rows, :] = jnp.exp2(s_ref[rows, :] - shift).astype(BF16)
        return m_new, jnp.exp2(m - m_new)

    def accumulate(n, p_ref, alpha, acc):
        i, g, first, _ = item(n)
        acc = jnp.where(first == 1, acc_scr[i], acc) * alpha + _dot(vall_ref[g], p_ref[...])
        acc_scr[i] = acc
        return acc

    scores(0, s0_ref, mb0_ref)
    scores(1, s1_ref, mb1_ref)
    m_init = jnp.full((1, MOBA_BLOCK), NEG_BIG, F32)
    m, alpha = softmax(0, s0_ref, mb0_ref, p0_ref, m_init)
    acc = jnp.zeros((V_AUG, MOBA_BLOCK), F32)

    def body(t, carry):
        m, alpha, acc = carry
        for k in range(MOBA_UNROLL):
            n = MOBA_UNROLL * t + k
            s_a, mb_a, p_a = bufs[k % 2]
            s_b, mb_b, p_b = bufs[(k + 1) % 2]
            acc = accumulate(n, p_a, alpha, acc)
            m, alpha = softmax(n + 1, s_b, mb_b, p_b, m)
            scores(n + 2, s_a, mb_a)
        return m, alpha, acc

    lax.fori_loop(0, n_items // MOBA_UNROLL, body, (m, alpha, acc))

    def finish(i, carry):
        acc = acc_scr[i]
        o = acc[:HEAD_DIM] / acc[HEAD_DIM:HEAD_DIM + 1]
        o_ref[i] = (o * sga_ref[i].astype(F32)).astype(BF16)
        return carry

    lax.fori_loop(0, nb, finish, 0, unroll=4)


def _moba_call(c1, qa, kaug, vT, sel, sga):
    bsz, _, nb = qa.shape[:3]
    s = nb * MOBA_BLOCK
    gk = MOBA_GROUP * MOBA_BLOCK
    assert nb % MOBA_GROUP == 0 and MOBA_GROUP % 2 == 0 and MOBA_UNROLL % 2 == 0
    tab, n_items = _moba_items(nb)
    per_bh = lambda *tail: pl.BlockSpec((None, None) + tail, lambda b, h: (b, h) + (0,) * len(tail))
    return pl.pallas_call(
        functools.partial(_moba_kernel, n_items=n_items),
        grid=(bsz, A_HEADS),
        in_specs=[
            pl.BlockSpec(memory_space=pltpu.SMEM),
            pl.BlockSpec(memory_space=pltpu.SMEM),
            per_bh(nb, K_AUG, MOBA_BLOCK),
            per_bh(s, K_AUG),
            per_bh(nb // MOBA_GROUP, V_AUG, gk),
            per_bh(nb, nb, MOBA_BLOCK),
            per_bh(nb, HEAD_DIM, MOBA_BLOCK),
        ],
        out_specs=per_bh(nb, HEAD_DIM, MOBA_BLOCK),
        out_shape=jax.ShapeDtypeStruct((bsz, A_HEADS, nb, HEAD_DIM, MOBA_BLOCK), BF16),
        scratch_shapes=[pltpu.VMEM((nb, 8, MOBA_BLOCK), F32), pltpu.VMEM((nb, V_AUG, MOBA_BLOCK), F32),
                        pltpu.VMEM((gk, MOBA_BLOCK), F32), pltpu.VMEM((gk, MOBA_BLOCK), F32),
                        pltpu.VMEM((MOBA_GROUP, 1, MOBA_BLOCK), F32), pltpu.VMEM((MOBA_GROUP, 1, MOBA_BLOCK), F32),
                        pltpu.VMEM((gk, MOBA_BLOCK), BF16), pltpu.VMEM((gk, MOBA_BLOCK), BF16)],
        compiler_params=pltpu.CompilerParams(dimension_semantics=("arbitrary", "arbitrary"),
                                             vmem_limit_bytes=MOBA_VMEM_LIMIT),
        name="moba",
    )(jnp.asarray(tab), c1, qa, kaug, vT, sel, sga)


def _gla_consts():
    c, sb = GLA_CHUNK, GLA_SUB
    r = np.arange(c)
    blk = r // sb
    tri = (r[None, :] <= r[:, None])
    same = blk[None, :] == blk[:, None]
    lall = np.concatenate([
        tri,
        tri & same,
        same,
        blk[None, :] == blk[:, None] - 1,
        blk[None, :] == blk[:, None] - 2,
    ], axis=0).astype(np.float32)
    diff = blk[:, None] - blk[None, :]
    band = np.stack([diff == 1, diff == 2, diff == 3, same & tri]).astype(np.float32)
    dk_head = np.arange(G_KW) // G_DK
    dv_head = np.arange(G_VW) // G_DV
    bdt = (dv_head[:, None] == dk_head[None, :]).astype(np.float32)
    return lall, band, bdt


def _gla_kernel(q_ref, k_ref, v_ref, g_ref, sgb_ref, lall_ref, band_ref, bdt_ref, bones_ref, gout_ref,
                o_ref, st_ref):
    c = GLA_CHUNK

    @pl.when(pl.program_id(1) == 0)
    def _():
        st_ref[...] = jnp.zeros_like(st_ref)

    lane_k = lax.broadcasted_iota(jnp.int32, (1, G_KW), 1) // G_DK
    lane_v = lax.broadcasted_iota(jnp.int32, (1, G_VW), 1) // G_DV
    n_chunks = q_ref.shape[0] // c

    def chunk(ci, st):
        r0 = pl.multiple_of(ci * c, c)
        q = q_ref[pl.ds(r0, c), :]
        k = k_ref[pl.ds(r0, c), :]
        v = v_ref[pl.ds(r0, c), :]
        g = g_ref[pl.ds(r0, c), :]

        g3 = jnp.concatenate(_split3(g), axis=1)
        r = _dot(lall_ref[...], g3)
        r = r[:, :G_KW] + r[:, G_KW:2 * G_KW] + r[:, 2 * G_KW:]
        b, cc, tt, p1, p2 = (r[n * c:(n + 1) * c] for n in range(5))
        b_last = b[c - 1:c]

        qt = q * jnp.exp(cc)
        q2 = qt * jnp.exp(p1)
        q3 = q2 * jnp.exp(p2)
        kt = (k * jnp.exp(tt - cc)).astype(BF16)
        kd = (k * jnp.exp(-cc)).astype(BF16)
        qi = (q * jnp.exp(b)).astype(BF16)
        kl = (k * jnp.exp(b_last - b)).astype(BF16)

        def heads(t):
            return jnp.concatenate([jnp.where(lane_k == hd, t, 0.0) for hd in range(G_HEADS)],
                                   axis=0).astype(BF16)

        xs = [_dg(heads(t), kt, _NT) for t in (qt, q2, q3)]
        yd = _dg(heads(qt), kd, _NT)
        o = _dg(qi, st.astype(BF16), _NT)
        for hd in range(G_HEADS):
            sl = slice(hd * c, (hd + 1) * c)
            att = (band_ref[0] * xs[0][sl] + band_ref[1] * xs[1][sl] + band_ref[2] * xs[2][sl]
                   + band_ref[3] * yd[sl])
            vh = jnp.where(lane_v == hd, v, jnp.zeros_like(v))
            o = o + _dot(att.astype(BF16), vh)
        st_new = st * jnp.exp(b_last) + _dg(v, kl, _TN) * bdt_ref[...]

        ss = _group_sumsq(o, bones_ref[...])
        y = o * lax.rsqrt(ss * (1.0 / G_DV) + EPS) * gout_ref[...]
        o_ref[pl.ds(r0, c), :] = (y * sgb_ref[pl.ds(r0, c), :].astype(F32)).astype(BF16)
        return st_new

    st_ref[...] = lax.fori_loop(0, n_chunks, chunk, st_ref[...], unroll=2)


def _gla_call(gq, gk, gv, gg, sgbm, lall, band, bdt, bones, gout):
    bsz, s, _ = gq.shape
    tm = ROW_TILE
    row = lambda w: pl.BlockSpec((None, tm, w), lambda b, t: (b, t, 0))
    return pl.pallas_call(
        _gla_kernel,
        grid=(bsz, s // tm),
        in_specs=[row(G_KW), row(G_KW), row(G_VW), row(G_KW), row(G_VW),
                  _full(lall.shape), _full(band.shape), _full(bdt.shape), _full(bones.shape), _full((1, G_VW))],
        out_specs=row(G_VW),
        out_shape=jax.ShapeDtypeStruct((bsz, s, G_VW), BF16),
        scratch_shapes=[pltpu.VMEM((G_VW, G_KW), F32)],
        compiler_params=_params(("arbitrary", "arbitrary")),
        name="gla",
    )(gq, gk, gv, gg, sgbm, lall, band, bdt, bones, gout)


def _mem_kernel(qm_ref, sgm_ref, mk_ref, mvx_ref, gq_ref, bones_ref, o_ref):
    qm = qm_ref[...]
    ss = _group_sumsq(qm, bones_ref[...])
    qn = qm * lax.rsqrt(ss * (1.0 / HEAD_DIM) + EPS) * gq_ref[...]
    qs = qn * (LOG2E * HEAD_DIM ** -0.5)
    lane = lax.broadcasted_iota(jnp.int32, (1, M_WIDTH), 1) // HEAD_DIM
    mk = mk_ref[...]
    acc = jnp.zeros((qm.shape[0], 2 * M_WIDTH), F32)
    for h in range(M_HEADS):
        qh = jnp.where(lane == h, qs, 0.0).astype(BF16)
        s = _dg(qh, mk, _NT)
        m = jnp.max(s, axis=-1, keepdims=True)
        p = jnp.exp2(s - m).astype(BF16)
        acc = acc + _dot(p, mvx_ref[h])
    om = acc[:, :M_WIDTH] / acc[:, M_WIDTH:]
    o_ref[...] = (om * sgm_ref[...].astype(F32)).astype(BF16)


def _mem_call(qm, sgbm, mk, mvx, gq, bones):
    bsz, s, _ = qm.shape
    tm = ROW_TILE
    mlen = mk.shape[1]
    return pl.pallas_call(
        _mem_kernel,
        grid=(bsz, s // tm),
        in_specs=[
            pl.BlockSpec((None, tm, M_WIDTH), lambda b, t: (b, t, 0)),
            pl.BlockSpec((None, tm, M_WIDTH), lambda b, t: (b, t, 1)),
            pl.BlockSpec((None, mlen, M_WIDTH), lambda b, t: (b, 0, 0)),
            pl.BlockSpec((None, M_HEADS, mlen, 2 * M_WIDTH), lambda b, t: (b, 0, 0, 0)),
            _full((1, M_WIDTH)), _full((M_WIDTH, M_WIDTH)),
        ],
        out_specs=pl.BlockSpec((None, tm, M_WIDTH), lambda b, t: (b, t, 0)),
        out_shape=jax.ShapeDtypeStruct((bsz, s, M_WIDTH), BF16),
        compiler_params=_params(("arbitrary", "arbitrary")),
        name="memattn",
    )(qm, sgbm, mk, mvx, gq, bones)


def _out_kernel(x_ref, oaT_ref, ob_ref, om_ref, wa_ref, wb_ref, o_ref):
    y = _dot(jnp.concatenate([ob_ref[...], om_ref[...]], axis=-1), wb_ref[...])
    for t in range(oaT_ref.shape[1]):
        rows = slice(t * MOBA_BLOCK, (t + 1) * MOBA_BLOCK)
        oaT = oaT_ref[:, t].reshape(A_WIDTH, MOBA_BLOCK)
        o_ref[rows, :] = x_ref[rows, :] + y[rows] + _dg(oaT, wa_ref[...], _TN)


def _out_call(x, oaT, ob, om, wa, wb):
    bsz, s, d = x.shape
    tm = ROW_TILE
    row = lambda w: pl.BlockSpec((None, tm, w), lambda b, t: (b, t, 0))
    return pl.pallas_call(
        _out_kernel,
        grid=(bsz, s // tm),
        in_specs=[row(d),
                  pl.BlockSpec((None, A_HEADS, tm // MOBA_BLOCK, HEAD_DIM, MOBA_BLOCK), lambda b, t: (b, 0, t, 0, 0)),
                  row(G_VW), row(M_WIDTH),
                  _full(wa.shape), _full(wb.shape)],
        out_specs=row(d),
        out_shape=jax.ShapeDtypeStruct((bsz, s, d), F32),
        compiler_params=_params(("arbitrary", "arbitrary")),
        name="outproj",
    )(x, oaT, ob, om, wa, wb)


def _alibi_consts():
    slopes = np.asarray([2.0 ** (-8.0 * (i + 1) / A_HEADS) for i in range(A_HEADS)], np.float32)
    c1 = (slopes * np.float32(LOG2E)).astype(np.float32)
    c1j = jnp.asarray(c1)
    pieces = list(_split3(c1j * 16.0)) + list(_split3(c1j))
    qx = jnp.zeros((A_HEADS, K_AUG - HEAD_DIM, MOBA_BLOCK), BF16)
    for n, pc in enumerate(pieces):
        qx = qx.at[:, n, :].set(jnp.broadcast_to(pc[:, None], (A_HEADS, MOBA_BLOCK)))
    pos = np.arange(MOBA_BLOCK)
    e = np.zeros((MOBA_BLOCK, K_AUG - HEAD_DIM), np.float32)
    e[:, 0:3] = (pos // 16)[:, None]
    e[:, 3:6] = (pos % 16)[:, None]
    return c1j, qx, jnp.asarray(e)


def kernel(x, mem, g_pre, w_in, g_q_moba, g_k_moba, w_gate_up, b_gate_up, g_gla_out,
           g_mem, w_mem_kv, g_q_mem, g_k_mem, w_out):
    depth = g_pre.shape[0]
    d = x.shape[-1]
    c1, qx, e = _alibi_consts()
    lall_np, band_np, bdt_np = _gla_consts()
    lall = jnp.asarray(lall_np, BF16)
    band = jnp.asarray(band_np)
    bdt = jnp.asarray(bdt_np)
    bones_a = jnp.asarray(_block_ones(A_WIDTH, HEAD_DIM), BF16)
    bones_m = jnp.asarray(_block_ones(M_WIDTH, HEAD_DIM), BF16)

    o_qa, o_ka, o_va, o_ga = 0, 512, 1024, 1536
    o_qb, o_kb, o_vb, o_gb, o_rb = 2048, 2176, 2304, 2560, 2816
    o_qm, o_gm = 2832, 3088

    mk_all, mvx_all = _memkv_call(mem, g_mem, w_mem_kv, g_k_mem)

    for l in range(depth):
        w = w_in[l]
        col = lambda o, n: w[:, o:o + n]
        rb_pad = jnp.pad(col(o_rb, GATE_RANK), ((0, 0), (0, G_KW - GATE_RANK)))
        wn = jnp.concatenate([col(o_ka, 512), col(o_gb, 256), col(o_gm, 256), col(o_qb, 128), col(o_kb, 128),
                              col(o_vb, 256), rb_pad, col(o_qm, 256)], axis=1).astype(BF16)
        wt = jnp.concatenate([col(o_qa, 512), col(o_va, 512), col(o_ga, 512)], axis=1).T.astype(BF16)
        wgu = jnp.pad(w_gate_up[l], ((0, G_KW - GATE_RANK), (0, 0)))
        gk = jnp.tile(g_k_moba[l], A_HEADS).reshape(1, A_WIDTH)

        (qT, qa, kaug, vT, kmean_nat, sga, sgbm, gq, gk2, gv, gg, qm) = _proj_call(
            x, g_pre[l].reshape(1, d), wn, wt, gk, g_q_moba[l].reshape(HEAD_DIM, 1), bones_a, e, qx,
            wgu, b_gate_up[l].reshape(1, G_KW))

        bsz, s = x.shape[0], x.shape[1]
        nb = s // MOBA_BLOCK
        kmean = kmean_nat.reshape(bsz, nb, A_HEADS, HEAD_DIM).transpose(0, 2, 1, 3)
        oaT = _moba_call(c1, qa, kaug, vT, _sel_call(qT, kmean), sga)
        ob = _gla_call(gq, gk2, gv, gg, sgbm, lall, band, bdt, bones_m,
                       jnp.tile(g_gla_out[l], G_HEADS).reshape(1, G_VW))
        om = _mem_call(qm, sgbm, mk_all[l], mvx_all[l],
                       jnp.tile(g_q_mem[l], M_HEADS).reshape(1, M_WIDTH), bones_m)
        wo = w_out[l].astype(BF16)
        x = _out_call(x, oaT, ob, om, wo[:A_WIDTH], wo[A_WIDTH:])
    return x
```

```python
import functools

import numpy as np
import jax
import jax.numpy as jnp
from jax import lax
from jax.experimental import pallas as pl
from jax.experimental.pallas import tpu as pltpu

F32 = jnp.float32
BF16 = jnp.bfloat16
HIGHEST = lax.Precision.HIGHEST

EPS = 1e-6
LOG2E = 1.4426950408889634
HEAD_DIM = 64
A_HEADS = 8
A_WIDTH = A_HEADS * HEAD_DIM
MOBA_BLOCK = 256
MOBA_TOPK = 3
MOBA_GROUP = 4
MOBA_UNROLL = 4
SEL_WIDTH = 2048
G_HEADS = 4
G_DK = 32
G_DV = 64
G_KW = G_HEADS * G_DK
G_VW = G_HEADS * G_DV
GATE_RANK = 16
GATE_TEMP = 16.0
GLA_CHUNK = 64
GLA_SUB = 16
M_HEADS = 4
M_WIDTH = M_HEADS * HEAD_DIM
K_AUG = 128
V_AUG = 80
ROW_TILE = 512
VMEM_LIMIT = 48 * 1024 * 1024
MOBA_VMEM_LIMIT = 58 * 1024 * 1024

NEG_BIG = -1e30
POS_BIG = 1e30
DEN_MIN = 1e-18
DEN_MAX = 1e30
UB_SLACK = 1.02

_NT = (((1,), (1,)), ((), ()))
_TN = (((0,), (0,)), ((), ()))


def _dot(a, b, **kw):
    return jnp.dot(a, b, preferred_element_type=F32, **kw)


def _dg(a, b, dims, **kw):
    return lax.dot_general(a, b, dims, preferred_element_type=F32, **kw)


def _split2(v):
    hi = v.astype(BF16)
    lo = (v - hi.astype(F32)).astype(BF16)
    return hi, lo


def _split3(v):
    hi = v.astype(BF16)
    r = v - hi.astype(F32)
    mid = r.astype(BF16)
    lo = (r - mid.astype(F32)).astype(BF16)
    return hi, mid, lo


def _group_sumsq(v, bones):
    hi, lo = _split2(v * v)
    return _dot(hi, bones) + _dot(lo, bones)


def _silu(v):
    return v / (1.0 + jnp.exp(-v))


def _block_ones(n, g):
    i = np.arange(n) // g
    return (i[:, None] == i[None, :]).astype(np.float32)


def _params(sem):
    return pltpu.CompilerParams(dimension_semantics=sem, vmem_limit_bytes=VMEM_LIMIT)


def _full(shape):
    return pl.BlockSpec(shape, lambda *_: (0,) * len(shape))


def _memkv_kernel(mem_ref, gmem_ref, w_ref, gk_ref, bones_ref, mk_ref, mvx_ref):
    m = mem_ref[...]
    ms = jnp.mean(m * m, axis=-1, keepdims=True)
    hn = (m * lax.rsqrt(ms + EPS) * gmem_ref[...]).astype(BF16)
    kv = _dot(hn, w_ref[...])
    mk = kv[:, :M_WIDTH]
    mv = kv[:, M_WIDTH:]
    ss = _group_sumsq(mk, bones_ref[...])
    mk_ref[...] = (mk * lax.rsqrt(ss * (1.0 / HEAD_DIM) + EPS) * gk_ref[...]).astype(BF16)
    lane = lax.broadcasted_iota(jnp.int32, mv.shape, 1) // HEAD_DIM
    for h in range(M_HEADS):
        hm = lane == h
        mvx_ref[h] = jnp.concatenate(
            [jnp.where(hm, mv, 0.0), jnp.where(hm, 1.0, 0.0)], axis=-1).astype(BF16)


def _memkv_call(mem, g_mem, w_mem_kv, g_k_mem):
    depth = g_mem.shape[0]
    bsz, mlen, d = mem.shape
    bones = jnp.asarray(_block_ones(M_WIDTH, HEAD_DIM), BF16)
    gk = jnp.tile(g_k_mem, (1, M_HEADS)).reshape(depth, 1, M_WIDTH)
    return pl.pallas_call(
        _memkv_kernel,
        grid=(depth, bsz),
        in_specs=[
            pl.BlockSpec((None, mlen, d), lambda l, b: (b, 0, 0)),
            pl.BlockSpec((None, 1, d), lambda l, b: (l, 0, 0)),
            pl.BlockSpec((None, d, 2 * M_WIDTH), lambda l, b: (l, 0, 0)),
            pl.BlockSpec((None, 1, M_WIDTH), lambda l, b: (l, 0, 0)),
            _full((M_WIDTH, M_WIDTH)),
        ],
        out_specs=[
            pl.BlockSpec((None, None, mlen, M_WIDTH), lambda l, b: (l, b, 0, 0)),
            pl.BlockSpec((None, None, M_HEADS, mlen, 2 * M_WIDTH), lambda l, b: (l, b, 0, 0, 0)),
        ],
        out_shape=[
            jax.ShapeDtypeStruct((depth, bsz, mlen, M_WIDTH), BF16),
            jax.ShapeDtypeStruct((depth, bsz, M_HEADS, mlen, 2 * M_WIDTH), BF16),
        ],
        compiler_params=_params(("arbitrary", "arbitrary")),
        name="memkv",
    )(mem, g_mem.reshape(depth, 1, d), w_mem_kv.astype(BF16), gk, bones)


_N_K = (0, 512)
_N_GBM = (512, 1024)
_N_QB = (1024, 1152)
_N_KB = (1152, 1280)
_N_VB = (1280, 1536)
_N_RB = (1536, 1664)
_N_QM = (1664, 1920)


def _proj_kernel(x_ref, gpre_ref, wn_ref, wt_ref, gk_ref, gqcol_ref, bones_ref, e_ref, qx_ref, wgu_ref, bgu_ref,
                 qT_ref, qa_ref, kaug_ref, vT_ref, kmean_ref, sga_ref, sgbm_ref, gq_ref, gk2_ref, gv_ref, gg_ref,
                 qm_ref):
    tm = x_ref.shape[0]
    nblk = tm // MOBA_BLOCK
    x = x_ref[...]
    ms = jnp.mean(x * x, axis=-1, keepdims=True)
    h = (x * lax.rsqrt(ms + EPS) * gpre_ref[...]).astype(BF16)

    def nat(cols):
        return _dot(h, wn_ref[:, cols[0]:cols[1]])

    k = nat(_N_K)
    ss = _group_sumsq(k, bones_ref[...])
    kn = k * lax.rsqrt(ss * (1.0 / HEAD_DIM) + EPS) * gk_ref[...]
    kmean_ref[...] = jnp.mean(kn.reshape(nblk, MOBA_BLOCK, A_WIDTH), axis=1)
    e = jnp.concatenate([e_ref[...]] * nblk, axis=0)
    for hh in range(A_HEADS):
        rows = kn[:, hh * HEAD_DIM:(hh + 1) * HEAD_DIM]
        kaug_ref[hh] = jnp.concatenate([rows, e], axis=-1).astype(BF16)

    qT = _dg(wt_ref[0:A_WIDTH, :], h, _NT)
    q3 = qT.reshape(A_HEADS, HEAD_DIM, tm)
    msq = jnp.mean(q3 * q3, axis=1, keepdims=True)
    qn = q3 * lax.rsqrt(msq + EPS) * gqcol_ref[...].reshape(1, HEAD_DIM, 1)
    qT_ref[...] = qn.reshape(A_WIDTH, tm)
    qs = (qn * (LOG2E * HEAD_DIM ** -0.5)).astype(BF16)
    blocks = [slice(t * MOBA_BLOCK, (t + 1) * MOBA_BLOCK) for t in range(nblk)]
    for hh in range(A_HEADS):
        for t, cols in enumerate(blocks):
            qa_ref[hh, t] = jnp.concatenate([qs[hh][:, cols], qx_ref[hh]], axis=0)

    vT = _dg(wt_ref[A_WIDTH:2 * A_WIDTH, :], h, _NT)
    ones_rows = jnp.where(lax.broadcasted_iota(jnp.int32, (V_AUG - HEAD_DIM, tm), 0) == 0, 1.0, 0.0)
    sga = _silu(_dg(wt_ref[2 * A_WIDTH:3 * A_WIDTH, :], h, _NT)).astype(BF16)
    for hh in range(A_HEADS):
        rows = slice(hh * HEAD_DIM, (hh + 1) * HEAD_DIM)
        vT_ref[hh] = jnp.concatenate([vT[rows], ones_rows], axis=0).astype(BF16)
        for t, cols in enumerate(blocks):
            sga_ref[hh, t] = sga[rows, cols]
    sgbm_ref[...] = _silu(nat(_N_GBM)).astype(BF16)

    gq_ref[...] = nat(_N_QB) * (G_DK ** -0.5)
    gk2_ref[...] = nat(_N_KB)
    gv_ref[...] = nat(_N_VB).astype(BF16)
    rb = nat(_N_RB)
    z = _dot(rb, wgu_ref[...], precision=HIGHEST) + bgu_ref[...]
    gg_ref[...] = (jnp.minimum(z, 0.0) - jnp.log(1.0 + jnp.exp(-jnp.abs(z)))) * (1.0 / GATE_TEMP)

    qm_ref[...] = nat(_N_QM)


def _proj_call(x, g_pre, wn, wt, gk, gqcol, bones, e, qx, wgu, bgu):
    bsz, s, d = x.shape
    tm = ROW_TILE
    nt = s // tm
    nblk = tm // MOBA_BLOCK
    nb = s // MOBA_BLOCK
    row = lambda w: pl.BlockSpec((None, tm, w), lambda b, t: (b, t, 0))
    in_specs = [
        row(d), _full((1, d)), _full(wn.shape), _full(wt.shape), _full((1, A_WIDTH)),
        _full((HEAD_DIM, 1)), _full((A_WIDTH, A_WIDTH)), _full(e.shape), _full(qx.shape), _full(wgu.shape),
        _full((1, G_KW)),
    ]
    group = MOBA_GROUP * MOBA_BLOCK
    tpg = group // tm
    blocked = lambda r: pl.BlockSpec((None, A_HEADS, nblk, r, MOBA_BLOCK), lambda b, t: (b, 0, t, 0, 0))
    out_specs = [
        pl.BlockSpec((None, A_WIDTH, tm), lambda b, t: (b, 0, t)),
        blocked(K_AUG),
        pl.BlockSpec((None, A_HEADS, tm, K_AUG), lambda b, t: (b, 0, t, 0)),
        pl.BlockSpec((None, A_HEADS, None, V_AUG, tm), lambda b, t: (b, 0, t // tpg, 0, t % tpg)),
        pl.BlockSpec((None, None, nblk, A_WIDTH), lambda b, t: (b, t, 0, 0)),
        blocked(HEAD_DIM),
        row(2 * M_WIDTH), row(G_KW), row(G_KW), row(G_VW), row(G_KW), row(M_WIDTH),
    ]
    out_shape = [
        jax.ShapeDtypeStruct((bsz, A_WIDTH, s), F32),
        jax.ShapeDtypeStruct((bsz, A_HEADS, nb, K_AUG, MOBA_BLOCK), BF16),
        jax.ShapeDtypeStruct((bsz, A_HEADS, s, K_AUG), BF16),
        jax.ShapeDtypeStruct((bsz, A_HEADS, s // group, V_AUG, group), BF16),
        jax.ShapeDtypeStruct((bsz, nt, nblk, A_WIDTH), F32),
        jax.ShapeDtypeStruct((bsz, A_HEADS, nb, HEAD_DIM, MOBA_BLOCK), BF16),
        jax.ShapeDtypeStruct((bsz, s, 2 * M_WIDTH), BF16),
        jax.ShapeDtypeStruct((bsz, s, G_KW), F32),
        jax.ShapeDtypeStruct((bsz, s, G_KW), F32),
        jax.ShapeDtypeStruct((bsz, s, G_VW), BF16),
        jax.ShapeDtypeStruct((bsz, s, G_KW), F32),
        jax.ShapeDtypeStruct((bsz, s, M_WIDTH), F32),
    ]
    return pl.pallas_call(
        _proj_kernel, grid=(bsz, nt), in_specs=in_specs, out_specs=out_specs, out_shape=out_shape,
        compiler_params=_params(("arbitrary", "arbitrary")), name="proj",
    )(x, g_pre, wn, wt, gk, gqcol, bones, e, qx, wgu, bgu)


def _sel_kernel(qT_ref, kmean_ref, o_ref):
    width = qT_ref.shape[1]
    gate = _dot(kmean_ref[...], qT_ref[...], precision=HIGHEST)
    nb = gate.shape[0]
    rowf = lax.broadcasted_iota(jnp.int32, gate.shape, 0).astype(F32)
    first = pl.program_id(2) * (width // MOBA_BLOCK)
    qblk = (first + lax.broadcasted_iota(jnp.int32, gate.shape, 1) // MOBA_BLOCK).astype(F32)
    g = jnp.where(rowf < qblk, gate, -jnp.inf)
    sel = jnp.zeros(gate.shape, F32)
    for _ in range(MOBA_TOPK):
        m = jnp.max(g, axis=0, keepdims=True)
        idx = jnp.min(jnp.where(g == m, rowf, float(nb)), axis=0, keepdims=True)
        hit = rowf == idx
        sel = jnp.where(hit, jnp.where(m > -jnp.inf, 1.0, sel), sel)
        g = jnp.where(hit, -jnp.inf, g)
    for u in range(width // MOBA_BLOCK):
        o_ref[u] = sel[:, u * MOBA_BLOCK:(u + 1) * MOBA_BLOCK]


def _sel_call(qT, kmean):
    bsz, _, s = qT.shape
    nb = s // MOBA_BLOCK
    width = min(SEL_WIDTH, s)
    return pl.pallas_call(
        _sel_kernel,
        grid=(bsz, A_HEADS, s // width),
        in_specs=[
            pl.BlockSpec((None, HEAD_DIM, width), lambda b, h, c: (b, h, c)),
            pl.BlockSpec((None, None, nb, HEAD_DIM), lambda b, h, c: (b, h, 0, 0)),
        ],
        out_specs=pl.BlockSpec((None, None, width // MOBA_BLOCK, nb, MOBA_BLOCK), lambda b, h, c: (b, h, c, 0, 0)),
        out_shape=jax.ShapeDtypeStruct((bsz, A_HEADS, nb, nb, MOBA_BLOCK), F32),
        compiler_params=_params(("arbitrary", "arbitrary", "arbitrary")),
        name="mobasel",
    )(qT, kmean)


def _moba_items(nb):
    qi, grp, first, valid = [], [], [], []
    for i in range(1, nb):
        for g in range((i - 1) // MOBA_GROUP + 1):
            qi.append(i); grp.append(g); first.append(int(g == 0)); valid.append(1)
    n_items = -(-len(qi) // MOBA_UNROLL) * MOBA_UNROLL
    for _ in range(n_items + 2 - len(qi)):
        qi.append(qi[-1]); grp.append(0); first.append(0); valid.append(0)
    tab = np.stack([qi, grp, first, valid]).astype(np.int32)
    return tab, n_items


def _moba_kernel(tab_ref, c1_ref, ub_ref, qa_ref, kall_ref, vall_ref, sel_ref, sga_ref, o_ref,
                 m_scr, acc_scr, out_scr, s0_ref, s1_ref, mb0_ref, mb1_ref, p0_ref, p1_ref, *, n_items):
    hh = pl.program_id(1)
    c1 = c1_ref[hh]
    ub = ub_ref[0]
    nb = qa_ref.shape[0]
    gk = MOBA_GROUP * MOBA_BLOCK
    half = gk // 2

    kk = lax.broadcasted_iota(jnp.int32, (MOBA_BLOCK, MOBA_BLOCK), 0)
    qq = lax.broadcasted_iota(jnp.int32, (MOBA_BLOCK, MOBA_BLOCK), 1)

    def own_scores(i, s_ref, mb_ref):
        i = jnp.minimum(i, nb - 1)
        r0 = pl.multiple_of(i * MOBA_BLOCK, MOBA_BLOCK)
        s = _dot(kall_ref[pl.ds(r0, MOBA_BLOCK), :], qa_ref[i])
        s = jnp.where(kk <= qq, s, NEG_BIG)
        s_ref[0:MOBA_BLOCK, :] = s
        mb_ref[0] = jnp.max(s, axis=0, keepdims=True)

    def own_softmax(i, s_ref, mb_ref, p_ref):
        i = jnp.minimum(i, nb - 1)
        m0 = mb_ref[0]
        m_scr[i] = jnp.broadcast_to(m0, m_scr.shape[1:])
        p_ref[0:MOBA_BLOCK, :] = jnp.exp2(s_ref[0:MOBA_BLOCK, :] - m0).astype(BF16)

    def own_pv(t, k, p_ref):
        vown = vall_ref[t][:, k * MOBA_BLOCK:(k + 1) * MOBA_BLOCK]
        acc_scr[MOBA_GROUP * t + k] = _dot(vown, p_ref[0:MOBA_BLOCK, :])

    bufs = ((s0_ref, mb0_ref, p0_ref), (s1_ref, mb1_ref, p1_ref))
    own_scores(0, s0_ref, mb0_ref)
    own_scores(1, s1_ref, mb1_ref)
    own_softmax(0, s0_ref, mb0_ref, p0_ref)

    def own_body(t, carry):
        for k in range(MOBA_GROUP):
            i = MOBA_GROUP * t + k
            s_a, mb_a, p_a = bufs[k % 2]
            s_b, mb_b, p_b = bufs[(k + 1) % 2]
            own_scores(i + 2, s_a, mb_a)
            own_softmax(i + 1, s_b, mb_b, p_b)
            own_pv(t, k, p_a)
        return carry

    lax.fori_loop(0, nb // MOBA_GROUP, own_body, 0)

    def item(n):
        return tab_ref[0, n], tab_ref[1, n], tab_ref[2, n], tab_ref[3, n]

    def scores(n, s_ref, mb_ref):
        i, g, _, _ = item(n)
        for part in range(2):
            r0 = pl.multiple_of(g * gk + part * half, half)
            s = _dot(kall_ref[pl.ds(r0, half), :], qa_ref[i])
            s_ref[part * half:(part + 1) * half, :] = s
            for u in range(MOBA_GROUP // 2):
                blk = s[u * MOBA_BLOCK:(u + 1) * MOBA_BLOCK]
                mb_ref[part * (MOBA_GROUP // 2) + u] = jnp.max(blk, axis=0, keepdims=True)

    def softmax(n, s_ref, mb_ref, p_ref, m):
        i, g, first, valid = item(n)
        m = jnp.where(first == 1, m_scr[i][0:1], m)
        m_new = m
        members = []
        for u in range(MOBA_GROUP):
            j = g * MOBA_GROUP + u
            picked = jnp.where(valid == 1, sel_ref[i, pl.ds(j, 1), :], 0.0) > 0.5
            cj = c1 * (MOBA_BLOCK * (j - i)).astype(F32)
            m_new = jnp.where(picked, jnp.maximum(m_new, mb_ref[u] + cj), m_new)
            members.append((picked, cj))
        for u, (picked, cj) in enumerate(members):
            shift = jnp.where(picked, m_new - cj, POS_BIG)
            rows = slice(u * MOBA_BLOCK, (u + 1) * MOBA_BLOCK)
            p_ref[rows, :] = jnp.exp2(s_ref[rows, :] - shift).astype(BF16)
        return m_new, jnp.exp2(m - m_new)

    def accumulate(n, p_ref, alpha, acc):
        i, g, first, _ = item(n)
        acc = jnp.where(first == 1, acc_scr[i], acc) * alpha + _dot(vall_ref[g], p_ref[...])
        acc_scr[i] = acc
        return acc

    def fast_probs(n, p_ref):
        i, g, _, valid = item(n)
        sigma = jnp.maximum(m_scr[i][0:1], ub)
        for part in range(2):
            r0 = pl.multiple_of(g * gk + part * half, half)
            s = _dot(kall_ref[pl.ds(r0, half), :], qa_ref[i])
            for u in range(MOBA_GROUP // 2):
                j = g * MOBA_GROUP + part * (MOBA_GROUP // 2) + u
                picked = jnp.where(valid == 1, sel_ref[i, pl.ds(j, 1), :], 0.0) > 0.5
                cj = c1 * (MOBA_BLOCK * (j - i)).astype(F32)
                shift = jnp.where(picked, sigma - cj, POS_BIG)
                rows = slice(part * half + u * MOBA_BLOCK, part * half + (u + 1) * MOBA_BLOCK)
                p_ref[rows, :] = jnp.exp2(s[u * MOBA_BLOCK:(u + 1) * MOBA_BLOCK] - shift).astype(BF16)

    def fast_accumulate(n, p_ref, acc):
        i, g, first, _ = item(n)
        m0 = m_scr[i][0:1]
        own = acc_scr[i] * jnp.exp2(m0 - jnp.maximum(m0, ub))
        acc = jnp.where(first == 1, own, acc) + _dot(vall_ref[g], p_ref[...])
        out_scr[i] = acc
        return acc

    out_scr[0] = acc_scr[0]
    fast_probs(0, p0_ref)

    def fast_body(t, acc):
        for k in range(MOBA_UNROLL):
            n = MOBA_UNROLL * t + k
            fast_probs(n + 1, bufs[(k + 1) % 2][2])
            acc = fast_accumulate(n, bufs[k % 2][2], acc)
        return acc

    lax.fori_loop(0, n_items // MOBA_UNROLL, fast_body, jnp.zeros((V_AUG, MOBA_BLOCK), F32))

    den = out_scr[:, HEAD_DIM:HEAD_DIM + 1, :]
    in_range = jnp.logical_and(jnp.min(den) > DEN_MIN, jnp.max(den) < DEN_MAX)

    @pl.when(jnp.logical_not(in_range))
    def _():
        scores(0, s0_ref, mb0_ref)
        scores(1, s1_ref, mb1_ref)
        m_init = jnp.full((1, MOBA_BLOCK), NEG_BIG, F32)
        m, alpha = softmax(0, s0_ref, mb0_ref, p0_ref, m_init)
        acc = jnp.zeros((V_AUG, MOBA_BLOCK), F32)

        def body(t, carry):
            m, alpha, acc = carry
            for k in range(MOBA_UNROLL):
                n = MOBA_UNROLL * t + k
                s_a, mb_a, p_a = bufs[k % 2]
                s_b, mb_b, p_b = bufs[(k + 1) % 2]
                acc = accumulate(n, p_a, alpha, acc)
                m, alpha = softmax(n + 1, s_b, mb_b, p_b, m)
                scores(n + 2, s_a, mb_a)
            return m, alpha, acc

        lax.fori_loop(0, n_items // MOBA_UNROLL, body, (m, alpha, acc))
        out_scr[...] = acc_scr[...]

    def finish(i, carry):
        acc = out_scr[i]
        o = acc[:HEAD_DIM] / acc[HEAD_DIM:HEAD_DIM + 1]
        o_ref[i] = (o * sga_ref[i].astype(F32)).astype(BF16)
        return carry

    lax.fori_loop(0, nb, finish, 0, unroll=4)


def _moba_call(c1, ub, qa, kaug, vT, sel, sga):
    bsz, _, nb = qa.shape[:3]
    s = nb * MOBA_BLOCK
    gk = MOBA_GROUP * MOBA_BLOCK
    assert nb % MOBA_GROUP == 0 and MOBA_GROUP % 2 == 0 and MOBA_UNROLL % 2 == 0
    tab, n_items = _moba_items(nb)
    per_bh = lambda *tail: pl.BlockSpec((None, None) + tail, lambda b, h: (b, h) + (0,) * len(tail))
    return pl.pallas_call(
        functools.partial(_moba_kernel, n_items=n_items),
        grid=(bsz, A_HEADS),
        in_specs=[
            pl.BlockSpec(memory_space=pltpu.SMEM),
            pl.BlockSpec(memory_space=pltpu.SMEM),
            pl.BlockSpec(memory_space=pltpu.SMEM),
            per_bh(nb, K_AUG, MOBA_BLOCK),
            per_bh(s, K_AUG),
            per_bh(nb // MOBA_GROUP, V_AUG, gk),
            per_bh(nb, nb, MOBA_BLOCK),
            per_bh(nb, HEAD_DIM, MOBA_BLOCK),
        ],
        out_specs=per_bh(nb, HEAD_DIM, MOBA_BLOCK),
        out_shape=jax.ShapeDtypeStruct((bsz, A_HEADS, nb, HEAD_DIM, MOBA_BLOCK), BF16),
        scratch_shapes=[pltpu.VMEM((nb, 8, MOBA_BLOCK), F32), pltpu.VMEM((nb, V_AUG, MOBA_BLOCK), F32),
                        pltpu.VMEM((nb, V_AUG, MOBA_BLOCK), F32),
                        pltpu.VMEM((gk, MOBA_BLOCK), F32), pltpu.VMEM((gk, MOBA_BLOCK), F32),
                        pltpu.VMEM((MOBA_GROUP, 1, MOBA_BLOCK), F32), pltpu.VMEM((MOBA_GROUP, 1, MOBA_BLOCK), F32),
                        pltpu.VMEM((gk, MOBA_BLOCK), BF16), pltpu.VMEM((gk, MOBA_BLOCK), BF16)],
        compiler_params=pltpu.CompilerParams(dimension_semantics=("arbitrary", "arbitrary"),
                                             vmem_limit_bytes=MOBA_VMEM_LIMIT),
        name="moba",
    )(jnp.asarray(tab), c1, ub, qa, kaug, vT, sel, sga)


def _gla_consts():
    c, sb = GLA_CHUNK, GLA_SUB
    r = np.arange(c)
    blk = r // sb
    tri = (r[None, :] <= r[:, None])
    same = blk[None, :] == blk[:, None]
    lall = np.concatenate([
        tri,
        tri & same,
        same,
        blk[None, :] == blk[:, None] - 1,
        blk[None, :] == blk[:, None] - 2,
    ], axis=0).astype(np.float32)
    diff = blk[:, None] - blk[None, :]
    band = np.stack([diff == 1, diff == 2, diff == 3, same & tri]).astype(np.float32)
    dk_head = np.arange(G_KW) // G_DK
    dv_head = np.arange(G_VW) // G_DV
    bdt = (dv_head[:, None] == dk_head[None, :]).astype(np.float32)
    return lall, band, bdt


def _gla_kernel(q_ref, k_ref, v_ref, g_ref, sgb_ref, lall_ref, band_ref, bdt_ref, bones_ref, gout_ref,
                o_ref, st_ref):
    c = GLA_CHUNK

    @pl.when(pl.program_id(1) == 0)
    def _():
        st_ref[...] = jnp.zeros_like(st_ref)

    lane_k = lax.broadcasted_iota(jnp.int32, (1, G_KW), 1) // G_DK
    lane_v = lax.broadcasted_iota(jnp.int32, (1, G_VW), 1) // G_DV
    n_chunks = q_ref.shape[0] // c

    def chunk(ci, st):
        r0 = pl.multiple_of(ci * c, c)
        q = q_ref[pl.ds(r0, c), :]
        k = k_ref[pl.ds(r0, c), :]
        v = v_ref[pl.ds(r0, c), :]
        g = g_ref[pl.ds(r0, c), :]

        g3 = jnp.concatenate(_split3(g), axis=1)
        r = _dot(lall_ref[...], g3)
        r = r[:, :G_KW] + r[:, G_KW:2 * G_KW] + r[:, 2 * G_KW:]
        b, cc, tt, p1, p2 = (r[n * c:(n + 1) * c] for n in range(5))
        b_last = b[c - 1:c]

        qt = q * jnp.exp(cc)
        q2 = qt * jnp.exp(p1)
        q3 = q2 * jnp.exp(p2)
        kt = (k * jnp.exp(tt - cc)).astype(BF16)
        kd = (k * jnp.exp(-cc)).astype(BF16)
        qi = (q * jnp.exp(b)).astype(BF16)
        kl = (k * jnp.exp(b_last - b)).astype(BF16)

        def heads(t):
            return jnp.concatenate([jnp.where(lane_k == hd, t, 0.0) for hd in range(G_HEADS)],
                                   axis=0).astype(BF16)

        xs = [_dg(heads(t), kt, _NT) for t in (qt, q2, q3)]
        yd = _dg(heads(qt), kd, _NT)
        o = _dg(qi, st.astype(BF16), _NT)
        for hd in range(G_HEADS):
            sl = slice(hd * c, (hd + 1) * c)
            att = (band_ref[0] * xs[0][sl] + band_ref[1] * xs[1][sl] + band_ref[2] * xs[2][sl]
                   + band_ref[3] * yd[sl])
            vh = jnp.where(lane_v == hd, v, jnp.zeros_like(v))
            o = o + _dot(att.astype(BF16), vh)
        st_new = st * jnp.exp(b_last) + _dg(v, kl, _TN) * bdt_ref[...]

        ss = _group_sumsq(o, bones_ref[...])
        y = o * lax.rsqrt(ss * (1.0 / G_DV) + EPS) * gout_ref[...]
        o_ref[pl.ds(r0, c), :] = (y * sgb_ref[pl.ds(r0, c), :].astype(F32)).astype(BF16)
        return st_new

    st_ref[...] = lax.fori_loop(0, n_chunks, chunk, st_ref[...], unroll=2)


def _gla_call(gq, gk, gv, gg, sgbm, lall, band, bdt, bones, gout):
    bsz, s, _ = gq.shape
    tm = ROW_TILE
    row = lambda w: pl.BlockSpec((None, tm, w), lambda b, t: (b, t, 0))
    return pl.pallas_call(
        _gla_kernel,
        grid=(bsz, s // tm),
        in_specs=[row(G_KW), row(G_KW), row(G_VW), row(G_KW), row(G_VW),
                  _full(lall.shape), _full(band.shape), _full(bdt.shape), _full(bones.shape), _full((1, G_VW))],
        out_specs=row(G_VW),
        out_shape=jax.ShapeDtypeStruct((bsz, s, G_VW), BF16),
        scratch_shapes=[pltpu.VMEM((G_VW, G_KW), F32)],
        compiler_params=_params(("arbitrary", "arbitrary")),
        name="gla",
    )(gq, gk, gv, gg, sgbm, lall, band, bdt, bones, gout)


def _mem_kernel(qm_ref, sgm_ref, mk_ref, mvx_ref, gq_ref, bones_ref, o_ref):
    qm = qm_ref[...]
    ss = _group_sumsq(qm, bones_ref[...])
    qn = qm * lax.rsqrt(ss * (1.0 / HEAD_DIM) + EPS) * gq_ref[...]
    qs = qn * (LOG2E * HEAD_DIM ** -0.5)
    lane = lax.broadcasted_iota(jnp.int32, (1, M_WIDTH), 1) // HEAD_DIM
    mk = mk_ref[...]
    acc = jnp.zeros((qm.shape[0], 2 * M_WIDTH), F32)
    for h in range(M_HEADS):
        qh = jnp.where(lane == h, qs, 0.0).astype(BF16)
        s = _dg(qh, mk, _NT)
        m = jnp.max(s, axis=-1, keepdims=True)
        p = jnp.exp2(s - m).astype(BF16)
        acc = acc + _dot(p, mvx_ref[h])
    om = acc[:, :M_WIDTH] / acc[:, M_WIDTH:]
    o_ref[...] = (om * sgm_ref[...].astype(F32)).astype(BF16)


def _mem_call(qm, sgbm, mk, mvx, gq, bones):
    bsz, s, _ = qm.shape
    tm = ROW_TILE
    mlen = mk.shape[1]
    return pl.pallas_call(
        _mem_kernel,
        grid=(bsz, s // tm),
        in_specs=[
            pl.BlockSpec((None, tm, M_WIDTH), lambda b, t: (b, t, 0)),
            pl.BlockSpec((None, tm, M_WIDTH), lambda b, t: (b, t, 1)),
            pl.BlockSpec((None, mlen, M_WIDTH), lambda b, t: (b, 0, 0)),
            pl.BlockSpec((None, M_HEADS, mlen, 2 * M_WIDTH), lambda b, t: (b, 0, 0, 0)),
            _full((1, M_WIDTH)), _full((M_WIDTH, M_WIDTH)),
        ],
        out_specs=pl.BlockSpec((None, tm, M_WIDTH), lambda b, t: (b, t, 0)),
        out_shape=jax.ShapeDtypeStruct((bsz, s, M_WIDTH), BF16),
        compiler_params=_params(("arbitrary", "arbitrary")),
        name="memattn",
    )(qm, sgbm, mk, mvx, gq, bones)


def _out_kernel(x_ref, oaT_ref, ob_ref, om_ref, wa_ref, wb_ref, o_ref):
    y = _dot(jnp.concatenate([ob_ref[...], om_ref[...]], axis=-1), wb_ref[...])
    for t in range(oaT_ref.shape[1]):
        rows = slice(t * MOBA_BLOCK, (t + 1) * MOBA_BLOCK)
        oaT = oaT_ref[:, t].reshape(A_WIDTH, MOBA_BLOCK)
        o_ref[rows, :] = x_ref[rows, :] + y[rows] + _dg(oaT, wa_ref[...], _TN)


def _out_call(x, oaT, ob, om, wa, wb):
    bsz, s, d = x.shape
    tm = ROW_TILE
    row = lambda w: pl.BlockSpec((None, tm, w), lambda b, t: (b, t, 0))
    return pl.pallas_call(
        _out_kernel,
        grid=(bsz, s // tm),
        in_specs=[row(d),
                  pl.BlockSpec((None, A_HEADS, tm // MOBA_BLOCK, HEAD_DIM, MOBA_BLOCK), lambda b, t: (b, 0, t, 0, 0)),
                  row(G_VW), row(M_WIDTH),
                  _full(wa.shape), _full(wb.shape)],
        out_specs=row(d),
        out_shape=jax.ShapeDtypeStruct((bsz, s, d), F32),
        compiler_params=_params(("arbitrary", "arbitrary")),
        name="outproj",
    )(x, oaT, ob, om, wa, wb)


def _alibi_consts():
    slopes = np.asarray([2.0 ** (-8.0 * (i + 1) / A_HEADS) for i in range(A_HEADS)], np.float32)
    c1 = (slopes * np.float32(LOG2E)).astype(np.float32)
    c1j = jnp.asarray(c1)
    pieces = list(_split3(c1j * 16.0)) + list(_split3(c1j))
    qx = jnp.zeros((A_HEADS, K_AUG - HEAD_DIM, MOBA_BLOCK), BF16)
    for n, pc in enumerate(pieces):
        qx = qx.at[:, n, :].set(jnp.broadcast_to(pc[:, None], (A_HEADS, MOBA_BLOCK)))
    pos = np.arange(MOBA_BLOCK)
    e = np.zeros((MOBA_BLOCK, K_AUG - HEAD_DIM), np.float32)
    e[:, 0:3] = (pos // 16)[:, None]
    e[:, 3:6] = (pos % 16)[:, None]
    return c1j, qx, jnp.asarray(e)


def kernel(x, mem, g_pre, w_in, g_q_moba, g_k_moba, w_gate_up, b_gate_up, g_gla_out,
           g_mem, w_mem_kv, g_q_mem, g_k_mem, w_out):
    depth = g_pre.shape[0]
    d = x.shape[-1]
    c1, qx, e = _alibi_consts()
    lall_np, band_np, bdt_np = _gla_consts()
    lall = jnp.asarray(lall_np, BF16)
    band = jnp.asarray(band_np)
    bdt = jnp.asarray(bdt_np)
    bones_a = jnp.asarray(_block_ones(A_WIDTH, HEAD_DIM), BF16)
    bones_m = jnp.asarray(_block_ones(M_WIDTH, HEAD_DIM), BF16)

    o_qa, o_ka, o_va, o_ga = 0, 512, 1024, 1536
    o_qb, o_kb, o_vb, o_gb, o_rb = 2048, 2176, 2304, 2560, 2816
    o_qm, o_gm = 2832, 3088

    mk_all, mvx_all = _memkv_call(mem, g_mem, w_mem_kv, g_k_mem)

    for l in range(depth):
        w = w_in[l]
        col = lambda o, n: w[:, o:o + n]
        rb_pad = jnp.pad(col(o_rb, GATE_RANK), ((0, 0), (0, G_KW - GATE_RANK)))
        wn = jnp.concatenate([col(o_ka, 512), col(o_gb, 256), col(o_gm, 256), col(o_qb, 128), col(o_kb, 128),
                              col(o_vb, 256), rb_pad, col(o_qm, 256)], axis=1).astype(BF16)
        wt = jnp.concatenate([col(o_qa, 512), col(o_va, 512), col(o_ga, 512)], axis=1).T.astype(BF16)
        wgu = jnp.pad(w_gate_up[l], ((0, G_KW - GATE_RANK), (0, 0)))
        gk = jnp.tile(g_k_moba[l], A_HEADS).reshape(1, A_WIDTH)

        (qT, qa, kaug, vT, kmean_nat, sga, sgbm, gq, gk2, gv, gg, qm) = _proj_call(
            x, g_pre[l].reshape(1, d), wn, wt, gk, g_q_moba[l].reshape(HEAD_DIM, 1), bones_a, e, qx,
            wgu, b_gate_up[l].reshape(1, G_KW))

        bsz, s = x.shape[0], x.shape[1]
        nb = s // MOBA_BLOCK
        kmean = kmean_nat.reshape(bsz, nb, A_HEADS, HEAD_DIM).transpose(0, 2, 1, 3)
        ub = (UB_SLACK * LOG2E * HEAD_DIM ** 0.5) * jnp.max(jnp.abs(g_q_moba[l])) * jnp.max(jnp.abs(g_k_moba[l]))
        oaT = _moba_call(c1, ub.reshape(1), qa, kaug, vT, _sel_call(qT, kmean), sga)
        ob = _gla_call(gq, gk2, gv, gg, sgbm, lall, band, bdt, bones_m,
                       jnp.tile(g_gla_out[l], G_HEADS).reshape(1, G_VW))
        om = _mem_call(qm, sgbm, mk_all[l], mvx_all[l],
                       jnp.tile(g_q_mem[l], M_HEADS).reshape(1, M_WIDTH), bones_m)
        wo = w_out[l].astype(BF16)
        x = _out_call(x, oaT, ob, om, wo[:A_WIDTH], wo[A_WIDTH:])
    return x
```

```python
import functools

import numpy as np
import jax
import jax.numpy as jnp
from jax import lax
from jax.experimental import pallas as pl
from jax.experimental.pallas import tpu as pltpu

F32 = jnp.float32
BF16 = jnp.bfloat16
HIGHEST = lax.Precision.HIGHEST

EPS = 1e-6
LOG2E = 1.4426950408889634
HEAD_DIM = 64
A_HEADS = 8
A_WIDTH = A_HEADS * HEAD_DIM
MOBA_BLOCK = 256
MOBA_TOPK = 3
MOBA_GROUP = 4
MOBA_UNROLL = 4
MOBA_FAST_UNROLL = 16
SEL_WIDTH = 2048
G_HEADS = 4
G_DK = 32
G_DV = 64
G_KW = G_HEADS * G_DK
G_VW = G_HEADS * G_DV
GATE_RANK = 16
GATE_TEMP = 16.0
GLA_CHUNK = 64
GLA_SUB = 16
M_HEADS = 4
M_WIDTH = M_HEADS * HEAD_DIM
K_AUG = 128
V_AUG = 80
ROW_TILE = 512
VMEM_LIMIT = 48 * 1024 * 1024
MOBA_VMEM_LIMIT = 58 * 1024 * 1024

NEG_BIG = -1e30
POS_BIG = 1e30
DEN_MIN = 1e-18
DEN_MAX = 1e30
UB_SLACK = 1.02

_NT = (((1,), (1,)), ((), ()))
_TN = (((0,), (0,)), ((), ()))


def _dot(a, b, **kw):
    return jnp.dot(a, b, preferred_element_type=F32, **kw)


def _dg(a, b, dims, **kw):
    return lax.dot_general(a, b, dims, preferred_element_type=F32, **kw)


def _split2(v):
    hi = v.astype(BF16)
    lo = (v - hi.astype(F32)).astype(BF16)
    return hi, lo


def _split3(v):
    hi = v.astype(BF16)
    r = v - hi.astype(F32)
    mid = r.astype(BF16)
    lo = (r - mid.astype(F32)).astype(BF16)
    return hi, mid, lo


def _group_sumsq(v, bones):
    hi, lo = _split2(v * v)
    return _dot(hi, bones) + _dot(lo, bones)


def _silu(v):
    return v / (1.0 + jnp.exp(-v))


def _block_ones(n, g):
    i = np.arange(n) // g
    return (i[:, None] == i[None, :]).astype(np.float32)


def _params(sem):
    return pltpu.CompilerParams(dimension_semantics=sem, vmem_limit_bytes=VMEM_LIMIT)


def _full(shape):
    return pl.BlockSpec(shape, lambda *_: (0,) * len(shape))


def _memkv_kernel(mem_ref, gmem_ref, w_ref, gk_ref, bones_ref, mk_ref, mvx_ref):
    m = mem_ref[...]
    ms = jnp.mean(m * m, axis=-1, keepdims=True)
    hn = (m * lax.rsqrt(ms + EPS) * gmem_ref[...]).astype(BF16)
    kv = _dot(hn, w_ref[...])
    mk = kv[:, :M_WIDTH]
    mv = kv[:, M_WIDTH:]
    ss = _group_sumsq(mk, bones_ref[...])
    mk_ref[...] = (mk * lax.rsqrt(ss * (1.0 / HEAD_DIM) + EPS) * gk_ref[...]).astype(BF16)
    lane = lax.broadcasted_iota(jnp.int32, mv.shape, 1) // HEAD_DIM
    for h in range(M_HEADS):
        hm = lane == h
        mvx_ref[h] = jnp.concatenate(
            [jnp.where(hm, mv, 0.0), jnp.where(hm, 1.0, 0.0)], axis=-1).astype(BF16)


def _memkv_call(mem, g_mem, w_mem_kv, g_k_mem):
    depth = g_mem.shape[0]
    bsz, mlen, d = mem.shape
    bones = jnp.asarray(_block_ones(M_WIDTH, HEAD_DIM), BF16)
    gk = jnp.tile(g_k_mem, (1, M_HEADS)).reshape(depth, 1, M_WIDTH)
    return pl.pallas_call(
        _memkv_kernel,
        grid=(depth, bsz),
        in_specs=[
            pl.BlockSpec((None, mlen, d), lambda l, b: (b, 0, 0)),
            pl.BlockSpec((None, 1, d), lambda l, b: (l, 0, 0)),
            pl.BlockSpec((None, d, 2 * M_WIDTH), lambda l, b: (l, 0, 0)),
            pl.BlockSpec((None, 1, M_WIDTH), lambda l, b: (l, 0, 0)),
            _full((M_WIDTH, M_WIDTH)),
        ],
        out_specs=[
            pl.BlockSpec((None, None, mlen, M_WIDTH), lambda l, b: (l, b, 0, 0)),
            pl.BlockSpec((None, None, M_HEADS, mlen, 2 * M_WIDTH), lambda l, b: (l, b, 0, 0, 0)),
        ],
        out_shape=[
            jax.ShapeDtypeStruct((depth, bsz, mlen, M_WIDTH), BF16),
            jax.ShapeDtypeStruct((depth, bsz, M_HEADS, mlen, 2 * M_WIDTH), BF16),
        ],
        compiler_params=_params(("arbitrary", "arbitrary")),
        name="memkv",
    )(mem, g_mem.reshape(depth, 1, d), w_mem_kv.astype(BF16), gk, bones)


_N_K = (0, 512)
_N_GBM = (512, 1024)
_N_QB = (1024, 1152)
_N_KB = (1152, 1280)
_N_VB = (1280, 1536)
_N_RB = (1536, 1664)
_N_QM = (1664, 1920)


def _proj_kernel(x_ref, gpre_ref, wn_ref, wt_ref, gk_ref, gqcol_ref, bones_ref, e_ref, qx_ref, wgu_ref, bgu_ref,
                 qT_ref, qa_ref, kaug_ref, vT_ref, kmean_ref, sga_ref, sgbm_ref, gq_ref, gk2_ref, gv_ref, gg_ref,
                 qm_ref):
    tm = x_ref.shape[0]
    nblk = tm // MOBA_BLOCK
    x = x_ref[...]
    ms = jnp.mean(x * x, axis=-1, keepdims=True)
    h = (x * lax.rsqrt(ms + EPS) * gpre_ref[...]).astype(BF16)

    def nat(cols):
        return _dot(h, wn_ref[:, cols[0]:cols[1]])

    k = nat(_N_K)
    ss = _group_sumsq(k, bones_ref[...])
    kn = k * lax.rsqrt(ss * (1.0 / HEAD_DIM) + EPS) * gk_ref[...]
    kmean_ref[...] = jnp.mean(kn.reshape(nblk, MOBA_BLOCK, A_WIDTH), axis=1)
    e = jnp.concatenate([e_ref[...]] * nblk, axis=0)
    for hh in range(A_HEADS):
        rows = kn[:, hh * HEAD_DIM:(hh + 1) * HEAD_DIM]
        kaug_ref[hh] = jnp.concatenate([rows, e], axis=-1).astype(BF16)

    qT = _dg(wt_ref[0:A_WIDTH, :], h, _NT)
    q3 = qT.reshape(A_HEADS, HEAD_DIM, tm)
    msq = jnp.mean(q3 * q3, axis=1, keepdims=True)
    qn = q3 * lax.rsqrt(msq + EPS) * gqcol_ref[...].reshape(1, HEAD_DIM, 1)
    qT_ref[...] = qn.reshape(A_WIDTH, tm)
    qs = (qn * (LOG2E * HEAD_DIM ** -0.5)).astype(BF16)
    blocks = [slice(t * MOBA_BLOCK, (t + 1) * MOBA_BLOCK) for t in range(nblk)]
    for hh in range(A_HEADS):
        for t, cols in enumerate(blocks):
            qa_ref[hh, t] = jnp.concatenate([qs[hh][:, cols], qx_ref[hh]], axis=0)

    vT = _dg(wt_ref[A_WIDTH:2 * A_WIDTH, :], h, _NT)
    ones_rows = jnp.where(lax.broadcasted_iota(jnp.int32, (V_AUG - HEAD_DIM, tm), 0) == 0, 1.0, 0.0)
    sga = _silu(_dg(wt_ref[2 * A_WIDTH:3 * A_WIDTH, :], h, _NT)).astype(BF16)
    for hh in range(A_HEADS):
        rows = slice(hh * HEAD_DIM, (hh + 1) * HEAD_DIM)
        vT_ref[hh] = jnp.concatenate([vT[rows], ones_rows], axis=0).astype(BF16)
        for t, cols in enumerate(blocks):
            sga_ref[hh, t] = sga[rows, cols]
    sgbm_ref[...] = _silu(nat(_N_GBM)).astype(BF16)

    gq_ref[...] = nat(_N_QB) * (G_DK ** -0.5)
    gk2_ref[...] = nat(_N_KB)
    gv_ref[...] = nat(_N_VB).astype(BF16)
    rb = nat(_N_RB)
    z = _dot(rb, wgu_ref[...], precision=HIGHEST) + bgu_ref[...]
    gg_ref[...] = (jnp.minimum(z, 0.0) - jnp.log(1.0 + jnp.exp(-jnp.abs(z)))) * (1.0 / GATE_TEMP)

    qm_ref[...] = nat(_N_QM)


def _proj_call(x, g_pre, wn, wt, gk, gqcol, bones, e, qx, wgu, bgu):
    bsz, s, d = x.shape
    tm = ROW_TILE
    nt = s // tm
    nblk = tm // MOBA_BLOCK
    nb = s // MOBA_BLOCK
    row = lambda w: pl.BlockSpec((None, tm, w), lambda b, t: (b, t, 0))
    in_specs = [
        row(d), _full((1, d)), _full(wn.shape), _full(wt.shape), _full((1, A_WIDTH)),
        _full((HEAD_DIM, 1)), _full((A_WIDTH, A_WIDTH)), _full(e.shape), _full(qx.shape), _full(wgu.shape),
        _full((1, G_KW)),
    ]
    group = MOBA_GROUP * MOBA_BLOCK
    tpg = group // tm
    blocked = lambda r: pl.BlockSpec((None, A_HEADS, nblk, r, MOBA_BLOCK), lambda b, t: (b, 0, t, 0, 0))
    out_specs = [
        pl.BlockSpec((None, A_WIDTH, tm), lambda b, t: (b, 0, t)),
        blocked(K_AUG),
        pl.BlockSpec((None, A_HEADS, tm, K_AUG), lambda b, t: (b, 0, t, 0)),
        pl.BlockSpec((None, A_HEADS, None, V_AUG, tm), lambda b, t: (b, 0, t // tpg, 0, t % tpg)),
        pl.BlockSpec((None, None, nblk, A_WIDTH), lambda b, t: (b, t, 0, 0)),
        blocked(HEAD_DIM),
        row(2 * M_WIDTH), row(G_KW), row(G_KW), row(G_VW), row(G_KW), row(M_WIDTH),
    ]
    out_shape = [
        jax.ShapeDtypeStruct((bsz, A_WIDTH, s), F32),
        jax.ShapeDtypeStruct((bsz, A_HEADS, nb, K_AUG, MOBA_BLOCK), BF16),
        jax.ShapeDtypeStruct((bsz, A_HEADS, s, K_AUG), BF16),
        jax.ShapeDtypeStruct((bsz, A_HEADS, s // group, V_AUG, group), BF16),
        jax.ShapeDtypeStruct((bsz, nt, nblk, A_WIDTH), F32),
        jax.ShapeDtypeStruct((bsz, A_HEADS, nb, HEAD_DIM, MOBA_BLOCK), BF16),
        jax.ShapeDtypeStruct((bsz, s, 2 * M_WIDTH), BF16),
        jax.ShapeDtypeStruct((bsz, s, G_KW), F32),
        jax.ShapeDtypeStruct((bsz, s, G_KW), F32),
        jax.ShapeDtypeStruct((bsz, s, G_VW), BF16),
        jax.ShapeDtypeStruct((bsz, s, G_KW), F32),
        jax.ShapeDtypeStruct((bsz, s, M_WIDTH), F32),
    ]
    return pl.pallas_call(
        _proj_kernel, grid=(bsz, nt), in_specs=in_specs, out_specs=out_specs, out_shape=out_shape,
        compiler_params=_params(("arbitrary", "arbitrary")), name="proj",
    )(x, g_pre, wn, wt, gk, gqcol, bones, e, qx, wgu, bgu)


def _sel_kernel(qT_ref, kmean_ref, o_ref):
    width = qT_ref.shape[1]
    gate = _dot(kmean_ref[...], qT_ref[...], precision=HIGHEST)
    nb = gate.shape[0]
    rowf = lax.broadcasted_iota(jnp.int32, gate.shape, 0).astype(F32)
    first = pl.program_id(2) * (width // MOBA_BLOCK)
    qblk = (first + lax.broadcasted_iota(jnp.int32, gate.shape, 1) // MOBA_BLOCK).astype(F32)
    g = jnp.where(rowf < qblk, gate, -jnp.inf)
    sel = jnp.zeros(gate.shape, F32)
    for _ in range(MOBA_TOPK):
        m = jnp.max(g, axis=0, keepdims=True)
        idx = jnp.min(jnp.where(g == m, rowf, float(nb)), axis=0, keepdims=True)
        hit = rowf == idx
        sel = jnp.where(hit, jnp.where(m > -jnp.inf, 1.0, sel), sel)
        g = jnp.where(hit, -jnp.inf, g)
    for u in range(width // MOBA_BLOCK):
        o_ref[u] = sel[:, u * MOBA_BLOCK:(u + 1) * MOBA_BLOCK]


def _sel_call(qT, kmean):
    bsz, _, s = qT.shape
    nb = s // MOBA_BLOCK
    width = min(SEL_WIDTH, s)
    return pl.pallas_call(
        _sel_kernel,
        grid=(bsz, A_HEADS, s // width),
        in_specs=[
            pl.BlockSpec((None, HEAD_DIM, width), lambda b, h, c: (b, h, c)),
            pl.BlockSpec((None, None, nb, HEAD_DIM), lambda b, h, c: (b, h, 0, 0)),
        ],
        out_specs=pl.BlockSpec((None, None, width // MOBA_BLOCK, nb, MOBA_BLOCK), lambda b, h, c: (b, h, c, 0, 0)),
        out_shape=jax.ShapeDtypeStruct((bsz, A_HEADS, nb, nb, MOBA_BLOCK), F32),
        compiler_params=_params(("arbitrary", "arbitrary", "arbitrary")),
        name="mobasel",
    )(qT, kmean)


def _moba_items(nb):
    qi, grp, first, valid = [], [], [], []
    for i in range(1, nb):
        for g in range((i - 1) // MOBA_GROUP + 1):
            qi.append(i); grp.append(g); first.append(int(g == 0)); valid.append(1)
    n_items = -(-len(qi) // MOBA_FAST_UNROLL) * MOBA_FAST_UNROLL
    for _ in range(n_items + 2 - len(qi)):
        qi.append(qi[-1]); grp.append(0); first.append(0); valid.append(0)
    tab = np.stack([qi, grp, first, valid]).astype(np.int32)
    return tab, n_items


def _moba_kernel(tab_ref, c1_ref, ub_ref, qa_ref, kall_ref, vall_ref, sel_ref, sga_ref, o_ref,
                 m_scr, acc_scr, out_scr, s0_ref, s1_ref, mb0_ref, mb1_ref, p0_ref, p1_ref, *, n_items):
    hh = pl.program_id(1)
    c1 = c1_ref[hh]
    ub = ub_ref[0]
    nb = qa_ref.shape[0]
    gk = MOBA_GROUP * MOBA_BLOCK
    half = gk // 2

    kk = lax.broadcasted_iota(jnp.int32, (MOBA_BLOCK, MOBA_BLOCK), 0)
    qq = lax.broadcasted_iota(jnp.int32, (MOBA_BLOCK, MOBA_BLOCK), 1)

    def own_scores(i, s_ref, mb_ref):
        i = jnp.minimum(i, nb - 1)
        r0 = pl.multiple_of(i * MOBA_BLOCK, MOBA_BLOCK)
        s = _dot(kall_ref[pl.ds(r0, MOBA_BLOCK), :], qa_ref[i])
        s = jnp.where(kk <= qq, s, NEG_BIG)
        s_ref[0:MOBA_BLOCK, :] = s
        mb_ref[0] = jnp.max(s, axis=0, keepdims=True)

    def own_softmax(i, s_ref, mb_ref, p_ref):
        i = jnp.minimum(i, nb - 1)
        m0 = mb_ref[0]
        m_scr[i] = jnp.broadcast_to(m0, m_scr.shape[1:])
        p_ref[0:MOBA_BLOCK, :] = jnp.exp2(s_ref[0:MOBA_BLOCK, :] - m0).astype(BF16)

    def own_pv(t, k, p_ref):
        vown = vall_ref[t][:, k * MOBA_BLOCK:(k + 1) * MOBA_BLOCK]
        acc_scr[MOBA_GROUP * t + k] = _dot(vown, p_ref[0:MOBA_BLOCK, :])

    bufs = ((s0_ref, mb0_ref, p0_ref), (s1_ref, mb1_ref, p1_ref))
    own_scores(0, s0_ref, mb0_ref)
    own_scores(1, s1_ref, mb1_ref)
    own_softmax(0, s0_ref, mb0_ref, p0_ref)

    def own_body(t, carry):
        for k in range(MOBA_GROUP):
            i = MOBA_GROUP * t + k
            s_a, mb_a, p_a = bufs[k % 2]
            s_b, mb_b, p_b = bufs[(k + 1) % 2]
            own_scores(i + 2, s_a, mb_a)
            own_softmax(i + 1, s_b, mb_b, p_b)
            own_pv(t, k, p_a)
        return carry

    lax.fori_loop(0, nb // MOBA_GROUP, own_body, 0)

    def item(n):
        return tab_ref[0, n], tab_ref[1, n], tab_ref[2, n], tab_ref[3, n]

    def scores(n, s_ref, mb_ref):
        i, g, _, _ = item(n)
        for part in range(2):
            r0 = pl.multiple_of(g * gk + part * half, half)
            s = _dot(kall_ref[pl.ds(r0, half), :], qa_ref[i])
            s_ref[part * half:(part + 1) * half, :] = s
            for u in range(MOBA_GROUP // 2):
                blk = s[u * MOBA_BLOCK:(u + 1) * MOBA_BLOCK]
                mb_ref[part * (MOBA_GROUP // 2) + u] = jnp.max(blk, axis=0, keepdims=True)

    def softmax(n, s_ref, mb_ref, p_ref, m):
        i, g, first, valid = item(n)
        m = jnp.where(first == 1, m_scr[i][0:1], m)
        m_new = m
        members = []
        for u in range(MOBA_GROUP):
            j = g * MOBA_GROUP + u
            picked = jnp.where(valid == 1, sel_ref[i, pl.ds(j, 1), :], 0.0) > 0.5
            cj = c1 * (MOBA_BLOCK * (j - i)).astype(F32)
            m_new = jnp.where(picked, jnp.maximum(m_new, mb_ref[u] + cj), m_new)
            members.append((picked, cj))
        for u, (picked, cj) in enumerate(members):
            shift = jnp.where(picked, m_new - cj, POS_BIG)
            rows = slice(u * MOBA_BLOCK, (u + 1) * MOBA_BLOCK)
            p_ref[rows, :] = jnp.exp2(s_ref[rows, :] - shift).astype(BF16)
        return m_new, jnp.exp2(m - m_new)

    def accumulate(n, p_ref, alpha, acc):
        i, g, first, _ = item(n)
        acc = jnp.where(first == 1, acc_scr[i], acc) * alpha + _dot(vall_ref[g], p_ref[...])
        acc_scr[i] = acc
        return acc

    def fast_probs(n, p_ref):
        i, g, _, valid = item(n)
        sigma = jnp.maximum(m_scr[i][0:1], ub)
        for part in range(2):
            r0 = pl.multiple_of(g * gk + part * half, half)
            s = _dot(kall_ref[pl.ds(r0, half), :], qa_ref[i])
            for u in range(MOBA_GROUP // 2):
                j = g * MOBA_GROUP + part * (MOBA_GROUP // 2) + u
                picked = jnp.where(valid == 1, sel_ref[i, pl.ds(j, 1), :], 0.0) > 0.5
                cj = c1 * (MOBA_BLOCK * (j - i)).astype(F32)
                shift = jnp.where(picked, sigma - cj, POS_BIG)
                rows = slice(part * half + u * MOBA_BLOCK, part * half + (u + 1) * MOBA_BLOCK)
                p_ref[rows, :] = jnp.exp2(s[u * MOBA_BLOCK:(u + 1) * MOBA_BLOCK] - shift).astype(BF16)

    def fast_accumulate(n, p_ref, acc):
        i, g, first, _ = item(n)
        m0 = m_scr[i][0:1]
        own = acc_scr[i] * jnp.exp2(m0 - jnp.maximum(m0, ub))
        acc = jnp.where(first == 1, own, acc) + _dot(vall_ref[g], p_ref[...])
        out_scr[i] = acc
        return acc

    out_scr[0] = acc_scr[0]
    fast_probs(0, p0_ref)

    def fast_body(t, acc):
        for k in range(MOBA_FAST_UNROLL):
            n = MOBA_FAST_UNROLL * t + k
            fast_probs(n + 1, bufs[(k + 1) % 2][2])
            acc = fast_accumulate(n, bufs[k % 2][2], acc)
        return acc

    lax.fori_loop(0, n_items // MOBA_FAST_UNROLL, fast_body, jnp.zeros((V_AUG, MOBA_BLOCK), F32))

    den = out_scr[:, HEAD_DIM:HEAD_DIM + 1, :]
    in_range = jnp.logical_and(jnp.min(den) > DEN_MIN, jnp.max(den) < DEN_MAX)

    @pl.when(jnp.logical_not(in_range))
    def _():
        scores(0, s0_ref, mb0_ref)
        scores(1, s1_ref, mb1_ref)
        m_init = jnp.full((1, MOBA_BLOCK), NEG_BIG, F32)
        m, alpha = softmax(0, s0_ref, mb0_ref, p0_ref, m_init)
        acc = jnp.zeros((V_AUG, MOBA_BLOCK), F32)

        def body(t, carry):
            m, alpha, acc = carry
            for k in range(MOBA_UNROLL):
                n = MOBA_UNROLL * t + k
                s_a, mb_a, p_a = bufs[k % 2]
                s_b, mb_b, p_b = bufs[(k + 1) % 2]
                acc = accumulate(n, p_a, alpha, acc)
                m, alpha = softmax(n + 1, s_b, mb_b, p_b, m)
                scores(n + 2, s_a, mb_a)
            return m, alpha, acc

        lax.fori_loop(0, n_items // MOBA_UNROLL, body, (m, alpha, acc))
        out_scr[...] = acc_scr[...]

    def finish(i, carry):
        acc = out_scr[i]
        o = acc[:HEAD_DIM] / acc[HEAD_DIM:HEAD_DIM + 1]
        o_ref[i] = (o * sga_ref[i].astype(F32)).astype(BF16)
        return carry

    lax.fori_loop(0, nb, finish, 0, unroll=4)


def _moba_call(c1, ub, qa, kaug, vT, sel, sga):
    bsz, _, nb = qa.shape[:3]
    s = nb * MOBA_BLOCK
    gk = MOBA_GROUP * MOBA_BLOCK
    assert nb % MOBA_GROUP == 0 and MOBA_GROUP % 2 == 0 and MOBA_UNROLL % 2 == 0
    assert MOBA_FAST_UNROLL % MOBA_UNROLL == 0
    tab, n_items = _moba_items(nb)
    per_bh = lambda *tail: pl.BlockSpec((None, None) + tail, lambda b, h: (b, h) + (0,) * len(tail))
    return pl.pallas_call(
        functools.partial(_moba_kernel, n_items=n_items),
        grid=(bsz, A_HEADS),
        in_specs=[
            pl.BlockSpec(memory_space=pltpu.SMEM),
            pl.BlockSpec(memory_space=pltpu.SMEM),
            pl.BlockSpec(memory_space=pltpu.SMEM),
            per_bh(nb, K_AUG, MOBA_BLOCK),
            per_bh(s, K_AUG),
            per_bh(nb // MOBA_GROUP, V_AUG, gk),
            per_bh(nb, nb, MOBA_BLOCK),
            per_bh(nb, HEAD_DIM, MOBA_BLOCK),
        ],
        out_specs=per_bh(nb, HEAD_DIM, MOBA_BLOCK),
        out_shape=jax.ShapeDtypeStruct((bsz, A_HEADS, nb, HEAD_DIM, MOBA_BLOCK), BF16),
        scratch_shapes=[pltpu.VMEM((nb, 8, MOBA_BLOCK), F32), pltpu.VMEM((nb, V_AUG, MOBA_BLOCK), F32),
                        pltpu.VMEM((nb, V_AUG, MOBA_BLOCK), F32),
                        pltpu.VMEM((gk, MOBA_BLOCK), F32), pltpu.VMEM((gk, MOBA_BLOCK), F32),
                        pltpu.VMEM((MOBA_GROUP, 1, MOBA_BLOCK), F32), pltpu.VMEM((MOBA_GROUP, 1, MOBA_BLOCK), F32),
                        pltpu.VMEM((gk, MOBA_BLOCK), BF16), pltpu.VMEM((gk, MOBA_BLOCK), BF16)],
        compiler_params=pltpu.CompilerParams(dimension_semantics=("arbitrary", "arbitrary"),
                                             vmem_limit_bytes=MOBA_VMEM_LIMIT),
        name="moba",
    )(jnp.asarray(tab), c1, ub, qa, kaug, vT, sel, sga)


def _gla_consts():
    c, sb = GLA_CHUNK, GLA_SUB
    r = np.arange(c)
    blk = r // sb
    tri = (r[None, :] <= r[:, None])
    same = blk[None, :] == blk[:, None]
    lall = np.concatenate([
        tri,
        tri & same,
        same,
        blk[None, :] == blk[:, None] - 1,
        blk[None, :] == blk[:, None] - 2,
    ], axis=0).astype(np.float32)
    diff = blk[:, None] - blk[None, :]
    band = np.stack([diff == 1, diff == 2, diff == 3, same & tri]).astype(np.float32)
    dk_head = np.arange(G_KW) // G_DK
    dv_head = np.arange(G_VW) // G_DV
    bdt = (dv_head[:, None] == dk_head[None, :]).astype(np.float32)
    return lall, band, bdt


def _gla_kernel(q_ref, k_ref, v_ref, g_ref, sgb_ref, lall_ref, band_ref, bdt_ref, bones_ref, gout_ref,
                o_ref, st_ref):
    c = GLA_CHUNK
    nbatch = q_ref.shape[0]

    @pl.when(pl.program_id(0) == 0)
    def _():
        st_ref[...] = jnp.zeros_like(st_ref)

    lane_k = lax.broadcasted_iota(jnp.int32, (1, G_KW), 1) // G_DK
    lane_v = lax.broadcasted_iota(jnp.int32, (1, G_VW), 1) // G_DV
    n_chunks = q_ref.shape[1] // c

    def heads(t):
        return jnp.concatenate([jnp.where(lane_k == hd, t, 0.0) for hd in range(G_HEADS)],
                               axis=0).astype(BF16)

    def chunk(ci, states):
        r0 = pl.multiple_of(ci * c, c)
        nbs = range(nbatch)
        q = [q_ref[bi, pl.ds(r0, c), :] for bi in nbs]
        k = [k_ref[bi, pl.ds(r0, c), :] for bi in nbs]
        v = [v_ref[bi, pl.ds(r0, c), :] for bi in nbs]

        r = [_dot(lall_ref[...], jnp.concatenate(_split3(g_ref[bi, pl.ds(r0, c), :]), axis=1)) for bi in nbs]
        r = [x[:, :G_KW] + x[:, G_KW:2 * G_KW] + x[:, 2 * G_KW:] for x in r]
        b, cc, tt, p1, p2 = ([x[n * c:(n + 1) * c] for x in r] for n in range(5))
        b_last = [x[c - 1:c] for x in b]

        qt = [q[bi] * jnp.exp(cc[bi]) for bi in nbs]
        q2 = [qt[bi] * jnp.exp(p1[bi]) for bi in nbs]
        q3 = [q2[bi] * jnp.exp(p2[bi]) for bi in nbs]
        kt = [(k[bi] * jnp.exp(tt[bi] - cc[bi])).astype(BF16) for bi in nbs]
        kd = [(k[bi] * jnp.exp(-cc[bi])).astype(BF16) for bi in nbs]
        qi = [(q[bi] * jnp.exp(b[bi])).astype(BF16) for bi in nbs]
        kl = [(k[bi] * jnp.exp(b_last[bi] - b[bi])).astype(BF16) for bi in nbs]

        xs = [[_dg(heads(t[bi]), kt[bi], _NT) for bi in nbs] for t in (qt, q2, q3)]
        yd = [_dg(heads(qt[bi]), kd[bi], _NT) for bi in nbs]
        o = [_dg(qi[bi], states[bi].astype(BF16), _NT) for bi in nbs]
        st_new = tuple(states[bi] * jnp.exp(b_last[bi]) + _dg(v[bi], kl[bi], _TN) * bdt_ref[...]
                       for bi in nbs)
        for hd in range(G_HEADS):
            sl = slice(hd * c, (hd + 1) * c)
            for bi in nbs:
                att = (band_ref[0] * xs[0][bi][sl] + band_ref[1] * xs[1][bi][sl]
                       + band_ref[2] * xs[2][bi][sl] + band_ref[3] * yd[bi][sl])
                vh = jnp.where(lane_v == hd, v[bi], jnp.zeros_like(v[bi]))
                o[bi] = o[bi] + _dot(att.astype(BF16), vh)

        ss = [_group_sumsq(o[bi], bones_ref[...]) for bi in nbs]
        for bi in nbs:
            y = o[bi] * lax.rsqrt(ss[bi] * (1.0 / G_DV) + EPS) * gout_ref[...]
            o_ref[bi, pl.ds(r0, c), :] = (y * sgb_ref[bi, pl.ds(r0, c), :].astype(F32)).astype(BF16)
        return st_new

    states = lax.fori_loop(0, n_chunks, chunk, tuple(st_ref[bi] for bi in range(nbatch)), unroll=2)
    for bi in range(nbatch):
        st_ref[bi] = states[bi]


def _gla_call(gq, gk, gv, gg, sgbm, lall, band, bdt, bones, gout):
    bsz, s, _ = gq.shape
    tm = ROW_TILE
    row = lambda w: pl.BlockSpec((bsz, tm, w), lambda t: (0, t, 0))
    return pl.pallas_call(
        _gla_kernel,
        grid=(s // tm,),
        in_specs=[row(G_KW), row(G_KW), row(G_VW), row(G_KW), row(G_VW),
                  _full(lall.shape), _full(band.shape), _full(bdt.shape), _full(bones.shape), _full((1, G_VW))],
        out_specs=row(G_VW),
        out_shape=jax.ShapeDtypeStruct((bsz, s, G_VW), BF16),
        scratch_shapes=[pltpu.VMEM((bsz, G_VW, G_KW), F32)],
        compiler_params=_params(("arbitrary",)),
        name="gla",
    )(gq, gk, gv, gg, sgbm, lall, band, bdt, bones, gout)


def _mem_kernel(qm_ref, sgm_ref, mk_ref, mvx_ref, gq_ref, bones_ref, o_ref):
    qm = qm_ref[...]
    ss = _group_sumsq(qm, bones_ref[...])
    qn = qm * lax.rsqrt(ss * (1.0 / HEAD_DIM) + EPS) * gq_ref[...]
    qs = qn * (LOG2E * HEAD_DIM ** -0.5)
    lane = lax.broadcasted_iota(jnp.int32, (1, M_WIDTH), 1) // HEAD_DIM
    mk = mk_ref[...]
    s = [_dg(jnp.where(lane == h, qs, 0.0).astype(BF16), mk, _NT) for h in range(M_HEADS)]
    p = [jnp.exp2(x - jnp.max(x, axis=-1, keepdims=True)).astype(BF16) for x in s]
    acc = _dot(p[0], mvx_ref[0])
    for h in range(1, M_HEADS):
        acc = acc + _dot(p[h], mvx_ref[h])
    om = acc[:, :M_WIDTH] / acc[:, M_WIDTH:]
    o_ref[...] = (om * sgm_ref[...].astype(F32)).astype(BF16)


def _mem_call(qm, sgbm, mk, mvx, gq, bones):
    bsz, s, _ = qm.shape
    tm = ROW_TILE
    mlen = mk.shape[1]
    return pl.pallas_call(
        _mem_kernel,
        grid=(bsz, s // tm),
        in_specs=[
            pl.BlockSpec((None, tm, M_WIDTH), lambda b, t: (b, t, 0)),
            pl.BlockSpec((None, tm, M_WIDTH), lambda b, t: (b, t, 1)),
            pl.BlockSpec((None, mlen, M_WIDTH), lambda b, t: (b, 0, 0)),
            pl.BlockSpec((None, M_HEADS, mlen, 2 * M_WIDTH), lambda b, t: (b, 0, 0, 0)),
            _full((1, M_WIDTH)), _full((M_WIDTH, M_WIDTH)),
        ],
        out_specs=pl.BlockSpec((None, tm, M_WIDTH), lambda b, t: (b, t, 0)),
        out_shape=jax.ShapeDtypeStruct((bsz, s, M_WIDTH), BF16),
        compiler_params=_params(("arbitrary", "arbitrary")),
        name="memattn",
    )(qm, sgbm, mk, mvx, gq, bones)


def _out_kernel(x_ref, oaT_ref, ob_ref, om_ref, wa_ref, wb_ref, o_ref):
    y = _dot(jnp.concatenate([ob_ref[...], om_ref[...]], axis=-1), wb_ref[...])
    for t in range(oaT_ref.shape[1]):
        rows = slice(t * MOBA_BLOCK, (t + 1) * MOBA_BLOCK)
        oaT = oaT_ref[:, t].reshape(A_WIDTH, MOBA_BLOCK)
        o_ref[rows, :] = x_ref[rows, :] + y[rows] + _dg(oaT, wa_ref[...], _TN)


def _out_call(x, oaT, ob, om, wa, wb):
    bsz, s, d = x.shape
    tm = ROW_TILE
    row = lambda w: pl.BlockSpec((None, tm, w), lambda b, t: (b, t, 0))
    return pl.pallas_call(
        _out_kernel,
        grid=(bsz, s // tm),
        in_specs=[row(d),
                  pl.BlockSpec((None, A_HEADS, tm // MOBA_BLOCK, HEAD_DIM, MOBA_BLOCK), lambda b, t: (b, 0, t, 0, 0)),
                  row(G_VW), row(M_WIDTH),
                  _full(wa.shape), _full(wb.shape)],
        out_specs=row(d),
        out_shape=jax.ShapeDtypeStruct((bsz, s, d), F32),
        compiler_params=_params(("arbitrary", "arbitrary")),
        name="outproj",
    )(x, oaT, ob, om, wa, wb)


def _alibi_consts():
    slopes = np.asarray([2.0 ** (-8.0 * (i + 1) / A_HEADS) for i in range(A_HEADS)], np.float32)
    c1 = (slopes * np.float32(LOG2E)).astype(np.float32)
    c1j = jnp.asarray(c1)
    pieces = list(_split3(c1j * 16.0)) + list(_split3(c1j))
    qx = jnp.zeros((A_HEADS, K_AUG - HEAD_DIM, MOBA_BLOCK), BF16)
    for n, pc in enumerate(pieces):
        qx = qx.at[:, n, :].set(jnp.broadcast_to(pc[:, None], (A_HEADS, MOBA_BLOCK)))
    pos = np.arange(MOBA_BLOCK)
    e = np.zeros((MOBA_BLOCK, K_AUG - HEAD_DIM), np.float32)
    e[:, 0:3] = (pos // 16)[:, None]
    e[:, 3:6] = (pos % 16)[:, None]
    return c1j, qx, jnp.asarray(e)


def kernel(x, mem, g_pre, w_in, g_q_moba, g_k_moba, w_gate_up, b_gate_up, g_gla_out,
           g_mem, w_mem_kv, g_q_mem, g_k_mem, w_out):
    depth = g_pre.shape[0]
    d = x.shape[-1]
    c1, qx, e = _alibi_consts()
    lall_np, band_np, bdt_np = _gla_consts()
    lall = jnp.asarray(lall_np, BF16)
    band = jnp.asarray(band_np)
    bdt = jnp.asarray(bdt_np)
    bones_a = jnp.asarray(_block_ones(A_WIDTH, HEAD_DIM), BF16)
    bones_m = jnp.asarray(_block_ones(M_WIDTH, HEAD_DIM), BF16)

    o_qa, o_ka, o_va, o_ga = 0, 512, 1024, 1536
    o_qb, o_kb, o_vb, o_gb, o_rb = 2048, 2176, 2304, 2560, 2816
    o_qm, o_gm = 2832, 3088

    mk_all, mvx_all = _memkv_call(mem, g_mem, w_mem_kv, g_k_mem)

    for l in range(depth):
        w = w_in[l]
        col = lambda o, n: w[:, o:o + n]
        rb_pad = jnp.pad(col(o_rb, GATE_RANK), ((0, 0), (0, G_KW - GATE_RANK)))
        wn = jnp.concatenate([col(o_ka, 512), col(o_gb, 256), col(o_gm, 256), col(o_qb, 128), col(o_kb, 128),
                              col(o_vb, 256), rb_pad, col(o_qm, 256)], axis=1).astype(BF16)
        wt = jnp.concatenate([col(o_qa, 512), col(o_va, 512), col(o_ga, 512)], axis=1).T.astype(BF16)
        wgu = jnp.pad(w_gate_up[l], ((0, G_KW - GATE_RANK), (0, 0)))
        gk = jnp.tile(g_k_moba[l], A_HEADS).reshape(1, A_WIDTH)

        (qT, qa, kaug, vT, kmean_nat, sga, sgbm, gq, gk2, gv, gg, qm) = _proj_call(
            x, g_pre[l].reshape(1, d), wn, wt, gk, g_q_moba[l].reshape(HEAD_DIM, 1), bones_a, e, qx,
            wgu, b_gate_up[l].reshape(1, G_KW))

        bsz, s = x.shape[0], x.shape[1]
        nb = s // MOBA_BLOCK
        kmean = kmean_nat.reshape(bsz, nb, A_HEADS, HEAD_DIM).transpose(0, 2, 1, 3)
        ub = (UB_SLACK * LOG2E * HEAD_DIM ** 0.5) * jnp.max(jnp.abs(g_q_moba[l])) * jnp.max(jnp.abs(g_k_moba[l]))
        oaT = _moba_call(c1, ub.reshape(1), qa, kaug, vT, _sel_call(qT, kmean), sga)
        ob = _gla_call(gq, gk2, gv, gg, sgbm, lall, band, bdt, bones_m,
                       jnp.tile(g_gla_out[l], G_HEADS).reshape(1, G_VW))
        om = _mem_call(qm, sgbm, mk_all[l], mvx_all[l],
                       jnp.tile(g_q_mem[l], M_HEADS).reshape(1, M_WIDTH), bones_m)
        wo = w_out[l].astype(BF16)
        x = _out_call(x, oaT, ob, om, wo[:A_WIDTH], wo[A_WIDTH:])
    return x
```

```python
import functools

import numpy as np
import jax
import jax.numpy as jnp
from jax import lax
from jax.experimental import pallas as pl
from jax.experimental.pallas import tpu as pltpu

F32 = jnp.float32
BF16 = jnp.bfloat16
HIGHEST = lax.Precision.HIGHEST

EPS = 1e-6
LOG2E = 1.4426950408889634
HEAD_DIM = 64
A_HEADS = 8
A_WIDTH = A_HEADS * HEAD_DIM
MOBA_BLOCK = 256
MOBA_TOPK = 3
MOBA_GROUP = 4
MOBA_UNROLL = 4
MOBA_FAST_UNROLL = 16
SEL_WIDTH = 2048
G_HEADS = 4
G_DK = 32
G_DV = 64
G_KW = G_HEADS * G_DK
G_VW = G_HEADS * G_DV
GATE_RANK = 16
GATE_TEMP = 16.0
GLA_CHUNK = 64
GLA_SUB = 16
GLA_LEVELS = (32, 16, 8, 4, 2, 1)
GLA_SAFE_EXP = 80.0
M_HEADS = 4
M_WIDTH = M_HEADS * HEAD_DIM
K_AUG = 128
V_AUG = 80
ROW_TILE = 512
VMEM_LIMIT = 48 * 1024 * 1024
MOBA_VMEM_LIMIT = 58 * 1024 * 1024

NEG_BIG = -1e30
POS_BIG = 1e30
DEN_MIN = 1e-18
DEN_MAX = 1e30
UB_SLACK = 1.02

_NT = (((1,), (1,)), ((), ()))
_TN = (((0,), (0,)), ((), ()))


def _dot(a, b, **kw):
    return jnp.dot(a, b, preferred_element_type=F32, **kw)


def _dg(a, b, dims, **kw):
    return lax.dot_general(a, b, dims, preferred_element_type=F32, **kw)


def _split2(v):
    hi = v.astype(BF16)
    lo = (v - hi.astype(F32)).astype(BF16)
    return hi, lo


def _split3(v):
    hi = v.astype(BF16)
    r = v - hi.astype(F32)
    mid = r.astype(BF16)
    lo = (r - mid.astype(F32)).astype(BF16)
    return hi, mid, lo


def _group_sumsq(v, bones):
    hi, lo = _split2(v * v)
    return _dot(hi, bones) + _dot(lo, bones)


def _silu(v):
    return v / (1.0 + jnp.exp(-v))


def _block_ones(n, g):
    i = np.arange(n) // g
    return (i[:, None] == i[None, :]).astype(np.float32)


def _params(sem):
    return pltpu.CompilerParams(dimension_semantics=sem, vmem_limit_bytes=VMEM_LIMIT)


def _full(shape):
    return pl.BlockSpec(shape, lambda *_: (0,) * len(shape))


def _memkv_kernel(mem_ref, gmem_ref, w_ref, gk_ref, bones_ref, mk_ref, mvx_ref):
    m = mem_ref[...]
    ms = jnp.mean(m * m, axis=-1, keepdims=True)
    hn = (m * lax.rsqrt(ms + EPS) * gmem_ref[...]).astype(BF16)
    kv = _dot(hn, w_ref[...])
    mk = kv[:, :M_WIDTH]
    mv = kv[:, M_WIDTH:]
    ss = _group_sumsq(mk, bones_ref[...])
    mk_ref[...] = (mk * lax.rsqrt(ss * (1.0 / HEAD_DIM) + EPS) * gk_ref[...]).astype(BF16)
    lane = lax.broadcasted_iota(jnp.int32, mv.shape, 1) // HEAD_DIM
    for h in range(M_HEADS):
        hm = lane == h
        mvx_ref[h] = jnp.concatenate(
            [jnp.where(hm, mv, 0.0), jnp.where(hm, 1.0, 0.0)], axis=-1).astype(BF16)


def _memkv_call(mem, g_mem, w_mem_kv, g_k_mem):
    depth = g_mem.shape[0]
    bsz, mlen, d = mem.shape
    bones = jnp.asarray(_block_ones(M_WIDTH, HEAD_DIM), BF16)
    gk = jnp.tile(g_k_mem, (1, M_HEADS)).reshape(depth, 1, M_WIDTH)
    return pl.pallas_call(
        _memkv_kernel,
        grid=(depth, bsz),
        in_specs=[
            pl.BlockSpec((None, mlen, d), lambda l, b: (b, 0, 0)),
            pl.BlockSpec((None, 1, d), lambda l, b: (l, 0, 0)),
            pl.BlockSpec((None, d, 2 * M_WIDTH), lambda l, b: (l, 0, 0)),
            pl.BlockSpec((None, 1, M_WIDTH), lambda l, b: (l, 0, 0)),
            _full((M_WIDTH, M_WIDTH)),
        ],
        out_specs=[
            pl.BlockSpec((None, None, mlen, M_WIDTH), lambda l, b: (l, b, 0, 0)),
            pl.BlockSpec((None, None, M_HEADS, mlen, 2 * M_WIDTH), lambda l, b: (l, b, 0, 0, 0)),
        ],
        out_shape=[
            jax.ShapeDtypeStruct((depth, bsz, mlen, M_WIDTH), BF16),
            jax.ShapeDtypeStruct((depth, bsz, M_HEADS, mlen, 2 * M_WIDTH), BF16),
        ],
        compiler_params=_params(("arbitrary", "arbitrary")),
        name="memkv",
    )(mem, g_mem.reshape(depth, 1, d), w_mem_kv.astype(BF16), gk, bones)


_N_K = (0, 512)
_N_GBM = (512, 1024)
_N_QB = (1024, 1152)
_N_KB = (1152, 1280)
_N_VB = (1280, 1536)
_N_RB = (1536, 1664)
_N_QM = (1664, 1920)


def _proj_kernel(x_ref, gpre_ref, wn_ref, wt_ref, gk_ref, gqcol_ref, bones_ref, e_ref, qx_ref, wgu_ref, bgu_ref,
                 qT_ref, qa_ref, kaug_ref, vT_ref, kmean_ref, sga_ref, sgbm_ref, gq_ref, gk2_ref, gv_ref, gg_ref,
                 qm_ref):
    tm = x_ref.shape[0]
    nblk = tm // MOBA_BLOCK
    x = x_ref[...]
    ms = jnp.mean(x * x, axis=-1, keepdims=True)
    h = (x * lax.rsqrt(ms + EPS) * gpre_ref[...]).astype(BF16)

    def nat(cols):
        return _dot(h, wn_ref[:, cols[0]:cols[1]])

    k = nat(_N_K)
    qT = _dg(wt_ref[0:A_WIDTH, :], h, _NT)

    ss = _group_sumsq(k, bones_ref[...])
    kn = k * lax.rsqrt(ss * (1.0 / HEAD_DIM) + EPS) * gk_ref[...]
    kmean_ref[...] = jnp.mean(kn.reshape(nblk, MOBA_BLOCK, A_WIDTH), axis=1)
    e = jnp.concatenate([e_ref[...]] * nblk, axis=0)
    for hh in range(A_HEADS):
        rows = kn[:, hh * HEAD_DIM:(hh + 1) * HEAD_DIM]
        kaug_ref[hh] = jnp.concatenate([rows, e], axis=-1).astype(BF16)

    q3 = qT.reshape(A_HEADS, HEAD_DIM, tm)
    msq = jnp.mean(q3 * q3, axis=1, keepdims=True)
    qn = q3 * lax.rsqrt(msq + EPS) * gqcol_ref[...].reshape(1, HEAD_DIM, 1)
    qT_ref[...] = qn.reshape(A_WIDTH, tm)
    qs = (qn * (LOG2E * HEAD_DIM ** -0.5)).astype(BF16)
    blocks = [slice(t * MOBA_BLOCK, (t + 1) * MOBA_BLOCK) for t in range(nblk)]
    for hh in range(A_HEADS):
        for t, cols in enumerate(blocks):
            qa_ref[hh, t] = jnp.concatenate([qs[hh][:, cols], qx_ref[hh]], axis=0)

    vT = _dg(wt_ref[A_WIDTH:2 * A_WIDTH, :], h, _NT)
    ones_rows = jnp.where(lax.broadcasted_iota(jnp.int32, (V_AUG - HEAD_DIM, tm), 0) == 0, 1.0, 0.0)
    sga = _silu(_dg(wt_ref[2 * A_WIDTH:3 * A_WIDTH, :], h, _NT)).astype(BF16)
    for hh in range(A_HEADS):
        rows = slice(hh * HEAD_DIM, (hh + 1) * HEAD_DIM)
        vT_ref[hh] = jnp.concatenate([vT[rows], ones_rows], axis=0).astype(BF16)
        for t, cols in enumerate(blocks):
            sga_ref[hh, t] = sga[rows, cols]
    sgbm_ref[...] = _silu(nat(_N_GBM)).astype(BF16)

    qkb = nat((_N_QB[0], _N_KB[1]))
    gq_ref[...] = qkb[:, :G_KW] * (G_DK ** -0.5)
    gk2_ref[...] = qkb[:, G_KW:]
    rb = nat(_N_RB)
    gv_ref[...] = nat(_N_VB).astype(BF16)
    qm_ref[...] = nat(_N_QM)
    z = _dot(rb, wgu_ref[...], precision=HIGHEST) + bgu_ref[...]
    gg_ref[...] = (jnp.minimum(z, 0.0) - jnp.log(1.0 + jnp.exp(-jnp.abs(z)))) * (1.0 / GATE_TEMP)


def _proj_call(x, g_pre, wn, wt, gk, gqcol, bones, e, qx, wgu, bgu):
    bsz, s, d = x.shape
    tm = ROW_TILE
    nt = s // tm
    nblk = tm // MOBA_BLOCK
    nb = s // MOBA_BLOCK
    row = lambda w: pl.BlockSpec((None, tm, w), lambda b, t: (b, t, 0))
    in_specs = [
        row(d), _full((1, d)), _full(wn.shape), _full(wt.shape), _full((1, A_WIDTH)),
        _full((HEAD_DIM, 1)), _full((A_WIDTH, A_WIDTH)), _full(e.shape), _full(qx.shape), _full(wgu.shape),
        _full((1, G_KW)),
    ]
    group = MOBA_GROUP * MOBA_BLOCK
    tpg = group // tm
    blocked = lambda r: pl.BlockSpec((None, A_HEADS, nblk, r, MOBA_BLOCK), lambda b, t: (b, 0, t, 0, 0))
    out_specs = [
        pl.BlockSpec((None, A_WIDTH, tm), lambda b, t: (b, 0, t)),
        blocked(K_AUG),
        pl.BlockSpec((None, A_HEADS, tm, K_AUG), lambda b, t: (b, 0, t, 0)),
        pl.BlockSpec((None, A_HEADS, None, V_AUG, tm), lambda b, t: (b, 0, t // tpg, 0, t % tpg)),
        pl.BlockSpec((None, None, nblk, A_WIDTH), lambda b, t: (b, t, 0, 0)),
        blocked(HEAD_DIM),
        row(2 * M_WIDTH), row(G_KW), row(G_KW), row(G_VW), row(G_KW), row(M_WIDTH),
    ]
    out_shape = [
        jax.ShapeDtypeStruct((bsz, A_WIDTH, s), F32),
        jax.ShapeDtypeStruct((bsz, A_HEADS, nb, K_AUG, MOBA_BLOCK), BF16),
        jax.ShapeDtypeStruct((bsz, A_HEADS, s, K_AUG), BF16),
        jax.ShapeDtypeStruct((bsz, A_HEADS, s // group, V_AUG, group), BF16),
        jax.ShapeDtypeStruct((bsz, nt, nblk, A_WIDTH), F32),
        jax.ShapeDtypeStruct((bsz, A_HEADS, nb, HEAD_DIM, MOBA_BLOCK), BF16),
        jax.ShapeDtypeStruct((bsz, s, 2 * M_WIDTH), BF16),
        jax.ShapeDtypeStruct((bsz, s, G_KW), F32),
        jax.ShapeDtypeStruct((bsz, s, G_KW), F32),
        jax.ShapeDtypeStruct((bsz, s, G_VW), BF16),
        jax.ShapeDtypeStruct((bsz, s, G_KW), F32),
        jax.ShapeDtypeStruct((bsz, s, M_WIDTH), F32),
    ]
    return pl.pallas_call(
        _proj_kernel, grid=(bsz, nt), in_specs=in_specs, out_specs=out_specs, out_shape=out_shape,
        compiler_params=_params(("arbitrary", "arbitrary")), name="proj",
    )(x, g_pre, wn, wt, gk, gqcol, bones, e, qx, wgu, bgu)


def _sel_kernel(qT_ref, kmean_ref, o_ref):
    width = qT_ref.shape[1]
    gate = _dot(kmean_ref[...], qT_ref[...], precision=HIGHEST)
    nb = gate.shape[0]
    rowf = lax.broadcasted_iota(jnp.int32, gate.shape, 0).astype(F32)
    first = pl.program_id(2) * (width // MOBA_BLOCK)
    qblk = (first + lax.broadcasted_iota(jnp.int32, gate.shape, 1) // MOBA_BLOCK).astype(F32)
    g = jnp.where(rowf < qblk, gate, -jnp.inf)
    sel = jnp.zeros(gate.shape, F32)
    for _ in range(MOBA_TOPK):
        m = jnp.max(g, axis=0, keepdims=True)
        idx = jnp.min(jnp.where(g == m, rowf, float(nb)), axis=0, keepdims=True)
        hit = rowf == idx
        sel = jnp.where(hit, 1.0, sel)
        g = jnp.where(hit, -jnp.inf, g)
    sel = jnp.where(rowf < qblk, sel, 0.0)
    for u in range(width // MOBA_BLOCK):
        o_ref[u] = sel[:, u * MOBA_BLOCK:(u + 1) * MOBA_BLOCK]


def _sel_call(qT, kmean):
    bsz, _, s = qT.shape
    nb = s // MOBA_BLOCK
    width = min(SEL_WIDTH, s)
    return pl.pallas_call(
        _sel_kernel,
        grid=(bsz, A_HEADS, s // width),
        in_specs=[
            pl.BlockSpec((None, HEAD_DIM, width), lambda b, h, c: (b, h, c)),
            pl.BlockSpec((None, None, nb, HEAD_DIM), lambda b, h, c: (b, h, 0, 0)),
        ],
        out_specs=pl.BlockSpec((None, None, width // MOBA_BLOCK, nb, MOBA_BLOCK), lambda b, h, c: (b, h, c, 0, 0)),
        out_shape=jax.ShapeDtypeStruct((bsz, A_HEADS, nb, nb, MOBA_BLOCK), F32),
        compiler_params=_params(("arbitrary", "arbitrary", "arbitrary")),
        name="mobasel",
    )(qT, kmean)


def _moba_items(nb):
    qi, grp, first, valid = [], [], [], []
    for i in range(1, nb):
        for g in range((i - 1) // MOBA_GROUP + 1):
            qi.append(i); grp.append(g); first.append(int(g == 0)); valid.append(1)
    n_items = -(-len(qi) // MOBA_FAST_UNROLL) * MOBA_FAST_UNROLL
    for _ in range(n_items + 2 - len(qi)):
        qi.append(qi[-1]); grp.append(0); first.append(0); valid.append(0)
    tab = np.stack([qi, grp, first, valid]).astype(np.int32)
    return tab, n_items


def _moba_kernel(tab_ref, c1_ref, ub_ref, qa_ref, kall_ref, vall_ref, sel_ref, sga_ref, o_ref,
                 m_scr, acc_scr, out_scr, s0_ref, s1_ref, mb0_ref, mb1_ref, p0_ref, p1_ref, *, n_items):
    hh = pl.program_id(1)
    c1 = c1_ref[hh]
    ub = ub_ref[0]
    nb = qa_ref.shape[0]
    gk = MOBA_GROUP * MOBA_BLOCK
    half = gk // 2

    kk = lax.broadcasted_iota(jnp.int32, (MOBA_BLOCK, MOBA_BLOCK), 0)
    qq = lax.broadcasted_iota(jnp.int32, (MOBA_BLOCK, MOBA_BLOCK), 1)

    def own_scores(i, s_ref, mb_ref):
        i = jnp.minimum(i, nb - 1)
        r0 = pl.multiple_of(i * MOBA_BLOCK, MOBA_BLOCK)
        s = _dot(kall_ref[pl.ds(r0, MOBA_BLOCK), :], qa_ref[i])
        s = jnp.where(kk <= qq, s, NEG_BIG)
        s_ref[0:MOBA_BLOCK, :] = s
        mb_ref[0] = jnp.max(s, axis=0, keepdims=True)

    def own_softmax(i, s_ref, mb_ref, p_ref):
        i = jnp.minimum(i, nb - 1)
        m0 = mb_ref[0]
        m_scr[i] = jnp.broadcast_to(m0, m_scr.shape[1:])
        p_ref[0:MOBA_BLOCK, :] = jnp.exp2(s_ref[0:MOBA_BLOCK, :] - m0).astype(BF16)

    def own_pv(t, k, p_ref):
        vown = vall_ref[t][:, k * MOBA_BLOCK:(k + 1) * MOBA_BLOCK]
        acc_scr[MOBA_GROUP * t + k] = _dot(vown, p_ref[0:MOBA_BLOCK, :])

    bufs = ((s0_ref, mb0_ref, p0_ref), (s1_ref, mb1_ref, p1_ref))
    own_scores(0, s0_ref, mb0_ref)
    own_scores(1, s1_ref, mb1_ref)
    own_softmax(0, s0_ref, mb0_ref, p0_ref)

    def own_body(t, carry):
        for k in range(MOBA_GROUP):
            i = MOBA_GROUP * t + k
            s_a, mb_a, p_a = bufs[k % 2]
            s_b, mb_b, p_b = bufs[(k + 1) % 2]
            own_scores(i + 2, s_a, mb_a)
            own_softmax(i + 1, s_b, mb_b, p_b)
            own_pv(t, k, p_a)
        return carry

    lax.fori_loop(0, nb // MOBA_GROUP, own_body, 0)

    def item(n):
        return tab_ref[0, n], tab_ref[1, n], tab_ref[2, n], tab_ref[3, n]

    def scores(n, s_ref, mb_ref):
        i, g, _, _ = item(n)
        for part in range(2):
            r0 = pl.multiple_of(g * gk + part * half, half)
            s = _dot(kall_ref[pl.ds(r0, half), :], qa_ref[i])
            s_ref[part * half:(part + 1) * half, :] = s
            for u in range(MOBA_GROUP // 2):
                blk = s[u * MOBA_BLOCK:(u + 1) * MOBA_BLOCK]
                mb_ref[part * (MOBA_GROUP // 2) + u] = jnp.max(blk, axis=0, keepdims=True)

    def softmax(n, s_ref, mb_ref, p_ref, m):
        i, g, first, valid = item(n)
        m = jnp.where(first == 1, m_scr[i][0:1], m)
        m_new = m
        members = []
        for u in range(MOBA_GROUP):
            j = g * MOBA_GROUP + u
            picked = jnp.where(valid == 1, sel_ref[i, pl.ds(j, 1), :], 0.0) > 0.5
            cj = c1 * (MOBA_BLOCK * (j - i)).astype(F32)
            m_new = jnp.where(picked, jnp.maximum(m_new, mb_ref[u] + cj), m_new)
            members.append((picked, cj))
        for u, (picked, cj) in enumerate(members):
            shift = jnp.where(picked, m_new - cj, POS_BIG)
            rows = slice(u * MOBA_BLOCK, (u + 1) * MOBA_BLOCK)
            p_ref[rows, :] = jnp.exp2(s_ref[rows, :] - shift).astype(BF16)
        return m_new, jnp.exp2(m - m_new)

    def accumulate(n, p_ref, alpha, acc):
        i, g, first, _ = item(n)
        acc = jnp.where(first == 1, acc_scr[i], acc) * alpha + _dot(vall_ref[g], p_ref[...])
        acc_scr[i] = acc
        return acc

    def fast_probs(n, p_ref):
        i, g, _, valid = item(n)
        sigma = jnp.maximum(m_scr[i][0:1], ub)
        for part in range(2):
            r0 = pl.multiple_of(g * gk + part * half, half)
            s = _dot(kall_ref[pl.ds(r0, half), :], qa_ref[i])
            for u in range(MOBA_GROUP // 2):
                j = g * MOBA_GROUP + part * (MOBA_GROUP // 2) + u
                picked = jnp.where(valid == 1, sel_ref[i, pl.ds(j, 1), :], 0.0) > 0.5
                cj = c1 * (MOBA_BLOCK * (j - i)).astype(F32)
                shift = jnp.where(picked, sigma - cj, POS_BIG)
                rows = slice(part * half + u * MOBA_BLOCK, part * half + (u + 1) * MOBA_BLOCK)
                p_ref[rows, :] = jnp.exp2(s[u * MOBA_BLOCK:(u + 1) * MOBA_BLOCK] - shift).astype(BF16)

    def fast_accumulate(n, p_ref, acc):
        i, g, first, _ = item(n)
        m0 = m_scr[i][0:1]
        own = acc_scr[i] * jnp.exp2(m0 - jnp.maximum(m0, ub))
        acc = jnp.where(first == 1, own, acc) + _dot(vall_ref[g], p_ref[...])
        out_scr[i] = acc
        return acc

    out_scr[0] = acc_scr[0]
    fast_probs(0, p0_ref)

    def fast_body(t, acc):
        for k in range(MOBA_FAST_UNROLL):
            n = MOBA_FAST_UNROLL * t + k
            fast_probs(n + 1, bufs[(k + 1) % 2][2])
            acc = fast_accumulate(n, bufs[k % 2][2], acc)
        return acc

    lax.fori_loop(0, n_items // MOBA_FAST_UNROLL, fast_body, jnp.zeros((V_AUG, MOBA_BLOCK), F32))

    den = out_scr[:, HEAD_DIM:HEAD_DIM + 1, :]
    in_range = jnp.logical_and(jnp.min(den) > DEN_MIN, jnp.max(den) < DEN_MAX)

    @pl.when(jnp.logical_not(in_range))
    def _():
        scores(0, s0_ref, mb0_ref)
        scores(1, s1_ref, mb1_ref)
        m_init = jnp.full((1, MOBA_BLOCK), NEG_BIG, F32)
        m, alpha = softmax(0, s0_ref, mb0_ref, p0_ref, m_init)
        acc = jnp.zeros((V_AUG, MOBA_BLOCK), F32)

        def body(t, carry):
            m, alpha, acc = carry
            for k in range(MOBA_UNROLL):
                n = MOBA_UNROLL * t + k
                s_a, mb_a, p_a = bufs[k % 2]
                s_b, mb_b, p_b = bufs[(k + 1) % 2]
                acc = accumulate(n, p_a, alpha, acc)
                m, alpha = softmax(n + 1, s_b, mb_b, p_b, m)
                scores(n + 2, s_a, mb_a)
            return m, alpha, acc

        lax.fori_loop(0, n_items // MOBA_UNROLL, body, (m, alpha, acc))
        out_scr[...] = acc_scr[...]

    def finish(i, carry):
        acc = out_scr[i]
        o = acc[:HEAD_DIM] / acc[HEAD_DIM:HEAD_DIM + 1]
        o_ref[i] = (o * sga_ref[i].astype(F32)).astype(BF16)
        return carry

    lax.fori_loop(0, nb, finish, 0, unroll=4)


def _moba_call(c1, ub, qa, kaug, vT, sel, sga):
    bsz, _, nb = qa.shape[:3]
    s = nb * MOBA_BLOCK
    gk = MOBA_GROUP * MOBA_BLOCK
    assert nb % MOBA_GROUP == 0 and MOBA_GROUP % 2 == 0 and MOBA_UNROLL % 2 == 0
    assert MOBA_FAST_UNROLL % MOBA_UNROLL == 0
    tab, n_items = _moba_items(nb)
    per_bh = lambda *tail: pl.BlockSpec((None, None) + tail, lambda b, h: (b, h) + (0,) * len(tail))
    return pl.pallas_call(
        functools.partial(_moba_kernel, n_items=n_items),
        grid=(bsz, A_HEADS),
        in_specs=[
            pl.BlockSpec(memory_space=pltpu.SMEM),
            pl.BlockSpec(memory_space=pltpu.SMEM),
            pl.BlockSpec(memory_space=pltpu.SMEM),
            per_bh(nb, K_AUG, MOBA_BLOCK),
            per_bh(s, K_AUG),
            per_bh(nb // MOBA_GROUP, V_AUG, gk),
            per_bh(nb, nb, MOBA_BLOCK),
            per_bh(nb, HEAD_DIM, MOBA_BLOCK),
        ],
        out_specs=per_bh(nb, HEAD_DIM, MOBA_BLOCK),
        out_shape=jax.ShapeDtypeStruct((bsz, A_HEADS, nb, HEAD_DIM, MOBA_BLOCK), BF16),
        scratch_shapes=[pltpu.VMEM((nb, 8, MOBA_BLOCK), F32), pltpu.VMEM((nb, V_AUG, MOBA_BLOCK), F32),
                        pltpu.VMEM((nb, V_AUG, MOBA_BLOCK), F32),
                        pltpu.VMEM((gk, MOBA_BLOCK), F32), pltpu.VMEM((gk, MOBA_BLOCK), F32),
                        pltpu.VMEM((MOBA_GROUP, 1, MOBA_BLOCK), F32), pltpu.VMEM((MOBA_GROUP, 1, MOBA_BLOCK), F32),
                        pltpu.VMEM((gk, MOBA_BLOCK), BF16), pltpu.VMEM((gk, MOBA_BLOCK), BF16)],
        compiler_params=pltpu.CompilerParams(dimension_semantics=("arbitrary", "arbitrary"),
                                             vmem_limit_bytes=MOBA_VMEM_LIMIT),
        name="moba",
    )(jnp.asarray(tab), c1, ub, qa, kaug, vT, sel, sga)


def _gla_consts():
    c, sb = GLA_CHUNK, GLA_SUB
    r = np.arange(c)
    blk = r // sb
    tri = (r[None, :] <= r[:, None])
    same = blk[None, :] == blk[:, None]
    lall = np.concatenate([
        tri,
        tri & same,
        same,
        blk[None, :] == blk[:, None] - 1,
        blk[None, :] == blk[:, None] - 2,
    ], axis=0).astype(np.float32)
    diff = blk[:, None] - blk[None, :]
    band = np.stack([diff == 1, diff == 2, diff == 3, same & tri]).astype(np.float32)
    dk_head = np.arange(G_KW) // G_DK
    dv_head = np.arange(G_VW) // G_DV
    bdt = (dv_head[:, None] == dk_head[None, :]).astype(np.float32)
    lex, mex = [tri], [np.eye(c, dtype=bool)]
    for s in GLA_LEVELS:
        same_s = (r[None, :] // s) == (r[:, None] // s)
        lex += [tri & same_s, same_s]
        pair = (r[None, :] // (2 * s)) == (r[:, None] // (2 * s))
        mex.append(pair & ((r[:, None] // s) % 2 == 1) & ((r[None, :] // s) % 2 == 0))
    lex = np.concatenate(lex, axis=0).astype(np.float32)
    mex = np.stack(mex).astype(np.float32)
    return lall, band, bdt, lex, mex


def _gla_kernel(q_ref, k_ref, v_ref, g_ref, sgb_ref, lall_ref, band_ref, bdt_ref, bones_ref, gout_ref,
                lex_ref, mex_ref, o_ref, st_ref):
    c = GLA_CHUNK
    nbatch = q_ref.shape[0]

    @pl.when(pl.program_id(0) == 0)
    def _():
        st_ref[...] = jnp.zeros_like(st_ref)

    lane_k = lax.broadcasted_iota(jnp.int32, (1, G_KW), 1) // G_DK
    lane_v = lax.broadcasted_iota(jnp.int32, (1, G_VW), 1) // G_DV
    n_chunks = q_ref.shape[1] // c

    def heads(t):
        return jnp.concatenate([jnp.where(lane_k == hd, t, 0.0) for hd in range(G_HEADS)],
                               axis=0).astype(BF16)

    def chunk(ci, states):
        r0 = pl.multiple_of(ci * c, c)
        nbs = range(nbatch)
        q = [q_ref[bi, pl.ds(r0, c), :] for bi in nbs]
        k = [k_ref[bi, pl.ds(r0, c), :] for bi in nbs]
        v = [v_ref[bi, pl.ds(r0, c), :] for bi in nbs]

        r = [_dot(lall_ref[...], jnp.concatenate(_split3(g_ref[bi, pl.ds(r0, c), :]), axis=1)) for bi in nbs]
        r = [x[:, :G_KW] + x[:, G_KW:2 * G_KW] + x[:, 2 * G_KW:] for x in r]
        b, cc, tt, p1, p2 = ([x[n * c:(n + 1) * c] for x in r] for n in range(5))
        b_last = [x[c - 1:c] for x in b]

        qt = [q[bi] * jnp.exp(cc[bi]) for bi in nbs]
        q2 = [qt[bi] * jnp.exp(p1[bi]) for bi in nbs]
        q3 = [q2[bi] * jnp.exp(p2[bi]) for bi in nbs]
        kt = [(k[bi] * jnp.exp(tt[bi] - cc[bi])).astype(BF16) for bi in nbs]
        kd = [(k[bi] * jnp.exp(-cc[bi])).astype(BF16) for bi in nbs]
        qi = [(q[bi] * jnp.exp(b[bi])).astype(BF16) for bi in nbs]
        kl = [(k[bi] * jnp.exp(b_last[bi] - b[bi])).astype(BF16) for bi in nbs]

        xs = [[_dg(heads(t[bi]), kt[bi], _NT) for bi in nbs] for t in (qt, q2, q3)]
        yd = [_dg(heads(qt[bi]), kd[bi], _NT) for bi in nbs]
        o = [_dg(qi[bi], states[bi].astype(BF16), _NT) for bi in nbs]
        st_new = tuple(states[bi] * jnp.exp(b_last[bi]) + _dg(v[bi], kl[bi], _TN) * bdt_ref[...]
                       for bi in nbs)
        for hd in range(G_HEADS):
            sl = slice(hd * c, (hd + 1) * c)
            for bi in nbs:
                att = (band_ref[0] * xs[0][bi][sl] + band_ref[1] * xs[1][bi][sl]
                       + band_ref[2] * xs[2][bi][sl] + band_ref[3] * yd[bi][sl])
                vh = jnp.where(lane_v == hd, v[bi], jnp.zeros_like(v[bi]))
                o[bi] = o[bi] + _dot(att.astype(BF16), vh)

        ss = [_group_sumsq(o[bi], bones_ref[...]) for bi in nbs]
        for bi in nbs:
            y = o[bi] * lax.rsqrt(ss[bi] * (1.0 / G_DV) + EPS) * gout_ref[...]
            o_ref[bi, pl.ds(r0, c), :] = (y * sgb_ref[bi, pl.ds(r0, c), :].astype(F32)).astype(BF16)
        return st_new

    def chunk_stable(ci, states):
        r0 = pl.multiple_of(ci * c, c)
        out = []
        for bi in range(nbatch):
            q = q_ref[bi, pl.ds(r0, c), :]
            k = k_ref[bi, pl.ds(r0, c), :]
            v = v_ref[bi, pl.ds(r0, c), :]
            r = _dot(lex_ref[...], jnp.concatenate(_split3(g_ref[bi, pl.ds(r0, c), :]), axis=1))
            r = r[:, :G_KW] + r[:, G_KW:2 * G_KW] + r[:, 2 * G_KW:]
            b = r[0:c]
            b_last = b[c - 1:c]
            x = [_dg(heads(q), k.astype(BF16), _NT)]
            for lv in range(len(GLA_LEVELS)):
                cs = r[(1 + 2 * lv) * c:(2 + 2 * lv) * c]
                ts = r[(2 + 2 * lv) * c:(3 + 2 * lv) * c]
                x.append(_dg(heads(q * jnp.exp(cs)), (k * jnp.exp(ts - cs)).astype(BF16), _NT))
            o = _dg((q * jnp.exp(b)).astype(BF16), states[bi].astype(BF16), _NT)
            for hd in range(G_HEADS):
                sl = slice(hd * c, (hd + 1) * c)
                att = mex_ref[0] * x[0][sl]
                for lv in range(len(GLA_LEVELS)):
                    att = att + mex_ref[lv + 1] * x[lv + 1][sl]
                o = o + _dot(att.astype(BF16), jnp.where(lane_v == hd, v, jnp.zeros_like(v)))
            kl = (k * jnp.exp(b_last - b)).astype(BF16)
            out.append(states[bi] * jnp.exp(b_last) + _dg(v, kl, _TN) * bdt_ref[...])
            ss = _group_sumsq(o, bones_ref[...])
            y = o * lax.rsqrt(ss * (1.0 / G_DV) + EPS) * gout_ref[...]
            o_ref[bi, pl.ds(r0, c), :] = (y * sgb_ref[bi, pl.ds(r0, c), :].astype(F32)).astype(BF16)
        return tuple(out)

    def run(body, unroll):
        states = lax.fori_loop(0, n_chunks, body, tuple(st_ref[bi] for bi in range(nbatch)), unroll=unroll)
        for bi in range(nbatch):
            st_ref[bi] = states[bi]

    risky = jnp.max(-g_ref[...]) * GLA_SUB > GLA_SAFE_EXP
    pl.when(jnp.logical_not(risky))(lambda: run(chunk, 2))
    pl.when(risky)(lambda: run(chunk_stable, 1))


def _gla_call(gq, gk, gv, gg, sgbm, lall, band, bdt, bones, gout, lex, mex):
    bsz, s, _ = gq.shape
    tm = ROW_TILE
    row = lambda w: pl.BlockSpec((bsz, tm, w), lambda t: (0, t, 0))
    return pl.pallas_call(
        _gla_kernel,
        grid=(s // tm,),
        in_specs=[row(G_KW), row(G_KW), row(G_VW), row(G_KW), row(G_VW),
                  _full(lall.shape), _full(band.shape), _full(bdt.shape), _full(bones.shape), _full((1, G_VW)),
                  _full(lex.shape), _full(mex.shape)],
        out_specs=row(G_VW),
        out_shape=jax.ShapeDtypeStruct((bsz, s, G_VW), BF16),
        scratch_shapes=[pltpu.VMEM((bsz, G_VW, G_KW), F32)],
        compiler_params=_params(("arbitrary",)),
        name="gla",
    )(gq, gk, gv, gg, sgbm, lall, band, bdt, bones, gout, lex, mex)


def _mem_kernel(qm_ref, sgm_ref, mk_ref, mvx_ref, gq_ref, bones_ref, o_ref):
    qm = qm_ref[...]
    ss = _group_sumsq(qm, bones_ref[...])
    qn = qm * lax.rsqrt(ss * (1.0 / HEAD_DIM) + EPS) * gq_ref[...]
    qs = qn * (LOG2E * HEAD_DIM ** -0.5)
    lane = lax.broadcasted_iota(jnp.int32, (1, M_WIDTH), 1) // HEAD_DIM
    mk = mk_ref[...]
    s = [_dg(jnp.where(lane == h, qs, 0.0).astype(BF16), mk, _NT) for h in range(M_HEADS)]
    p = [jnp.exp2(x - jnp.max(x, axis=-1, keepdims=True)).astype(BF16) for x in s]
    acc = _dot(p[0], mvx_ref[0])
    for h in range(1, M_HEADS):
        acc = acc + _dot(p[h], mvx_ref[h])
    om = acc[:, :M_WIDTH] / acc[:, M_WIDTH:]
    o_ref[...] = (om * sgm_ref[...].astype(F32)).astype(BF16)


def _mem_call(qm, sgbm, mk, mvx, gq, bones):
    bsz, s, _ = qm.shape
    tm = ROW_TILE
    mlen = mk.shape[1]
    return pl.pallas_call(
        _mem_kernel,
        grid=(bsz, s // tm),
        in_specs=[
            pl.BlockSpec((None, tm, M_WIDTH), lambda b, t: (b, t, 0)),
            pl.BlockSpec((None, tm, M_WIDTH), lambda b, t: (b, t, 1)),
            pl.BlockSpec((None, mlen, M_WIDTH), lambda b, t: (b, 0, 0)),
            pl.BlockSpec((None, M_HEADS, mlen, 2 * M_WIDTH), lambda b, t: (b, 0, 0, 0)),
            _full((1, M_WIDTH)), _full((M_WIDTH, M_WIDTH)),
        ],
        out_specs=pl.BlockSpec((None, tm, M_WIDTH), lambda b, t: (b, t, 0)),
        out_shape=jax.ShapeDtypeStruct((bsz, s, M_WIDTH), BF16),
        compiler_params=_params(("arbitrary", "arbitrary")),
        name="memattn",
    )(qm, sgbm, mk, mvx, gq, bones)


def _out_kernel(x_ref, oaT_ref, ob_ref, om_ref, wa_ref, wb_ref, o_ref):
    y = _dot(jnp.concatenate([ob_ref[...], om_ref[...]], axis=-1), wb_ref[...])
    for t in range(oaT_ref.shape[1]):
        rows = slice(t * MOBA_BLOCK, (t + 1) * MOBA_BLOCK)
        oaT = oaT_ref[:, t].reshape(A_WIDTH, MOBA_BLOCK)
        o_ref[rows, :] = x_ref[rows, :] + y[rows] + _dg(oaT, wa_ref[...], _TN)


def _out_call(x, oaT, ob, om, wa, wb):
    bsz, s, d = x.shape
    tm = ROW_TILE
    row = lambda w: pl.BlockSpec((None, tm, w), lambda b, t: (b, t, 0))
    return pl.pallas_call(
        _out_kernel,
        grid=(bsz, s // tm),
        in_specs=[row(d),
                  pl.BlockSpec((None, A_HEADS, tm // MOBA_BLOCK, HEAD_DIM, MOBA_BLOCK), lambda b, t: (b, 0, t, 0, 0)),
                  row(G_VW), row(M_WIDTH),
                  _full(wa.shape), _full(wb.shape)],
        out_specs=row(d),
        out_shape=jax.ShapeDtypeStruct((bsz, s, d), F32),
        compiler_params=_params(("arbitrary", "arbitrary")),
        name="outproj",
    )(x, oaT, ob, om, wa, wb)


def _alibi_consts():
    slopes = np.asarray([2.0 ** (-8.0 * (i + 1) / A_HEADS) for i in range(A_HEADS)], np.float32)
    c1 = (slopes * np.float32(LOG2E)).astype(np.float32)
    c1j = jnp.asarray(c1)
    pieces = list(_split3(c1j * 16.0)) + list(_split3(c1j))
    qx = jnp.zeros((A_HEADS, K_AUG - HEAD_DIM, MOBA_BLOCK), BF16)
    for n, pc in enumerate(pieces):
        qx = qx.at[:, n, :].set(jnp.broadcast_to(pc[:, None], (A_HEADS, MOBA_BLOCK)))
    pos = np.arange(MOBA_BLOCK)
    e = np.zeros((MOBA_BLOCK, K_AUG - HEAD_DIM), np.float32)
    e[:, 0:3] = (pos // 16)[:, None]
    e[:, 3:6] = (pos % 16)[:, None]
    return c1j, qx, jnp.asarray(e)


def kernel(x, mem, g_pre, w_in, g_q_moba, g_k_moba, w_gate_up, b_gate_up, g_gla_out,
           g_mem, w_mem_kv, g_q_mem, g_k_mem, w_out):
    depth = g_pre.shape[0]
    d = x.shape[-1]
    c1, qx, e = _alibi_consts()
    lall_np, band_np, bdt_np, lex_np, mex_np = _gla_consts()
    lall = jnp.asarray(lall_np, BF16)
    band = jnp.asarray(band_np)
    bdt = jnp.asarray(bdt_np)
    lex = jnp.asarray(lex_np, BF16)
    mex = jnp.asarray(mex_np)
    bones_a = jnp.asarray(_block_ones(A_WIDTH, HEAD_DIM), BF16)
    bones_m = jnp.asarray(_block_ones(M_WIDTH, HEAD_DIM), BF16)

    o_qa, o_ka, o_va, o_ga = 0, 512, 1024, 1536
    o_qb, o_kb, o_vb, o_gb, o_rb = 2048, 2176, 2304, 2560, 2816
    o_qm, o_gm = 2832, 3088

    mk_all, mvx_all = _memkv_call(mem, g_mem, w_mem_kv, g_k_mem)

    for l in range(depth):
        w = w_in[l]
        col = lambda o, n: w[:, o:o + n]
        rb_pad = jnp.pad(col(o_rb, GATE_RANK), ((0, 0), (0, G_KW - GATE_RANK)))
        wn = jnp.concatenate([col(o_ka, 512), col(o_gb, 256), col(o_gm, 256), col(o_qb, 128), col(o_kb, 128),
                              col(o_vb, 256), rb_pad, col(o_qm, 256)], axis=1).astype(BF16)
        wt = jnp.concatenate([col(o_qa, 512), col(o_va, 512), col(o_ga, 512)], axis=1).T.astype(BF16)
        wgu = jnp.pad(w_gate_up[l], ((0, G_KW - GATE_RANK), (0, 0)))
        gk = jnp.tile(g_k_moba[l], A_HEADS).reshape(1, A_WIDTH)

        (qT, qa, kaug, vT, kmean_nat, sga, sgbm, gq, gk2, gv, gg, qm) = _proj_call(
            x, g_pre[l].reshape(1, d), wn, wt, gk, g_q_moba[l].reshape(HEAD_DIM, 1), bones_a, e, qx,
            wgu, b_gate_up[l].reshape(1, G_KW))

        bsz, s = x.shape[0], x.shape[1]
        nb = s // MOBA_BLOCK
        kmean = kmean_nat.reshape(bsz, nb, A_HEADS, HEAD_DIM).transpose(0, 2, 1, 3)
        ub = (UB_SLACK * LOG2E * HEAD_DIM ** 0.5) * jnp.max(jnp.abs(g_q_moba[l])) * jnp.max(jnp.abs(g_k_moba[l]))
        oaT = _moba_call(c1, ub.reshape(1), qa, kaug, vT, _sel_call(qT, kmean), sga)
        ob = _gla_call(gq, gk2, gv, gg, sgbm, lall, band, bdt, bones_m,
                       jnp.tile(g_gla_out[l], G_HEADS).reshape(1, G_VW), lex, mex)
        om = _mem_call(qm, sgbm, mk_all[l], mvx_all[l],
                       jnp.tile(g_q_mem[l], M_HEADS).reshape(1, M_WIDTH), bones_m)
        wo = w_out[l].astype(BF16)
        x = _out_call(x, oaT, ob, om, wo[:A_WIDTH], wo[A_WIDTH:])
    return x
```

```python
import functools

import numpy as np
import jax
import jax.numpy as jnp
from jax import lax
from jax.experimental import pallas as pl
from jax.experimental.pallas import tpu as pltpu

F32 = jnp.float32
BF16 = jnp.bfloat16
HIGHEST = lax.Precision.HIGHEST

EPS = 1e-6
LOG2E = 1.4426950408889634
HEAD_DIM = 64
A_HEADS = 8
A_WIDTH = A_HEADS * HEAD_DIM
MOBA_BLOCK = 256
MOBA_TOPK = 3
MOBA_GROUP = 4
MOBA_UNROLL = 4
MOBA_FAST_UNROLL = 16
SEL_WIDTH = 2048
G_HEADS = 4
G_DK = 32
G_DV = 64
G_KW = G_HEADS * G_DK
G_VW = G_HEADS * G_DV
GATE_RANK = 16
GATE_TEMP = 16.0
GLA_CHUNK = 64
GLA_SUB = 16
GLA_LEVELS = (32, 16, 8, 4, 2, 1)
GLA_SAFE_EXP = 80.0
M_HEADS = 4
M_WIDTH = M_HEADS * HEAD_DIM
K_AUG = 128
V_AUG = 80
ROW_TILE = 512
V7X_VMEM_BYTES = 64 * 1024 * 1024
VMEM_LIMIT = 3 * V7X_VMEM_BYTES // 4
MOBA_VMEM_LIMIT = 29 * V7X_VMEM_BYTES // 32

NEG_BIG = -1e30
POS_BIG = 1e30
DEN_MIN = 1e-18
DEN_MAX = 1e30
UB_SLACK = 1.02

_NT = (((1,), (1,)), ((), ()))
_TN = (((0,), (0,)), ((), ()))


def _dot(a, b, **kw):
    return jnp.dot(a, b, preferred_element_type=F32, **kw)


def _dg(a, b, dims, **kw):
    return lax.dot_general(a, b, dims, preferred_element_type=F32, **kw)


def _split2(v):
    hi = v.astype(BF16)
    lo = (v - hi.astype(F32)).astype(BF16)
    return hi, lo


def _split3(v):
    hi = v.astype(BF16)
    r = v - hi.astype(F32)
    mid = r.astype(BF16)
    lo = (r - mid.astype(F32)).astype(BF16)
    return hi, mid, lo


def _group_sumsq(v, bones):
    hi, lo = _split2(v * v)
    return _dot(hi, bones) + _dot(lo, bones)


def _silu(v):
    return v / (1.0 + jnp.exp(-v))


def _block_ones(n, g):
    i = np.arange(n) // g
    return (i[:, None] == i[None, :]).astype(np.float32)


def _params(sem):
    return pltpu.CompilerParams(dimension_semantics=sem, vmem_limit_bytes=VMEM_LIMIT)


def _full(shape):
    return pl.BlockSpec(shape, lambda *_: (0,) * len(shape))


def _memkv_kernel(mem_ref, gmem_ref, w_ref, gk_ref, bones_ref, mk_ref, mvx_ref):
    m = mem_ref[...]
    ms = jnp.mean(m * m, axis=-1, keepdims=True)
    hn = (m * lax.rsqrt(ms + EPS) * gmem_ref[...]).astype(BF16)
    kv = _dot(hn, w_ref[...])
    mk = kv[:, :M_WIDTH]
    mv = kv[:, M_WIDTH:]
    ss = _group_sumsq(mk, bones_ref[...])
    mk_ref[...] = (mk * lax.rsqrt(ss * (1.0 / HEAD_DIM) + EPS) * gk_ref[...]).astype(BF16)
    lane = lax.broadcasted_iota(jnp.int32, mv.shape, 1) // HEAD_DIM
    for h in range(M_HEADS):
        hm = lane == h
        mvx_ref[h] = jnp.concatenate(
            [jnp.where(hm, mv, 0.0), jnp.where(hm, 1.0, 0.0)], axis=-1).astype(BF16)


def _memkv_call(mem, g_mem, w_mem_kv, g_k_mem):
    depth = g_mem.shape[0]
    bsz, mlen, d = mem.shape
    bones = jnp.asarray(_block_ones(M_WIDTH, HEAD_DIM), BF16)
    gk = jnp.tile(g_k_mem, (1, M_HEADS)).reshape(depth, 1, M_WIDTH)
    return pl.pallas_call(
        _memkv_kernel,
        grid=(depth, bsz),
        in_specs=[
            pl.BlockSpec((None, mlen, d), lambda l, b: (b, 0, 0)),
            pl.BlockSpec((None, 1, d), lambda l, b: (l, 0, 0)),
            pl.BlockSpec((None, d, 2 * M_WIDTH), lambda l, b: (l, 0, 0)),
            pl.BlockSpec((None, 1, M_WIDTH), lambda l, b: (l, 0, 0)),
            _full((M_WIDTH, M_WIDTH)),
        ],
        out_specs=[
            pl.BlockSpec((None, None, mlen, M_WIDTH), lambda l, b: (l, b, 0, 0)),
            pl.BlockSpec((None, None, M_HEADS, mlen, 2 * M_WIDTH), lambda l, b: (l, b, 0, 0, 0)),
        ],
        out_shape=[
            jax.ShapeDtypeStruct((depth, bsz, mlen, M_WIDTH), BF16),
            jax.ShapeDtypeStruct((depth, bsz, M_HEADS, mlen, 2 * M_WIDTH), BF16),
        ],
        compiler_params=_params(("arbitrary", "arbitrary")),
        name="memkv",
    )(mem, g_mem.reshape(depth, 1, d), w_mem_kv.astype(BF16), gk, bones)


_N_K = (0, 512)
_N_GBM = (512, 1024)
_N_QB = (1024, 1152)
_N_KB = (1152, 1280)
_N_VB = (1280, 1536)
_N_RB = (1536, 1664)
_N_QM = (1664, 1920)


def _proj_kernel(x_ref, gpre_ref, wn_ref, wt_ref, gk_ref, gqcol_ref, bones_ref, e_ref, qx_ref, wgu_ref, bgu_ref,
                 qT_ref, qa_ref, kaug_ref, vT_ref, kmean_ref, sga_ref, sgbm_ref, gq_ref, gk2_ref, gv_ref, gg_ref,
                 qm_ref):
    tm = x_ref.shape[0]
    nblk = tm // MOBA_BLOCK
    x = x_ref[...]
    ms = jnp.mean(x * x, axis=-1, keepdims=True)
    h = (x * lax.rsqrt(ms + EPS) * gpre_ref[...]).astype(BF16)

    def nat(cols):
        return _dot(h, wn_ref[:, cols[0]:cols[1]])

    k = nat(_N_K)
    qT = _dg(wt_ref[0:A_WIDTH, :], h, _NT)

    ss = _group_sumsq(k, bones_ref[...])
    kn = k * lax.rsqrt(ss * (1.0 / HEAD_DIM) + EPS) * gk_ref[...]
    kmean_ref[...] = jnp.mean(kn.reshape(nblk, MOBA_BLOCK, A_WIDTH), axis=1)
    e = jnp.concatenate([e_ref[...]] * nblk, axis=0)
    for hh in range(A_HEADS):
        rows = kn[:, hh * HEAD_DIM:(hh + 1) * HEAD_DIM]
        kaug_ref[hh] = jnp.concatenate([rows, e], axis=-1).astype(BF16)

    q3 = qT.reshape(A_HEADS, HEAD_DIM, tm)
    msq = jnp.mean(q3 * q3, axis=1, keepdims=True)
    qn = q3 * lax.rsqrt(msq + EPS) * gqcol_ref[...].reshape(1, HEAD_DIM, 1)
    qT_ref[...] = qn.reshape(A_WIDTH, tm)
    qs = (qn * (LOG2E * HEAD_DIM ** -0.5)).astype(BF16)
    blocks = [slice(t * MOBA_BLOCK, (t + 1) * MOBA_BLOCK) for t in range(nblk)]
    for hh in range(A_HEADS):
        for t, cols in enumerate(blocks):
            qa_ref[hh, t] = jnp.concatenate([qs[hh][:, cols], qx_ref[hh]], axis=0)

    vT = _dg(wt_ref[A_WIDTH:2 * A_WIDTH, :], h, _NT)
    ones_rows = jnp.where(lax.broadcasted_iota(jnp.int32, (V_AUG - HEAD_DIM, tm), 0) == 0, 1.0, 0.0)
    sga = _silu(_dg(wt_ref[2 * A_WIDTH:3 * A_WIDTH, :], h, _NT)).astype(BF16)
    for hh in range(A_HEADS):
        rows = slice(hh * HEAD_DIM, (hh + 1) * HEAD_DIM)
        vT_ref[hh] = jnp.concatenate([vT[rows], ones_rows], axis=0).astype(BF16)
        for t, cols in enumerate(blocks):
            sga_ref[hh, t] = sga[rows, cols]
    sgbm_ref[...] = _silu(nat(_N_GBM)).astype(BF16)

    qkb = nat((_N_QB[0], _N_KB[1]))
    gq_ref[...] = qkb[:, :G_KW] * (G_DK ** -0.5)
    gk2_ref[...] = qkb[:, G_KW:]
    rb = nat(_N_RB)
    gv_ref[...] = nat(_N_VB).astype(BF16)
    qm_ref[...] = nat(_N_QM)
    z = _dot(rb, wgu_ref[...], precision=HIGHEST) + bgu_ref[...]
    gg_ref[...] = (jnp.minimum(z, 0.0) - jnp.log(1.0 + jnp.exp(-jnp.abs(z)))) * (1.0 / GATE_TEMP)


def _proj_call(x, g_pre, wn, wt, gk, gqcol, bones, e, qx, wgu, bgu):
    bsz, s, d = x.shape
    tm = ROW_TILE
    nt = s // tm
    nblk = tm // MOBA_BLOCK
    nb = s // MOBA_BLOCK
    row = lambda w: pl.BlockSpec((None, tm, w), lambda b, t: (b, t, 0))
    in_specs = [
        row(d), _full((1, d)), _full(wn.shape), _full(wt.shape), _full((1, A_WIDTH)),
        _full((HEAD_DIM, 1)), _full((A_WIDTH, A_WIDTH)), _full(e.shape), _full(qx.shape), _full(wgu.shape),
        _full((1, G_KW)),
    ]
    group = MOBA_GROUP * MOBA_BLOCK
    assert d == wn.shape[0] and s % group == 0 and group % tm == 0 and tm % MOBA_BLOCK == 0
    tpg = group // tm
    blocked = lambda r: pl.BlockSpec((None, A_HEADS, nblk, r, MOBA_BLOCK), lambda b, t: (b, 0, t, 0, 0))
    out_specs = [
        pl.BlockSpec((None, A_WIDTH, tm), lambda b, t: (b, 0, t)),
        blocked(K_AUG),
        pl.BlockSpec((None, A_HEADS, tm, K_AUG), lambda b, t: (b, 0, t, 0)),
        pl.BlockSpec((None, A_HEADS, None, V_AUG, tm), lambda b, t: (b, 0, t // tpg, 0, t % tpg)),
        pl.BlockSpec((None, None, nblk, A_WIDTH), lambda b, t: (b, t, 0, 0)),
        blocked(HEAD_DIM),
        row(2 * M_WIDTH), row(G_KW), row(G_KW), row(G_VW), row(G_KW), row(M_WIDTH),
    ]
    out_shape = [
        jax.ShapeDtypeStruct((bsz, A_WIDTH, s), F32),
        jax.ShapeDtypeStruct((bsz, A_HEADS, nb, K_AUG, MOBA_BLOCK), BF16),
        jax.ShapeDtypeStruct((bsz, A_HEADS, s, K_AUG), BF16),
        jax.ShapeDtypeStruct((bsz, A_HEADS, s // group, V_AUG, group), BF16),
        jax.ShapeDtypeStruct((bsz, nt, nblk, A_WIDTH), F32),
        jax.ShapeDtypeStruct((bsz, A_HEADS, nb, HEAD_DIM, MOBA_BLOCK), BF16),
        jax.ShapeDtypeStruct((bsz, s, 2 * M_WIDTH), BF16),
        jax.ShapeDtypeStruct((bsz, s, G_KW), F32),
        jax.ShapeDtypeStruct((bsz, s, G_KW), F32),
        jax.ShapeDtypeStruct((bsz, s, G_VW), BF16),
        jax.ShapeDtypeStruct((bsz, s, G_KW), F32),
        jax.ShapeDtypeStruct((bsz, s, M_WIDTH), F32),
    ]
    return pl.pallas_call(
        _proj_kernel, grid=(bsz, nt), in_specs=in_specs, out_specs=out_specs, out_shape=out_shape,
        compiler_params=_params(("arbitrary", "arbitrary")), name="proj",
    )(x, g_pre, wn, wt, gk, gqcol, bones, e, qx, wgu, bgu)


def _sel_kernel(qT_ref, kmean_ref, o_ref):
    width = qT_ref.shape[1]
    nb = kmean_ref.shape[0]
    per = width // MOBA_BLOCK

    def choose(first):
        rows = min(nb, first + per)
        gate = _dot(kmean_ref[0:rows, :], qT_ref[...], precision=HIGHEST)
        rowf = lax.broadcasted_iota(jnp.int32, gate.shape, 0).astype(F32)
        qblk = (first + lax.broadcasted_iota(jnp.int32, gate.shape, 1) // MOBA_BLOCK).astype(F32)
        g = jnp.where(rowf < qblk, gate, -jnp.inf)
        sel = jnp.zeros(gate.shape, F32)
        for _ in range(MOBA_TOPK):
            m = jnp.max(g, axis=0, keepdims=True)
            idx = jnp.min(jnp.where(g == m, rowf, float(rows)), axis=0, keepdims=True)
            hit = rowf == idx
            sel = jnp.where(hit, 1.0, sel)
            g = jnp.where(hit, -jnp.inf, g)
        sel = jnp.where(rowf < qblk, sel, 0.0)
        for u in range(per):
            o_ref[u, 0:rows, :] = sel[:, u * MOBA_BLOCK:(u + 1) * MOBA_BLOCK]
            if rows < nb:
                o_ref[u, rows:nb, :] = jnp.zeros((nb - rows, MOBA_BLOCK), F32)

    for step in range(nb // per):
        pl.when(pl.program_id(2) == step)(functools.partial(choose, step * per))


def _sel_call(qT, kmean):
    bsz, _, s = qT.shape
    nb = s // MOBA_BLOCK
    width = min(SEL_WIDTH, s)
    return pl.pallas_call(
        _sel_kernel,
        grid=(bsz, A_HEADS, s // width),
        in_specs=[
            pl.BlockSpec((None, HEAD_DIM, width), lambda b, h, c: (b, h, c)),
            pl.BlockSpec((None, None, nb, HEAD_DIM), lambda b, h, c: (b, h, 0, 0)),
        ],
        out_specs=pl.BlockSpec((None, None, width // MOBA_BLOCK, nb, MOBA_BLOCK), lambda b, h, c: (b, h, c, 0, 0)),
        out_shape=jax.ShapeDtypeStruct((bsz, A_HEADS, nb, nb, MOBA_BLOCK), F32),
        compiler_params=_params(("arbitrary", "arbitrary", "arbitrary")),
        name="mobasel",
    )(qT, kmean)


def _moba_items(nb):
    qi, grp, first, valid = [], [], [], []
    for i in range(1, nb):
        for g in range((i - 1) // MOBA_GROUP + 1):
            qi.append(i); grp.append(g); first.append(int(g == 0)); valid.append(1)
    n_items = -(-len(qi) // MOBA_FAST_UNROLL) * MOBA_FAST_UNROLL
    for _ in range(n_items + 2 - len(qi)):
        qi.append(qi[-1]); grp.append(0); first.append(0); valid.append(0)
    tab = np.stack([qi, grp, first, valid]).astype(np.int32)
    return tab, n_items


def _moba_kernel(tab_ref, c1_ref, ub_ref, qa_ref, kall_ref, vall_ref, sel_ref, sga_ref, o_ref,
                 m_scr, acc_scr, out_scr, s0_ref, s1_ref, mb0_ref, mb1_ref, p0_ref, p1_ref, *, n_items):
    hh = pl.program_id(1)
    c1 = c1_ref[hh]
    ub = ub_ref[0]
    nb = qa_ref.shape[0]
    gk = MOBA_GROUP * MOBA_BLOCK
    half = gk // 2

    kk = lax.broadcasted_iota(jnp.int32, (MOBA_BLOCK, MOBA_BLOCK), 0)
    qq = lax.broadcasted_iota(jnp.int32, (MOBA_BLOCK, MOBA_BLOCK), 1)

    def own_scores(i, s_ref, mb_ref):
        i = jnp.minimum(i, nb - 1)
        r0 = pl.multiple_of(i * MOBA_BLOCK, MOBA_BLOCK)
        s = _dot(kall_ref[pl.ds(r0, MOBA_BLOCK), :], qa_ref[i])
        s = jnp.where(kk <= qq, s, NEG_BIG)
        s_ref[0:MOBA_BLOCK, :] = s
        mb_ref[0] = jnp.max(s, axis=0, keepdims=True)

    def own_softmax(i, s_ref, mb_ref, p_ref):
        i = jnp.minimum(i, nb - 1)
        m0 = mb_ref[0]
        m_scr[i] = jnp.broadcast_to(m0, m_scr.shape[1:])
        p_ref[0:MOBA_BLOCK, :] = jnp.exp2(s_ref[0:MOBA_BLOCK, :] - m0).astype(BF16)

    def own_pv(t, k, p_ref):
        vown = vall_ref[t][:, k * MOBA_BLOCK:(k + 1) * MOBA_BLOCK]
        acc_scr[MOBA_GROUP * t + k] = _dot(vown, p_ref[0:MOBA_BLOCK, :])

    bufs = ((s0_ref, mb0_ref, p0_ref), (s1_ref, mb1_ref, p1_ref))
    own_scores(0, s0_ref, mb0_ref)
    own_scores(1, s1_ref, mb1_ref)
    own_softmax(0, s0_ref, mb0_ref, p0_ref)

    def own_body(t, carry):
        for k in range(MOBA_GROUP):
            i = MOBA_GROUP * t + k
            s_a, mb_a, p_a = bufs[k % 2]
            s_b, mb_b, p_b = bufs[(k + 1) % 2]
            own_scores(i + 2, s_a, mb_a)
            own_softmax(i + 1, s_b, mb_b, p_b)
            own_pv(t, k, p_a)
        return carry

    lax.fori_loop(0, nb // MOBA_GROUP, own_body, 0)

    def item(n):
        return tab_ref[0, n], tab_ref[1, n], tab_ref[2, n], tab_ref[3, n]

    def scores(n, s_ref, mb_ref):
        i, g, _, _ = item(n)
        for part in range(2):
            r0 = pl.multiple_of(g * gk + part * half, half)
            s = _dot(kall_ref[pl.ds(r0, half), :], qa_ref[i])
            s_ref[part * half:(part + 1) * half, :] = s
            for u in range(MOBA_GROUP // 2):
                blk = s[u * MOBA_BLOCK:(u + 1) * MOBA_BLOCK]
                mb_ref[part * (MOBA_GROUP // 2) + u] = jnp.max(blk, axis=0, keepdims=True)

    def softmax(n, s_ref, mb_ref, p_ref, m):
        i, g, first, valid = item(n)
        m = jnp.where(first == 1, m_scr[i][0:1], m)
        m_new = m
        members = []
        for u in range(MOBA_GROUP):
            j = g * MOBA_GROUP + u
            picked = jnp.where(valid == 1, sel_ref[i, pl.ds(j, 1), :], 0.0) > 0.5
            cj = c1 * (MOBA_BLOCK * (j - i)).astype(F32)
            m_new = jnp.where(picked, jnp.maximum(m_new, mb_ref[u] + cj), m_new)
            members.append((picked, cj))
        for u, (picked, cj) in enumerate(members):
            shift = jnp.where(picked, m_new - cj, POS_BIG)
            rows = slice(u * MOBA_BLOCK, (u + 1) * MOBA_BLOCK)
            p_ref[rows, :] = jnp.exp2(s_ref[rows, :] - shift).astype(BF16)
        return m_new, jnp.exp2(m - m_new)

    def accumulate(n, p_ref, alpha, acc):
        i, g, first, _ = item(n)
        acc = jnp.where(first == 1, acc_scr[i], acc) * alpha + _dot(vall_ref[g], p_ref[...])
        acc_scr[i] = acc
        return acc

    def fast_probs(n, p_ref):
        i, g, _, valid = item(n)
        sigma = jnp.maximum(m_scr[i][0:1], ub)
        for part in range(2):
            r0 = pl.multiple_of(g * gk + part * half, half)
            s = _dot(kall_ref[pl.ds(r0, half), :], qa_ref[i])
            for u in range(MOBA_GROUP // 2):
                j = g * MOBA_GROUP + part * (MOBA_GROUP // 2) + u
                picked = jnp.where(valid == 1, sel_ref[i, pl.ds(j, 1), :], 0.0) > 0.5
                cj = c1 * (MOBA_BLOCK * (j - i)).astype(F32)
                shift = jnp.where(picked, sigma - cj, POS_BIG)
                rows = slice(part * half + u * MOBA_BLOCK, part * half + (u + 1) * MOBA_BLOCK)
                p_ref[rows, :] = jnp.exp2(s[u * MOBA_BLOCK:(u + 1) * MOBA_BLOCK] - shift).astype(BF16)

    def fast_accumulate(n, p_ref, acc):
        i, g, first, _ = item(n)
        m0 = m_scr[i][0:1]
        own = acc_scr[i] * jnp.exp2(m0 - jnp.maximum(m0, ub))
        acc = jnp.where(first == 1, own, acc) + _dot(vall_ref[g], p_ref[...])
        out_scr[i] = acc
        return acc

    out_scr[0] = acc_scr[0]
    fast_probs(0, p0_ref)

    def fast_body(t, acc):
        for k in range(MOBA_FAST_UNROLL):
            n = MOBA_FAST_UNROLL * t + k
            fast_probs(n + 1, bufs[(k + 1) % 2][2])
            acc = fast_accumulate(n, bufs[k % 2][2], acc)
        return acc

    lax.fori_loop(0, n_items // MOBA_FAST_UNROLL, fast_body, jnp.zeros((V_AUG, MOBA_BLOCK), F32))

    den = out_scr[:, HEAD_DIM:HEAD_DIM + 1, :]
    in_range = jnp.logical_and(jnp.min(den) > DEN_MIN, jnp.max(den) < DEN_MAX)

    @pl.when(jnp.logical_not(in_range))
    def _():
        scores(0, s0_ref, mb0_ref)
        scores(1, s1_ref, mb1_ref)
        m_init = jnp.full((1, MOBA_BLOCK), NEG_BIG, F32)
        m, alpha = softmax(0, s0_ref, mb0_ref, p0_ref, m_init)
        acc = jnp.zeros((V_AUG, MOBA_BLOCK), F32)

        def body(t, carry):
            m, alpha, acc = carry
            for k in range(MOBA_UNROLL):
                n = MOBA_UNROLL * t + k
                s_a, mb_a, p_a = bufs[k % 2]
                s_b, mb_b, p_b = bufs[(k + 1) % 2]
                acc = accumulate(n, p_a, alpha, acc)
                m, alpha = softmax(n + 1, s_b, mb_b, p_b, m)
                scores(n + 2, s_a, mb_a)
            return m, alpha, acc

        lax.fori_loop(0, n_items // MOBA_UNROLL, body, (m, alpha, acc))
        out_scr[...] = acc_scr[...]

    def finish(i, carry):
        acc = out_scr[i]
        o = acc[:HEAD_DIM] / acc[HEAD_DIM:HEAD_DIM + 1]
        o_ref[i] = (o * sga_ref[i].astype(F32)).astype(BF16)
        return carry

    lax.fori_loop(0, nb, finish, 0, unroll=4)


def _moba_call(c1, ub, qa, kaug, vT, sel, sga):
    bsz, _, nb = qa.shape[:3]
    s = nb * MOBA_BLOCK
    gk = MOBA_GROUP * MOBA_BLOCK
    assert nb % MOBA_GROUP == 0 and MOBA_GROUP % 2 == 0 and MOBA_UNROLL % 2 == 0
    assert MOBA_FAST_UNROLL % MOBA_UNROLL == 0
    tab, n_items = _moba_items(nb)
    per_bh = lambda *tail: pl.BlockSpec((None, None) + tail, lambda b, h: (b, h) + (0,) * len(tail))
    return pl.pallas_call(
        functools.partial(_moba_kernel, n_items=n_items),
        grid=(bsz, A_HEADS),
        in_specs=[
            pl.BlockSpec(memory_space=pltpu.SMEM),
            pl.BlockSpec(memory_space=pltpu.SMEM),
            pl.BlockSpec(memory_space=pltpu.SMEM),
            per_bh(nb, K_AUG, MOBA_BLOCK),
            per_bh(s, K_AUG),
            per_bh(nb // MOBA_GROUP, V_AUG, gk),
            per_bh(nb, nb, MOBA_BLOCK),
            per_bh(nb, HEAD_DIM, MOBA_BLOCK),
        ],
        out_specs=per_bh(nb, HEAD_DIM, MOBA_BLOCK),
        out_shape=jax.ShapeDtypeStruct((bsz, A_HEADS, nb, HEAD_DIM, MOBA_BLOCK), BF16),
        scratch_shapes=[pltpu.VMEM((nb, 8, MOBA_BLOCK), F32), pltpu.VMEM((nb, V_AUG, MOBA_BLOCK), F32),
                        pltpu.VMEM((nb, V_AUG, MOBA_BLOCK), F32),
                        pltpu.VMEM((gk, MOBA_BLOCK), F32), pltpu.VMEM((gk, MOBA_BLOCK), F32),
                        pltpu.VMEM((MOBA_GROUP, 1, MOBA_BLOCK), F32), pltpu.VMEM((MOBA_GROUP, 1, MOBA_BLOCK), F32),
                        pltpu.VMEM((gk, MOBA_BLOCK), BF16), pltpu.VMEM((gk, MOBA_BLOCK), BF16)],
        compiler_params=pltpu.CompilerParams(dimension_semantics=("arbitrary", "arbitrary"),
                                             vmem_limit_bytes=MOBA_VMEM_LIMIT),
        name="moba",
    )(jnp.asarray(tab), c1, ub, qa, kaug, vT, sel, sga)


def _gla_consts():
    c, sb = GLA_CHUNK, GLA_SUB
    r = np.arange(c)
    blk = r // sb
    tri = (r[None, :] <= r[:, None])
    same = blk[None, :] == blk[:, None]
    lall = np.concatenate([
        tri,
        tri & same,
        same,
        blk[None, :] == blk[:, None] - 1,
        blk[None, :] == blk[:, None] - 2,
    ], axis=0).astype(np.float32)
    diff = blk[:, None] - blk[None, :]
    band = np.stack([diff == 1, diff == 2, diff == 3, same & tri]).astype(np.float32)
    dk_head = np.arange(G_KW) // G_DK
    dv_head = np.arange(G_VW) // G_DV
    bdt = (dv_head[:, None] == dk_head[None, :]).astype(np.float32)
    lex, mex = [tri], [np.eye(c, dtype=bool)]
    for s in GLA_LEVELS:
        same_s = (r[None, :] // s) == (r[:, None] // s)
        lex += [tri & same_s, same_s]
        pair = (r[None, :] // (2 * s)) == (r[:, None] // (2 * s))
        mex.append(pair & ((r[:, None] // s) % 2 == 1) & ((r[None, :] // s) % 2 == 0))
    lex = np.concatenate(lex, axis=0).astype(np.float32)
    mex = np.stack(mex).astype(np.float32)
    return lall, band, bdt, lex, mex


def _gla_kernel(q_ref, k_ref, v_ref, g_ref, sgb_ref, lall_ref, band_ref, bdt_ref, bones_ref, gout_ref,
                lex_ref, mex_ref, o_ref, st_ref):
    c = GLA_CHUNK
    nbatch = q_ref.shape[0]

    @pl.when(pl.program_id(0) == 0)
    def _():
        st_ref[...] = jnp.zeros_like(st_ref)

    lane_k = lax.broadcasted_iota(jnp.int32, (1, G_KW), 1) // G_DK
    lane_v = lax.broadcasted_iota(jnp.int32, (1, G_VW), 1) // G_DV
    n_chunks = q_ref.shape[1] // c

    def heads(t):
        return jnp.concatenate([jnp.where(lane_k == hd, t, 0.0) for hd in range(G_HEADS)],
                               axis=0).astype(BF16)

    def chunk(ci, states):
        r0 = pl.multiple_of(ci * c, c)
        nbs = range(nbatch)
        q = [q_ref[bi, pl.ds(r0, c), :] for bi in nbs]
        k = [k_ref[bi, pl.ds(r0, c), :] for bi in nbs]
        v = [v_ref[bi, pl.ds(r0, c), :] for bi in nbs]

        r = [_dot(lall_ref[...], jnp.concatenate(_split3(g_ref[bi, pl.ds(r0, c), :]), axis=1)) for bi in nbs]
        r = [x[:, :G_KW] + x[:, G_KW:2 * G_KW] + x[:, 2 * G_KW:] for x in r]
        b, cc, tt, p1, p2 = ([x[n * c:(n + 1) * c] for x in r] for n in range(5))
        b_last = [x[c - 1:c] for x in b]

        qt = [q[bi] * jnp.exp(cc[bi]) for bi in nbs]
        q2 = [qt[bi] * jnp.exp(p1[bi]) for bi in nbs]
        q3 = [q2[bi] * jnp.exp(p2[bi]) for bi in nbs]
        kt = [(k[bi] * jnp.exp(tt[bi] - cc[bi])).astype(BF16) for bi in nbs]
        kd = [(k[bi] * jnp.exp(-cc[bi])).astype(BF16) for bi in nbs]
        qi = [(q[bi] * jnp.exp(b[bi])).astype(BF16) for bi in nbs]
        kl = [(k[bi] * jnp.exp(b_last[bi] - b[bi])).astype(BF16) for bi in nbs]

        xs = [[_dg(heads(t[bi]), kt[bi], _NT) for bi in nbs] for t in (qt, q2, q3)]
        yd = [_dg(heads(qt[bi]), kd[bi], _NT) for bi in nbs]
        o = [_dg(qi[bi], states[bi].astype(BF16), _NT) for bi in nbs]
        st_new = tuple(states[bi] * jnp.exp(b_last[bi]) + _dg(v[bi], kl[bi], _TN) * bdt_ref[...]
                       for bi in nbs)
        for hd in range(G_HEADS):
            sl = slice(hd * c, (hd + 1) * c)
            for bi in nbs:
                att = (band_ref[0] * xs[0][bi][sl] + band_ref[1] * xs[1][bi][sl]
                       + band_ref[2] * xs[2][bi][sl] + band_ref[3] * yd[bi][sl])
                vh = jnp.where(lane_v == hd, v[bi], jnp.zeros_like(v[bi]))
                o[bi] = o[bi] + _dot(att.astype(BF16), vh)

        ss = [_group_sumsq(o[bi], bones_ref[...]) for bi in nbs]
        for bi in nbs:
            y = o[bi] * lax.rsqrt(ss[bi] * (1.0 / G_DV) + EPS) * gout_ref[...]
            o_ref[bi, pl.ds(r0, c), :] = (y * sgb_ref[bi, pl.ds(r0, c), :].astype(F32)).astype(BF16)
        return st_new

    def chunk_stable(ci, states):
        r0 = pl.multiple_of(ci * c, c)
        out = []
        for bi in range(nbatch):
            q = q_ref[bi, pl.ds(r0, c), :]
            k = k_ref[bi, pl.ds(r0, c), :]
            v = v_ref[bi, pl.ds(r0, c), :]
            r = _dot(lex_ref[...], jnp.concatenate(_split3(g_ref[bi, pl.ds(r0, c), :]), axis=1))
            r = r[:, :G_KW] + r[:, G_KW:2 * G_KW] + r[:, 2 * G_KW:]
            b = r[0:c]
            b_last = b[c - 1:c]
            x = [_dg(heads(q), k.astype(BF16), _NT)]
            for lv in range(len(GLA_LEVELS)):
                cs = r[(1 + 2 * lv) * c:(2 + 2 * lv) * c]
                ts = r[(2 + 2 * lv) * c:(3 + 2 * lv) * c]
                x.append(_dg(heads(q * jnp.exp(cs)), (k * jnp.exp(ts - cs)).astype(BF16), _NT))
            o = _dg((q * jnp.exp(b)).astype(BF16), states[bi].astype(BF16), _NT)
            for hd in range(G_HEADS):
                sl = slice(hd * c, (hd + 1) * c)
                att = mex_ref[0] * x[0][sl]
                for lv in range(len(GLA_LEVELS)):
                    att = att + mex_ref[lv + 1] * x[lv + 1][sl]
                o = o + _dot(att.astype(BF16), jnp.where(lane_v == hd, v, jnp.zeros_like(v)))
            kl = (k * jnp.exp(b_last - b)).astype(BF16)
            out.append(states[bi] * jnp.exp(b_last) + _dg(v, kl, _TN) * bdt_ref[...])
            ss = _group_sumsq(o, bones_ref[...])
            y = o * lax.rsqrt(ss * (1.0 / G_DV) + EPS) * gout_ref[...]
            o_ref[bi, pl.ds(r0, c), :] = (y * sgb_ref[bi, pl.ds(r0, c), :].astype(F32)).astype(BF16)
        return tuple(out)

    def run(body, unroll):
        states = lax.fori_loop(0, n_chunks, body, tuple(st_ref[bi] for bi in range(nbatch)), unroll=unroll)
        for bi in range(nbatch):
            st_ref[bi] = states[bi]

    risky = jnp.max(-g_ref[...]) * GLA_SUB > GLA_SAFE_EXP
    pl.when(jnp.logical_not(risky))(lambda: run(chunk, 2))
    pl.when(risky)(lambda: run(chunk_stable, 1))


def _gla_call(gq, gk, gv, gg, sgbm, lall, band, bdt, bones, gout, lex, mex):
    bsz, s, _ = gq.shape
    tm = ROW_TILE
    row = lambda w: pl.BlockSpec((bsz, tm, w), lambda t: (0, t, 0))
    return pl.pallas_call(
        _gla_kernel,
        grid=(s // tm,),
        in_specs=[row(G_KW), row(G_KW), row(G_VW), row(G_KW), row(G_VW),
                  _full(lall.shape), _full(band.shape), _full(bdt.shape), _full(bones.shape), _full((1, G_VW)),
                  _full(lex.shape), _full(mex.shape)],
        out_specs=row(G_VW),
        out_shape=jax.ShapeDtypeStruct((bsz, s, G_VW), BF16),
        scratch_shapes=[pltpu.VMEM((bsz, G_VW, G_KW), F32)],
        compiler_params=_params(("arbitrary",)),
        name="gla",
    )(gq, gk, gv, gg, sgbm, lall, band, bdt, bones, gout, lex, mex)


def _mem_kernel(qm_ref, sgm_ref, mk_ref, mvx_ref, gq_ref, bones_ref, o_ref):
    qm = qm_ref[...]
    ss = _group_sumsq(qm, bones_ref[...])
    qn = qm * lax.rsqrt(ss * (1.0 / HEAD_DIM) + EPS) * gq_ref[...]
    qs = qn * (LOG2E * HEAD_DIM ** -0.5)
    lane = lax.broadcasted_iota(jnp.int32, (1, M_WIDTH), 1) // HEAD_DIM
    mk = mk_ref[...]
    s = [_dg(jnp.where(lane == h, qs, 0.0).astype(BF16), mk, _NT) for h in range(M_HEADS)]
    p = [jnp.exp2(x - jnp.max(x, axis=-1, keepdims=True)).astype(BF16) for x in s]
    acc = _dot(p[0], mvx_ref[0])
    for h in range(1, M_HEADS):
        acc = acc + _dot(p[h], mvx_ref[h])
    om = acc[:, :M_WIDTH] / acc[:, M_WIDTH:]
    o_ref[...] = (om * sgm_ref[...].astype(F32)).astype(BF16)


def _mem_call(qm, sgbm, mk, mvx, gq, bones):
    bsz, s, _ = qm.shape
    tm = ROW_TILE
    mlen = mk.shape[1]
    return pl.pallas_call(
        _mem_kernel,
        grid=(bsz, s // tm),
        in_specs=[
            pl.BlockSpec((None, tm, M_WIDTH), lambda b, t: (b, t, 0)),
            pl.BlockSpec((None, tm, M_WIDTH), lambda b, t: (b, t, 1)),
            pl.BlockSpec((None, mlen, M_WIDTH), lambda b, t: (b, 0, 0)),
            pl.BlockSpec((None, M_HEADS, mlen, 2 * M_WIDTH), lambda b, t: (b, 0, 0, 0)),
            _full((1, M_WIDTH)), _full((M_WIDTH, M_WIDTH)),
        ],
        out_specs=pl.BlockSpec((None, tm, M_WIDTH), lambda b, t: (b, t, 0)),
        out_shape=jax.ShapeDtypeStruct((bsz, s, M_WIDTH), BF16),
        compiler_params=_params(("arbitrary", "arbitrary")),
        name="memattn",
    )(qm, sgbm, mk, mvx, gq, bones)


def _out_kernel(x_ref, oaT_ref, ob_ref, om_ref, wa_ref, wb_ref, o_ref):
    y = _dot(jnp.concatenate([ob_ref[...], om_ref[...]], axis=-1), wb_ref[...])
    for t in range(oaT_ref.shape[1]):
        rows = slice(t * MOBA_BLOCK, (t + 1) * MOBA_BLOCK)
        oaT = oaT_ref[:, t].reshape(A_WIDTH, MOBA_BLOCK)
        o_ref[rows, :] = x_ref[rows, :] + y[rows] + _dg(oaT, wa_ref[...], _TN)


def _out_call(x, oaT, ob, om, wa, wb):
    bsz, s, d = x.shape
    tm = ROW_TILE
    row = lambda w: pl.BlockSpec((None, tm, w), lambda b, t: (b, t, 0))
    return pl.pallas_call(
        _out_kernel,
        grid=(bsz, s // tm),
        in_specs=[row(d),
                  pl.BlockSpec((None, A_HEADS, tm // MOBA_BLOCK, HEAD_DIM, MOBA_BLOCK), lambda b, t: (b, 0, t, 0, 0)),
                  row(G_VW), row(M_WIDTH),
                  _full(wa.shape), _full(wb.shape)],
        out_specs=row(d),
        out_shape=jax.ShapeDtypeStruct((bsz, s, d), F32),
        compiler_params=_params(("arbitrary", "arbitrary")),
        name="outproj",
    )(x, oaT, ob, om, wa, wb)


def _alibi_consts():
    slopes = np.asarray([2.0 ** (-8.0 * (i + 1) / A_HEADS) for i in range(A_HEADS)], np.float32)
    c1 = (slopes * np.float32(LOG2E)).astype(np.float32)
    c1j = jnp.asarray(c1)
    pieces = list(_split3(c1j * 16.0)) + list(_split3(c1j))
    qx = jnp.zeros((A_HEADS, K_AUG - HEAD_DIM, MOBA_BLOCK), BF16)
    for n, pc in enumerate(pieces):
        qx = qx.at[:, n, :].set(jnp.broadcast_to(pc[:, None], (A_HEADS, MOBA_BLOCK)))
    pos = np.arange(MOBA_BLOCK)
    e = np.zeros((MOBA_BLOCK, K_AUG - HEAD_DIM), np.float32)
    e[:, 0:3] = (pos // 16)[:, None]
    e[:, 3:6] = (pos % 16)[:, None]
    return c1j, qx, jnp.asarray(e)


def kernel(x, mem, g_pre, w_in, g_q_moba, g_k_moba, w_gate_up, b_gate_up, g_gla_out,
           g_mem, w_mem_kv, g_q_mem, g_k_mem, w_out):
    depth = g_pre.shape[0]
    d = x.shape[-1]
    c1, qx, e = _alibi_consts()
    lall_np, band_np, bdt_np, lex_np, mex_np = _gla_consts()
    lall = jnp.asarray(lall_np, BF16)
    band = jnp.asarray(band_np)
    bdt = jnp.asarray(bdt_np)
    lex = jnp.asarray(lex_np, BF16)
    mex = jnp.asarray(mex_np)
    bones_a = jnp.asarray(_block_ones(A_WIDTH, HEAD_DIM), BF16)
    bones_m = jnp.asarray(_block_ones(M_WIDTH, HEAD_DIM), BF16)

    o_qa, o_ka, o_va, o_ga = 0, 512, 1024, 1536
    o_qb, o_kb, o_vb, o_gb, o_rb = 2048, 2176, 2304, 2560, 2816
    o_qm, o_gm = 2832, 3088

    mk_all, mvx_all = _memkv_call(mem, g_mem, w_mem_kv, g_k_mem)

    for l in range(depth):
        w = w_in[l]
        col = lambda o, n: w[:, o:o + n]
        rb_pad = jnp.pad(col(o_rb, GATE_RANK), ((0, 0), (0, G_KW - GATE_RANK)))
        wn = jnp.concatenate([col(o_ka, 512), col(o_gb, 256), col(o_gm, 256), col(o_qb, 128), col(o_kb, 128),
                              col(o_vb, 256), rb_pad, col(o_qm, 256)], axis=1).astype(BF16)
        wt = jnp.concatenate([col(o_qa, 512), col(o_va, 512), col(o_ga, 512)], axis=1).T.astype(BF16)
        wgu = jnp.pad(w_gate_up[l], ((0, G_KW - GATE_RANK), (0, 0)))
        gk = jnp.tile(g_k_moba[l], A_HEADS).reshape(1, A_WIDTH)

        (qT, qa, kaug, vT, kmean_nat, sga, sgbm, gq, gk2, gv, gg, qm) = _proj_call(
            x, g_pre[l].reshape(1, d), wn, wt, gk, g_q_moba[l].reshape(HEAD_DIM, 1), bones_a, e, qx,
            wgu, b_gate_up[l].reshape(1, G_KW))

        bsz, s = x.shape[0], x.shape[1]
        nb = s // MOBA_BLOCK
        kmean = kmean_nat.reshape(bsz, nb, A_HEADS, HEAD_DIM).transpose(0, 2, 1, 3)
        ub = (UB_SLACK * LOG2E * HEAD_DIM ** 0.5) * jnp.max(jnp.abs(g_q_moba[l])) * jnp.max(jnp.abs(g_k_moba[l]))
        oaT = _moba_call(c1, ub.reshape(1), qa, kaug, vT, _sel_call(qT, kmean), sga)
        ob = _gla_call(gq, gk2, gv, gg, sgbm, lall, band, bdt, bones_m,
                       jnp.tile(g_gla_out[l], G_HEADS).reshape(1, G_VW), lex, mex)
        om = _mem_call(qm, sgbm, mk_all[l], mvx_all[l],
                       jnp.tile(g_q_mem[l], M_HEADS).reshape(1, M_WIDTH), bones_m)
        wo = w_out[l].astype(BF16)
        x = _out_call(x, oaT, ob, om, wo[:A_WIDTH], wo[A_WIDTH:])
    return x
```

```python
import functools

import numpy as np
import jax
import jax.numpy as jnp
from jax import lax
from jax.experimental import pallas as pl
from jax.experimental.pallas import tpu as pltpu

F32 = jnp.float32
BF16 = jnp.bfloat16
HIGHEST = lax.Precision.HIGHEST

EPS = 1e-6
LOG2E = 1.4426950408889634
HEAD_DIM = 64
A_HEADS = 8
A_WIDTH = A_HEADS * HEAD_DIM
MOBA_BLOCK = 256
MOBA_TOPK = 3
MOBA_GROUP = 4
MOBA_UNROLL = 4
MOBA_FAST_UNROLL = 16
SEL_WIDTH = 2048
G_HEADS = 4
G_DK = 32
G_DV = 64
G_KW = G_HEADS * G_DK
G_VW = G_HEADS * G_DV
GATE_RANK = 16
GATE_TEMP = 16.0
GLA_CHUNK = 64
GLA_SUB = 16
GLA_LEVELS = (32, 16, 8, 4, 2, 1)
GLA_SAFE_EXP = 80.0
M_HEADS = 4
M_WIDTH = M_HEADS * HEAD_DIM
K_AUG = 128
V_AUG = 80
ROW_TILE = 512
V7X_VMEM_BYTES = 64 * 1024 * 1024
VMEM_LIMIT = 3 * V7X_VMEM_BYTES // 4
MOBA_VMEM_LIMIT = 29 * V7X_VMEM_BYTES // 32

NEG_BIG = -1e30
POS_BIG = 1e30
DEN_MIN = 1e-18
DEN_MAX = 1e30
UB_SLACK = 1.02

_NT = (((1,), (1,)), ((), ()))
_TN = (((0,), (0,)), ((), ()))


def _dot(a, b, **kw):
    return jnp.dot(a, b, preferred_element_type=F32, **kw)


def _dg(a, b, dims, **kw):
    return lax.dot_general(a, b, dims, preferred_element_type=F32, **kw)


def _split2(v):
    hi = v.astype(BF16)
    lo = (v - hi.astype(F32)).astype(BF16)
    return hi, lo


def _split3(v):
    hi = v.astype(BF16)
    r = v - hi.astype(F32)
    mid = r.astype(BF16)
    lo = (r - mid.astype(F32)).astype(BF16)
    return hi, mid, lo


def _group_sumsq(v, bones):
    hi, lo = _split2(v * v)
    return _dot(hi, bones) + _dot(lo, bones)


def _silu(v):
    return v / (1.0 + jnp.exp(-v))


def _block_ones(n, g):
    i = np.arange(n) // g
    return (i[:, None] == i[None, :]).astype(np.float32)


def _params(sem):
    return pltpu.CompilerParams(dimension_semantics=sem, vmem_limit_bytes=VMEM_LIMIT)


def _full(shape):
    return pl.BlockSpec(shape, lambda *_: (0,) * len(shape))


def _memkv_kernel(mem_ref, gmem_ref, w_ref, gk_ref, bones_ref, mk_ref, mvx_ref):
    m = mem_ref[...]
    ms = jnp.mean(m * m, axis=-1, keepdims=True)
    hn = (m * lax.rsqrt(ms + EPS) * gmem_ref[...]).astype(BF16)
    kv = _dot(hn, w_ref[...])
    mk = kv[:, :M_WIDTH]
    mv = kv[:, M_WIDTH:]
    ss = _group_sumsq(mk, bones_ref[...])
    mk_ref[...] = (mk * lax.rsqrt(ss * (1.0 / HEAD_DIM) + EPS) * gk_ref[...]).astype(BF16)
    lane = lax.broadcasted_iota(jnp.int32, mv.shape, 1) // HEAD_DIM
    for h in range(M_HEADS):
        hm = lane == h
        mvx_ref[h] = jnp.concatenate(
            [jnp.where(hm, mv, 0.0), jnp.where(hm, 1.0, 0.0)], axis=-1).astype(BF16)


def _memkv_call(mem, g_mem, w_mem_kv, g_k_mem):
    depth = g_mem.shape[0]
    bsz, mlen, d = mem.shape
    bones = jnp.asarray(_block_ones(M_WIDTH, HEAD_DIM), BF16)
    gk = jnp.tile(g_k_mem, (1, M_HEADS)).reshape(depth, 1, M_WIDTH)
    return pl.pallas_call(
        _memkv_kernel,
        grid=(depth, bsz),
        in_specs=[
            pl.BlockSpec((None, mlen, d), lambda l, b: (b, 0, 0)),
            pl.BlockSpec((None, 1, d), lambda l, b: (l, 0, 0)),
            pl.BlockSpec((None, d, 2 * M_WIDTH), lambda l, b: (l, 0, 0)),
            pl.BlockSpec((None, 1, M_WIDTH), lambda l, b: (l, 0, 0)),
            _full((M_WIDTH, M_WIDTH)),
        ],
        out_specs=[
            pl.BlockSpec((None, None, mlen, M_WIDTH), lambda l, b: (l, b, 0, 0)),
            pl.BlockSpec((None, None, M_HEADS, mlen, 2 * M_WIDTH), lambda l, b: (l, b, 0, 0, 0)),
        ],
        out_shape=[
            jax.ShapeDtypeStruct((depth, bsz, mlen, M_WIDTH), BF16),
            jax.ShapeDtypeStruct((depth, bsz, M_HEADS, mlen, 2 * M_WIDTH), BF16),
        ],
        compiler_params=_params(("arbitrary", "arbitrary")),
        name="memkv",
    )(mem, g_mem.reshape(depth, 1, d), w_mem_kv.astype(BF16), gk, bones)


_N_K = (0, 512)
_N_GBM = (512, 1024)
_N_QB = (1024, 1152)
_N_KB = (1152, 1280)
_N_VB = (1280, 1536)
_N_RB = (1536, 1664)
_N_QM = (1664, 1920)


def _proj_kernel(x_ref, gpre_ref, wn_ref, wt_ref, gk_ref, gqcol_ref, bones_ref, e_ref, qx_ref, wgu_ref, bgu_ref,
                 qT_ref, qa_ref, kaug_ref, vT_ref, kmean_ref, sga_ref, sgbm_ref, gq_ref, gk2_ref, gv_ref, gg_ref,
                 qm_ref):
    tm = x_ref.shape[0]
    nblk = tm // MOBA_BLOCK
    x = x_ref[...]
    ms = jnp.mean(x * x, axis=-1, keepdims=True)
    h = (x * lax.rsqrt(ms + EPS) * gpre_ref[...]).astype(BF16)

    def nat(cols):
        return _dot(h, wn_ref[:, cols[0]:cols[1]])

    k = nat(_N_K)
    qT = _dg(wt_ref[0:A_WIDTH, :], h, _NT)

    ss = _group_sumsq(k, bones_ref[...])
    kn = k * lax.rsqrt(ss * (1.0 / HEAD_DIM) + EPS) * gk_ref[...]
    kmean_ref[...] = jnp.mean(kn.reshape(nblk, MOBA_BLOCK, A_WIDTH), axis=1)
    e = jnp.concatenate([e_ref[...]] * nblk, axis=0)
    for hh in range(A_HEADS):
        rows = kn[:, hh * HEAD_DIM:(hh + 1) * HEAD_DIM]
        kaug_ref[hh] = jnp.concatenate([rows, e], axis=-1).astype(BF16)

    q3 = qT.reshape(A_HEADS, HEAD_DIM, tm)
    msq = jnp.mean(q3 * q3, axis=1, keepdims=True)
    qn = q3 * lax.rsqrt(msq + EPS) * gqcol_ref[...].reshape(1, HEAD_DIM, 1)
    qT_ref[...] = qn.reshape(A_WIDTH, tm)
    qs = (qn * (LOG2E * HEAD_DIM ** -0.5)).astype(BF16)
    blocks = [slice(t * MOBA_BLOCK, (t + 1) * MOBA_BLOCK) for t in range(nblk)]
    for hh in range(A_HEADS):
        for t, cols in enumerate(blocks):
            qa_ref[hh, t] = jnp.concatenate([qs[hh][:, cols], qx_ref[hh]], axis=0)

    vT = _dg(wt_ref[A_WIDTH:2 * A_WIDTH, :], h, _NT)
    ones_rows = jnp.where(lax.broadcasted_iota(jnp.int32, (V_AUG - HEAD_DIM, tm), 0) == 0, 1.0, 0.0)
    sga = _silu(_dg(wt_ref[2 * A_WIDTH:3 * A_WIDTH, :], h, _NT)).astype(BF16)
    for hh in range(A_HEADS):
        rows = slice(hh * HEAD_DIM, (hh + 1) * HEAD_DIM)
        vT_ref[hh] = jnp.concatenate([vT[rows], ones_rows], axis=0).astype(BF16)
        for t, cols in enumerate(blocks):
            sga_ref[hh, t] = sga[rows, cols]
    sgbm_ref[...] = _silu(nat(_N_GBM)).astype(BF16)

    qkb = nat((_N_QB[0], _N_KB[1]))
    gq_ref[...] = qkb[:, :G_KW] * (G_DK ** -0.5)
    gk2_ref[...] = qkb[:, G_KW:]
    rb = nat(_N_RB)
    gv_ref[...] = nat(_N_VB).astype(BF16)
    qm_ref[...] = nat(_N_QM)
    z = _dot(rb, wgu_ref[...], precision=HIGHEST) + bgu_ref[...]
    gg_ref[...] = (jnp.minimum(z, 0.0) - jnp.log(1.0 + jnp.exp(-jnp.abs(z)))) * (1.0 / GATE_TEMP)


def _proj_call(x, g_pre, wn, wt, gk, gqcol, bones, e, qx, wgu, bgu):
    bsz, s, d = x.shape
    tm = ROW_TILE
    nt = s // tm
    nblk = tm // MOBA_BLOCK
    nb = s // MOBA_BLOCK
    row = lambda w: pl.BlockSpec((None, tm, w), lambda b, t: (b, t, 0))
    in_specs = [
        row(d), _full((1, d)), _full(wn.shape), _full(wt.shape), _full((1, A_WIDTH)),
        _full((HEAD_DIM, 1)), _full((A_WIDTH, A_WIDTH)), _full(e.shape), _full(qx.shape), _full(wgu.shape),
        _full((1, G_KW)),
    ]
    group = MOBA_GROUP * MOBA_BLOCK
    assert d == wn.shape[0] and s % group == 0 and group % tm == 0 and tm % MOBA_BLOCK == 0
    tpg = group // tm
    blocked = lambda r: pl.BlockSpec((None, A_HEADS, nblk, r, MOBA_BLOCK), lambda b, t: (b, 0, t, 0, 0))
    out_specs = [
        pl.BlockSpec((None, A_WIDTH, tm), lambda b, t: (b, 0, t)),
        blocked(K_AUG),
        pl.BlockSpec((None, A_HEADS, tm, K_AUG), lambda b, t: (b, 0, t, 0)),
        pl.BlockSpec((None, A_HEADS, None, V_AUG, tm), lambda b, t: (b, 0, t // tpg, 0, t % tpg)),
        pl.BlockSpec((None, None, nblk, A_WIDTH), lambda b, t: (b, t, 0, 0)),
        blocked(HEAD_DIM),
        row(2 * M_WIDTH), row(G_KW), row(G_KW), row(G_VW), row(G_KW), row(M_WIDTH),
    ]
    out_shape = [
        jax.ShapeDtypeStruct((bsz, A_WIDTH, s), F32),
        jax.ShapeDtypeStruct((bsz, A_HEADS, nb, K_AUG, MOBA_BLOCK), BF16),
        jax.ShapeDtypeStruct((bsz, A_HEADS, s, K_AUG), BF16),
        jax.ShapeDtypeStruct((bsz, A_HEADS, s // group, V_AUG, group), BF16),
        jax.ShapeDtypeStruct((bsz, nt, nblk, A_WIDTH), F32),
        jax.ShapeDtypeStruct((bsz, A_HEADS, nb, HEAD_DIM, MOBA_BLOCK), BF16),
        jax.ShapeDtypeStruct((bsz, s, 2 * M_WIDTH), BF16),
        jax.ShapeDtypeStruct((bsz, s, G_KW), F32),
        jax.ShapeDtypeStruct((bsz, s, G_KW), F32),
        jax.ShapeDtypeStruct((bsz, s, G_VW), BF16),
        jax.ShapeDtypeStruct((bsz, s, G_KW), F32),
        jax.ShapeDtypeStruct((bsz, s, M_WIDTH), F32),
    ]
    return pl.pallas_call(
        _proj_kernel, grid=(bsz, nt), in_specs=in_specs, out_specs=out_specs, out_shape=out_shape,
        compiler_params=_params(("arbitrary", "arbitrary")), name="proj",
    )(x, g_pre, wn, wt, gk, gqcol, bones, e, qx, wgu, bgu)


def _sel_kernel(qT_ref, kmean_ref, o_ref):
    width = qT_ref.shape[1]
    nb = kmean_ref.shape[0]
    per = width // MOBA_BLOCK

    def choose(first):
        rows = min(nb, first + per)
        gate = _dot(kmean_ref[0:rows, :], qT_ref[...], precision=HIGHEST)
        rowf = lax.broadcasted_iota(jnp.int32, gate.shape, 0).astype(F32)
        qblk = (first + lax.broadcasted_iota(jnp.int32, gate.shape, 1) // MOBA_BLOCK).astype(F32)
        g = jnp.where(rowf < qblk, gate, -jnp.inf)
        sel = jnp.zeros(gate.shape, F32)
        for _ in range(MOBA_TOPK):
            m = jnp.max(g, axis=0, keepdims=True)
            idx = jnp.min(jnp.where(g == m, rowf, float(rows)), axis=0, keepdims=True)
            hit = rowf == idx
            sel = jnp.where(hit, 1.0, sel)
            g = jnp.where(hit, -jnp.inf, g)
        sel = jnp.where(rowf < qblk, sel, 0.0)
        for u in range(per):
            o_ref[u, 0:rows, :] = sel[:, u * MOBA_BLOCK:(u + 1) * MOBA_BLOCK]
            if rows < nb:
                o_ref[u, rows:nb, :] = jnp.zeros((nb - rows, MOBA_BLOCK), F32)

    for step in range(nb // per):
        pl.when(pl.program_id(2) == step)(functools.partial(choose, step * per))


def _sel_call(qT, kmean):
    bsz, _, s = qT.shape
    nb = s // MOBA_BLOCK
    width = min(SEL_WIDTH, s)
    return pl.pallas_call(
        _sel_kernel,
        grid=(bsz, A_HEADS, s // width),
        in_specs=[
            pl.BlockSpec((None, HEAD_DIM, width), lambda b, h, c: (b, h, c)),
            pl.BlockSpec((None, None, nb, HEAD_DIM), lambda b, h, c: (b, h, 0, 0)),
        ],
        out_specs=pl.BlockSpec((None, None, width // MOBA_BLOCK, nb, MOBA_BLOCK), lambda b, h, c: (b, h, c, 0, 0)),
        out_shape=jax.ShapeDtypeStruct((bsz, A_HEADS, nb, nb, MOBA_BLOCK), F32),
        compiler_params=_params(("arbitrary", "arbitrary", "arbitrary")),
        name="mobasel",
    )(qT, kmean)


def _moba_items(nb):
    def table(rows, multiple):
        n_items = -(-len(rows) // multiple) * multiple
        rows = rows + [(rows[-1][0], 0, 0, 0)] * (n_items + 2 - len(rows))
        return np.asarray(rows, np.int32).T, n_items

    fast = [(i, g, int(g == 0), 1) for i in range(MOBA_GROUP, nb) for g in range(i // MOBA_GROUP)]
    robust = [(i, g, int(g == 0), 1) for i in range(1, nb) for g in range((i - 1) // MOBA_GROUP + 1)]
    return table(fast, MOBA_FAST_UNROLL), table(robust, MOBA_UNROLL)


def _moba_kernel(tabf_ref, tabr_ref, c1_ref, ub_ref, qa_ref, kall_ref, vall_ref, sel_ref, sga_ref, o_ref,
                 m_scr, acc_scr, out_scr, s0_ref, s1_ref, mb0_ref, mb1_ref, p0_ref, p1_ref,
                 *, n_fast, n_robust):
    hh = pl.program_id(1)
    c1 = c1_ref[hh]
    ub = ub_ref[0]
    nb = qa_ref.shape[0]
    gk = MOBA_GROUP * MOBA_BLOCK
    half = gk // 2
    per_half = MOBA_GROUP // 2
    bufs = ((s0_ref, mb0_ref, p0_ref), (s1_ref, mb1_ref, p1_ref))
    kk = lax.broadcasted_iota(jnp.int32, (MOBA_BLOCK, MOBA_BLOCK), 0)
    qq = lax.broadcasted_iota(jnp.int32, (MOBA_BLOCK, MOBA_BLOCK), 1)

    sigma = ub + c1 * lax.broadcasted_iota(jnp.int32, (1, MOBA_BLOCK), 1).astype(F32)

    def past_probs(s_blk, i, j, valid):
        picked = jnp.where(valid == 1, sel_ref[i, pl.ds(j, 1), :], 0.0) > 0.5
        cj = c1 * (MOBA_BLOCK * (j - i)).astype(F32)
        shift = jnp.where(picked, sigma - cj, POS_BIG)
        return jnp.exp2(s_blk - shift).astype(BF16)

    def own_probs(t, k, p_ref):
        t = jnp.minimum(t, nb // MOBA_GROUP - 1)
        i = MOBA_GROUP * t + k
        for part in range(k // per_half + 1):
            members = min(per_half, k + 1 - per_half * part)
            r0 = pl.multiple_of(t * gk + part * half, half)
            s = _dot(kall_ref[pl.ds(r0, members * MOBA_BLOCK), :], qa_ref[i])
            for u2 in range(members):
                u = per_half * part + u2
                blk = s[u2 * MOBA_BLOCK:(u2 + 1) * MOBA_BLOCK]
                if u == k:
                    p = jnp.exp2(jnp.where(kk <= qq, blk, NEG_BIG) - sigma).astype(BF16)
                else:
                    p = past_probs(blk, i, MOBA_GROUP * t + u, 1)
                p_ref[u * MOBA_BLOCK:(u + 1) * MOBA_BLOCK, :] = p

    def own_accumulate(t, k, p_ref):
        n = (k + 1) * MOBA_BLOCK
        acc = _dot(vall_ref[t][:, 0:n], p_ref[0:n, :])
        acc_scr[MOBA_GROUP * t + k] = acc
        out_scr[MOBA_GROUP * t + k] = acc

    own_probs(0, 0, p0_ref)

    def own_body(t, carry):
        for k in range(MOBA_GROUP):
            nxt = (t, k + 1) if k + 1 < MOBA_GROUP else (t + 1, 0)
            own_probs(nxt[0], nxt[1], bufs[(k + 1) % 2][2])
            own_accumulate(t, k, bufs[k % 2][2])
        return carry

    lax.fori_loop(0, nb // MOBA_GROUP, own_body, 0)

    def fast_probs(n, p_ref):
        i, g, valid = tabf_ref[0, n], tabf_ref[1, n], tabf_ref[3, n]
        for part in range(2):
            r0 = pl.multiple_of(g * gk + part * half, half)
            s = _dot(kall_ref[pl.ds(r0, half), :], qa_ref[i])
            for u2 in range(per_half):
                u = per_half * part + u2
                p_ref[u * MOBA_BLOCK:(u + 1) * MOBA_BLOCK, :] = past_probs(
                    s[u2 * MOBA_BLOCK:(u2 + 1) * MOBA_BLOCK], i, g * MOBA_GROUP + u, valid)

    def fast_accumulate(n, p_ref, acc):
        i, g, first = tabf_ref[0, n], tabf_ref[1, n], tabf_ref[2, n]
        acc = jnp.where(first == 1, acc_scr[i], acc) + _dot(vall_ref[g], p_ref[...])
        out_scr[i] = acc
        return acc

    fast_probs(0, p0_ref)

    def fast_body(t, acc):
        for k in range(MOBA_FAST_UNROLL):
            n = MOBA_FAST_UNROLL * t + k
            fast_probs(n + 1, bufs[(k + 1) % 2][2])
            acc = fast_accumulate(n, bufs[k % 2][2], acc)
        return acc

    lax.fori_loop(0, n_fast // MOBA_FAST_UNROLL, fast_body, jnp.zeros((V_AUG, MOBA_BLOCK), F32))

    den = out_scr[:, HEAD_DIM:HEAD_DIM + 1, :]
    in_range = jnp.logical_and(jnp.min(den) > DEN_MIN, jnp.max(den) < DEN_MAX)

    def item(n):
        return tabr_ref[0, n], tabr_ref[1, n], tabr_ref[2, n], tabr_ref[3, n]

    def diag_scores(i, s_ref, mb_ref):
        i = jnp.minimum(i, nb - 1)
        r0 = pl.multiple_of(i * MOBA_BLOCK, MOBA_BLOCK)
        s = _dot(kall_ref[pl.ds(r0, MOBA_BLOCK), :], qa_ref[i])
        s = jnp.where(kk <= qq, s, NEG_BIG)
        s_ref[0:MOBA_BLOCK, :] = s
        mb_ref[0] = jnp.max(s, axis=0, keepdims=True)

    def diag_softmax(i, s_ref, mb_ref, p_ref):
        i = jnp.minimum(i, nb - 1)
        m0 = mb_ref[0]
        m_scr[i] = jnp.broadcast_to(m0, m_scr.shape[1:])
        p_ref[0:MOBA_BLOCK, :] = jnp.exp2(s_ref[0:MOBA_BLOCK, :] - m0).astype(BF16)

    def diag_accumulate(t, k, p_ref):
        vown = vall_ref[t][:, k * MOBA_BLOCK:(k + 1) * MOBA_BLOCK]
        acc_scr[MOBA_GROUP * t + k] = _dot(vown, p_ref[0:MOBA_BLOCK, :])

    def scores(n, s_ref, mb_ref):
        i, g, _, _ = item(n)
        for part in range(2):
            r0 = pl.multiple_of(g * gk + part * half, half)
            s = _dot(kall_ref[pl.ds(r0, half), :], qa_ref[i])
            s_ref[part * half:(part + 1) * half, :] = s
            for u in range(per_half):
                blk = s[u * MOBA_BLOCK:(u + 1) * MOBA_BLOCK]
                mb_ref[part * per_half + u] = jnp.max(blk, axis=0, keepdims=True)

    def softmax(n, s_ref, mb_ref, p_ref, m):
        i, g, first, valid = item(n)
        m = jnp.where(first == 1, m_scr[i][0:1], m)
        m_new = m
        members = []
        for u in range(MOBA_GROUP):
            j = g * MOBA_GROUP + u
            picked = jnp.where(valid == 1, sel_ref[i, pl.ds(j, 1), :], 0.0) > 0.5
            cj = c1 * (MOBA_BLOCK * (j - i)).astype(F32)
            m_new = jnp.where(picked, jnp.maximum(m_new, mb_ref[u] + cj), m_new)
            members.append((picked, cj))
        for u, (picked, cj) in enumerate(members):
            shift = jnp.where(picked, m_new - cj, POS_BIG)
            rows = slice(u * MOBA_BLOCK, (u + 1) * MOBA_BLOCK)
            p_ref[rows, :] = jnp.exp2(s_ref[rows, :] - shift).astype(BF16)
        return m_new, jnp.exp2(m - m_new)

    def accumulate(n, p_ref, alpha, acc):
        i, g, first, _ = item(n)
        acc = jnp.where(first == 1, acc_scr[i], acc) * alpha + _dot(vall_ref[g], p_ref[...])
        acc_scr[i] = acc
        return acc

    @pl.when(jnp.logical_not(in_range))
    def _():
        diag_scores(0, s0_ref, mb0_ref)
        diag_scores(1, s1_ref, mb1_ref)
        diag_softmax(0, s0_ref, mb0_ref, p0_ref)

        def diag_body(t, carry):
            for k in range(MOBA_GROUP):
                i = MOBA_GROUP * t + k
                s_a, mb_a, p_a = bufs[k % 2]
                s_b, mb_b, p_b = bufs[(k + 1) % 2]
                diag_scores(i + 2, s_a, mb_a)
                diag_softmax(i + 1, s_b, mb_b, p_b)
                diag_accumulate(t, k, p_a)
            return carry

        lax.fori_loop(0, nb // MOBA_GROUP, diag_body, 0)

        scores(0, s0_ref, mb0_ref)
        scores(1, s1_ref, mb1_ref)
        m_init = jnp.full((1, MOBA_BLOCK), NEG_BIG, F32)
        m, alpha = softmax(0, s0_ref, mb0_ref, p0_ref, m_init)
        acc = jnp.zeros((V_AUG, MOBA_BLOCK), F32)

        def body(t, carry):
            m, alpha, acc = carry
            for k in range(MOBA_UNROLL):
                n = MOBA_UNROLL * t + k
                s_a, mb_a, p_a = bufs[k % 2]
                s_b, mb_b, p_b = bufs[(k + 1) % 2]
                acc = accumulate(n, p_a, alpha, acc)
                m, alpha = softmax(n + 1, s_b, mb_b, p_b, m)
                scores(n + 2, s_a, mb_a)
            return m, alpha, acc

        lax.fori_loop(0, n_robust // MOBA_UNROLL, body, (m, alpha, acc))
        out_scr[...] = acc_scr[...]

    def finish(i, carry):
        acc = out_scr[i]
        o = acc[:HEAD_DIM] / acc[HEAD_DIM:HEAD_DIM + 1]
        o_ref[i] = (o * sga_ref[i].astype(F32)).astype(BF16)
        return carry

    lax.fori_loop(0, nb, finish, 0, unroll=4)


def _moba_call(c1, ub, qa, kaug, vT, sel, sga):
    bsz, _, nb = qa.shape[:3]
    s = nb * MOBA_BLOCK
    gk = MOBA_GROUP * MOBA_BLOCK
    assert nb % MOBA_GROUP == 0 and nb > MOBA_GROUP and MOBA_GROUP % 2 == 0 and MOBA_UNROLL % 2 == 0
    assert MOBA_FAST_UNROLL % MOBA_UNROLL == 0
    (tabf, n_fast), (tabr, n_robust) = _moba_items(nb)
    per_bh = lambda *tail: pl.BlockSpec((None, None) + tail, lambda b, h: (b, h) + (0,) * len(tail))
    return pl.pallas_call(
        functools.partial(_moba_kernel, n_fast=n_fast, n_robust=n_robust),
        grid=(bsz, A_HEADS),
        in_specs=[
            pl.BlockSpec(memory_space=pltpu.SMEM),
            pl.BlockSpec(memory_space=pltpu.SMEM),
            pl.BlockSpec(memory_space=pltpu.SMEM),
            pl.BlockSpec(memory_space=pltpu.SMEM),
            per_bh(nb, K_AUG, MOBA_BLOCK),
            per_bh(s, K_AUG),
            per_bh(nb // MOBA_GROUP, V_AUG, gk),
            per_bh(nb, nb, MOBA_BLOCK),
            per_bh(nb, HEAD_DIM, MOBA_BLOCK),
        ],
        out_specs=per_bh(nb, HEAD_DIM, MOBA_BLOCK),
        out_shape=jax.ShapeDtypeStruct((bsz, A_HEADS, nb, HEAD_DIM, MOBA_BLOCK), BF16),
        scratch_shapes=[pltpu.VMEM((nb, 8, MOBA_BLOCK), F32), pltpu.VMEM((nb, V_AUG, MOBA_BLOCK), F32),
                        pltpu.VMEM((nb, V_AUG, MOBA_BLOCK), F32),
                        pltpu.VMEM((gk, MOBA_BLOCK), F32), pltpu.VMEM((gk, MOBA_BLOCK), F32),
                        pltpu.VMEM((MOBA_GROUP, 1, MOBA_BLOCK), F32), pltpu.VMEM((MOBA_GROUP, 1, MOBA_BLOCK), F32),
                        pltpu.VMEM((gk, MOBA_BLOCK), BF16), pltpu.VMEM((gk, MOBA_BLOCK), BF16)],
        compiler_params=pltpu.CompilerParams(dimension_semantics=("arbitrary", "arbitrary"),
                                             vmem_limit_bytes=MOBA_VMEM_LIMIT),
        name="moba",
    )(jnp.asarray(tabf), jnp.asarray(tabr), c1, ub, qa, kaug, vT, sel, sga)


def _gla_consts():
    c, sb = GLA_CHUNK, GLA_SUB
    r = np.arange(c)
    blk = r // sb
    tri = (r[None, :] <= r[:, None])
    same = blk[None, :] == blk[:, None]
    lall = np.concatenate([
        tri,
        tri & same,
        same,
        blk[None, :] == blk[:, None] - 1,
        blk[None, :] == blk[:, None] - 2,
    ], axis=0).astype(np.float32)
    diff = blk[:, None] - blk[None, :]
    band = np.stack([diff == 1, diff == 2, diff == 3, same & tri]).astype(np.float32)
    dk_head = np.arange(G_KW) // G_DK
    dv_head = np.arange(G_VW) // G_DV
    bdt = (dv_head[:, None] == dk_head[None, :]).astype(np.float32)
    lex, mex = [tri], [np.eye(c, dtype=bool)]
    for s in GLA_LEVELS:
        same_s = (r[None, :] // s) == (r[:, None] // s)
        lex += [tri & same_s, same_s]
        pair = (r[None, :] // (2 * s)) == (r[:, None] // (2 * s))
        mex.append(pair & ((r[:, None] // s) % 2 == 1) & ((r[None, :] // s) % 2 == 0))
    lex = np.concatenate(lex, axis=0).astype(np.float32)
    mex = np.stack(mex).astype(np.float32)
    return lall, band, bdt, lex, mex


def _gla_kernel(q_ref, k_ref, v_ref, g_ref, sgb_ref, lall_ref, band_ref, bdt_ref, bones_ref, gout_ref,
                lex_ref, mex_ref, o_ref, st_ref):
    c = GLA_CHUNK
    nbatch = q_ref.shape[0]

    @pl.when(pl.program_id(0) == 0)
    def _():
        st_ref[...] = jnp.zeros_like(st_ref)

    lane_k = lax.broadcasted_iota(jnp.int32, (1, G_KW), 1) // G_DK
    lane_v = lax.broadcasted_iota(jnp.int32, (1, G_VW), 1) // G_DV
    n_chunks = q_ref.shape[1] // c

    def heads(t):
        return jnp.concatenate([jnp.where(lane_k == hd, t, 0.0) for hd in range(G_HEADS)],
                               axis=0).astype(BF16)

    def chunk(ci, states):
        r0 = pl.multiple_of(ci * c, c)
        nbs = range(nbatch)
        q = [q_ref[bi, pl.ds(r0, c), :] for bi in nbs]
        k = [k_ref[bi, pl.ds(r0, c), :] for bi in nbs]
        v = [v_ref[bi, pl.ds(r0, c), :] for bi in nbs]

        r = [_dot(lall_ref[...], jnp.concatenate(_split3(g_ref[bi, pl.ds(r0, c), :]), axis=1)) for bi in nbs]
        r = [x[:, :G_KW] + x[:, G_KW:2 * G_KW] + x[:, 2 * G_KW:] for x in r]
        b, cc, tt, p1, p2 = ([x[n * c:(n + 1) * c] for x in r] for n in range(5))
        b_last = [x[c - 1:c] for x in b]

        qt = [q[bi] * jnp.exp(cc[bi]) for bi in nbs]
        q2 = [qt[bi] * jnp.exp(p1[bi]) for bi in nbs]
        q3 = [q2[bi] * jnp.exp(p2[bi]) for bi in nbs]
        kt = [(k[bi] * jnp.exp(tt[bi] - cc[bi])).astype(BF16) for bi in nbs]
        kd = [(k[bi] * jnp.exp(-cc[bi])).astype(BF16) for bi in nbs]
        qi = [(q[bi] * jnp.exp(b[bi])).astype(BF16) for bi in nbs]
        kl = [(k[bi] * jnp.exp(b_last[bi] - b[bi])).astype(BF16) for bi in nbs]

        xs = [[_dg(heads(t[bi]), kt[bi], _NT) for bi in nbs] for t in (qt, q2, q3)]
        yd = [_dg(heads(qt[bi]), kd[bi], _NT) for bi in nbs]
        o = [_dg(qi[bi], states[bi].astype(BF16), _NT) for bi in nbs]
        st_new = tuple(states[bi] * jnp.exp(b_last[bi]) + _dg(v[bi], kl[bi], _TN) * bdt_ref[...]
                       for bi in nbs)
        for hd in range(G_HEADS):
            sl = slice(hd * c, (hd + 1) * c)
            for bi in nbs:
                att = (band_ref[0] * xs[0][bi][sl] + band_ref[1] * xs[1][bi][sl]
                       + band_ref[2] * xs[2][bi][sl] + band_ref[3] * yd[bi][sl])
                vh = jnp.where(lane_v == hd, v[bi], jnp.zeros_like(v[bi]))
                o[bi] = o[bi] + _dot(att.astype(BF16), vh)

        ss = [_group_sumsq(o[bi], bones_ref[...]) for bi in nbs]
        for bi in nbs:
            y = o[bi] * lax.rsqrt(ss[bi] * (1.0 / G_DV) + EPS) * gout_ref[...]
            o_ref[bi, pl.ds(r0, c), :] = (y * sgb_ref[bi, pl.ds(r0, c), :].astype(F32)).astype(BF16)
        return st_new

    def chunk_stable(ci, states):
        r0 = pl.multiple_of(ci * c, c)
        out = []
        for bi in range(nbatch):
            q = q_ref[bi, pl.ds(r0, c), :]
            k = k_ref[bi, pl.ds(r0, c), :]
            v = v_ref[bi, pl.ds(r0, c), :]
            r = _dot(lex_ref[...], jnp.concatenate(_split3(g_ref[bi, pl.ds(r0, c), :]), axis=1))
            r = r[:, :G_KW] + r[:, G_KW:2 * G_KW] + r[:, 2 * G_KW:]
            b = r[0:c]
            b_last = b[c - 1:c]
            x = [_dg(heads(q), k.astype(BF16), _NT)]
            for lv in range(len(GLA_LEVELS)):
                cs = r[(1 + 2 * lv) * c:(2 + 2 * lv) * c]
                ts = r[(2 + 2 * lv) * c:(3 + 2 * lv) * c]
                x.append(_dg(heads(q * jnp.exp(cs)), (k * jnp.exp(ts - cs)).astype(BF16), _NT))
            o = _dg((q * jnp.exp(b)).astype(BF16), states[bi].astype(BF16), _NT)
            for hd in range(G_HEADS):
                sl = slice(hd * c, (hd + 1) * c)
                att = mex_ref[0] * x[0][sl]
                for lv in range(len(GLA_LEVELS)):
                    att = att + mex_ref[lv + 1] * x[lv + 1][sl]
                o = o + _dot(att.astype(BF16), jnp.where(lane_v == hd, v, jnp.zeros_like(v)))
            kl = (k * jnp.exp(b_last - b)).astype(BF16)
            out.append(states[bi] * jnp.exp(b_last) + _dg(v, kl, _TN) * bdt_ref[...])
            ss = _group_sumsq(o, bones_ref[...])
            y = o * lax.rsqrt(ss * (1.0 / G_DV) + EPS) * gout_ref[...]
            o_ref[bi, pl.ds(r0, c), :] = (y * sgb_ref[bi, pl.ds(r0, c), :].astype(F32)).astype(BF16)
        return tuple(out)

    def run(body, unroll):
        states = lax.fori_loop(0, n_chunks, body, tuple(st_ref[bi] for bi in range(nbatch)), unroll=unroll)
        for bi in range(nbatch):
            st_ref[bi] = states[bi]

    risky = jnp.max(-g_ref[...]) * GLA_SUB > GLA_SAFE_EXP
    pl.when(jnp.logical_not(risky))(lambda: run(chunk, 2))
    pl.when(risky)(lambda: run(chunk_stable, 1))


def _gla_call(gq, gk, gv, gg, sgbm, lall, band, bdt, bones, gout, lex, mex):
    bsz, s, _ = gq.shape
    tm = ROW_TILE
    row = lambda w: pl.BlockSpec((bsz, tm, w), lambda t: (0, t, 0))
    return pl.pallas_call(
        _gla_kernel,
        grid=(s // tm,),
        in_specs=[row(G_KW), row(G_KW), row(G_VW), row(G_KW), row(G_VW),
                  _full(lall.shape), _full(band.shape), _full(bdt.shape), _full(bones.shape), _full((1, G_VW)),
                  _full(lex.shape), _full(mex.shape)],
        out_specs=row(G_VW),
        out_shape=jax.ShapeDtypeStruct((bsz, s, G_VW), BF16),
        scratch_shapes=[pltpu.VMEM((bsz, G_VW, G_KW), F32)],
        compiler_params=_params(("arbitrary",)),
        name="gla",
    )(gq, gk, gv, gg, sgbm, lall, band, bdt, bones, gout, lex, mex)


def _mem_kernel(qm_ref, sgm_ref, mk_ref, mvx_ref, gq_ref, bones_ref, o_ref):
    qm = qm_ref[...]
    ss = _group_sumsq(qm, bones_ref[...])
    qn = qm * lax.rsqrt(ss * (1.0 / HEAD_DIM) + EPS) * gq_ref[...]
    qs = qn * (LOG2E * HEAD_DIM ** -0.5)
    lane = lax.broadcasted_iota(jnp.int32, (1, M_WIDTH), 1) // HEAD_DIM
    mk = mk_ref[...]
    s = [_dg(jnp.where(lane == h, qs, 0.0).astype(BF16), mk, _NT) for h in range(M_HEADS)]
    p = [jnp.exp2(x - jnp.max(x, axis=-1, keepdims=True)).astype(BF16) for x in s]
    acc = _dot(p[0], mvx_ref[0])
    for h in range(1, M_HEADS):
        acc = acc + _dot(p[h], mvx_ref[h])
    om = acc[:, :M_WIDTH] / acc[:, M_WIDTH:]
    o_ref[...] = (om * sgm_ref[...].astype(F32)).astype(BF16)


def _mem_call(qm, sgbm, mk, mvx, gq, bones):
    bsz, s, _ = qm.shape
    tm = ROW_TILE
    mlen = mk.shape[1]
    return pl.pallas_call(
        _mem_kernel,
        grid=(bsz, s // tm),
        in_specs=[
            pl.BlockSpec((None, tm, M_WIDTH), lambda b, t: (b, t, 0)),
            pl.BlockSpec((None, tm, M_WIDTH), lambda b, t: (b, t, 1)),
            pl.BlockSpec((None, mlen, M_WIDTH), lambda b, t: (b, 0, 0)),
            pl.BlockSpec((None, M_HEADS, mlen, 2 * M_WIDTH), lambda b, t: (b, 0, 0, 0)),
            _full((1, M_WIDTH)), _full((M_WIDTH, M_WIDTH)),
        ],
        out_specs=pl.BlockSpec((None, tm, M_WIDTH), lambda b, t: (b, t, 0)),
        out_shape=jax.ShapeDtypeStruct((bsz, s, M_WIDTH), BF16),
        compiler_params=_params(("arbitrary", "arbitrary")),
        name="memattn",
    )(qm, sgbm, mk, mvx, gq, bones)


def _out_kernel(x_ref, oaT_ref, ob_ref, om_ref, wa_ref, wb_ref, o_ref):
    y = _dot(jnp.concatenate([ob_ref[...], om_ref[...]], axis=-1), wb_ref[...])
    for t in range(oaT_ref.shape[1]):
        rows = slice(t * MOBA_BLOCK, (t + 1) * MOBA_BLOCK)
        oaT = oaT_ref[:, t].reshape(A_WIDTH, MOBA_BLOCK)
        o_ref[rows, :] = x_ref[rows, :] + y[rows] + _dg(oaT, wa_ref[...], _TN)


def _out_call(x, oaT, ob, om, wa, wb):
    bsz, s, d = x.shape
    tm = ROW_TILE
    row = lambda w: pl.BlockSpec((None, tm, w), lambda b, t: (b, t, 0))
    return pl.pallas_call(
        _out_kernel,
        grid=(bsz, s // tm),
        in_specs=[row(d),
                  pl.BlockSpec((None, A_HEADS, tm // MOBA_BLOCK, HEAD_DIM, MOBA_BLOCK), lambda b, t: (b, 0, t, 0, 0)),
                  row(G_VW), row(M_WIDTH),
                  _full(wa.shape), _full(wb.shape)],
        out_specs=row(d),
        out_shape=jax.ShapeDtypeStruct((bsz, s, d), F32),
        compiler_params=_params(("arbitrary", "arbitrary")),
        name="outproj",
    )(x, oaT, ob, om, wa, wb)


def _alibi_consts():
    slopes = np.asarray([2.0 ** (-8.0 * (i + 1) / A_HEADS) for i in range(A_HEADS)], np.float32)
    c1 = (slopes * np.float32(LOG2E)).astype(np.float32)
    c1j = jnp.asarray(c1)
    pieces = list(_split3(c1j * 16.0)) + list(_split3(c1j))
    qx = jnp.zeros((A_HEADS, K_AUG - HEAD_DIM, MOBA_BLOCK), BF16)
    for n, pc in enumerate(pieces):
        qx = qx.at[:, n, :].set(jnp.broadcast_to(pc[:, None], (A_HEADS, MOBA_BLOCK)))
    pos = np.arange(MOBA_BLOCK)
    e = np.zeros((MOBA_BLOCK, K_AUG - HEAD_DIM), np.float32)
    e[:, 0:3] = (pos // 16)[:, None]
    e[:, 3:6] = (pos % 16)[:, None]
    return c1j, qx, jnp.asarray(e)


def kernel(x, mem, g_pre, w_in, g_q_moba, g_k_moba, w_gate_up, b_gate_up, g_gla_out,
           g_mem, w_mem_kv, g_q_mem, g_k_mem, w_out):
    depth = g_pre.shape[0]
    d = x.shape[-1]
    c1, qx, e = _alibi_consts()
    lall_np, band_np, bdt_np, lex_np, mex_np = _gla_consts()
    lall = jnp.asarray(lall_np, BF16)
    band = jnp.asarray(band_np)
    bdt = jnp.asarray(bdt_np)
    lex = jnp.asarray(lex_np, BF16)
    mex = jnp.asarray(mex_np)
    bones_a = jnp.asarray(_block_ones(A_WIDTH, HEAD_DIM), BF16)
    bones_m = jnp.asarray(_block_ones(M_WIDTH, HEAD_DIM), BF16)

    o_qa, o_ka, o_va, o_ga = 0, 512, 1024, 1536
    o_qb, o_kb, o_vb, o_gb, o_rb = 2048, 2176, 2304, 2560, 2816
    o_qm, o_gm = 2832, 3088

    mk_all, mvx_all = _memkv_call(mem, g_mem, w_mem_kv, g_k_mem)

    for l in range(depth):
        w = w_in[l]
        col = lambda o, n: w[:, o:o + n]
        rb_pad = jnp.pad(col(o_rb, GATE_RANK), ((0, 0), (0, G_KW - GATE_RANK)))
        wn = jnp.concatenate([col(o_ka, 512), col(o_gb, 256), col(o_gm, 256), col(o_qb, 128), col(o_kb, 128),
                              col(o_vb, 256), rb_pad, col(o_qm, 256)], axis=1).astype(BF16)
        wt = jnp.concatenate([col(o_qa, 512), col(o_va, 512), col(o_ga, 512)], axis=1).T.astype(BF16)
        wgu = jnp.pad(w_gate_up[l], ((0, G_KW - GATE_RANK), (0, 0)))
        gk = jnp.tile(g_k_moba[l], A_HEADS).reshape(1, A_WIDTH)

        (qT, qa, kaug, vT, kmean_nat, sga, sgbm, gq, gk2, gv, gg, qm) = _proj_call(
            x, g_pre[l].reshape(1, d), wn, wt, gk, g_q_moba[l].reshape(HEAD_DIM, 1), bones_a, e, qx,
            wgu, b_gate_up[l].reshape(1, G_KW))

        bsz, s = x.shape[0], x.shape[1]
        nb = s // MOBA_BLOCK
        kmean = kmean_nat.reshape(bsz, nb, A_HEADS, HEAD_DIM).transpose(0, 2, 1, 3)
        ub = (UB_SLACK * LOG2E * HEAD_DIM ** 0.5) * jnp.max(jnp.abs(g_q_moba[l])) * jnp.max(jnp.abs(g_k_moba[l]))
        oaT = _moba_call(c1, ub.reshape(1), qa, kaug, vT, _sel_call(qT, kmean), sga)
        ob = _gla_call(gq, gk2, gv, gg, sgbm, lall, band, bdt, bones_m,
                       jnp.tile(g_gla_out[l], G_HEADS).reshape(1, G_VW), lex, mex)
        om = _mem_call(qm, sgbm, mk_all[l], mvx_all[l],
                       jnp.tile(g_q_mem[l], M_HEADS).reshape(1, M_WIDTH), bones_m)
        wo = w_out[l].astype(BF16)
        x = _out_call(x, oaT, ob, om, wo[:A_WIDTH], wo[A_WIDTH:])
    return x
```

```python
import functools

import numpy as np
import jax
import jax.numpy as jnp
from jax import lax
from jax.experimental import pallas as pl
from jax.experimental.pallas import tpu as pltpu

F32 = jnp.float32
BF16 = jnp.bfloat16
HIGHEST = lax.Precision.HIGHEST

EPS = 1e-6
LOG2E = 1.4426950408889634
HEAD_DIM = 64
A_HEADS = 8
A_WIDTH = A_HEADS * HEAD_DIM
MOBA_BLOCK = 256
MOBA_TOPK = 3
MOBA_GROUP = 4
MOBA_UNROLL = 4
MOBA_FAST_UNROLL = 16
MOBA_OWN_GROUPS = 2
SEL_WIDTH = 2048
G_HEADS = 4
G_DK = 32
G_DV = 64
G_KW = G_HEADS * G_DK
G_VW = G_HEADS * G_DV
GATE_RANK = 16
GATE_TEMP = 16.0
GLA_CHUNK = 64
GLA_SUB = 16
GLA_LEVELS = (32, 16, 8, 4, 2, 1)
GLA_SAFE_EXP = 80.0
M_HEADS = 4
M_WIDTH = M_HEADS * HEAD_DIM
K_AUG = 128
V_AUG = 80
ROW_TILE = 512
OUT_TILE = 1024
V7X_VMEM_BYTES = 64 * 1024 * 1024
VMEM_LIMIT = 3 * V7X_VMEM_BYTES // 4
MOBA_VMEM_LIMIT = 29 * V7X_VMEM_BYTES // 32

NEG_BIG = -1e30
POS_BIG = 1e30
DEN_MIN = 1e-18
DEN_MAX = 1e30
UB_SLACK = 1.02

_NT = (((1,), (1,)), ((), ()))
_TN = (((0,), (0,)), ((), ()))


def _dot(a, b, **kw):
    return jnp.dot(a, b, preferred_element_type=F32, **kw)


def _dg(a, b, dims, **kw):
    return lax.dot_general(a, b, dims, preferred_element_type=F32, **kw)


def _split2(v):
    hi = v.astype(BF16)
    lo = (v - hi.astype(F32)).astype(BF16)
    return hi, lo


def _split3(v):
    hi = v.astype(BF16)
    r = v - hi.astype(F32)
    mid = r.astype(BF16)
    lo = (r - mid.astype(F32)).astype(BF16)
    return hi, mid, lo


def _group_sumsq(v, bones):
    hi, lo = _split2(v * v)
    return _dot(hi, bones) + _dot(lo, bones)


def _silu(v):
    return v / (1.0 + jnp.exp(-v))


def _block_ones(n, g):
    i = np.arange(n) // g
    return (i[:, None] == i[None, :]).astype(np.float32)


def _params(sem):
    return pltpu.CompilerParams(dimension_semantics=sem, vmem_limit_bytes=VMEM_LIMIT)


def _full(shape):
    return pl.BlockSpec(shape, lambda *_: (0,) * len(shape))


def _memkv_kernel(mem_ref, gmem_ref, w_ref, gk_ref, bones_ref, mk_ref, mvx_ref):
    m = mem_ref[...]
    ms = jnp.mean(m * m, axis=-1, keepdims=True)
    hn = (m * lax.rsqrt(ms + EPS) * gmem_ref[...]).astype(BF16)
    kv = _dot(hn, w_ref[...])
    mk = kv[:, :M_WIDTH]
    mv = kv[:, M_WIDTH:]
    ss = _group_sumsq(mk, bones_ref[...])
    mk_ref[...] = (mk * lax.rsqrt(ss * (1.0 / HEAD_DIM) + EPS) * gk_ref[...]).astype(BF16)
    lane = lax.broadcasted_iota(jnp.int32, mv.shape, 1) // HEAD_DIM
    for h in range(M_HEADS):
        hm = lane == h
        mvx_ref[h] = jnp.concatenate(
            [jnp.where(hm, mv, 0.0), jnp.where(hm, 1.0, 0.0)], axis=-1).astype(BF16)


def _memkv_call(mem, g_mem, w_mem_kv, g_k_mem):
    depth = g_mem.shape[0]
    bsz, mlen, d = mem.shape
    bones = jnp.asarray(_block_ones(M_WIDTH, HEAD_DIM), BF16)
    gk = jnp.tile(g_k_mem, (1, M_HEADS)).reshape(depth, 1, M_WIDTH)
    return pl.pallas_call(
        _memkv_kernel,
        grid=(depth, bsz),
        in_specs=[
            pl.BlockSpec((None, mlen, d), lambda l, b: (b, 0, 0)),
            pl.BlockSpec((None, 1, d), lambda l, b: (l, 0, 0)),
            pl.BlockSpec((None, d, 2 * M_WIDTH), lambda l, b: (l, 0, 0)),
            pl.BlockSpec((None, 1, M_WIDTH), lambda l, b: (l, 0, 0)),
            _full((M_WIDTH, M_WIDTH)),
        ],
        out_specs=[
            pl.BlockSpec((None, None, mlen, M_WIDTH), lambda l, b: (l, b, 0, 0)),
            pl.BlockSpec((None, None, M_HEADS, mlen, 2 * M_WIDTH), lambda l, b: (l, b, 0, 0, 0)),
        ],
        out_shape=[
            jax.ShapeDtypeStruct((depth, bsz, mlen, M_WIDTH), BF16),
            jax.ShapeDtypeStruct((depth, bsz, M_HEADS, mlen, 2 * M_WIDTH), BF16),
        ],
        compiler_params=_params(("arbitrary", "arbitrary")),
        name="memkv",
    )(mem, g_mem.reshape(depth, 1, d), w_mem_kv.astype(BF16), gk, bones)


_N_K = (0, 512)
_N_GBM = (512, 1024)
_N_QB = (1024, 1152)
_N_KB = (1152, 1280)
_N_VB = (1280, 1536)
_N_RB = (1536, 1664)
_N_QM = (1664, 1920)


def _proj_kernel(x_ref, gpre_ref, wn_ref, wt_ref, gk_ref, gqcol_ref, bones_ref, e_ref, qx_ref, wgu_ref, bgu_ref,
                 qT_ref, qa_ref, kaug_ref, vT_ref, kmean_ref, sga_ref, sgbm_ref, gq_ref, gk2_ref, gv_ref, gg_ref,
                 qm_ref):
    tm = x_ref.shape[0]
    nblk = tm // MOBA_BLOCK
    x = x_ref[...]
    ms = jnp.mean(x * x, axis=-1, keepdims=True)
    h = (x * lax.rsqrt(ms + EPS) * gpre_ref[...]).astype(BF16)

    def nat(cols):
        return _dot(h, wn_ref[:, cols[0]:cols[1]])

    k = nat(_N_K)
    qT = _dg(wt_ref[0:A_WIDTH, :], h, _NT)

    ss = _group_sumsq(k, bones_ref[...])
    kn = k * lax.rsqrt(ss * (1.0 / HEAD_DIM) + EPS) * gk_ref[...]
    kmean_ref[...] = jnp.mean(kn.reshape(nblk, MOBA_BLOCK, A_WIDTH), axis=1)
    e = jnp.concatenate([e_ref[...]] * nblk, axis=0)
    for hh in range(A_HEADS):
        rows = kn[:, hh * HEAD_DIM:(hh + 1) * HEAD_DIM]
        kaug_ref[hh] = jnp.concatenate([rows, e], axis=-1).astype(BF16)

    q3 = qT.reshape(A_HEADS, HEAD_DIM, tm)
    msq = jnp.mean(q3 * q3, axis=1, keepdims=True)
    qn = q3 * lax.rsqrt(msq + EPS) * gqcol_ref[...].reshape(1, HEAD_DIM, 1)
    qT_ref[...] = qn.reshape(A_WIDTH, tm)
    qs = (qn * (LOG2E * HEAD_DIM ** -0.5)).astype(BF16)
    blocks = [slice(t * MOBA_BLOCK, (t + 1) * MOBA_BLOCK) for t in range(nblk)]
    for hh in range(A_HEADS):
        for t, cols in enumerate(blocks):
            qa_ref[hh, t] = jnp.concatenate([qs[hh][:, cols], qx_ref[hh]], axis=0)

    vT = _dg(wt_ref[A_WIDTH:2 * A_WIDTH, :], h, _NT)
    ones_rows = jnp.where(lax.broadcasted_iota(jnp.int32, (V_AUG - HEAD_DIM, tm), 0) == 0, 1.0, 0.0)
    sga = _silu(_dg(wt_ref[2 * A_WIDTH:3 * A_WIDTH, :], h, _NT)).astype(BF16)
    for hh in range(A_HEADS):
        rows = slice(hh * HEAD_DIM, (hh + 1) * HEAD_DIM)
        vT_ref[hh] = jnp.concatenate([vT[rows], ones_rows], axis=0).astype(BF16)
        for t, cols in enumerate(blocks):
            sga_ref[hh, t] = sga[rows, cols]
    sgbm_ref[...] = _silu(nat(_N_GBM)).astype(BF16)

    qkb = nat((_N_QB[0], _N_KB[1]))
    gq_ref[...] = qkb[:, :G_KW] * (G_DK ** -0.5)
    gk2_ref[...] = qkb[:, G_KW:]
    rb = nat(_N_RB)
    gv_ref[...] = nat(_N_VB).astype(BF16)
    qm_ref[...] = nat(_N_QM)
    z = _dot(rb, wgu_ref[...], precision=HIGHEST) + bgu_ref[...]
    gg_ref[...] = (jnp.minimum(z, 0.0) - jnp.log(1.0 + jnp.exp(-jnp.abs(z)))) * (1.0 / GATE_TEMP)


def _proj_call(x, g_pre, wn, wt, gk, gqcol, bones, e, qx, wgu, bgu):
    bsz, s, d = x.shape
    tm = ROW_TILE
    nt = s // tm
    nblk = tm // MOBA_BLOCK
    nb = s // MOBA_BLOCK
    row = lambda w: pl.BlockSpec((None, tm, w), lambda b, t: (b, t, 0))
    in_specs = [
        row(d), _full((1, d)), _full(wn.shape), _full(wt.shape), _full((1, A_WIDTH)),
        _full((HEAD_DIM, 1)), _full((A_WIDTH, A_WIDTH)), _full(e.shape), _full(qx.shape), _full(wgu.shape),
        _full((1, G_KW)),
    ]
    group = MOBA_GROUP * MOBA_BLOCK
    assert d == wn.shape[0] and s % group == 0 and group % tm == 0 and tm % MOBA_BLOCK == 0
    tpg = group // tm
    blocked = lambda r: pl.BlockSpec((None, A_HEADS, nblk, r, MOBA_BLOCK), lambda b, t: (b, 0, t, 0, 0))
    out_specs = [
        pl.BlockSpec((None, A_WIDTH, tm), lambda b, t: (b, 0, t)),
        blocked(K_AUG),
        pl.BlockSpec((None, A_HEADS, tm, K_AUG), lambda b, t: (b, 0, t, 0)),
        pl.BlockSpec((None, A_HEADS, None, V_AUG, tm), lambda b, t: (b, 0, t // tpg, 0, t % tpg)),
        pl.BlockSpec((None, None, nblk, A_WIDTH), lambda b, t: (b, t, 0, 0)),
        blocked(HEAD_DIM),
        row(2 * M_WIDTH), row(G_KW), row(G_KW), row(G_VW), row(G_KW), row(M_WIDTH),
    ]
    out_shape = [
        jax.ShapeDtypeStruct((bsz, A_WIDTH, s), F32),
        jax.ShapeDtypeStruct((bsz, A_HEADS, nb, K_AUG, MOBA_BLOCK), BF16),
        jax.ShapeDtypeStruct((bsz, A_HEADS, s, K_AUG), BF16),
        jax.ShapeDtypeStruct((bsz, A_HEADS, s // group, V_AUG, group), BF16),
        jax.ShapeDtypeStruct((bsz, nt, nblk, A_WIDTH), F32),
        jax.ShapeDtypeStruct((bsz, A_HEADS, nb, HEAD_DIM, MOBA_BLOCK), BF16),
        jax.ShapeDtypeStruct((bsz, s, 2 * M_WIDTH), BF16),
        jax.ShapeDtypeStruct((bsz, s, G_KW), F32),
        jax.ShapeDtypeStruct((bsz, s, G_KW), F32),
        jax.ShapeDtypeStruct((bsz, s, G_VW), BF16),
        jax.ShapeDtypeStruct((bsz, s, G_KW), F32),
        jax.ShapeDtypeStruct((bsz, s, M_WIDTH), F32),
    ]
    return pl.pallas_call(
        _proj_kernel, grid=(bsz, nt), in_specs=in_specs, out_specs=out_specs, out_shape=out_shape,
        compiler_params=_params(("arbitrary", "arbitrary")), name="proj",
    )(x, g_pre, wn, wt, gk, gqcol, bones, e, qx, wgu, bgu)


def _sel_kernel(qT_ref, kmean_ref, o_ref):
    width = qT_ref.shape[1]
    nb = kmean_ref.shape[0]
    per = width // MOBA_BLOCK

    def choose(first):
        rows = min(nb, first + per)
        gate = _dot(kmean_ref[0:rows, :], qT_ref[...], precision=HIGHEST)
        rowf = lax.broadcasted_iota(jnp.int32, gate.shape, 0).astype(F32)
        qblk = (first + lax.broadcasted_iota(jnp.int32, gate.shape, 1) // MOBA_BLOCK).astype(F32)
        g = jnp.where(rowf < qblk, gate, -jnp.inf)
        sel = jnp.zeros(gate.shape, F32)
        for _ in range(MOBA_TOPK):
            m = jnp.max(g, axis=0, keepdims=True)
            idx = jnp.min(jnp.where(g == m, rowf, float(rows)), axis=0, keepdims=True)
            hit = rowf == idx
            sel = jnp.where(hit, 1.0, sel)
            g = jnp.where(hit, -jnp.inf, g)
        sel = jnp.where(rowf < qblk, sel, 0.0)
        for u in range(per):
            o_ref[u, 0:rows, :] = sel[:, u * MOBA_BLOCK:(u + 1) * MOBA_BLOCK]
            if rows < nb:
                o_ref[u, rows:nb, :] = jnp.zeros((nb - rows, MOBA_BLOCK), F32)

    for step in range(nb // per):
        pl.when(pl.program_id(2) == step)(functools.partial(choose, step * per))


def _sel_call(qT, kmean):
    bsz, _, s = qT.shape
    nb = s // MOBA_BLOCK
    width = min(SEL_WIDTH, s)
    return pl.pallas_call(
        _sel_kernel,
        grid=(bsz, A_HEADS, s // width),
        in_specs=[
            pl.BlockSpec((None, HEAD_DIM, width), lambda b, h, c: (b, h, c)),
            pl.BlockSpec((None, None, nb, HEAD_DIM), lambda b, h, c: (b, h, 0, 0)),
        ],
        out_specs=pl.BlockSpec((None, None, width // MOBA_BLOCK, nb, MOBA_BLOCK), lambda b, h, c: (b, h, c, 0, 0)),
        out_shape=jax.ShapeDtypeStruct((bsz, A_HEADS, nb, nb, MOBA_BLOCK), F32),
        compiler_params=_params(("arbitrary", "arbitrary", "arbitrary")),
        name="mobasel",
    )(qT, kmean)


def _moba_items(nb):
    def table(rows, multiple):
        n_items = -(-len(rows) // multiple) * multiple
        rows = rows + [(rows[-1][0], 0, 0, 0)] * (n_items + 2 - len(rows))
        return np.asarray(rows, np.int32).T, n_items

    fast = [(i, g, int(g == 0), 1) for i in range(MOBA_GROUP, nb) for g in range(i // MOBA_GROUP)]
    robust = [(i, g, int(g == 0), 1) for i in range(1, nb) for g in range((i - 1) // MOBA_GROUP + 1)]
    return table(fast, MOBA_FAST_UNROLL), table(robust, MOBA_UNROLL)


def _moba_kernel(tabf_ref, tabr_ref, c1_ref, ub_ref, qa_ref, kall_ref, vall_ref, sel_ref, sga_ref, o_ref,
                 m_scr, acc_scr, out_scr, s0_ref, s1_ref, mb0_ref, mb1_ref, p0_ref, p1_ref,
                 *, n_fast, n_robust):
    hh = pl.program_id(1)
    c1 = c1_ref[hh]
    ub = ub_ref[0]
    nb = qa_ref.shape[0]
    gk = MOBA_GROUP * MOBA_BLOCK
    half = gk // 2
    per_half = MOBA_GROUP // 2
    bufs = ((s0_ref, mb0_ref, p0_ref), (s1_ref, mb1_ref, p1_ref))
    kk = lax.broadcasted_iota(jnp.int32, (MOBA_BLOCK, MOBA_BLOCK), 0)
    qq = lax.broadcasted_iota(jnp.int32, (MOBA_BLOCK, MOBA_BLOCK), 1)

    sigma = ub + c1 * lax.broadcasted_iota(jnp.int32, (1, MOBA_BLOCK), 1).astype(F32)

    def past_probs(s_blk, i, j, valid):
        picked = jnp.where(valid == 1, sel_ref[i, pl.ds(j, 1), :], 0.0) > 0.5
        cj = c1 * (MOBA_BLOCK * (j - i)).astype(F32)
        shift = jnp.where(picked, sigma - cj, POS_BIG)
        return jnp.exp2(s_blk - shift).astype(BF16)

    def own_probs(t, k, p_ref):
        t = jnp.minimum(t, nb // MOBA_GROUP - 1)
        i = MOBA_GROUP * t + k
        for part in range(k // per_half + 1):
            members = min(per_half, k + 1 - per_half * part)
            r0 = pl.multiple_of(t * gk + part * half, half)
            s = _dot(kall_ref[pl.ds(r0, members * MOBA_BLOCK), :], qa_ref[i])
            for u2 in range(members):
                u = per_half * part + u2
                blk = s[u2 * MOBA_BLOCK:(u2 + 1) * MOBA_BLOCK]
                if u == k:
                    p = jnp.exp2(jnp.where(kk <= qq, blk, NEG_BIG) - sigma).astype(BF16)
                else:
                    p = past_probs(blk, i, MOBA_GROUP * t + u, 1)
                p_ref[u * MOBA_BLOCK:(u + 1) * MOBA_BLOCK, :] = p

    def own_accumulate(t, k, p_ref):
        n = (k + 1) * MOBA_BLOCK
        acc = _dot(vall_ref[t][:, 0:n], p_ref[0:n, :])
        acc_scr[MOBA_GROUP * t + k] = acc
        out_scr[MOBA_GROUP * t + k] = acc

    own_probs(0, 0, p0_ref)

    def own_body(tt, carry):
        for step in range(MOBA_OWN_GROUPS * MOBA_GROUP):
            t, k = MOBA_OWN_GROUPS * tt + step // MOBA_GROUP, step % MOBA_GROUP
            nxt = (t, k + 1) if k + 1 < MOBA_GROUP else (t + 1, 0)
            own_probs(nxt[0], nxt[1], bufs[(step + 1) % 2][2])
            own_accumulate(t, k, bufs[step % 2][2])
        return carry

    lax.fori_loop(0, nb // (MOBA_GROUP * MOBA_OWN_GROUPS), own_body, 0)

    def fast_probs(n, p_ref):
        i, g, valid = tabf_ref[0, n], tabf_ref[1, n], tabf_ref[3, n]
        for part in range(2):
            r0 = pl.multiple_of(g * gk + part * half, half)
            s = _dot(kall_ref[pl.ds(r0, half), :], qa_ref[i])
            for u2 in range(per_half):
                u = per_half * part + u2
                p_ref[u * MOBA_BLOCK:(u + 1) * MOBA_BLOCK, :] = past_probs(
                    s[u2 * MOBA_BLOCK:(u2 + 1) * MOBA_BLOCK], i, g * MOBA_GROUP + u, valid)

    def fast_accumulate(n, p_ref, acc):
        i, g, first = tabf_ref[0, n], tabf_ref[1, n], tabf_ref[2, n]
        acc = jnp.where(first == 1, acc_scr[i], acc) + _dot(vall_ref[g], p_ref[...])
        out_scr[i] = acc
        return acc

    fast_probs(0, p0_ref)

    def fast_body(t, acc):
        for k in range(MOBA_FAST_UNROLL):
            n = MOBA_FAST_UNROLL * t + k
            fast_probs(n + 1, bufs[(k + 1) % 2][2])
            acc = fast_accumulate(n, bufs[k % 2][2], acc)
        return acc

    lax.fori_loop(0, n_fast // MOBA_FAST_UNROLL, fast_body, jnp.zeros((V_AUG, MOBA_BLOCK), F32))

    den = out_scr[:, HEAD_DIM:HEAD_DIM + 1, :]
    in_range = jnp.logical_and(jnp.min(den) > DEN_MIN, jnp.max(den) < DEN_MAX)

    def item(n):
        return tabr_ref[0, n], tabr_ref[1, n], tabr_ref[2, n], tabr_ref[3, n]

    def diag_scores(i, s_ref, mb_ref):
        i = jnp.minimum(i, nb - 1)
        r0 = pl.multiple_of(i * MOBA_BLOCK, MOBA_BLOCK)
        s = _dot(kall_ref[pl.ds(r0, MOBA_BLOCK), :], qa_ref[i])
        s = jnp.where(kk <= qq, s, NEG_BIG)
        s_ref[0:MOBA_BLOCK, :] = s
        mb_ref[0] = jnp.max(s, axis=0, keepdims=True)

    def diag_softmax(i, s_ref, mb_ref, p_ref):
        i = jnp.minimum(i, nb - 1)
        m0 = mb_ref[0]
        m_scr[i] = jnp.broadcast_to(m0, m_scr.shape[1:])
        p_ref[0:MOBA_BLOCK, :] = jnp.exp2(s_ref[0:MOBA_BLOCK, :] - m0).astype(BF16)

    def diag_accumulate(t, k, p_ref):
        vown = vall_ref[t][:, k * MOBA_BLOCK:(k + 1) * MOBA_BLOCK]
        acc_scr[MOBA_GROUP * t + k] = _dot(vown, p_ref[0:MOBA_BLOCK, :])

    def scores(n, s_ref, mb_ref):
        i, g, _, _ = item(n)
        for part in range(2):
            r0 = pl.multiple_of(g * gk + part * half, half)
            s = _dot(kall_ref[pl.ds(r0, half), :], qa_ref[i])
            s_ref[part * half:(part + 1) * half, :] = s
            for u in range(per_half):
                blk = s[u * MOBA_BLOCK:(u + 1) * MOBA_BLOCK]
                mb_ref[part * per_half + u] = jnp.max(blk, axis=0, keepdims=True)

    def softmax(n, s_ref, mb_ref, p_ref, m):
        i, g, first, valid = item(n)
        m = jnp.where(first == 1, m_scr[i][0:1], m)
        m_new = m
        members = []
        for u in range(MOBA_GROUP):
            j = g * MOBA_GROUP + u
            picked = jnp.where(valid == 1, sel_ref[i, pl.ds(j, 1), :], 0.0) > 0.5
            cj = c1 * (MOBA_BLOCK * (j - i)).astype(F32)
            m_new = jnp.where(picked, jnp.maximum(m_new, mb_ref[u] + cj), m_new)
            members.append((picked, cj))
        for u, (picked, cj) in enumerate(members):
            shift = jnp.where(picked, m_new - cj, POS_BIG)
            rows = slice(u * MOBA_BLOCK, (u + 1) * MOBA_BLOCK)
            p_ref[rows, :] = jnp.exp2(s_ref[rows, :] - shift).astype(BF16)
        return m_new, jnp.exp2(m - m_new)

    def accumulate(n, p_ref, alpha, acc):
        i, g, first, _ = item(n)
        acc = jnp.where(first == 1, acc_scr[i], acc) * alpha + _dot(vall_ref[g], p_ref[...])
        acc_scr[i] = acc
        return acc

    @pl.when(jnp.logical_not(in_range))
    def _():
        diag_scores(0, s0_ref, mb0_ref)
        diag_scores(1, s1_ref, mb1_ref)
        diag_softmax(0, s0_ref, mb0_ref, p0_ref)

        def diag_body(t, carry):
            for k in range(MOBA_GROUP):
                i = MOBA_GROUP * t + k
                s_a, mb_a, p_a = bufs[k % 2]
                s_b, mb_b, p_b = bufs[(k + 1) % 2]
                diag_scores(i + 2, s_a, mb_a)
                diag_softmax(i + 1, s_b, mb_b, p_b)
                diag_accumulate(t, k, p_a)
            return carry

        lax.fori_loop(0, nb // MOBA_GROUP, diag_body, 0)

        scores(0, s0_ref, mb0_ref)
        scores(1, s1_ref, mb1_ref)
        m_init = jnp.full((1, MOBA_BLOCK), NEG_BIG, F32)
        m, alpha = softmax(0, s0_ref, mb0_ref, p0_ref, m_init)
        acc = jnp.zeros((V_AUG, MOBA_BLOCK), F32)

        def body(t, carry):
            m, alpha, acc = carry
            for k in range(MOBA_UNROLL):
                n = MOBA_UNROLL * t + k
                s_a, mb_a, p_a = bufs[k % 2]
                s_b, mb_b, p_b = bufs[(k + 1) % 2]
                acc = accumulate(n, p_a, alpha, acc)
                m, alpha = softmax(n + 1, s_b, mb_b, p_b, m)
                scores(n + 2, s_a, mb_a)
            return m, alpha, acc

        lax.fori_loop(0, n_robust // MOBA_UNROLL, body, (m, alpha, acc))
        out_scr[...] = acc_scr[...]

    def finish(i, carry):
        acc = out_scr[i]
        o = acc[:HEAD_DIM] / acc[HEAD_DIM:HEAD_DIM + 1]
        o_ref[i] = (o * sga_ref[i].astype(F32)).astype(BF16)
        return carry

    lax.fori_loop(0, nb, finish, 0, unroll=4)


def _moba_call(c1, ub, qa, kaug, vT, sel, sga):
    bsz, _, nb = qa.shape[:3]
    s = nb * MOBA_BLOCK
    gk = MOBA_GROUP * MOBA_BLOCK
    assert nb % (MOBA_GROUP * MOBA_OWN_GROUPS) == 0 and MOBA_GROUP % 2 == 0 and MOBA_UNROLL % 2 == 0
    assert MOBA_FAST_UNROLL % MOBA_UNROLL == 0
    (tabf, n_fast), (tabr, n_robust) = _moba_items(nb)
    per_bh = lambda *tail: pl.BlockSpec((None, None) + tail, lambda b, h: (b, h) + (0,) * len(tail))
    return pl.pallas_call(
        functools.partial(_moba_kernel, n_fast=n_fast, n_robust=n_robust),
        grid=(bsz, A_HEADS),
        in_specs=[
            pl.BlockSpec(memory_space=pltpu.SMEM),
            pl.BlockSpec(memory_space=pltpu.SMEM),
            pl.BlockSpec(memory_space=pltpu.SMEM),
            pl.BlockSpec(memory_space=pltpu.SMEM),
            per_bh(nb, K_AUG, MOBA_BLOCK),
            per_bh(s, K_AUG),
            per_bh(nb // MOBA_GROUP, V_AUG, gk),
            per_bh(nb, nb, MOBA_BLOCK),
            per_bh(nb, HEAD_DIM, MOBA_BLOCK),
        ],
        out_specs=per_bh(nb, HEAD_DIM, MOBA_BLOCK),
        out_shape=jax.ShapeDtypeStruct((bsz, A_HEADS, nb, HEAD_DIM, MOBA_BLOCK), BF16),
        scratch_shapes=[pltpu.VMEM((nb, 8, MOBA_BLOCK), F32), pltpu.VMEM((nb, V_AUG, MOBA_BLOCK), F32),
                        pltpu.VMEM((nb, V_AUG, MOBA_BLOCK), F32),
                        pltpu.VMEM((gk, MOBA_BLOCK), F32), pltpu.VMEM((gk, MOBA_BLOCK), F32),
                        pltpu.VMEM((MOBA_GROUP, 1, MOBA_BLOCK), F32), pltpu.VMEM((MOBA_GROUP, 1, MOBA_BLOCK), F32),
                        pltpu.VMEM((gk, MOBA_BLOCK), BF16), pltpu.VMEM((gk, MOBA_BLOCK), BF16)],
        compiler_params=pltpu.CompilerParams(dimension_semantics=("arbitrary", "arbitrary"),
                                             vmem_limit_bytes=MOBA_VMEM_LIMIT),
        name="moba",
    )(jnp.asarray(tabf), jnp.asarray(tabr), c1, ub, qa, kaug, vT, sel, sga)


def _gla_consts():
    c, sb = GLA_CHUNK, GLA_SUB
    r = np.arange(c)
    blk = r // sb
    tri = (r[None, :] <= r[:, None])
    same = blk[None, :] == blk[:, None]
    lall = np.concatenate([
        tri,
        tri & same,
        same,
        blk[None, :] == blk[:, None] - 1,
        blk[None, :] == blk[:, None] - 2,
    ], axis=0).astype(np.float32)
    diff = blk[:, None] - blk[None, :]
    band = np.stack([diff == 1, diff == 2, diff == 3, same & tri]).astype(np.float32)
    dk_head = np.arange(G_KW) // G_DK
    dv_head = np.arange(G_VW) // G_DV
    bdt = (dv_head[:, None] == dk_head[None, :]).astype(np.float32)
    lex, mex = [tri], [np.eye(c, dtype=bool)]
    for s in GLA_LEVELS:
        same_s = (r[None, :] // s) == (r[:, None] // s)
        lex += [tri & same_s, same_s]
        pair = (r[None, :] // (2 * s)) == (r[:, None] // (2 * s))
        mex.append(pair & ((r[:, None] // s) % 2 == 1) & ((r[None, :] // s) % 2 == 0))
    lex = np.concatenate(lex, axis=0).astype(np.float32)
    mex = np.stack(mex).astype(np.float32)
    return lall, band, bdt, lex, mex


def _gla_kernel(q_ref, k_ref, v_ref, g_ref, sgb_ref, lall_ref, band_ref, bdt_ref, bones_ref, gout_ref,
                lex_ref, mex_ref, o_ref, st_ref):
    c = GLA_CHUNK
    nbatch = q_ref.shape[0]

    @pl.when(pl.program_id(0) == 0)
    def _():
        st_ref[...] = jnp.zeros_like(st_ref)

    lane_k = lax.broadcasted_iota(jnp.int32, (1, G_KW), 1) // G_DK
    lane_v = lax.broadcasted_iota(jnp.int32, (1, G_VW), 1) // G_DV
    n_chunks = q_ref.shape[1] // c

    def heads(t):
        return jnp.concatenate([jnp.where(lane_k == hd, t, 0.0) for hd in range(G_HEADS)],
                               axis=0).astype(BF16)

    def chunk(ci, states):
        r0 = pl.multiple_of(ci * c, c)
        nbs = range(nbatch)
        q = [q_ref[bi, pl.ds(r0, c), :] for bi in nbs]
        k = [k_ref[bi, pl.ds(r0, c), :] for bi in nbs]
        v = [v_ref[bi, pl.ds(r0, c), :] for bi in nbs]

        r = [_dot(lall_ref[...], jnp.concatenate(_split3(g_ref[bi, pl.ds(r0, c), :]), axis=1)) for bi in nbs]
        r = [x[:, :G_KW] + x[:, G_KW:2 * G_KW] + x[:, 2 * G_KW:] for x in r]
        b, cc, tt, p1, p2 = ([x[n * c:(n + 1) * c] for x in r] for n in range(5))
        b_last = [x[c - 1:c] for x in b]

        qt = [q[bi] * jnp.exp(cc[bi]) for bi in nbs]
        q2 = [qt[bi] * jnp.exp(p1[bi]) for bi in nbs]
        q3 = [q2[bi] * jnp.exp(p2[bi]) for bi in nbs]
        kt = [(k[bi] * jnp.exp(tt[bi] - cc[bi])).astype(BF16) for bi in nbs]
        kd = [(k[bi] * jnp.exp(-cc[bi])).astype(BF16) for bi in nbs]
        qi = [(q[bi] * jnp.exp(b[bi])).astype(BF16) for bi in nbs]
        kl = [(k[bi] * jnp.exp(b_last[bi] - b[bi])).astype(BF16) for bi in nbs]

        xs = [[_dg(heads(t[bi]), kt[bi], _NT) for bi in nbs] for t in (qt, q2, q3)]
        yd = [_dg(heads(qt[bi]), kd[bi], _NT) for bi in nbs]
        o = [_dg(qi[bi], states[bi].astype(BF16), _NT) for bi in nbs]
        st_new = tuple(states[bi] * jnp.exp(b_last[bi]) + _dg(v[bi], kl[bi], _TN) * bdt_ref[...]
                       for bi in nbs)
        for hd in range(G_HEADS):
            sl = slice(hd * c, (hd + 1) * c)
            for bi in nbs:
                att = (band_ref[0] * xs[0][bi][sl] + band_ref[1] * xs[1][bi][sl]
                       + band_ref[2] * xs[2][bi][sl] + band_ref[3] * yd[bi][sl])
                vh = jnp.where(lane_v == hd, v[bi], jnp.zeros_like(v[bi]))
                o[bi] = o[bi] + _dot(att.astype(BF16), vh)

        ss = [_group_sumsq(o[bi], bones_ref[...]) for bi in nbs]
        for bi in nbs:
            y = o[bi] * lax.rsqrt(ss[bi] * (1.0 / G_DV) + EPS) * gout_ref[...]
            o_ref[bi, pl.ds(r0, c), :] = (y * sgb_ref[bi, pl.ds(r0, c), :].astype(F32)).astype(BF16)
        return st_new

    def chunk_stable(ci, states):
        r0 = pl.multiple_of(ci * c, c)
        out = []
        for bi in range(nbatch):
            q = q_ref[bi, pl.ds(r0, c), :]
            k = k_ref[bi, pl.ds(r0, c), :]
            v = v_ref[bi, pl.ds(r0, c), :]
            r = _dot(lex_ref[...], jnp.concatenate(_split3(g_ref[bi, pl.ds(r0, c), :]), axis=1))
            r = r[:, :G_KW] + r[:, G_KW:2 * G_KW] + r[:, 2 * G_KW:]
            b = r[0:c]
            b_last = b[c - 1:c]
            x = [_dg(heads(q), k.astype(BF16), _NT)]
            for lv in range(len(GLA_LEVELS)):
                cs = r[(1 + 2 * lv) * c:(2 + 2 * lv) * c]
                ts = r[(2 + 2 * lv) * c:(3 + 2 * lv) * c]
                x.append(_dg(heads(q * jnp.exp(cs)), (k * jnp.exp(ts - cs)).astype(BF16), _NT))
            o = _dg((q * jnp.exp(b)).astype(BF16), states[bi].astype(BF16), _NT)
            for hd in range(G_HEADS):
                sl = slice(hd * c, (hd + 1) * c)
                att = mex_ref[0] * x[0][sl]
                for lv in range(len(GLA_LEVELS)):
                    att = att + mex_ref[lv + 1] * x[lv + 1][sl]
                o = o + _dot(att.astype(BF16), jnp.where(lane_v == hd, v, jnp.zeros_like(v)))
            kl = (k * jnp.exp(b_last - b)).astype(BF16)
            out.append(states[bi] * jnp.exp(b_last) + _dg(v, kl, _TN) * bdt_ref[...])
            ss = _group_sumsq(o, bones_ref[...])
            y = o * lax.rsqrt(ss * (1.0 / G_DV) + EPS) * gout_ref[...]
            o_ref[bi, pl.ds(r0, c), :] = (y * sgb_ref[bi, pl.ds(r0, c), :].astype(F32)).astype(BF16)
        return tuple(out)

    def run(body, unroll):
        states = lax.fori_loop(0, n_chunks, body, tuple(st_ref[bi] for bi in range(nbatch)), unroll=unroll)
        for bi in range(nbatch):
            st_ref[bi] = states[bi]

    risky = jnp.max(-g_ref[...]) * GLA_SUB > GLA_SAFE_EXP
    pl.when(jnp.logical_not(risky))(lambda: run(chunk, 2))
    pl.when(risky)(lambda: run(chunk_stable, 1))


def _gla_call(gq, gk, gv, gg, sgbm, lall, band, bdt, bones, gout, lex, mex):
    bsz, s, _ = gq.shape
    tm = ROW_TILE
    row = lambda w: pl.BlockSpec((bsz, tm, w), lambda t: (0, t, 0))
    return pl.pallas_call(
        _gla_kernel,
        grid=(s // tm,),
        in_specs=[row(G_KW), row(G_KW), row(G_VW), row(G_KW), row(G_VW),
                  _full(lall.shape), _full(band.shape), _full(bdt.shape), _full(bones.shape), _full((1, G_VW)),
                  _full(lex.shape), _full(mex.shape)],
        out_specs=row(G_VW),
        out_shape=jax.ShapeDtypeStruct((bsz, s, G_VW), BF16),
        scratch_shapes=[pltpu.VMEM((bsz, G_VW, G_KW), F32)],
        compiler_params=_params(("arbitrary",)),
        name="gla",
    )(gq, gk, gv, gg, sgbm, lall, band, bdt, bones, gout, lex, mex)


def _mem_kernel(qm_ref, sgm_ref, mk_ref, mvx_ref, gq_ref, bones_ref, o_ref):
    qm = qm_ref[...]
    ss = _group_sumsq(qm, bones_ref[...])
    qn = qm * lax.rsqrt(ss * (1.0 / HEAD_DIM) + EPS) * gq_ref[...]
    qs = qn * (LOG2E * HEAD_DIM ** -0.5)
    lane = lax.broadcasted_iota(jnp.int32, (1, M_WIDTH), 1) // HEAD_DIM
    mk = mk_ref[...]
    s = [_dg(jnp.where(lane == h, qs, 0.0).astype(BF16), mk, _NT) for h in range(M_HEADS)]
    p = [jnp.exp2(x - jnp.max(x, axis=-1, keepdims=True)).astype(BF16) for x in s]
    acc = _dot(p[0], mvx_ref[0])
    for h in range(1, M_HEADS):
        acc = acc + _dot(p[h], mvx_ref[h])
    om = acc[:, :M_WIDTH] / acc[:, M_WIDTH:]
    o_ref[...] = (om * sgm_ref[...].astype(F32)).astype(BF16)


def _mem_call(qm, sgbm, mk, mvx, gq, bones):
    bsz, s, _ = qm.shape
    tm = ROW_TILE
    mlen = mk.shape[1]
    return pl.pallas_call(
        _mem_kernel,
        grid=(bsz, s // tm),
        in_specs=[
            pl.BlockSpec((None, tm, M_WIDTH), lambda b, t: (b, t, 0)),
            pl.BlockSpec((None, tm, M_WIDTH), lambda b, t: (b, t, 1)),
            pl.BlockSpec((None, mlen, M_WIDTH), lambda b, t: (b, 0, 0)),
            pl.BlockSpec((None, M_HEADS, mlen, 2 * M_WIDTH), lambda b, t: (b, 0, 0, 0)),
            _full((1, M_WIDTH)), _full((M_WIDTH, M_WIDTH)),
        ],
        out_specs=pl.BlockSpec((None, tm, M_WIDTH), lambda b, t: (b, t, 0)),
        out_shape=jax.ShapeDtypeStruct((bsz, s, M_WIDTH), BF16),
        compiler_params=_params(("arbitrary", "arbitrary")),
        name="memattn",
    )(qm, sgbm, mk, mvx, gq, bones)


def _out_kernel(x_ref, oaT_ref, ob_ref, om_ref, wa_ref, wb_ref, o_ref):
    y = _dot(jnp.concatenate([ob_ref[...], om_ref[...]], axis=-1), wb_ref[...])
    for t in range(oaT_ref.shape[1]):
        rows = slice(t * MOBA_BLOCK, (t + 1) * MOBA_BLOCK)
        oaT = oaT_ref[:, t].reshape(A_WIDTH, MOBA_BLOCK)
        o_ref[rows, :] = x_ref[rows, :] + y[rows] + _dg(oaT, wa_ref[...], _TN)


def _out_call(x, oaT, ob, om, wa, wb):
    bsz, s, d = x.shape
    tm = OUT_TILE
    assert s % tm == 0 and tm % MOBA_BLOCK == 0
    row = lambda w: pl.BlockSpec((None, tm, w), lambda b, t: (b, t, 0))
    return pl.pallas_call(
        _out_kernel,
        grid=(bsz, s // tm),
        in_specs=[row(d),
                  pl.BlockSpec((None, A_HEADS, tm // MOBA_BLOCK, HEAD_DIM, MOBA_BLOCK), lambda b, t: (b, 0, t, 0, 0)),
                  row(G_VW), row(M_WIDTH),
                  _full(wa.shape), _full(wb.shape)],
        out_specs=row(d),
        out_shape=jax.ShapeDtypeStruct((bsz, s, d), F32),
        compiler_params=_params(("arbitrary", "arbitrary")),
        name="outproj",
    )(x, oaT, ob, om, wa, wb)


def _alibi_consts():
    slopes = np.asarray([2.0 ** (-8.0 * (i + 1) / A_HEADS) for i in range(A_HEADS)], np.float32)
    c1 = (slopes * np.float32(LOG2E)).astype(np.float32)
    c1j = jnp.asarray(c1)
    pieces = list(_split3(c1j * 16.0)) + list(_split3(c1j))
    qx = jnp.zeros((A_HEADS, K_AUG - HEAD_DIM, MOBA_BLOCK), BF16)
    for n, pc in enumerate(pieces):
        qx = qx.at[:, n, :].set(jnp.broadcast_to(pc[:, None], (A_HEADS, MOBA_BLOCK)))
    pos = np.arange(MOBA_BLOCK)
    e = np.zeros((MOBA_BLOCK, K_AUG - HEAD_DIM), np.float32)
    e[:, 0:3] = (pos // 16)[:, None]
    e[:, 3:6] = (pos % 16)[:, None]
    return c1j, qx, jnp.asarray(e)


def kernel(x, mem, g_pre, w_in, g_q_moba, g_k_moba, w_gate_up, b_gate_up, g_gla_out,
           g_mem, w_mem_kv, g_q_mem, g_k_mem, w_out):
    depth = g_pre.shape[0]
    d = x.shape[-1]
    c1, qx, e = _alibi_consts()
    lall_np, band_np, bdt_np, lex_np, mex_np = _gla_consts()
    lall = jnp.asarray(lall_np, BF16)
    band = jnp.asarray(band_np)
    bdt = jnp.asarray(bdt_np)
    lex = jnp.asarray(lex_np, BF16)
    mex = jnp.asarray(mex_np)
    bones_a = jnp.asarray(_block_ones(A_WIDTH, HEAD_DIM), BF16)
    bones_m = jnp.asarray(_block_ones(M_WIDTH, HEAD_DIM), BF16)

    o_qa, o_ka, o_va, o_ga = 0, 512, 1024, 1536
    o_qb, o_kb, o_vb, o_gb, o_rb = 2048, 2176, 2304, 2560, 2816
    o_qm, o_gm = 2832, 3088

    mk_all, mvx_all = _memkv_call(mem, g_mem, w_mem_kv, g_k_mem)

    for l in range(depth):
        w = w_in[l]
        col = lambda o, n: w[:, o:o + n]
        rb_pad = jnp.pad(col(o_rb, GATE_RANK), ((0, 0), (0, G_KW - GATE_RANK)))
        wn = jnp.concatenate([col(o_ka, 512), col(o_gb, 256), col(o_gm, 256), col(o_qb, 128), col(o_kb, 128),
                              col(o_vb, 256), rb_pad, col(o_qm, 256)], axis=1).astype(BF16)
        wt = jnp.concatenate([col(o_qa, 512), col(o_va, 512), col(o_ga, 512)], axis=1).T.astype(BF16)
        wgu = jnp.pad(w_gate_up[l], ((0, G_KW - GATE_RANK), (0, 0)))
        gk = jnp.tile(g_k_moba[l], A_HEADS).reshape(1, A_WIDTH)

        (qT, qa, kaug, vT, kmean_nat, sga, sgbm, gq, gk2, gv, gg, qm) = _proj_call(
            x, g_pre[l].reshape(1, d), wn, wt, gk, g_q_moba[l].reshape(HEAD_DIM, 1), bones_a, e, qx,
            wgu, b_gate_up[l].reshape(1, G_KW))

        bsz, s = x.shape[0], x.shape[1]
        nb = s // MOBA_BLOCK
        kmean = kmean_nat.reshape(bsz, nb, A_HEADS, HEAD_DIM).transpose(0, 2, 1, 3)
        ub = (UB_SLACK * LOG2E * HEAD_DIM ** 0.5) * jnp.max(jnp.abs(g_q_moba[l])) * jnp.max(jnp.abs(g_k_moba[l]))
        oaT = _moba_call(c1, ub.reshape(1), qa, kaug, vT, _sel_call(qT, kmean), sga)
        ob = _gla_call(gq, gk2, gv, gg, sgbm, lall, band, bdt, bones_m,
                       jnp.tile(g_gla_out[l], G_HEADS).reshape(1, G_VW), lex, mex)
        om = _mem_call(qm, sgbm, mk_all[l], mvx_all[l],
                       jnp.tile(g_q_mem[l], M_HEADS).reshape(1, M_WIDTH), bones_m)
        wo = w_out[l].astype(BF16)
        x = _out_call(x, oaT, ob, om, wo[:A_WIDTH], wo[A_WIDTH:])
    return x
```

```python
import functools

import numpy as np
import jax
import jax.numpy as jnp
from jax import lax
from jax.experimental import pallas as pl
from jax.experimental.pallas import tpu as pltpu

F32 = jnp.float32
BF16 = jnp.bfloat16
HIGHEST = lax.Precision.HIGHEST

EPS = 1e-6
LOG2E = 1.4426950408889634
HEAD_DIM = 64
A_HEADS = 8
A_WIDTH = A_HEADS * HEAD_DIM
MOBA_BLOCK = 256
MOBA_TOPK = 3
MOBA_GROUP = 4
MOBA_UNROLL = 4
MOBA_FAST_UNROLL = 32
MOBA_OWN_GROUPS = 2
SEL_WIDTH = 4096
G_HEADS = 4
G_DK = 32
G_DV = 64
G_KW = G_HEADS * G_DK
G_VW = G_HEADS * G_DV
GATE_RANK = 16
GATE_TEMP = 16.0
GLA_CHUNK = 64
GLA_SUB = 16
GLA_LEVELS = (32, 16, 8, 4, 2, 1)
GLA_SAFE_EXP = 80.0
M_HEADS = 4
M_WIDTH = M_HEADS * HEAD_DIM
K_AUG = 128
V_AUG = 80
ROW_TILE = 512
OUT_TILE = 1024
V7X_VMEM_BYTES = 64 * 1024 * 1024
VMEM_LIMIT = 3 * V7X_VMEM_BYTES // 4
MOBA_VMEM_LIMIT = 29 * V7X_VMEM_BYTES // 32

NEG_BIG = -1e30
POS_BIG = 1e30
DEN_MIN = 1e-18
DEN_MAX = 1e30
UB_SLACK = 1.02

_NT = (((1,), (1,)), ((), ()))
_TN = (((0,), (0,)), ((), ()))


def _dot(a, b, **kw):
    return jnp.dot(a, b, preferred_element_type=F32, **kw)


def _dg(a, b, dims, **kw):
    return lax.dot_general(a, b, dims, preferred_element_type=F32, **kw)


def _split2(v):
    hi = v.astype(BF16)
    lo = (v - hi.astype(F32)).astype(BF16)
    return hi, lo


def _split3(v):
    hi = v.astype(BF16)
    r = v - hi.astype(F32)
    mid = r.astype(BF16)
    lo = (r - mid.astype(F32)).astype(BF16)
    return hi, mid, lo


def _group_sumsq(v, bones):
    hi, lo = _split2(v * v)
    return _dot(hi, bones) + _dot(lo, bones)


def _silu(v):
    return v / (1.0 + jnp.exp(-v))


def _block_ones(n, g):
    i = np.arange(n) // g
    return (i[:, None] == i[None, :]).astype(np.float32)


def _params(sem):
    return pltpu.CompilerParams(dimension_semantics=sem, vmem_limit_bytes=VMEM_LIMIT)


def _full(shape):
    return pl.BlockSpec(shape, lambda *_: (0,) * len(shape))


def _memkv_kernel(mem_ref, gmem_ref, w_ref, gk_ref, bones_ref, mk_ref, mvx_ref):
    m = mem_ref[...]
    ms = jnp.mean(m * m, axis=-1, keepdims=True)
    hn = (m * lax.rsqrt(ms + EPS) * gmem_ref[...]).astype(BF16)
    kv = _dot(hn, w_ref[...])
    mk = kv[:, :M_WIDTH]
    mv = kv[:, M_WIDTH:]
    ss = _group_sumsq(mk, bones_ref[...])
    mk_ref[...] = (mk * lax.rsqrt(ss * (1.0 / HEAD_DIM) + EPS) * gk_ref[...]).astype(BF16)
    lane = lax.broadcasted_iota(jnp.int32, mv.shape, 1) // HEAD_DIM
    for h in range(M_HEADS):
        hm = lane == h
        mvx_ref[h] = jnp.concatenate(
            [jnp.where(hm, mv, 0.0), jnp.where(hm, 1.0, 0.0)], axis=-1).astype(BF16)


def _memkv_call(mem, g_mem, w_mem_kv, g_k_mem):
    depth = g_mem.shape[0]
    bsz, mlen, d = mem.shape
    bones = jnp.asarray(_block_ones(M_WIDTH, HEAD_DIM), BF16)
    gk = jnp.tile(g_k_mem, (1, M_HEADS)).reshape(depth, 1, M_WIDTH)
    return pl.pallas_call(
        _memkv_kernel,
        grid=(depth, bsz),
        in_specs=[
            pl.BlockSpec((None, mlen, d), lambda l, b: (b, 0, 0)),
            pl.BlockSpec((None, 1, d), lambda l, b: (l, 0, 0)),
            pl.BlockSpec((None, d, 2 * M_WIDTH), lambda l, b: (l, 0, 0)),
            pl.BlockSpec((None, 1, M_WIDTH), lambda l, b: (l, 0, 0)),
            _full((M_WIDTH, M_WIDTH)),
        ],
        out_specs=[
            pl.BlockSpec((None, None, mlen, M_WIDTH), lambda l, b: (l, b, 0, 0)),
            pl.BlockSpec((None, None, M_HEADS, mlen, 2 * M_WIDTH), lambda l, b: (l, b, 0, 0, 0)),
        ],
        out_shape=[
            jax.ShapeDtypeStruct((depth, bsz, mlen, M_WIDTH), BF16),
            jax.ShapeDtypeStruct((depth, bsz, M_HEADS, mlen, 2 * M_WIDTH), BF16),
        ],
        compiler_params=_params(("arbitrary", "arbitrary")),
        name="memkv",
    )(mem, g_mem.reshape(depth, 1, d), w_mem_kv.astype(BF16), gk, bones)


_N_K = (0, 512)
_N_GBM = (512, 1024)
_N_QB = (1024, 1152)
_N_KB = (1152, 1280)
_N_VB = (1280, 1536)
_N_RB = (1536, 1664)
_N_QM = (1664, 1920)


def _proj_kernel(x_ref, gpre_ref, wn_ref, wt_ref, gk_ref, gqcol_ref, bones_ref, e_ref, qx_ref, wgu_ref, bgu_ref,
                 qT_ref, qa_ref, kaug_ref, vT_ref, kmean_ref, sga_ref, sgbm_ref, gq_ref, gk2_ref, gv_ref, gg_ref,
                 qm_ref):
    tm = x_ref.shape[0]
    nblk = tm // MOBA_BLOCK
    x = x_ref[...]
    ms = jnp.mean(x * x, axis=-1, keepdims=True)
    h = (x * lax.rsqrt(ms + EPS) * gpre_ref[...]).astype(BF16)

    def nat(cols):
        return _dot(h, wn_ref[:, cols[0]:cols[1]])

    k = nat(_N_K)
    qT = _dg(wt_ref[0:A_WIDTH, :], h, _NT)

    ss = _group_sumsq(k, bones_ref[...])
    kn = k * lax.rsqrt(ss * (1.0 / HEAD_DIM) + EPS) * gk_ref[...]
    kmean_ref[...] = jnp.mean(kn.reshape(nblk, MOBA_BLOCK, A_WIDTH), axis=1)
    e = jnp.concatenate([e_ref[...]] * nblk, axis=0)
    for hh in range(A_HEADS):
        rows = kn[:, hh * HEAD_DIM:(hh + 1) * HEAD_DIM]
        kaug_ref[hh] = jnp.concatenate([rows, e], axis=-1).astype(BF16)

    q3 = qT.reshape(A_HEADS, HEAD_DIM, tm)
    msq = jnp.mean(q3 * q3, axis=1, keepdims=True)
    qn = q3 * lax.rsqrt(msq + EPS) * gqcol_ref[...].reshape(1, HEAD_DIM, 1)
    qT_ref[...] = qn.reshape(A_WIDTH, tm)
    qs = (qn * (LOG2E * HEAD_DIM ** -0.5)).astype(BF16)
    blocks = [slice(t * MOBA_BLOCK, (t + 1) * MOBA_BLOCK) for t in range(nblk)]
    for hh in range(A_HEADS):
        for t, cols in enumerate(blocks):
            qa_ref[hh, t] = jnp.concatenate([qs[hh][:, cols], qx_ref[hh]], axis=0)

    vT = _dg(wt_ref[A_WIDTH:2 * A_WIDTH, :], h, _NT)
    ones_rows = jnp.where(lax.broadcasted_iota(jnp.int32, (V_AUG - HEAD_DIM, tm), 0) == 0, 1.0, 0.0)
    sga = _silu(_dg(wt_ref[2 * A_WIDTH:3 * A_WIDTH, :], h, _NT)).astype(BF16)
    for hh in range(A_HEADS):
        rows = slice(hh * HEAD_DIM, (hh + 1) * HEAD_DIM)
        vT_ref[hh] = jnp.concatenate([vT[rows], ones_rows], axis=0).astype(BF16)
        for t, cols in enumerate(blocks):
            sga_ref[hh, t] = sga[rows, cols]
    sgbm_ref[...] = _silu(nat(_N_GBM)).astype(BF16)

    qkb = nat((_N_QB[0], _N_KB[1]))
    gq_ref[...] = qkb[:, :G_KW] * (G_DK ** -0.5)
    gk2_ref[...] = qkb[:, G_KW:]
    rb = nat(_N_RB)
    gv_ref[...] = nat(_N_VB).astype(BF16)
    qm_ref[...] = nat(_N_QM)
    z = _dot(rb, wgu_ref[...], precision=HIGHEST) + bgu_ref[...]
    gg_ref[...] = (jnp.minimum(z, 0.0) - jnp.log(1.0 + jnp.exp(-jnp.abs(z)))) * (1.0 / GATE_TEMP)


def _proj_call(x, g_pre, wn, wt, gk, gqcol, bones, e, qx, wgu, bgu):
    bsz, s, d = x.shape
    tm = ROW_TILE
    nt = s // tm
    nblk = tm // MOBA_BLOCK
    nb = s // MOBA_BLOCK
    row = lambda w: pl.BlockSpec((None, tm, w), lambda b, t: (b, t, 0))
    in_specs = [
        row(d), _full((1, d)), _full(wn.shape), _full(wt.shape), _full((1, A_WIDTH)),
        _full((HEAD_DIM, 1)), _full((A_WIDTH, A_WIDTH)), _full(e.shape), _full(qx.shape), _full(wgu.shape),
        _full((1, G_KW)),
    ]
    group = MOBA_GROUP * MOBA_BLOCK
    assert d == wn.shape[0] and s % group == 0 and group % tm == 0 and tm % MOBA_BLOCK == 0
    tpg = group // tm
    blocked = lambda r: pl.BlockSpec((None, A_HEADS, nblk, r, MOBA_BLOCK), lambda b, t: (b, 0, t, 0, 0))
    out_specs = [
        pl.BlockSpec((None, A_WIDTH, tm), lambda b, t: (b, 0, t)),
        blocked(K_AUG),
        pl.BlockSpec((None, A_HEADS, tm, K_AUG), lambda b, t: (b, 0, t, 0)),
        pl.BlockSpec((None, A_HEADS, None, V_AUG, tm), lambda b, t: (b, 0, t // tpg, 0, t % tpg)),
        pl.BlockSpec((None, None, nblk, A_WIDTH), lambda b, t: (b, t, 0, 0)),
        blocked(HEAD_DIM),
        row(2 * M_WIDTH), row(G_KW), row(G_KW), row(G_VW), row(G_KW), row(M_WIDTH),
    ]
    out_shape = [
        jax.ShapeDtypeStruct((bsz, A_WIDTH, s), F32),
        jax.ShapeDtypeStruct((bsz, A_HEADS, nb, K_AUG, MOBA_BLOCK), BF16),
        jax.ShapeDtypeStruct((bsz, A_HEADS, s, K_AUG), BF16),
        jax.ShapeDtypeStruct((bsz, A_HEADS, s // group, V_AUG, group), BF16),
        jax.ShapeDtypeStruct((bsz, nt, nblk, A_WIDTH), F32),
        jax.ShapeDtypeStruct((bsz, A_HEADS, nb, HEAD_DIM, MOBA_BLOCK), BF16),
        jax.ShapeDtypeStruct((bsz, s, 2 * M_WIDTH), BF16),
        jax.ShapeDtypeStruct((bsz, s, G_KW), F32),
        jax.ShapeDtypeStruct((bsz, s, G_KW), F32),
        jax.ShapeDtypeStruct((bsz, s, G_VW), BF16),
        jax.ShapeDtypeStruct((bsz, s, G_KW), F32),
        jax.ShapeDtypeStruct((bsz, s, M_WIDTH), F32),
    ]
    return pl.pallas_call(
        _proj_kernel, grid=(bsz, nt), in_specs=in_specs, out_specs=out_specs, out_shape=out_shape,
        compiler_params=_params(("arbitrary", "arbitrary")), name="proj",
    )(x, g_pre, wn, wt, gk, gqcol, bones, e, qx, wgu, bgu)


def _sel_kernel(qT_ref, kmean_ref, o_ref):
    width = qT_ref.shape[1]
    nb = kmean_ref.shape[0]
    per = width // MOBA_BLOCK

    def choose(first):
        rows = min(nb, first + per)
        gate = _dot(kmean_ref[0:rows, :], qT_ref[...], precision=HIGHEST)
        rowf = lax.broadcasted_iota(jnp.int32, gate.shape, 0).astype(F32)
        qblk = (first + lax.broadcasted_iota(jnp.int32, gate.shape, 1) // MOBA_BLOCK).astype(F32)
        g = jnp.where(rowf < qblk, gate, -jnp.inf)
        sel = jnp.zeros(gate.shape, F32)
        for _ in range(MOBA_TOPK):
            m = jnp.max(g, axis=0, keepdims=True)
            idx = jnp.min(jnp.where(g == m, rowf, float(rows)), axis=0, keepdims=True)
            hit = rowf == idx
            sel = jnp.where(hit, 1.0, sel)
            g = jnp.where(hit, -jnp.inf, g)
        sel = jnp.where(rowf < qblk, sel, 0.0)
        for u in range(per):
            o_ref[u, 0:rows, :] = sel[:, u * MOBA_BLOCK:(u + 1) * MOBA_BLOCK]
            if rows < nb:
                o_ref[u, rows:nb, :] = jnp.zeros((nb - rows, MOBA_BLOCK), F32)

    for step in range(nb // per):
        pl.when(pl.program_id(2) == step)(functools.partial(choose, step * per))


def _sel_call(qT, kmean):
    bsz, _, s = qT.shape
    nb = s // MOBA_BLOCK
    width = min(SEL_WIDTH, s)
    return pl.pallas_call(
        _sel_kernel,
        grid=(bsz, A_HEADS, s // width),
        in_specs=[
            pl.BlockSpec((None, HEAD_DIM, width), lambda b, h, c: (b, h, c)),
            pl.BlockSpec((None, None, nb, HEAD_DIM), lambda b, h, c: (b, h, 0, 0)),
        ],
        out_specs=pl.BlockSpec((None, None, width // MOBA_BLOCK, nb, MOBA_BLOCK), lambda b, h, c: (b, h, c, 0, 0)),
        out_shape=jax.ShapeDtypeStruct((bsz, A_HEADS, nb, nb, MOBA_BLOCK), F32),
        compiler_params=_params(("arbitrary", "arbitrary", "arbitrary")),
        name="mobasel",
    )(qT, kmean)


def _moba_items(nb):
    def table(rows, multiple):
        n_items = -(-len(rows) // multiple) * multiple
        rows = rows + [(rows[-1][0], 0, 0, 0)] * (n_items + 2 - len(rows))
        return np.asarray(rows, np.int32).T, n_items

    fast = [(i, g, int(g == 0), 1) for i in range(MOBA_GROUP, nb) for g in range(i // MOBA_GROUP)]
    robust = [(i, g, int(g == 0), 1) for i in range(1, nb) for g in range((i - 1) // MOBA_GROUP + 1)]
    return table(fast, MOBA_FAST_UNROLL), table(robust, MOBA_UNROLL)


def _moba_kernel(tabf_ref, tabr_ref, c1_ref, ub_ref, qa_ref, kall_ref, vall_ref, sel_ref, sga_ref, o_ref,
                 m_scr, acc_scr, out_scr, s0_ref, s1_ref, mb0_ref, mb1_ref, p0_ref, p1_ref,
                 *, n_fast, n_robust):
    hh = pl.program_id(1)
    c1 = c1_ref[hh]
    ub = ub_ref[0]
    nb = qa_ref.shape[0]
    gk = MOBA_GROUP * MOBA_BLOCK
    half = gk // 2
    per_half = MOBA_GROUP // 2
    bufs = ((s0_ref, mb0_ref, p0_ref), (s1_ref, mb1_ref, p1_ref))
    kk = lax.broadcasted_iota(jnp.int32, (MOBA_BLOCK, MOBA_BLOCK), 0)
    qq = lax.broadcasted_iota(jnp.int32, (MOBA_BLOCK, MOBA_BLOCK), 1)

    sigma = ub + c1 * lax.broadcasted_iota(jnp.int32, (1, MOBA_BLOCK), 1).astype(F32)

    def past_probs(s_blk, i, j, valid):
        picked = jnp.where(valid == 1, sel_ref[i, pl.ds(j, 1), :], 0.0) > 0.5
        cj = c1 * (MOBA_BLOCK * (j - i)).astype(F32)
        shift = jnp.where(picked, sigma - cj, POS_BIG)
        return jnp.exp2(s_blk - shift).astype(BF16)

    def own_probs(t, k, p_ref):
        t = jnp.minimum(t, nb // MOBA_GROUP - 1)
        i = MOBA_GROUP * t + k
        for part in range(k // per_half + 1):
            members = min(per_half, k + 1 - per_half * part)
            r0 = pl.multiple_of(t * gk + part * half, half)
            s = _dot(kall_ref[pl.ds(r0, members * MOBA_BLOCK), :], qa_ref[i])
            for u2 in range(members):
                u = per_half * part + u2
                blk = s[u2 * MOBA_BLOCK:(u2 + 1) * MOBA_BLOCK]
                if u == k:
                    p = jnp.exp2(jnp.where(kk <= qq, blk, NEG_BIG) - sigma).astype(BF16)
                else:
                    p = past_probs(blk, i, MOBA_GROUP * t + u, 1)
                p_ref[u * MOBA_BLOCK:(u + 1) * MOBA_BLOCK, :] = p

    def own_accumulate(t, k, p_ref):
        n = (k + 1) * MOBA_BLOCK
        acc = _dot(vall_ref[t][:, 0:n], p_ref[0:n, :])
        acc_scr[MOBA_GROUP * t + k] = acc
        out_scr[MOBA_GROUP * t + k] = acc

    own_probs(0, 0, p0_ref)

    def own_body(tt, carry):
        for step in range(MOBA_OWN_GROUPS * MOBA_GROUP):
            t, k = MOBA_OWN_GROUPS * tt + step // MOBA_GROUP, step % MOBA_GROUP
            nxt = (t, k + 1) if k + 1 < MOBA_GROUP else (t + 1, 0)
            own_probs(nxt[0], nxt[1], bufs[(step + 1) % 2][2])
            own_accumulate(t, k, bufs[step % 2][2])
        return carry

    lax.fori_loop(0, nb // (MOBA_GROUP * MOBA_OWN_GROUPS), own_body, 0)

    def fast_probs(n, p_ref):
        i, g, valid = tabf_ref[0, n], tabf_ref[1, n], tabf_ref[3, n]
        for part in range(2):
            r0 = pl.multiple_of(g * gk + part * half, half)
            s = _dot(kall_ref[pl.ds(r0, half), :], qa_ref[i])
            for u2 in range(per_half):
                u = per_half * part + u2
                p_ref[u * MOBA_BLOCK:(u + 1) * MOBA_BLOCK, :] = past_probs(
                    s[u2 * MOBA_BLOCK:(u2 + 1) * MOBA_BLOCK], i, g * MOBA_GROUP + u, valid)

    def fast_accumulate(n, p_ref, acc):
        i, g, first = tabf_ref[0, n], tabf_ref[1, n], tabf_ref[2, n]
        acc = jnp.where(first == 1, acc_scr[i], acc) + _dot(vall_ref[g], p_ref[...])
        out_scr[i] = acc
        return acc

    fast_probs(0, p0_ref)

    def fast_body(t, acc):
        for k in range(MOBA_FAST_UNROLL):
            n = MOBA_FAST_UNROLL * t + k
            fast_probs(n + 1, bufs[(k + 1) % 2][2])
            acc = fast_accumulate(n, bufs[k % 2][2], acc)
        return acc

    lax.fori_loop(0, n_fast // MOBA_FAST_UNROLL, fast_body, jnp.zeros((V_AUG, MOBA_BLOCK), F32))

    den = out_scr[:, HEAD_DIM:HEAD_DIM + 1, :]
    in_range = jnp.logical_and(jnp.min(den) > DEN_MIN, jnp.max(den) < DEN_MAX)

    def item(n):
        return tabr_ref[0, n], tabr_ref[1, n], tabr_ref[2, n], tabr_ref[3, n]

    def diag_scores(i, s_ref, mb_ref):
        i = jnp.minimum(i, nb - 1)
        r0 = pl.multiple_of(i * MOBA_BLOCK, MOBA_BLOCK)
        s = _dot(kall_ref[pl.ds(r0, MOBA_BLOCK), :], qa_ref[i])
        s = jnp.where(kk <= qq, s, NEG_BIG)
        s_ref[0:MOBA_BLOCK, :] = s
        mb_ref[0] = jnp.max(s, axis=0, keepdims=True)

    def diag_softmax(i, s_ref, mb_ref, p_ref):
        i = jnp.minimum(i, nb - 1)
        m0 = mb_ref[0]
        m_scr[i] = jnp.broadcast_to(m0, m_scr.shape[1:])
        p_ref[0:MOBA_BLOCK, :] = jnp.exp2(s_ref[0:MOBA_BLOCK, :] - m0).astype(BF16)

    def diag_accumulate(t, k, p_ref):
        vown = vall_ref[t][:, k * MOBA_BLOCK:(k + 1) * MOBA_BLOCK]
        acc_scr[MOBA_GROUP * t + k] = _dot(vown, p_ref[0:MOBA_BLOCK, :])

    def scores(n, s_ref, mb_ref):
        i, g, _, _ = item(n)
        for part in range(2):
            r0 = pl.multiple_of(g * gk + part * half, half)
            s = _dot(kall_ref[pl.ds(r0, half), :], qa_ref[i])
            s_ref[part * half:(part + 1) * half, :] = s
            for u in range(per_half):
                blk = s[u * MOBA_BLOCK:(u + 1) * MOBA_BLOCK]
                mb_ref[part * per_half + u] = jnp.max(blk, axis=0, keepdims=True)

    def softmax(n, s_ref, mb_ref, p_ref, m):
        i, g, first, valid = item(n)
        m = jnp.where(first == 1, m_scr[i][0:1], m)
        m_new = m
        members = []
        for u in range(MOBA_GROUP):
            j = g * MOBA_GROUP + u
            picked = jnp.where(valid == 1, sel_ref[i, pl.ds(j, 1), :], 0.0) > 0.5
            cj = c1 * (MOBA_BLOCK * (j - i)).astype(F32)
            m_new = jnp.where(picked, jnp.maximum(m_new, mb_ref[u] + cj), m_new)
            members.append((picked, cj))
        for u, (picked, cj) in enumerate(members):
            shift = jnp.where(picked, m_new - cj, POS_BIG)
            rows = slice(u * MOBA_BLOCK, (u + 1) * MOBA_BLOCK)
            p_ref[rows, :] = jnp.exp2(s_ref[rows, :] - shift).astype(BF16)
        return m_new, jnp.exp2(m - m_new)

    def accumulate(n, p_ref, alpha, acc):
        i, g, first, _ = item(n)
        acc = jnp.where(first == 1, acc_scr[i], acc) * alpha + _dot(vall_ref[g], p_ref[...])
        acc_scr[i] = acc
        return acc

    @pl.when(jnp.logical_not(in_range))
    def _():
        diag_scores(0, s0_ref, mb0_ref)
        diag_scores(1, s1_ref, mb1_ref)
        diag_softmax(0, s0_ref, mb0_ref, p0_ref)

        def diag_body(t, carry):
            for k in range(MOBA_GROUP):
                i = MOBA_GROUP * t + k
                s_a, mb_a, p_a = bufs[k % 2]
                s_b, mb_b, p_b = bufs[(k + 1) % 2]
                diag_scores(i + 2, s_a, mb_a)
                diag_softmax(i + 1, s_b, mb_b, p_b)
                diag_accumulate(t, k, p_a)
            return carry

        lax.fori_loop(0, nb // MOBA_GROUP, diag_body, 0)

        scores(0, s0_ref, mb0_ref)
        scores(1, s1_ref, mb1_ref)
        m_init = jnp.full((1, MOBA_BLOCK), NEG_BIG, F32)
        m, alpha = softmax(0, s0_ref, mb0_ref, p0_ref, m_init)
        acc = jnp.zeros((V_AUG, MOBA_BLOCK), F32)

        def body(t, carry):
            m, alpha, acc = carry
            for k in range(MOBA_UNROLL):
                n = MOBA_UNROLL * t + k
                s_a, mb_a, p_a = bufs[k % 2]
                s_b, mb_b, p_b = bufs[(k + 1) % 2]
                acc = accumulate(n, p_a, alpha, acc)
                m, alpha = softmax(n + 1, s_b, mb_b, p_b, m)
                scores(n + 2, s_a, mb_a)
            return m, alpha, acc

        lax.fori_loop(0, n_robust // MOBA_UNROLL, body, (m, alpha, acc))
        out_scr[...] = acc_scr[...]

    def finish(i, carry):
        acc = out_scr[i]
        o = acc[:HEAD_DIM] / acc[HEAD_DIM:HEAD_DIM + 1]
        o_ref[i] = (o * sga_ref[i].astype(F32)).astype(BF16)
        return carry

    lax.fori_loop(0, nb, finish, 0, unroll=4)


def _moba_call(c1, ub, qa, kaug, vT, sel, sga):
    bsz, _, nb = qa.shape[:3]
    s = nb * MOBA_BLOCK
    gk = MOBA_GROUP * MOBA_BLOCK
    assert nb % (MOBA_GROUP * MOBA_OWN_GROUPS) == 0 and MOBA_GROUP % 2 == 0 and MOBA_UNROLL % 2 == 0
    assert MOBA_FAST_UNROLL % MOBA_UNROLL == 0
    (tabf, n_fast), (tabr, n_robust) = _moba_items(nb)
    per_bh = lambda *tail: pl.BlockSpec((None, None) + tail, lambda b, h: (b, h) + (0,) * len(tail))
    return pl.pallas_call(
        functools.partial(_moba_kernel, n_fast=n_fast, n_robust=n_robust),
        grid=(bsz, A_HEADS),
        in_specs=[
            pl.BlockSpec(memory_space=pltpu.SMEM),
            pl.BlockSpec(memory_space=pltpu.SMEM),
            pl.BlockSpec(memory_space=pltpu.SMEM),
            pl.BlockSpec(memory_space=pltpu.SMEM),
            per_bh(nb, K_AUG, MOBA_BLOCK),
            per_bh(s, K_AUG),
            per_bh(nb // MOBA_GROUP, V_AUG, gk),
            per_bh(nb, nb, MOBA_BLOCK),
            per_bh(nb, HEAD_DIM, MOBA_BLOCK),
        ],
        out_specs=per_bh(nb, HEAD_DIM, MOBA_BLOCK),
        out_shape=jax.ShapeDtypeStruct((bsz, A_HEADS, nb, HEAD_DIM, MOBA_BLOCK), BF16),
        scratch_shapes=[pltpu.VMEM((nb, 8, MOBA_BLOCK), F32), pltpu.VMEM((nb, V_AUG, MOBA_BLOCK), F32),
                        pltpu.VMEM((nb, V_AUG, MOBA_BLOCK), F32),
                        pltpu.VMEM((gk, MOBA_BLOCK), F32), pltpu.VMEM((gk, MOBA_BLOCK), F32),
                        pltpu.VMEM((MOBA_GROUP, 1, MOBA_BLOCK), F32), pltpu.VMEM((MOBA_GROUP, 1, MOBA_BLOCK), F32),
                        pltpu.VMEM((gk, MOBA_BLOCK), BF16), pltpu.VMEM((gk, MOBA_BLOCK), BF16)],
        compiler_params=pltpu.CompilerParams(dimension_semantics=("arbitrary", "arbitrary"),
                                             vmem_limit_bytes=MOBA_VMEM_LIMIT),
        name="moba",
    )(jnp.asarray(tabf), jnp.asarray(tabr), c1, ub, qa, kaug, vT, sel, sga)


def _gla_consts():
    c, sb = GLA_CHUNK, GLA_SUB
    r = np.arange(c)
    blk = r // sb
    tri = (r[None, :] <= r[:, None])
    same = blk[None, :] == blk[:, None]
    lall = np.concatenate([
        tri,
        tri & same,
        same,
        blk[None, :] == blk[:, None] - 1,
        blk[None, :] == blk[:, None] - 2,
    ], axis=0).astype(np.float32)
    diff = blk[:, None] - blk[None, :]
    band = np.stack([diff == 1, diff == 2, diff == 3, same & tri]).astype(np.float32)
    dk_head = np.arange(G_KW) // G_DK
    dv_head = np.arange(G_VW) // G_DV
    bdt = (dv_head[:, None] == dk_head[None, :]).astype(np.float32)
    lex, mex = [tri], [np.eye(c, dtype=bool)]
    for s in GLA_LEVELS:
        same_s = (r[None, :] // s) == (r[:, None] // s)
        lex += [tri & same_s, same_s]
        pair = (r[None, :] // (2 * s)) == (r[:, None] // (2 * s))
        mex.append(pair & ((r[:, None] // s) % 2 == 1) & ((r[None, :] // s) % 2 == 0))
    lex = np.concatenate(lex, axis=0).astype(np.float32)
    mex = np.stack(mex).astype(np.float32)
    return lall, band, bdt, lex, mex


def _gla_kernel(q_ref, k_ref, v_ref, g_ref, sgb_ref, lall_ref, band_ref, bdt_ref, bones_ref, gout_ref,
                lex_ref, mex_ref, o_ref, st_ref):
    c = GLA_CHUNK
    nbatch = q_ref.shape[0]

    @pl.when(pl.program_id(0) == 0)
    def _():
        st_ref[...] = jnp.zeros_like(st_ref)

    lane_k = lax.broadcasted_iota(jnp.int32, (1, G_KW), 1) // G_DK
    lane_v = lax.broadcasted_iota(jnp.int32, (1, G_VW), 1) // G_DV
    n_chunks = q_ref.shape[1] // c

    def heads(t):
        return jnp.concatenate([jnp.where(lane_k == hd, t, 0.0) for hd in range(G_HEADS)],
                               axis=0).astype(BF16)

    def chunk(ci, states):
        r0 = pl.multiple_of(ci * c, c)
        nbs = range(nbatch)
        q = [q_ref[bi, pl.ds(r0, c), :] for bi in nbs]
        k = [k_ref[bi, pl.ds(r0, c), :] for bi in nbs]
        v = [v_ref[bi, pl.ds(r0, c), :] for bi in nbs]

        r = [_dot(lall_ref[...], jnp.concatenate(_split3(g_ref[bi, pl.ds(r0, c), :]), axis=1)) for bi in nbs]
        r = [x[:, :G_KW] + x[:, G_KW:2 * G_KW] + x[:, 2 * G_KW:] for x in r]
        b, cc, tt, p1, p2 = ([x[n * c:(n + 1) * c] for x in r] for n in range(5))
        b_last = [x[c - 1:c] for x in b]

        qt = [q[bi] * jnp.exp(cc[bi]) for bi in nbs]
        q2 = [qt[bi] * jnp.exp(p1[bi]) for bi in nbs]
        q3 = [q2[bi] * jnp.exp(p2[bi]) for bi in nbs]
        kt = [(k[bi] * jnp.exp(tt[bi] - cc[bi])).astype(BF16) for bi in nbs]
        kd = [(k[bi] * jnp.exp(-cc[bi])).astype(BF16) for bi in nbs]
        qi = [(q[bi] * jnp.exp(b[bi])).astype(BF16) for bi in nbs]
        kl = [(k[bi] * jnp.exp(b_last[bi] - b[bi])).astype(BF16) for bi in nbs]

        xs = [[_dg(heads(t[bi]), kt[bi], _NT) for bi in nbs] for t in (qt, q2, q3)]
        yd = [_dg(heads(qt[bi]), kd[bi], _NT) for bi in nbs]
        o = [_dg(qi[bi], states[bi].astype(BF16), _NT) for bi in nbs]
        st_new = tuple(states[bi] * jnp.exp(b_last[bi]) + _dg(v[bi], kl[bi], _TN) * bdt_ref[...]
                       for bi in nbs)
        for hd in range(G_HEADS):
            sl = slice(hd * c, (hd + 1) * c)
            for bi in nbs:
                att = (band_ref[0] * xs[0][bi][sl] + band_ref[1] * xs[1][bi][sl]
                       + band_ref[2] * xs[2][bi][sl] + band_ref[3] * yd[bi][sl])
                vh = jnp.where(lane_v == hd, v[bi], jnp.zeros_like(v[bi]))
                o[bi] = o[bi] + _dot(att.astype(BF16), vh)

        ss = [_group_sumsq(o[bi], bones_ref[...]) for bi in nbs]
        for bi in nbs:
            y = o[bi] * lax.rsqrt(ss[bi] * (1.0 / G_DV) + EPS) * gout_ref[...]
            o_ref[bi, pl.ds(r0, c), :] = (y * sgb_ref[bi, pl.ds(r0, c), :].astype(F32)).astype(BF16)
        return st_new

    def chunk_stable(ci, states):
        r0 = pl.multiple_of(ci * c, c)
        out = []
        for bi in range(nbatch):
            q = q_ref[bi, pl.ds(r0, c), :]
            k = k_ref[bi, pl.ds(r0, c), :]
            v = v_ref[bi, pl.ds(r0, c), :]
            r = _dot(lex_ref[...], jnp.concatenate(_split3(g_ref[bi, pl.ds(r0, c), :]), axis=1))
            r = r[:, :G_KW] + r[:, G_KW:2 * G_KW] + r[:, 2 * G_KW:]
            b = r[0:c]
            b_last = b[c - 1:c]
            x = [_dg(heads(q), k.astype(BF16), _NT)]
            for lv in range(len(GLA_LEVELS)):
                cs = r[(1 + 2 * lv) * c:(2 + 2 * lv) * c]
                ts = r[(2 + 2 * lv) * c:(3 + 2 * lv) * c]
                x.append(_dg(heads(q * jnp.exp(cs)), (k * jnp.exp(ts - cs)).astype(BF16), _NT))
            o = _dg((q * jnp.exp(b)).astype(BF16), states[bi].astype(BF16), _NT)
            for hd in range(G_HEADS):
                sl = slice(hd * c, (hd + 1) * c)
                att = mex_ref[0] * x[0][sl]
                for lv in range(len(GLA_LEVELS)):
                    att = att + mex_ref[lv + 1] * x[lv + 1][sl]
                o = o + _dot(att.astype(BF16), jnp.where(lane_v == hd, v, jnp.zeros_like(v)))
            kl = (k * jnp.exp(b_last - b)).astype(BF16)
            out.append(states[bi] * jnp.exp(b_last) + _dg(v, kl, _TN) * bdt_ref[...])
            ss = _group_sumsq(o, bones_ref[...])
            y = o * lax.rsqrt(ss * (1.0 / G_DV) + EPS) * gout_ref[...]
            o_ref[bi, pl.ds(r0, c), :] = (y * sgb_ref[bi, pl.ds(r0, c), :].astype(F32)).astype(BF16)
        return tuple(out)

    def run(body, unroll):
        states = lax.fori_loop(0, n_chunks, body, tuple(st_ref[bi] for bi in range(nbatch)), unroll=unroll)
        for bi in range(nbatch):
            st_ref[bi] = states[bi]

    risky = jnp.max(-g_ref[...]) * GLA_SUB > GLA_SAFE_EXP
    pl.when(jnp.logical_not(risky))(lambda: run(chunk, 2))
    pl.when(risky)(lambda: run(chunk_stable, 1))


def _gla_call(gq, gk, gv, gg, sgbm, lall, band, bdt, bones, gout, lex, mex):
    bsz, s, _ = gq.shape
    tm = ROW_TILE
    row = lambda w: pl.BlockSpec((bsz, tm, w), lambda t: (0, t, 0))
    return pl.pallas_call(
        _gla_kernel,
        grid=(s // tm,),
        in_specs=[row(G_KW), row(G_KW), row(G_VW), row(G_KW), row(G_VW),
                  _full(lall.shape), _full(band.shape), _full(bdt.shape), _full(bones.shape), _full((1, G_VW)),
                  _full(lex.shape), _full(mex.shape)],
        out_specs=row(G_VW),
        out_shape=jax.ShapeDtypeStruct((bsz, s, G_VW), BF16),
        scratch_shapes=[pltpu.VMEM((bsz, G_VW, G_KW), F32)],
        compiler_params=_params(("arbitrary",)),
        name="gla",
    )(gq, gk, gv, gg, sgbm, lall, band, bdt, bones, gout, lex, mex)


def _mem_kernel(qm_ref, sgm_ref, mk_ref, mvx_ref, gq_ref, bones_ref, o_ref):
    qm = qm_ref[...]
    ss = _group_sumsq(qm, bones_ref[...])
    qn = qm * lax.rsqrt(ss * (1.0 / HEAD_DIM) + EPS) * gq_ref[...]
    qs = qn * (LOG2E * HEAD_DIM ** -0.5)
    lane = lax.broadcasted_iota(jnp.int32, (1, M_WIDTH), 1) // HEAD_DIM
    mk = mk_ref[...]
    s = [_dg(jnp.where(lane == h, qs, 0.0).astype(BF16), mk, _NT) for h in range(M_HEADS)]
    p = [jnp.exp2(x - jnp.max(x, axis=-1, keepdims=True)).astype(BF16) for x in s]
    acc = _dot(p[0], mvx_ref[0])
    for h in range(1, M_HEADS):
        acc = acc + _dot(p[h], mvx_ref[h])
    om = acc[:, :M_WIDTH] / acc[:, M_WIDTH:]
    o_ref[...] = (om * sgm_ref[...].astype(F32)).astype(BF16)


def _mem_call(qm, sgbm, mk, mvx, gq, bones):
    bsz, s, _ = qm.shape
    tm = ROW_TILE
    mlen = mk.shape[1]
    return pl.pallas_call(
        _mem_kernel,
        grid=(bsz, s // tm),
        in_specs=[
            pl.BlockSpec((None, tm, M_WIDTH), lambda b, t: (b, t, 0)),
            pl.BlockSpec((None, tm, M_WIDTH), lambda b, t: (b, t, 1)),
            pl.BlockSpec((None, mlen, M_WIDTH), lambda b, t: (b, 0, 0)),
            pl.BlockSpec((None, M_HEADS, mlen, 2 * M_WIDTH), lambda b, t: (b, 0, 0, 0)),
            _full((1, M_WIDTH)), _full((M_WIDTH, M_WIDTH)),
        ],
        out_specs=pl.BlockSpec((None, tm, M_WIDTH), lambda b, t: (b, t, 0)),
        out_shape=jax.ShapeDtypeStruct((bsz, s, M_WIDTH), BF16),
        compiler_params=_params(("arbitrary", "arbitrary")),
        name="memattn",
    )(qm, sgbm, mk, mvx, gq, bones)


def _out_kernel(x_ref, oaT_ref, ob_ref, om_ref, wa_ref, wb_ref, o_ref):
    y = _dot(jnp.concatenate([ob_ref[...], om_ref[...]], axis=-1), wb_ref[...])
    for t in range(oaT_ref.shape[1]):
        rows = slice(t * MOBA_BLOCK, (t + 1) * MOBA_BLOCK)
        oaT = oaT_ref[:, t].reshape(A_WIDTH, MOBA_BLOCK)
        o_ref[rows, :] = x_ref[rows, :] + y[rows] + _dg(oaT, wa_ref[...], _TN)


def _out_call(x, oaT, ob, om, wa, wb):
    bsz, s, d = x.shape
    tm = OUT_TILE
    assert s % tm == 0 and tm % MOBA_BLOCK == 0
    row = lambda w: pl.BlockSpec((None, tm, w), lambda b, t: (b, t, 0))
    return pl.pallas_call(
        _out_kernel,
        grid=(bsz, s // tm),
        in_specs=[row(d),
                  pl.BlockSpec((None, A_HEADS, tm // MOBA_BLOCK, HEAD_DIM, MOBA_BLOCK), lambda b, t: (b, 0, t, 0, 0)),
                  row(G_VW), row(M_WIDTH),
                  _full(wa.shape), _full(wb.shape)],
        out_specs=row(d),
        out_shape=jax.ShapeDtypeStruct((bsz, s, d), F32),
        compiler_params=_params(("arbitrary", "arbitrary")),
        name="outproj",
    )(x, oaT, ob, om, wa, wb)


def _alibi_consts():
    slopes = np.asarray([2.0 ** (-8.0 * (i + 1) / A_HEADS) for i in range(A_HEADS)], np.float32)
    c1 = (slopes * np.float32(LOG2E)).astype(np.float32)
    c1j = jnp.asarray(c1)
    pieces = list(_split3(c1j * 16.0)) + list(_split3(c1j))
    qx = jnp.zeros((A_HEADS, K_AUG - HEAD_DIM, MOBA_BLOCK), BF16)
    for n, pc in enumerate(pieces):
        qx = qx.at[:, n, :].set(jnp.broadcast_to(pc[:, None], (A_HEADS, MOBA_BLOCK)))
    pos = np.arange(MOBA_BLOCK)
    e = np.zeros((MOBA_BLOCK, K_AUG - HEAD_DIM), np.float32)
    e[:, 0:3] = (pos // 16)[:, None]
    e[:, 3:6] = (pos % 16)[:, None]
    return c1j, qx, jnp.asarray(e)


def kernel(x, mem, g_pre, w_in, g_q_moba, g_k_moba, w_gate_up, b_gate_up, g_gla_out,
           g_mem, w_mem_kv, g_q_mem, g_k_mem, w_out):
    depth = g_pre.shape[0]
    d = x.shape[-1]
    c1, qx, e = _alibi_consts()
    lall_np, band_np, bdt_np, lex_np, mex_np = _gla_consts()
    lall = jnp.asarray(lall_np, BF16)
    band = jnp.asarray(band_np)
    bdt = jnp.asarray(bdt_np)
    lex = jnp.asarray(lex_np, BF16)
    mex = jnp.asarray(mex_np)
    bones_a = jnp.asarray(_block_ones(A_WIDTH, HEAD_DIM), BF16)
    bones_m = jnp.asarray(_block_ones(M_WIDTH, HEAD_DIM), BF16)

    o_qa, o_ka, o_va, o_ga = 0, 512, 1024, 1536
    o_qb, o_kb, o_vb, o_gb, o_rb = 2048, 2176, 2304, 2560, 2816
    o_qm, o_gm = 2832, 3088

    mk_all, mvx_all = _memkv_call(mem, g_mem, w_mem_kv, g_k_mem)

    for l in range(depth):
        w = w_in[l]
        col = lambda o, n: w[:, o:o + n]
        rb_pad = jnp.pad(col(o_rb, GATE_RANK), ((0, 0), (0, G_KW - GATE_RANK)))
        wn = jnp.concatenate([col(o_ka, 512), col(o_gb, 256), col(o_gm, 256), col(o_qb, 128), col(o_kb, 128),
                              col(o_vb, 256), rb_pad, col(o_qm, 256)], axis=1).astype(BF16)
        wt = jnp.concatenate([col(o_qa, 512), col(o_va, 512), col(o_ga, 512)], axis=1).T.astype(BF16)
        wgu = jnp.pad(w_gate_up[l], ((0, G_KW - GATE_RANK), (0, 0)))
        gk = jnp.tile(g_k_moba[l], A_HEADS).reshape(1, A_WIDTH)

        (qT, qa, kaug, vT, kmean_nat, sga, sgbm, gq, gk2, gv, gg, qm) = _proj_call(
            x, g_pre[l].reshape(1, d), wn, wt, gk, g_q_moba[l].reshape(HEAD_DIM, 1), bones_a, e, qx,
            wgu, b_gate_up[l].reshape(1, G_KW))

        bsz, s = x.shape[0], x.shape[1]
        nb = s // MOBA_BLOCK
        kmean = kmean_nat.reshape(bsz, nb, A_HEADS, HEAD_DIM).transpose(0, 2, 1, 3)
        ub = (UB_SLACK * LOG2E * HEAD_DIM ** 0.5) * jnp.max(jnp.abs(g_q_moba[l])) * jnp.max(jnp.abs(g_k_moba[l]))
        oaT = _moba_call(c1, ub.reshape(1), qa, kaug, vT, _sel_call(qT, kmean), sga)
        ob = _gla_call(gq, gk2, gv, gg, sgbm, lall, band, bdt, bones_m,
                       jnp.tile(g_gla_out[l], G_HEADS).reshape(1, G_VW), lex, mex)
        om = _mem_call(qm, sgbm, mk_all[l], mvx_all[l],
                       jnp.tile(g_q_mem[l], M_HEADS).reshape(1, M_WIDTH), bones_m)
        wo = w_out[l].astype(BF16)
        x = _out_call(x, oaT, ob, om, wo[:A_WIDTH], wo[A_WIDTH:])
    return x
```

```python
import functools

import numpy as np
import jax
import jax.numpy as jnp
from jax import lax
from jax.experimental import pallas as pl
from jax.experimental.pallas import tpu as pltpu

F32 = jnp.float32
BF16 = jnp.bfloat16
HIGHEST = lax.Precision.HIGHEST

EPS = 1e-6
LOG2E = 1.4426950408889634
HEAD_DIM = 64
A_HEADS = 8
A_WIDTH = A_HEADS * HEAD_DIM
MOBA_BLOCK = 256
MOBA_TOPK = 3
MOBA_GROUP = 4
MOBA_UNROLL = 4
MOBA_FAST_UNROLL = 96
MOBA_OWN_GROUPS = 4
SEL_WIDTH = 4096
G_HEADS = 4
G_DK = 32
G_DV = 64
G_KW = G_HEADS * G_DK
G_VW = G_HEADS * G_DV
GATE_RANK = 16
GATE_TEMP = 16.0
GLA_CHUNK = 64
GLA_SUB = 16
GLA_UNROLL = 4
GLA_LEVELS = (32, 16, 8, 4, 2, 1)
GLA_SAFE_EXP = 80.0
M_HEADS = 4
M_WIDTH = M_HEADS * HEAD_DIM
K_AUG = 128
V_AUG = 80
ROW_TILE = 512
OUT_TILE = 1024
V7X_VMEM_BYTES = 64 * 1024 * 1024
VMEM_LIMIT = 3 * V7X_VMEM_BYTES // 4
MOBA_VMEM_LIMIT = 29 * V7X_VMEM_BYTES // 32

NEG_BIG = -1e30
POS_BIG = 1e30
DEN_MIN = 1e-18
DEN_MAX = 1e30
UB_SLACK = 1.02

_NT = (((1,), (1,)), ((), ()))
_TN = (((0,), (0,)), ((), ()))


def _dot(a, b, **kw):
    return jnp.dot(a, b, preferred_element_type=F32, **kw)


def _dg(a, b, dims, **kw):
    return lax.dot_general(a, b, dims, preferred_element_type=F32, **kw)


def _split2(v):
    hi = v.astype(BF16)
    lo = (v - hi.astype(F32)).astype(BF16)
    return hi, lo


def _split3(v):
    hi = v.astype(BF16)
    r = v - hi.astype(F32)
    mid = r.astype(BF16)
    lo = (r - mid.astype(F32)).astype(BF16)
    return hi, mid, lo


def _group_sumsq(v, bones):
    hi, lo = _split2(v * v)
    return _dot(hi, bones) + _dot(lo, bones)


def _silu(v):
    return v / (1.0 + jnp.exp(-v))


def _block_ones(n, g):
    i = np.arange(n) // g
    return (i[:, None] == i[None, :]).astype(np.float32)


def _params(sem):
    return pltpu.CompilerParams(dimension_semantics=sem, vmem_limit_bytes=VMEM_LIMIT)


def _full(shape):
    return pl.BlockSpec(shape, lambda *_: (0,) * len(shape))


def _memkv_kernel(mem_ref, gmem_ref, w_ref, gk_ref, bones_ref, mk_ref, mvx_ref):
    m = mem_ref[...]
    ms = jnp.mean(m * m, axis=-1, keepdims=True)
    hn = (m * lax.rsqrt(ms + EPS) * gmem_ref[...]).astype(BF16)
    kv = _dot(hn, w_ref[...])
    mk = kv[:, :M_WIDTH]
    mv = kv[:, M_WIDTH:]
    ss = _group_sumsq(mk, bones_ref[...])
    mk_ref[...] = (mk * lax.rsqrt(ss * (1.0 / HEAD_DIM) + EPS) * gk_ref[...]).astype(BF16)
    lane = lax.broadcasted_iota(jnp.int32, mv.shape, 1) // HEAD_DIM
    for h in range(M_HEADS):
        hm = lane == h
        mvx_ref[h] = jnp.concatenate(
            [jnp.where(hm, mv, 0.0), jnp.where(hm, 1.0, 0.0)], axis=-1).astype(BF16)


def _memkv_call(mem, g_mem, w_mem_kv, g_k_mem):
    depth = g_mem.shape[0]
    bsz, mlen, d = mem.shape
    bones = jnp.asarray(_block_ones(M_WIDTH, HEAD_DIM), BF16)
    gk = jnp.tile(g_k_mem, (1, M_HEADS)).reshape(depth, 1, M_WIDTH)
    return pl.pallas_call(
        _memkv_kernel,
        grid=(depth, bsz),
        in_specs=[
            pl.BlockSpec((None, mlen, d), lambda l, b: (b, 0, 0)),
            pl.BlockSpec((None, 1, d), lambda l, b: (l, 0, 0)),
            pl.BlockSpec((None, d, 2 * M_WIDTH), lambda l, b: (l, 0, 0)),
            pl.BlockSpec((None, 1, M_WIDTH), lambda l, b: (l, 0, 0)),
            _full((M_WIDTH, M_WIDTH)),
        ],
        out_specs=[
            pl.BlockSpec((None, None, mlen, M_WIDTH), lambda l, b: (l, b, 0, 0)),
            pl.BlockSpec((None, None, M_HEADS, mlen, 2 * M_WIDTH), lambda l, b: (l, b, 0, 0, 0)),
        ],
        out_shape=[
            jax.ShapeDtypeStruct((depth, bsz, mlen, M_WIDTH), BF16),
            jax.ShapeDtypeStruct((depth, bsz, M_HEADS, mlen, 2 * M_WIDTH), BF16),
        ],
        compiler_params=_params(("arbitrary", "arbitrary")),
        name="memkv",
    )(mem, g_mem.reshape(depth, 1, d), w_mem_kv.astype(BF16), gk, bones)


_N_K = (0, 512)
_N_GBM = (512, 1024)
_N_QB = (1024, 1152)
_N_KB = (1152, 1280)
_N_VB = (1280, 1536)
_N_RB = (1536, 1664)
_N_QM = (1664, 1920)


def _proj_kernel(x_ref, gpre_ref, wn_ref, wt_ref, gk_ref, gqcol_ref, bones_ref, e_ref, qx_ref, wgu_ref, bgu_ref,
                 qT_ref, qa_ref, kaug_ref, vT_ref, kmean_ref, sga_ref, sgbm_ref, gq_ref, gk2_ref, gv_ref, gg_ref,
                 qm_ref):
    tm = x_ref.shape[0]
    nblk = tm // MOBA_BLOCK
    x = x_ref[...]
    ms = jnp.mean(x * x, axis=-1, keepdims=True)
    h = (x * lax.rsqrt(ms + EPS) * gpre_ref[...]).astype(BF16)

    def nat(cols):
        return _dot(h, wn_ref[:, cols[0]:cols[1]])

    k = nat(_N_K)
    qT = _dg(wt_ref[0:A_WIDTH, :], h, _NT)

    ss = _group_sumsq(k, bones_ref[...])
    kn = k * lax.rsqrt(ss * (1.0 / HEAD_DIM) + EPS) * gk_ref[...]
    kmean_ref[...] = jnp.mean(kn.reshape(nblk, MOBA_BLOCK, A_WIDTH), axis=1)
    e = jnp.concatenate([e_ref[...]] * nblk, axis=0)
    for hh in range(A_HEADS):
        rows = kn[:, hh * HEAD_DIM:(hh + 1) * HEAD_DIM]
        kaug_ref[hh] = jnp.concatenate([rows, e], axis=-1).astype(BF16)

    q3 = qT.reshape(A_HEADS, HEAD_DIM, tm)
    msq = jnp.mean(q3 * q3, axis=1, keepdims=True)
    qn = q3 * lax.rsqrt(msq + EPS) * gqcol_ref[...].reshape(1, HEAD_DIM, 1)
    qT_ref[...] = qn.reshape(A_WIDTH, tm)
    qs = (qn * (LOG2E * HEAD_DIM ** -0.5)).astype(BF16)
    blocks = [slice(t * MOBA_BLOCK, (t + 1) * MOBA_BLOCK) for t in range(nblk)]
    for hh in range(A_HEADS):
        for t, cols in enumerate(blocks):
            qa_ref[hh, t] = jnp.concatenate([qs[hh][:, cols], qx_ref[hh]], axis=0)

    vT = _dg(wt_ref[A_WIDTH:2 * A_WIDTH, :], h, _NT)
    ones_rows = jnp.where(lax.broadcasted_iota(jnp.int32, (V_AUG - HEAD_DIM, tm), 0) == 0, 1.0, 0.0)
    sga = _silu(_dg(wt_ref[2 * A_WIDTH:3 * A_WIDTH, :], h, _NT)).astype(BF16)
    for hh in range(A_HEADS):
        rows = slice(hh * HEAD_DIM, (hh + 1) * HEAD_DIM)
        vT_ref[hh] = jnp.concatenate([vT[rows], ones_rows], axis=0).astype(BF16)
        for t, cols in enumerate(blocks):
            sga_ref[hh, t] = sga[rows, cols]
    sgbm_ref[...] = _silu(nat(_N_GBM)).astype(BF16)

    qkb = nat((_N_QB[0], _N_KB[1]))
    gq_ref[...] = qkb[:, :G_KW] * (G_DK ** -0.5)
    gk2_ref[...] = qkb[:, G_KW:]
    rb = nat(_N_RB)
    gv_ref[...] = nat(_N_VB).astype(BF16)
    qm_ref[...] = nat(_N_QM)
    z = _dot(rb, wgu_ref[...], precision=HIGHEST) + bgu_ref[...]
    gg_ref[...] = (jnp.minimum(z, 0.0) - jnp.log(1.0 + jnp.exp(-jnp.abs(z)))) * (1.0 / GATE_TEMP)


def _proj_call(x, g_pre, wn, wt, gk, gqcol, bones, e, qx, wgu, bgu):
    bsz, s, d = x.shape
    tm = ROW_TILE
    nt = s // tm
    nblk = tm // MOBA_BLOCK
    nb = s // MOBA_BLOCK
    row = lambda w: pl.BlockSpec((None, tm, w), lambda b, t: (b, t, 0))
    in_specs = [
        row(d), _full((1, d)), _full(wn.shape), _full(wt.shape), _full((1, A_WIDTH)),
        _full((HEAD_DIM, 1)), _full((A_WIDTH, A_WIDTH)), _full(e.shape), _full(qx.shape), _full(wgu.shape),
        _full((1, G_KW)),
    ]
    group = MOBA_GROUP * MOBA_BLOCK
    assert d == wn.shape[0] and s % group == 0 and group % tm == 0 and tm % MOBA_BLOCK == 0
    tpg = group // tm
    blocked = lambda r: pl.BlockSpec((None, A_HEADS, nblk, r, MOBA_BLOCK), lambda b, t: (b, 0, t, 0, 0))
    out_specs = [
        pl.BlockSpec((None, A_WIDTH, tm), lambda b, t: (b, 0, t)),
        blocked(K_AUG),
        pl.BlockSpec((None, A_HEADS, tm, K_AUG), lambda b, t: (b, 0, t, 0)),
        pl.BlockSpec((None, A_HEADS, None, V_AUG, tm), lambda b, t: (b, 0, t // tpg, 0, t % tpg)),
        pl.BlockSpec((None, None, nblk, A_WIDTH), lambda b, t: (b, t, 0, 0)),
        blocked(HEAD_DIM),
        row(2 * M_WIDTH), row(G_KW), row(G_KW), row(G_VW), row(G_KW), row(M_WIDTH),
    ]
    out_shape = [
        jax.ShapeDtypeStruct((bsz, A_WIDTH, s), F32),
        jax.ShapeDtypeStruct((bsz, A_HEADS, nb, K_AUG, MOBA_BLOCK), BF16),
        jax.ShapeDtypeStruct((bsz, A_HEADS, s, K_AUG), BF16),
        jax.ShapeDtypeStruct((bsz, A_HEADS, s // group, V_AUG, group), BF16),
        jax.ShapeDtypeStruct((bsz, nt, nblk, A_WIDTH), F32),
        jax.ShapeDtypeStruct((bsz, A_HEADS, nb, HEAD_DIM, MOBA_BLOCK), BF16),
        jax.ShapeDtypeStruct((bsz, s, 2 * M_WIDTH), BF16),
        jax.ShapeDtypeStruct((bsz, s, G_KW), F32),
        jax.ShapeDtypeStruct((bsz, s, G_KW), F32),
        jax.ShapeDtypeStruct((bsz, s, G_VW), BF16),
        jax.ShapeDtypeStruct((bsz, s, G_KW), F32),
        jax.ShapeDtypeStruct((bsz, s, M_WIDTH), F32),
    ]
    return pl.pallas_call(
        _proj_kernel, grid=(bsz, nt), in_specs=in_specs, out_specs=out_specs, out_shape=out_shape,
        compiler_params=_params(("arbitrary", "arbitrary")), name="proj",
    )(x, g_pre, wn, wt, gk, gqcol, bones, e, qx, wgu, bgu)


def _sel_kernel(qT_ref, kmean_ref, o_ref):
    width = qT_ref.shape[1]
    nb = kmean_ref.shape[0]
    per = width // MOBA_BLOCK

    def choose(first):
        rows = min(nb, first + per)
        gate = _dot(kmean_ref[0:rows, :], qT_ref[...], precision=HIGHEST)
        rowf = lax.broadcasted_iota(jnp.int32, gate.shape, 0).astype(F32)
        qblk = (first + lax.broadcasted_iota(jnp.int32, gate.shape, 1) // MOBA_BLOCK).astype(F32)
        g = jnp.where(rowf < qblk, gate, -jnp.inf)
        sel = jnp.zeros(gate.shape, F32)
        for _ in range(MOBA_TOPK):
            m = jnp.max(g, axis=0, keepdims=True)
            idx = jnp.min(jnp.where(g == m, rowf, float(rows)), axis=0, keepdims=True)
            hit = rowf == idx
            sel = jnp.where(hit, 1.0, sel)
            g = jnp.where(hit, -jnp.inf, g)
        sel = jnp.where(rowf < qblk, sel, 0.0)
        for u in range(per):
            o_ref[u, 0:rows, :] = sel[:, u * MOBA_BLOCK:(u + 1) * MOBA_BLOCK]
            if rows < nb:
                o_ref[u, rows:nb, :] = jnp.zeros((nb - rows, MOBA_BLOCK), F32)

    for step in range(nb // per):
        pl.when(pl.program_id(2) == step)(functools.partial(choose, step * per))


def _sel_call(qT, kmean):
    bsz, _, s = qT.shape
    nb = s // MOBA_BLOCK
    width = min(SEL_WIDTH, s)
    return pl.pallas_call(
        _sel_kernel,
        grid=(bsz, A_HEADS, s // width),
        in_specs=[
            pl.BlockSpec((None, HEAD_DIM, width), lambda b, h, c: (b, h, c)),
            pl.BlockSpec((None, None, nb, HEAD_DIM), lambda b, h, c: (b, h, 0, 0)),
        ],
        out_specs=pl.BlockSpec((None, None, width // MOBA_BLOCK, nb, MOBA_BLOCK), lambda b, h, c: (b, h, c, 0, 0)),
        out_shape=jax.ShapeDtypeStruct((bsz, A_HEADS, nb, nb, MOBA_BLOCK), F32),
        compiler_params=_params(("arbitrary", "arbitrary", "arbitrary")),
        name="mobasel",
    )(qT, kmean)


def _moba_items(nb):
    def table(rows, multiple):
        n_items = -(-len(rows) // multiple) * multiple
        rows = rows + [(rows[-1][0], 0, 0, 0)] * (n_items + 2 - len(rows))
        return np.asarray(rows, np.int32).T, n_items

    fast = [(i, g, int(g == 0), 1) for i in range(MOBA_GROUP, nb) for g in range(i // MOBA_GROUP)]
    robust = [(i, g, int(g == 0), 1) for i in range(1, nb) for g in range((i - 1) // MOBA_GROUP + 1)]
    return table(fast, MOBA_FAST_UNROLL), table(robust, MOBA_UNROLL)


def _moba_kernel(tabf_ref, tabr_ref, c1_ref, ub_ref, qa_ref, kall_ref, vall_ref, sel_ref, sga_ref, o_ref,
                 m_scr, acc_scr, out_scr, s0_ref, s1_ref, mb0_ref, mb1_ref, p0_ref, p1_ref,
                 *, n_fast, n_robust):
    hh = pl.program_id(1)
    c1 = c1_ref[hh]
    ub = ub_ref[0]
    nb = qa_ref.shape[0]
    gk = MOBA_GROUP * MOBA_BLOCK
    half = gk // 2
    per_half = MOBA_GROUP // 2
    bufs = ((s0_ref, mb0_ref, p0_ref), (s1_ref, mb1_ref, p1_ref))
    kk = lax.broadcasted_iota(jnp.int32, (MOBA_BLOCK, MOBA_BLOCK), 0)
    qq = lax.broadcasted_iota(jnp.int32, (MOBA_BLOCK, MOBA_BLOCK), 1)

    sigma = ub + c1 * lax.broadcasted_iota(jnp.int32, (1, MOBA_BLOCK), 1).astype(F32)

    def past_probs(s_blk, i, j, valid):
        picked = jnp.where(valid == 1, sel_ref[i, pl.ds(j, 1), :], 0.0) > 0.5
        cj = c1 * (MOBA_BLOCK * (j - i)).astype(F32)
        shift = jnp.where(picked, sigma - cj, POS_BIG)
        return jnp.exp2(s_blk - shift).astype(BF16)

    def own_probs(t, k, p_ref):
        t = jnp.minimum(t, nb // MOBA_GROUP - 1)
        i = MOBA_GROUP * t + k
        for part in range(k // per_half + 1):
            members = min(per_half, k + 1 - per_half * part)
            r0 = pl.multiple_of(t * gk + part * half, half)
            s = _dot(kall_ref[pl.ds(r0, members * MOBA_BLOCK), :], qa_ref[i])
            for u2 in range(members):
                u = per_half * part + u2
                blk = s[u2 * MOBA_BLOCK:(u2 + 1) * MOBA_BLOCK]
                if u == k:
                    p = jnp.exp2(jnp.where(kk <= qq, blk, NEG_BIG) - sigma).astype(BF16)
                else:
                    p = past_probs(blk, i, MOBA_GROUP * t + u, 1)
                p_ref[u * MOBA_BLOCK:(u + 1) * MOBA_BLOCK, :] = p

    def own_accumulate(t, k, p_ref):
        n = (k + 1) * MOBA_BLOCK
        acc = _dot(vall_ref[t][:, 0:n], p_ref[0:n, :])
        acc_scr[MOBA_GROUP * t + k] = acc
        out_scr[MOBA_GROUP * t + k] = acc

    own_probs(0, 0, p0_ref)

    def own_body(tt, carry):
        for step in range(MOBA_OWN_GROUPS * MOBA_GROUP):
            t, k = MOBA_OWN_GROUPS * tt + step // MOBA_GROUP, step % MOBA_GROUP
            nxt = (t, k + 1) if k + 1 < MOBA_GROUP else (t + 1, 0)
            own_probs(nxt[0], nxt[1], bufs[(step + 1) % 2][2])
            own_accumulate(t, k, bufs[step % 2][2])
        return carry

    lax.fori_loop(0, nb // (MOBA_GROUP * MOBA_OWN_GROUPS), own_body, 0)

    def fast_probs(n, p_ref):
        i, g, valid = tabf_ref[0, n], tabf_ref[1, n], tabf_ref[3, n]
        for part in range(2):
            r0 = pl.multiple_of(g * gk + part * half, half)
            s = _dot(kall_ref[pl.ds(r0, half), :], qa_ref[i])
            for u2 in range(per_half):
                u = per_half * part + u2
                p_ref[u * MOBA_BLOCK:(u + 1) * MOBA_BLOCK, :] = past_probs(
                    s[u2 * MOBA_BLOCK:(u2 + 1) * MOBA_BLOCK], i, g * MOBA_GROUP + u, valid)

    def fast_accumulate(n, p_ref, acc):
        i, g, first = tabf_ref[0, n], tabf_ref[1, n], tabf_ref[2, n]
        acc = jnp.where(first == 1, acc_scr[i], acc) + _dot(vall_ref[g], p_ref[...])
        out_scr[i] = acc
        return acc

    fast_probs(0, p0_ref)

    def fast_body(t, acc):
        for k in range(MOBA_FAST_UNROLL):
            n = MOBA_FAST_UNROLL * t + k
            fast_probs(n + 1, bufs[(k + 1) % 2][2])
            acc = fast_accumulate(n, bufs[k % 2][2], acc)
        return acc

    lax.fori_loop(0, n_fast // MOBA_FAST_UNROLL, fast_body, jnp.zeros((V_AUG, MOBA_BLOCK), F32))

    den = out_scr[:, HEAD_DIM:HEAD_DIM + 1, :]
    in_range = jnp.logical_and(jnp.min(den) > DEN_MIN, jnp.max(den) < DEN_MAX)

    def item(n):
        return tabr_ref[0, n], tabr_ref[1, n], tabr_ref[2, n], tabr_ref[3, n]

    def diag_scores(i, s_ref, mb_ref):
        i = jnp.minimum(i, nb - 1)
        r0 = pl.multiple_of(i * MOBA_BLOCK, MOBA_BLOCK)
        s = _dot(kall_ref[pl.ds(r0, MOBA_BLOCK), :], qa_ref[i])
        s = jnp.where(kk <= qq, s, NEG_BIG)
        s_ref[0:MOBA_BLOCK, :] = s
        mb_ref[0] = jnp.max(s, axis=0, keepdims=True)

    def diag_softmax(i, s_ref, mb_ref, p_ref):
        i = jnp.minimum(i, nb - 1)
        m0 = mb_ref[0]
        m_scr[i] = jnp.broadcast_to(m0, m_scr.shape[1:])
        p_ref[0:MOBA_BLOCK, :] = jnp.exp2(s_ref[0:MOBA_BLOCK, :] - m0).astype(BF16)

    def diag_accumulate(t, k, p_ref):
        vown = vall_ref[t][:, k * MOBA_BLOCK:(k + 1) * MOBA_BLOCK]
        acc_scr[MOBA_GROUP * t + k] = _dot(vown, p_ref[0:MOBA_BLOCK, :])

    def scores(n, s_ref, mb_ref):
        i, g, _, _ = item(n)
        for part in range(2):
            r0 = pl.multiple_of(g * gk + part * half, half)
            s = _dot(kall_ref[pl.ds(r0, half), :], qa_ref[i])
            s_ref[part * half:(part + 1) * half, :] = s
            for u in range(per_half):
                blk = s[u * MOBA_BLOCK:(u + 1) * MOBA_BLOCK]
                mb_ref[part * per_half + u] = jnp.max(blk, axis=0, keepdims=True)

    def softmax(n, s_ref, mb_ref, p_ref, m):
        i, g, first, valid = item(n)
        m = jnp.where(first == 1, m_scr[i][0:1], m)
        m_new = m
        members = []
        for u in range(MOBA_GROUP):
            j = g * MOBA_GROUP + u
            picked = jnp.where(valid == 1, sel_ref[i, pl.ds(j, 1), :], 0.0) > 0.5
            cj = c1 * (MOBA_BLOCK * (j - i)).astype(F32)
            m_new = jnp.where(picked, jnp.maximum(m_new, mb_ref[u] + cj), m_new)
            members.append((picked, cj))
        for u, (picked, cj) in enumerate(members):
            shift = jnp.where(picked, m_new - cj, POS_BIG)
            rows = slice(u * MOBA_BLOCK, (u + 1) * MOBA_BLOCK)
            p_ref[rows, :] = jnp.exp2(s_ref[rows, :] - shift).astype(BF16)
        return m_new, jnp.exp2(m - m_new)

    def accumulate(n, p_ref, alpha, acc):
        i, g, first, _ = item(n)
        acc = jnp.where(first == 1, acc_scr[i], acc) * alpha + _dot(vall_ref[g], p_ref[...])
        acc_scr[i] = acc
        return acc

    @pl.when(jnp.logical_not(in_range))
    def _():
        diag_scores(0, s0_ref, mb0_ref)
        diag_scores(1, s1_ref, mb1_ref)
        diag_softmax(0, s0_ref, mb0_ref, p0_ref)

        def diag_body(t, carry):
            for k in range(MOBA_GROUP):
                i = MOBA_GROUP * t + k
                s_a, mb_a, p_a = bufs[k % 2]
                s_b, mb_b, p_b = bufs[(k + 1) % 2]
                diag_scores(i + 2, s_a, mb_a)
                diag_softmax(i + 1, s_b, mb_b, p_b)
                diag_accumulate(t, k, p_a)
            return carry

        lax.fori_loop(0, nb // MOBA_GROUP, diag_body, 0)

        scores(0, s0_ref, mb0_ref)
        scores(1, s1_ref, mb1_ref)
        m_init = jnp.full((1, MOBA_BLOCK), NEG_BIG, F32)
        m, alpha = softmax(0, s0_ref, mb0_ref, p0_ref, m_init)
        acc = jnp.zeros((V_AUG, MOBA_BLOCK), F32)

        def body(t, carry):
            m, alpha, acc = carry
            for k in range(MOBA_UNROLL):
                n = MOBA_UNROLL * t + k
                s_a, mb_a, p_a = bufs[k % 2]
                s_b, mb_b, p_b = bufs[(k + 1) % 2]
                acc = accumulate(n, p_a, alpha, acc)
                m, alpha = softmax(n + 1, s_b, mb_b, p_b, m)
                scores(n + 2, s_a, mb_a)
            return m, alpha, acc

        lax.fori_loop(0, n_robust // MOBA_UNROLL, body, (m, alpha, acc))
        out_scr[...] = acc_scr[...]

    def finish(i, carry):
        acc = out_scr[i]
        o = acc[:HEAD_DIM] / acc[HEAD_DIM:HEAD_DIM + 1]
        o_ref[i] = (o * sga_ref[i].astype(F32)).astype(BF16)
        return carry

    lax.fori_loop(0, nb, finish, 0, unroll=4)


def _moba_call(c1, ub, qa, kaug, vT, sel, sga):
    bsz, _, nb = qa.shape[:3]
    s = nb * MOBA_BLOCK
    gk = MOBA_GROUP * MOBA_BLOCK
    assert nb % (MOBA_GROUP * MOBA_OWN_GROUPS) == 0 and MOBA_GROUP % 2 == 0 and MOBA_UNROLL % 2 == 0
    assert MOBA_FAST_UNROLL % MOBA_UNROLL == 0
    (tabf, n_fast), (tabr, n_robust) = _moba_items(nb)
    per_bh = lambda *tail: pl.BlockSpec((None, None) + tail, lambda b, h: (b, h) + (0,) * len(tail))
    return pl.pallas_call(
        functools.partial(_moba_kernel, n_fast=n_fast, n_robust=n_robust),
        grid=(bsz, A_HEADS),
        in_specs=[
            pl.BlockSpec(memory_space=pltpu.SMEM),
            pl.BlockSpec(memory_space=pltpu.SMEM),
            pl.BlockSpec(memory_space=pltpu.SMEM),
            pl.BlockSpec(memory_space=pltpu.SMEM),
            per_bh(nb, K_AUG, MOBA_BLOCK),
            per_bh(s, K_AUG),
            per_bh(nb // MOBA_GROUP, V_AUG, gk),
            per_bh(nb, nb, MOBA_BLOCK),
            per_bh(nb, HEAD_DIM, MOBA_BLOCK),
        ],
        out_specs=per_bh(nb, HEAD_DIM, MOBA_BLOCK),
        out_shape=jax.ShapeDtypeStruct((bsz, A_HEADS, nb, HEAD_DIM, MOBA_BLOCK), BF16),
        scratch_shapes=[pltpu.VMEM((nb, 8, MOBA_BLOCK), F32), pltpu.VMEM((nb, V_AUG, MOBA_BLOCK), F32),
                        pltpu.VMEM((nb, V_AUG, MOBA_BLOCK), F32),
                        pltpu.VMEM((gk, MOBA_BLOCK), F32), pltpu.VMEM((gk, MOBA_BLOCK), F32),
                        pltpu.VMEM((MOBA_GROUP, 1, MOBA_BLOCK), F32), pltpu.VMEM((MOBA_GROUP, 1, MOBA_BLOCK), F32),
                        pltpu.VMEM((gk, MOBA_BLOCK), BF16), pltpu.VMEM((gk, MOBA_BLOCK), BF16)],
        compiler_params=pltpu.CompilerParams(dimension_semantics=("arbitrary", "arbitrary"),
                                             vmem_limit_bytes=MOBA_VMEM_LIMIT),
        name="moba",
    )(jnp.asarray(tabf), jnp.asarray(tabr), c1, ub, qa, kaug, vT, sel, sga)


def _gla_consts():
    c, sb = GLA_CHUNK, GLA_SUB
    r = np.arange(c)
    blk = r // sb
    tri = (r[None, :] <= r[:, None])
    same = blk[None, :] == blk[:, None]
    lall = np.concatenate([
        tri,
        tri & same,
        same,
        blk[None, :] == blk[:, None] - 1,
        blk[None, :] == blk[:, None] - 2,
    ], axis=0).astype(np.float32)
    diff = blk[:, None] - blk[None, :]
    band = np.stack([diff == 1, diff == 2, diff == 3, same & tri]).astype(np.float32)
    dk_head = np.arange(G_KW) // G_DK
    dv_head = np.arange(G_VW) // G_DV
    bdt = (dv_head[:, None] == dk_head[None, :]).astype(np.float32)
    lex, mex = [tri], [np.eye(c, dtype=bool)]
    for s in GLA_LEVELS:
        same_s = (r[None, :] // s) == (r[:, None] // s)
        lex += [tri & same_s, same_s]
        pair = (r[None, :] // (2 * s)) == (r[:, None] // (2 * s))
        mex.append(pair & ((r[:, None] // s) % 2 == 1) & ((r[None, :] // s) % 2 == 0))
    lex = np.concatenate(lex, axis=0).astype(np.float32)
    mex = np.stack(mex).astype(np.float32)
    return lall, band, bdt, lex, mex


def _gla_kernel(q_ref, k_ref, v_ref, g_ref, sgb_ref, lall_ref, band_ref, bdt_ref, bones_ref, gout_ref,
                lex_ref, mex_ref, o_ref, st_ref):
    c = GLA_CHUNK
    nbatch = q_ref.shape[0]

    @pl.when(pl.program_id(0) == 0)
    def _():
        st_ref[...] = jnp.zeros_like(st_ref)

    lane_k = lax.broadcasted_iota(jnp.int32, (1, G_KW), 1) // G_DK
    lane_v = lax.broadcasted_iota(jnp.int32, (1, G_VW), 1) // G_DV
    n_chunks = q_ref.shape[1] // c

    def heads(t):
        return jnp.concatenate([jnp.where(lane_k == hd, t, 0.0) for hd in range(G_HEADS)],
                               axis=0).astype(BF16)

    def chunk(ci, states):
        r0 = pl.multiple_of(ci * c, c)
        nbs = range(nbatch)
        q = [q_ref[bi, pl.ds(r0, c), :] for bi in nbs]
        k = [k_ref[bi, pl.ds(r0, c), :] for bi in nbs]
        v = [v_ref[bi, pl.ds(r0, c), :] for bi in nbs]

        r = [_dot(lall_ref[...], jnp.concatenate(_split3(g_ref[bi, pl.ds(r0, c), :]), axis=1)) for bi in nbs]
        r = [x[:, :G_KW] + x[:, G_KW:2 * G_KW] + x[:, 2 * G_KW:] for x in r]
        b, cc, tt, p1, p2 = ([x[n * c:(n + 1) * c] for x in r] for n in range(5))
        b_last = [x[c - 1:c] for x in b]

        qt = [q[bi] * jnp.exp(cc[bi]) for bi in nbs]
        q2 = [qt[bi] * jnp.exp(p1[bi]) for bi in nbs]
        q3 = [q2[bi] * jnp.exp(p2[bi]) for bi in nbs]
        kt = [(k[bi] * jnp.exp(tt[bi] - cc[bi])).astype(BF16) for bi in nbs]
        kd = [(k[bi] * jnp.exp(-cc[bi])).astype(BF16) for bi in nbs]
        qi = [(q[bi] * jnp.exp(b[bi])).astype(BF16) for bi in nbs]
        kl = [(k[bi] * jnp.exp(b_last[bi] - b[bi])).astype(BF16) for bi in nbs]

        xs = [[_dg(heads(t[bi]), kt[bi], _NT) for bi in nbs] for t in (qt, q2, q3)]
        yd = [_dg(heads(qt[bi]), kd[bi], _NT) for bi in nbs]
        o = [_dg(qi[bi], states[bi].astype(BF16), _NT) for bi in nbs]
        st_new = tuple(states[bi] * jnp.exp(b_last[bi]) + _dg(v[bi], kl[bi], _TN) * bdt_ref[...]
                       for bi in nbs)
        for hd in range(G_HEADS):
            sl = slice(hd * c, (hd + 1) * c)
            for bi in nbs:
                att = (band_ref[0] * xs[0][bi][sl] + band_ref[1] * xs[1][bi][sl]
                       + band_ref[2] * xs[2][bi][sl] + band_ref[3] * yd[bi][sl])
                vh = jnp.where(lane_v == hd, v[bi], jnp.zeros_like(v[bi]))
                o[bi] = o[bi] + _dot(att.astype(BF16), vh)

        ss = [_group_sumsq(o[bi], bones_ref[...]) for bi in nbs]
        for bi in nbs:
            y = o[bi] * lax.rsqrt(ss[bi] * (1.0 / G_DV) + EPS) * gout_ref[...]
            o_ref[bi, pl.ds(r0, c), :] = (y * sgb_ref[bi, pl.ds(r0, c), :].astype(F32)).astype(BF16)
        return st_new

    def chunk_stable(ci, states):
        r0 = pl.multiple_of(ci * c, c)
        out = []
        for bi in range(nbatch):
            q = q_ref[bi, pl.ds(r0, c), :]
            k = k_ref[bi, pl.ds(r0, c), :]
            v = v_ref[bi, pl.ds(r0, c), :]
            r = _dot(lex_ref[...], jnp.concatenate(_split3(g_ref[bi, pl.ds(r0, c), :]), axis=1))
            r = r[:, :G_KW] + r[:, G_KW:2 * G_KW] + r[:, 2 * G_KW:]
            b = r[0:c]
            b_last = b[c - 1:c]
            x = [_dg(heads(q), k.astype(BF16), _NT)]
            for lv in range(len(GLA_LEVELS)):
                cs = r[(1 + 2 * lv) * c:(2 + 2 * lv) * c]
                ts = r[(2 + 2 * lv) * c:(3 + 2 * lv) * c]
                x.append(_dg(heads(q * jnp.exp(cs)), (k * jnp.exp(ts - cs)).astype(BF16), _NT))
            o = _dg((q * jnp.exp(b)).astype(BF16), states[bi].astype(BF16), _NT)
            for hd in range(G_HEADS):
                sl = slice(hd * c, (hd + 1) * c)
                att = mex_ref[0] * x[0][sl]
                for lv in range(len(GLA_LEVELS)):
                    att = att + mex_ref[lv + 1] * x[lv + 1][sl]
                o = o + _dot(att.astype(BF16), jnp.where(lane_v == hd, v, jnp.zeros_like(v)))
            kl = (k * jnp.exp(b_last - b)).astype(BF16)
            out.append(states[bi] * jnp.exp(b_last) + _dg(v, kl, _TN) * bdt_ref[...])
            ss = _group_sumsq(o, bones_ref[...])
            y = o * lax.rsqrt(ss * (1.0 / G_DV) + EPS) * gout_ref[...]
            o_ref[bi, pl.ds(r0, c), :] = (y * sgb_ref[bi, pl.ds(r0, c), :].astype(F32)).astype(BF16)
        return tuple(out)

    def run(body, unroll):
        states = lax.fori_loop(0, n_chunks, body, tuple(st_ref[bi] for bi in range(nbatch)), unroll=unroll)
        for bi in range(nbatch):
            st_ref[bi] = states[bi]

    risky = jnp.max(-g_ref[...]) * GLA_SUB > GLA_SAFE_EXP
    pl.when(jnp.logical_not(risky))(lambda: run(chunk, GLA_UNROLL))
    pl.when(risky)(lambda: run(chunk_stable, 1))


def _gla_call(gq, gk, gv, gg, sgbm, lall, band, bdt, bones, gout, lex, mex):
    bsz, s, _ = gq.shape
    tm = ROW_TILE
    row = lambda w: pl.BlockSpec((bsz, tm, w), lambda t: (0, t, 0))
    return pl.pallas_call(
        _gla_kernel,
        grid=(s // tm,),
        in_specs=[row(G_KW), row(G_KW), row(G_VW), row(G_KW), row(G_VW),
                  _full(lall.shape), _full(band.shape), _full(bdt.shape), _full(bones.shape), _full((1, G_VW)),
                  _full(lex.shape), _full(mex.shape)],
        out_specs=row(G_VW),
        out_shape=jax.ShapeDtypeStruct((bsz, s, G_VW), BF16),
        scratch_shapes=[pltpu.VMEM((bsz, G_VW, G_KW), F32)],
        compiler_params=_params(("arbitrary",)),
        name="gla",
    )(gq, gk, gv, gg, sgbm, lall, band, bdt, bones, gout, lex, mex)


def _mem_kernel(qm_ref, sgm_ref, mk_ref, mvx_ref, gq_ref, bones_ref, o_ref):
    qm = qm_ref[...]
    ss = _group_sumsq(qm, bones_ref[...])
    qn = qm * lax.rsqrt(ss * (1.0 / HEAD_DIM) + EPS) * gq_ref[...]
    qs = qn * (LOG2E * HEAD_DIM ** -0.5)
    lane = lax.broadcasted_iota(jnp.int32, (1, M_WIDTH), 1) // HEAD_DIM
    mk = mk_ref[...]
    s = [_dg(jnp.where(lane == h, qs, 0.0).astype(BF16), mk, _NT) for h in range(M_HEADS)]
    p = [jnp.exp2(x - jnp.max(x, axis=-1, keepdims=True)).astype(BF16) for x in s]
    acc = _dot(p[0], mvx_ref[0])
    for h in range(1, M_HEADS):
        acc = acc + _dot(p[h], mvx_ref[h])
    om = acc[:, :M_WIDTH] / acc[:, M_WIDTH:]
    o_ref[...] = (om * sgm_ref[...].astype(F32)).astype(BF16)


def _mem_call(qm, sgbm, mk, mvx, gq, bones):
    bsz, s, _ = qm.shape
    tm = ROW_TILE
    mlen = mk.shape[1]
    return pl.pallas_call(
        _mem_kernel,
        grid=(bsz, s // tm),
        in_specs=[
            pl.BlockSpec((None, tm, M_WIDTH), lambda b, t: (b, t, 0)),
            pl.BlockSpec((None, tm, M_WIDTH), lambda b, t: (b, t, 1)),
            pl.BlockSpec((None, mlen, M_WIDTH), lambda b, t: (b, 0, 0)),
            pl.BlockSpec((None, M_HEADS, mlen, 2 * M_WIDTH), lambda b, t: (b, 0, 0, 0)),
            _full((1, M_WIDTH)), _full((M_WIDTH, M_WIDTH)),
        ],
        out_specs=pl.BlockSpec((None, tm, M_WIDTH), lambda b, t: (b, t, 0)),
        out_shape=jax.ShapeDtypeStruct((bsz, s, M_WIDTH), BF16),
        compiler_params=_params(("arbitrary", "arbitrary")),
        name="memattn",
    )(qm, sgbm, mk, mvx, gq, bones)


def _out_kernel(x_ref, oaT_ref, ob_ref, om_ref, wa_ref, wb_ref, o_ref):
    y = _dot(jnp.concatenate([ob_ref[...], om_ref[...]], axis=-1), wb_ref[...])
    for t in range(oaT_ref.shape[1]):
        rows = slice(t * MOBA_BLOCK, (t + 1) * MOBA_BLOCK)
        oaT = oaT_ref[:, t].reshape(A_WIDTH, MOBA_BLOCK)
        o_ref[rows, :] = x_ref[rows, :] + y[rows] + _dg(oaT, wa_ref[...], _TN)


def _out_call(x, oaT, ob, om, wa, wb):
    bsz, s, d = x.shape
    tm = OUT_TILE
    assert s % tm == 0 and tm % MOBA_BLOCK == 0
    row = lambda w: pl.BlockSpec((None, tm, w), lambda b, t: (b, t, 0))
    return pl.pallas_call(
        _out_kernel,
        grid=(bsz, s // tm),
        in_specs=[row(d),
                  pl.BlockSpec((None, A_HEADS, tm // MOBA_BLOCK, HEAD_DIM, MOBA_BLOCK), lambda b, t: (b, 0, t, 0, 0)),
                  row(G_VW), row(M_WIDTH),
                  _full(wa.shape), _full(wb.shape)],
        out_specs=row(d),
        out_shape=jax.ShapeDtypeStruct((bsz, s, d), F32),
        compiler_params=_params(("arbitrary", "arbitrary")),
        name="outproj",
    )(x, oaT, ob, om, wa, wb)


def _alibi_consts():
    slopes = np.asarray([2.0 ** (-8.0 * (i + 1) / A_HEADS) for i in range(A_HEADS)], np.float32)
    c1 = (slopes * np.float32(LOG2E)).astype(np.float32)
    c1j = jnp.asarray(c1)
    pieces = list(_split3(c1j * 16.0)) + list(_split3(c1j))
    qx = jnp.zeros((A_HEADS, K_AUG - HEAD_DIM, MOBA_BLOCK), BF16)
    for n, pc in enumerate(pieces):
        qx = qx.at[:, n, :].set(jnp.broadcast_to(pc[:, None], (A_HEADS, MOBA_BLOCK)))
    pos = np.arange(MOBA_BLOCK)
    e = np.zeros((MOBA_BLOCK, K_AUG - HEAD_DIM), np.float32)
    e[:, 0:3] = (pos // 16)[:, None]
    e[:, 3:6] = (pos % 16)[:, None]
    return c1j, qx, jnp.asarray(e)


def kernel(x, mem, g_pre, w_in, g_q_moba, g_k_moba, w_gate_up, b_gate_up, g_gla_out,
           g_mem, w_mem_kv, g_q_mem, g_k_mem, w_out):
    depth = g_pre.shape[0]
    d = x.shape[-1]
    c1, qx, e = _alibi_consts()
    lall_np, band_np, bdt_np, lex_np, mex_np = _gla_consts()
    lall = jnp.asarray(lall_np, BF16)
    band = jnp.asarray(band_np)
    bdt = jnp.asarray(bdt_np)
    lex = jnp.asarray(lex_np, BF16)
    mex = jnp.asarray(mex_np)
    bones_a = jnp.asarray(_block_ones(A_WIDTH, HEAD_DIM), BF16)
    bones_m = jnp.asarray(_block_ones(M_WIDTH, HEAD_DIM), BF16)

    o_qa, o_ka, o_va, o_ga = 0, 512, 1024, 1536
    o_qb, o_kb, o_vb, o_gb, o_rb = 2048, 2176, 2304, 2560, 2816
    o_qm, o_gm = 2832, 3088

    mk_all, mvx_all = _memkv_call(mem, g_mem, w_mem_kv, g_k_mem)

    for l in range(depth):
        w = w_in[l]
        col = lambda o, n: w[:, o:o + n]
        rb_pad = jnp.pad(col(o_rb, GATE_RANK), ((0, 0), (0, G_KW - GATE_RANK)))
        wn = jnp.concatenate([col(o_ka, 512), col(o_gb, 256), col(o_gm, 256), col(o_qb, 128), col(o_kb, 128),
                              col(o_vb, 256), rb_pad, col(o_qm, 256)], axis=1).astype(BF16)
        wt = jnp.concatenate([col(o_qa, 512), col(o_va, 512), col(o_ga, 512)], axis=1).T.astype(BF16)
        wgu = jnp.pad(w_gate_up[l], ((0, G_KW - GATE_RANK), (0, 0)))
        gk = jnp.tile(g_k_moba[l], A_HEADS).reshape(1, A_WIDTH)

        (qT, qa, kaug, vT, kmean_nat, sga, sgbm, gq, gk2, gv, gg, qm) = _proj_call(
            x, g_pre[l].reshape(1, d), wn, wt, gk, g_q_moba[l].reshape(HEAD_DIM, 1), bones_a, e, qx,
            wgu, b_gate_up[l].reshape(1, G_KW))

        bsz, s = x.shape[0], x.shape[1]
        nb = s // MOBA_BLOCK
        kmean = kmean_nat.reshape(bsz, nb, A_HEADS, HEAD_DIM).transpose(0, 2, 1, 3)
        ub = (UB_SLACK * LOG2E * HEAD_DIM ** 0.5) * jnp.max(jnp.abs(g_q_moba[l])) * jnp.max(jnp.abs(g_k_moba[l]))
        oaT = _moba_call(c1, ub.reshape(1), qa, kaug, vT, _sel_call(qT, kmean), sga)
        ob = _gla_call(gq, gk2, gv, gg, sgbm, lall, band, bdt, bones_m,
                       jnp.tile(g_gla_out[l], G_HEADS).reshape(1, G_VW), lex, mex)
        om = _mem_call(qm, sgbm, mk_all[l], mvx_all[l],
                       jnp.tile(g_q_mem[l], M_HEADS).reshape(1, M_WIDTH), bones_m)
        wo = w_out[l].astype(BF16)
        x = _out_call(x, oaT, ob, om, wo[:A_WIDTH], wo[A_WIDTH:])
    return x
```

```python
import functools

import numpy as np
import jax
import jax.numpy as jnp
from jax import lax
from jax.experimental import pallas as pl
from jax.experimental.pallas import tpu as pltpu

F32 = jnp.float32
BF16 = jnp.bfloat16
HIGHEST = lax.Precision.HIGHEST

EPS = 1e-6
LOG2E = 1.4426950408889634
HEAD_DIM = 64
A_HEADS = 8
A_WIDTH = A_HEADS * HEAD_DIM
MOBA_BLOCK = 256
MOBA_TOPK = 3
MOBA_GROUP = 4
MOBA_UNROLL = 4
MOBA_FAST_UNROLL = 48
MOBA_OWN_GROUPS = 2
SEL_WIDTH = 4096
G_HEADS = 4
G_DK = 32
G_DV = 64
G_KW = G_HEADS * G_DK
G_VW = G_HEADS * G_DV
GATE_RANK = 16
GATE_TEMP = 16.0
GLA_CHUNK = 64
GLA_SUB = 16
GLA_UNROLL = 4
GLA_LEVELS = (32, 16, 8, 4, 2, 1)
GLA_SAFE_EXP = 80.0
M_HEADS = 4
M_WIDTH = M_HEADS * HEAD_DIM
K_AUG = 128
V_AUG = 80
ROW_TILE = 512
OUT_TILE = 1024
V7X_VMEM_BYTES = 64 * 1024 * 1024
VMEM_LIMIT = 3 * V7X_VMEM_BYTES // 4
MOBA_VMEM_LIMIT = 29 * V7X_VMEM_BYTES // 32

NEG_BIG = -1e30
POS_BIG = 1e30
DEN_MIN = 1e-18
DEN_MAX = 1e30
UB_SLACK = 1.02

_NT = (((1,), (1,)), ((), ()))
_TN = (((0,), (0,)), ((), ()))


def _dot(a, b, **kw):
    return jnp.dot(a, b, preferred_element_type=F32, **kw)


def _dg(a, b, dims, **kw):
    return lax.dot_general(a, b, dims, preferred_element_type=F32, **kw)


def _split2(v):
    hi = v.astype(BF16)
    lo = (v - hi.astype(F32)).astype(BF16)
    return hi, lo


def _split3(v):
    hi = v.astype(BF16)
    r = v - hi.astype(F32)
    mid = r.astype(BF16)
    lo = (r - mid.astype(F32)).astype(BF16)
    return hi, mid, lo


def _group_sumsq(v, bones):
    hi, lo = _split2(v * v)
    return _dot(hi, bones) + _dot(lo, bones)


def _silu(v):
    return v / (1.0 + jnp.exp(-v))


def _block_ones(n, g):
    i = np.arange(n) // g
    return (i[:, None] == i[None, :]).astype(np.float32)


def _params(sem):
    return pltpu.CompilerParams(dimension_semantics=sem, vmem_limit_bytes=VMEM_LIMIT)


def _full(shape):
    return pl.BlockSpec(shape, lambda *_: (0,) * len(shape))


def _memkv_kernel(mem_ref, gmem_ref, w_ref, gk_ref, bones_ref, mk_ref, mvx_ref):
    m = mem_ref[...]
    ms = jnp.mean(m * m, axis=-1, keepdims=True)
    hn = (m * lax.rsqrt(ms + EPS) * gmem_ref[...]).astype(BF16)
    kv = _dot(hn, w_ref[...])
    mk = kv[:, :M_WIDTH]
    mv = kv[:, M_WIDTH:]
    ss = _group_sumsq(mk, bones_ref[...])
    mk_ref[...] = (mk * lax.rsqrt(ss * (1.0 / HEAD_DIM) + EPS) * gk_ref[...]).astype(BF16)
    lane = lax.broadcasted_iota(jnp.int32, mv.shape, 1) // HEAD_DIM
    for h in range(M_HEADS):
        hm = lane == h
        mvx_ref[h] = jnp.concatenate(
            [jnp.where(hm, mv, 0.0), jnp.where(hm, 1.0, 0.0)], axis=-1).astype(BF16)


def _memkv_call(mem, g_mem, w_mem_kv, g_k_mem):
    depth = g_mem.shape[0]
    bsz, mlen, d = mem.shape
    bones = jnp.asarray(_block_ones(M_WIDTH, HEAD_DIM), BF16)
    gk = jnp.tile(g_k_mem, (1, M_HEADS)).reshape(depth, 1, M_WIDTH)
    return pl.pallas_call(
        _memkv_kernel,
        grid=(depth, bsz),
        in_specs=[
            pl.BlockSpec((None, mlen, d), lambda l, b: (b, 0, 0)),
            pl.BlockSpec((None, 1, d), lambda l, b: (l, 0, 0)),
            pl.BlockSpec((None, d, 2 * M_WIDTH), lambda l, b: (l, 0, 0)),
            pl.BlockSpec((None, 1, M_WIDTH), lambda l, b: (l, 0, 0)),
            _full((M_WIDTH, M_WIDTH)),
        ],
        out_specs=[
            pl.BlockSpec((None, None, mlen, M_WIDTH), lambda l, b: (l, b, 0, 0)),
            pl.BlockSpec((None, None, M_HEADS, mlen, 2 * M_WIDTH), lambda l, b: (l, b, 0, 0, 0)),
        ],
        out_shape=[
            jax.ShapeDtypeStruct((depth, bsz, mlen, M_WIDTH), BF16),
            jax.ShapeDtypeStruct((depth, bsz, M_HEADS, mlen, 2 * M_WIDTH), BF16),
        ],
        compiler_params=_params(("arbitrary", "arbitrary")),
        name="memkv",
    )(mem, g_mem.reshape(depth, 1, d), w_mem_kv.astype(BF16), gk, bones)


_N_K = (0, 512)
_N_GBM = (512, 1024)
_N_QB = (1024, 1152)
_N_KB = (1152, 1280)
_N_VB = (1280, 1536)
_N_RB = (1536, 1664)
_N_QM = (1664, 1920)


def _proj_kernel(x_ref, gpre_ref, wn_ref, wt_ref, gk_ref, gqcol_ref, bones_ref, e_ref, qx_ref, wgu_ref, bgu_ref,
                 qT_ref, qa_ref, kaug_ref, vT_ref, kmean_ref, sga_ref, sgbm_ref, gq_ref, gk2_ref, gv_ref, gg_ref,
                 qm_ref):
    tm = x_ref.shape[0]
    nblk = tm // MOBA_BLOCK
    x = x_ref[...]
    ms = jnp.mean(x * x, axis=-1, keepdims=True)
    h = (x * lax.rsqrt(ms + EPS) * gpre_ref[...]).astype(BF16)

    def nat(cols):
        return _dot(h, wn_ref[:, cols[0]:cols[1]])

    k = nat(_N_K)
    qT = _dg(wt_ref[0:A_WIDTH, :], h, _NT)

    ss = _group_sumsq(k, bones_ref[...])
    kn = k * lax.rsqrt(ss * (1.0 / HEAD_DIM) + EPS) * gk_ref[...]
    kmean_ref[...] = jnp.mean(kn.reshape(nblk, MOBA_BLOCK, A_WIDTH), axis=1)
    e = jnp.concatenate([e_ref[...]] * nblk, axis=0)
    for hh in range(A_HEADS):
        rows = kn[:, hh * HEAD_DIM:(hh + 1) * HEAD_DIM]
        kaug_ref[hh] = jnp.concatenate([rows, e], axis=-1).astype(BF16)

    q3 = qT.reshape(A_HEADS, HEAD_DIM, tm)
    msq = jnp.mean(q3 * q3, axis=1, keepdims=True)
    qn = q3 * lax.rsqrt(msq + EPS) * gqcol_ref[...].reshape(1, HEAD_DIM, 1)
    qT_ref[...] = qn.reshape(A_WIDTH, tm)
    qs = (qn * (LOG2E * HEAD_DIM ** -0.5)).astype(BF16)
    blocks = [slice(t * MOBA_BLOCK, (t + 1) * MOBA_BLOCK) for t in range(nblk)]
    for hh in range(A_HEADS):
        for t, cols in enumerate(blocks):
            qa_ref[hh, t] = jnp.concatenate([qs[hh][:, cols], qx_ref[hh]], axis=0)

    vT = _dg(wt_ref[A_WIDTH:2 * A_WIDTH, :], h, _NT)
    ones_rows = jnp.where(lax.broadcasted_iota(jnp.int32, (V_AUG - HEAD_DIM, tm), 0) == 0, 1.0, 0.0)
    sga = _silu(_dg(wt_ref[2 * A_WIDTH:3 * A_WIDTH, :], h, _NT)).astype(BF16)
    for hh in range(A_HEADS):
        rows = slice(hh * HEAD_DIM, (hh + 1) * HEAD_DIM)
        vT_ref[hh] = jnp.concatenate([vT[rows], ones_rows], axis=0).astype(BF16)
        for t, cols in enumerate(blocks):
            sga_ref[hh, t] = sga[rows, cols]
    sgbm_ref[...] = _silu(nat(_N_GBM)).astype(BF16)

    qkb = nat((_N_QB[0], _N_KB[1]))
    gq_ref[...] = qkb[:, :G_KW] * (G_DK ** -0.5)
    gk2_ref[...] = qkb[:, G_KW:]
    rb = nat(_N_RB)
    gv_ref[...] = nat(_N_VB).astype(BF16)
    qm_ref[...] = nat(_N_QM)
    z = _dot(rb, wgu_ref[...], precision=HIGHEST) + bgu_ref[...]
    gg_ref[...] = (jnp.minimum(z, 0.0) - jnp.log(1.0 + jnp.exp(-jnp.abs(z)))) * (1.0 / GATE_TEMP)


def _proj_call(x, g_pre, wn, wt, gk, gqcol, bones, e, qx, wgu, bgu):
    bsz, s, d = x.shape
    tm = ROW_TILE
    nt = s // tm
    nblk = tm // MOBA_BLOCK
    nb = s // MOBA_BLOCK
    row = lambda w: pl.BlockSpec((None, tm, w), lambda b, t: (b, t, 0))
    in_specs = [
        row(d), _full((1, d)), _full(wn.shape), _full(wt.shape), _full((1, A_WIDTH)),
        _full((HEAD_DIM, 1)), _full((A_WIDTH, A_WIDTH)), _full(e.shape), _full(qx.shape), _full(wgu.shape),
        _full((1, G_KW)),
    ]
    group = MOBA_GROUP * MOBA_BLOCK
    assert d == wn.shape[0] and s % group == 0 and group % tm == 0 and tm % MOBA_BLOCK == 0
    tpg = group // tm
    blocked = lambda r: pl.BlockSpec((None, A_HEADS, nblk, r, MOBA_BLOCK), lambda b, t: (b, 0, t, 0, 0))
    out_specs = [
        pl.BlockSpec((None, A_WIDTH, tm), lambda b, t: (b, 0, t)),
        blocked(K_AUG),
        pl.BlockSpec((None, A_HEADS, tm, K_AUG), lambda b, t: (b, 0, t, 0)),
        pl.BlockSpec((None, A_HEADS, None, V_AUG, tm), lambda b, t: (b, 0, t // tpg, 0, t % tpg)),
        pl.BlockSpec((None, None, nblk, A_WIDTH), lambda b, t: (b, t, 0, 0)),
        blocked(HEAD_DIM),
        row(2 * M_WIDTH), row(G_KW), row(G_KW), row(G_VW), row(G_KW), row(M_WIDTH),
    ]
    out_shape = [
        jax.ShapeDtypeStruct((bsz, A_WIDTH, s), F32),
        jax.ShapeDtypeStruct((bsz, A_HEADS, nb, K_AUG, MOBA_BLOCK), BF16),
        jax.ShapeDtypeStruct((bsz, A_HEADS, s, K_AUG), BF16),
        jax.ShapeDtypeStruct((bsz, A_HEADS, s // group, V_AUG, group), BF16),
        jax.ShapeDtypeStruct((bsz, nt, nblk, A_WIDTH), F32),
        jax.ShapeDtypeStruct((bsz, A_HEADS, nb, HEAD_DIM, MOBA_BLOCK), BF16),
        jax.ShapeDtypeStruct((bsz, s, 2 * M_WIDTH), BF16),
        jax.ShapeDtypeStruct((bsz, s, G_KW), F32),
        jax.ShapeDtypeStruct((bsz, s, G_KW), F32),
        jax.ShapeDtypeStruct((bsz, s, G_VW), BF16),
        jax.ShapeDtypeStruct((bsz, s, G_KW), F32),
        jax.ShapeDtypeStruct((bsz, s, M_WIDTH), F32),
    ]
    return pl.pallas_call(
        _proj_kernel, grid=(bsz, nt), in_specs=in_specs, out_specs=out_specs, out_shape=out_shape,
        compiler_params=_params(("arbitrary", "arbitrary")), name="proj",
    )(x, g_pre, wn, wt, gk, gqcol, bones, e, qx, wgu, bgu)


def _sel_kernel(qT_ref, kmean_ref, o_ref):
    width = qT_ref.shape[1]
    nb = kmean_ref.shape[0]
    per = width // MOBA_BLOCK

    def choose(first):
        rows = min(nb, first + per)
        gate = _dot(kmean_ref[0:rows, :], qT_ref[...], precision=HIGHEST)
        rowf = lax.broadcasted_iota(jnp.int32, gate.shape, 0).astype(F32)
        qblk = (first + lax.broadcasted_iota(jnp.int32, gate.shape, 1) // MOBA_BLOCK).astype(F32)
        g = jnp.where(rowf < qblk, gate, -jnp.inf)
        sel = jnp.zeros(gate.shape, F32)
        for _ in range(MOBA_TOPK):
            m = jnp.max(g, axis=0, keepdims=True)
            idx = jnp.min(jnp.where(g == m, rowf, float(rows)), axis=0, keepdims=True)
            hit = rowf == idx
            sel = jnp.where(hit, 1.0, sel)
            g = jnp.where(hit, -jnp.inf, g)
        sel = jnp.where(rowf < qblk, sel, 0.0)
        for u in range(per):
            o_ref[u, 0:rows, :] = sel[:, u * MOBA_BLOCK:(u + 1) * MOBA_BLOCK]
            if rows < nb:
                o_ref[u, rows:nb, :] = jnp.zeros((nb - rows, MOBA_BLOCK), F32)

    for step in range(nb // per):
        pl.when(pl.program_id(2) == step)(functools.partial(choose, step * per))


def _sel_call(qT, kmean):
    bsz, _, s = qT.shape
    nb = s // MOBA_BLOCK
    width = min(SEL_WIDTH, s)
    return pl.pallas_call(
        _sel_kernel,
        grid=(bsz, A_HEADS, s // width),
        in_specs=[
            pl.BlockSpec((None, HEAD_DIM, width), lambda b, h, c: (b, h, c)),
            pl.BlockSpec((None, None, nb, HEAD_DIM), lambda b, h, c: (b, h, 0, 0)),
        ],
        out_specs=pl.BlockSpec((None, None, width // MOBA_BLOCK, nb, MOBA_BLOCK), lambda b, h, c: (b, h, c, 0, 0)),
        out_shape=jax.ShapeDtypeStruct((bsz, A_HEADS, nb, nb, MOBA_BLOCK), F32),
        compiler_params=_params(("arbitrary", "arbitrary", "arbitrary")),
        name="mobasel",
    )(qT, kmean)


def _moba_items(nb):
    def table(rows, multiple):
        n_items = -(-len(rows) // multiple) * multiple
        rows = rows + [(rows[-1][0], 0, 0, 0)] * (n_items + 2 - len(rows))
        return np.asarray(rows, np.int32).T, n_items

    fast = [(i, g, int(g == 0), 1) for i in range(MOBA_GROUP, nb) for g in range(i // MOBA_GROUP)]
    robust = [(i, g, int(g == 0), 1) for i in range(1, nb) for g in range((i - 1) // MOBA_GROUP + 1)]
    return table(fast, MOBA_FAST_UNROLL), table(robust, MOBA_UNROLL)


def _moba_kernel(tabf_ref, tabr_ref, c1_ref, ub_ref, qa_ref, kall_ref, vall_ref, sel_ref, sga_ref, o_ref,
                 m_scr, acc_scr, out_scr, s0_ref, s1_ref, mb0_ref, mb1_ref, p0_ref, p1_ref,
                 *, n_fast, n_robust):
    hh = pl.program_id(1)
    c1 = c1_ref[hh]
    ub = ub_ref[0]
    nb = qa_ref.shape[0]
    gk = MOBA_GROUP * MOBA_BLOCK
    half = gk // 2
    per_half = MOBA_GROUP // 2
    bufs = ((s0_ref, mb0_ref, p0_ref), (s1_ref, mb1_ref, p1_ref))
    kk = lax.broadcasted_iota(jnp.int32, (MOBA_BLOCK, MOBA_BLOCK), 0)
    qq = lax.broadcasted_iota(jnp.int32, (MOBA_BLOCK, MOBA_BLOCK), 1)

    sigma = ub + c1 * lax.broadcasted_iota(jnp.int32, (1, MOBA_BLOCK), 1).astype(F32)

    def past_probs(s_blk, i, j, valid):
        picked = jnp.where(valid == 1, sel_ref[i, pl.ds(j, 1), :], 0.0) > 0.5
        cj = c1 * (MOBA_BLOCK * (j - i)).astype(F32)
        shift = jnp.where(picked, sigma - cj, POS_BIG)
        return jnp.exp2(s_blk - shift).astype(BF16)

    def own_probs(t, k, p_ref):
        t = jnp.minimum(t, nb // MOBA_GROUP - 1)
        i = MOBA_GROUP * t + k
        for part in range(k // per_half + 1):
            members = min(per_half, k + 1 - per_half * part)
            r0 = pl.multiple_of(t * gk + part * half, half)
            s = _dot(kall_ref[pl.ds(r0, members * MOBA_BLOCK), :], qa_ref[i])
            for u2 in range(members):
                u = per_half * part + u2
                blk = s[u2 * MOBA_BLOCK:(u2 + 1) * MOBA_BLOCK]
                if u == k:
                    p = jnp.exp2(jnp.where(kk <= qq, blk, NEG_BIG) - sigma).astype(BF16)
                else:
                    p = past_probs(blk, i, MOBA_GROUP * t + u, 1)
                p_ref[u * MOBA_BLOCK:(u + 1) * MOBA_BLOCK, :] = p

    def own_accumulate(t, k, p_ref):
        n = (k + 1) * MOBA_BLOCK
        acc = _dot(vall_ref[t][:, 0:n], p_ref[0:n, :])
        acc_scr[MOBA_GROUP * t + k] = acc
        out_scr[MOBA_GROUP * t + k] = acc

    own_probs(0, 0, p0_ref)

    def own_body(tt, carry):
        for step in range(MOBA_OWN_GROUPS * MOBA_GROUP):
            t, k = MOBA_OWN_GROUPS * tt + step // MOBA_GROUP, step % MOBA_GROUP
            nxt = (t, k + 1) if k + 1 < MOBA_GROUP else (t + 1, 0)
            own_probs(nxt[0], nxt[1], bufs[(step + 1) % 2][2])
            own_accumulate(t, k, bufs[step % 2][2])
        return carry

    lax.fori_loop(0, nb // (MOBA_GROUP * MOBA_OWN_GROUPS), own_body, 0)

    def fast_probs(n, p_ref):
        i, g, valid = tabf_ref[0, n], tabf_ref[1, n], tabf_ref[3, n]
        for part in range(2):
            r0 = pl.multiple_of(g * gk + part * half, half)
            s = _dot(kall_ref[pl.ds(r0, half), :], qa_ref[i])
            for u2 in range(per_half):
                u = per_half * part + u2
                p_ref[u * MOBA_BLOCK:(u + 1) * MOBA_BLOCK, :] = past_probs(
                    s[u2 * MOBA_BLOCK:(u2 + 1) * MOBA_BLOCK], i, g * MOBA_GROUP + u, valid)

    def fast_accumulate(n, p_ref, acc):
        i, g, first = tabf_ref[0, n], tabf_ref[1, n], tabf_ref[2, n]
        acc = jnp.where(first == 1, acc_scr[i], acc) + _dot(vall_ref[g], p_ref[...])
        out_scr[i] = acc
        return acc

    fast_probs(0, p0_ref)

    def fast_body(t, acc):
        for k in range(MOBA_FAST_UNROLL):
            n = MOBA_FAST_UNROLL * t + k
            fast_probs(n + 1, bufs[(k + 1) % 2][2])
            acc = fast_accumulate(n, bufs[k % 2][2], acc)
        return acc

    lax.fori_loop(0, n_fast // MOBA_FAST_UNROLL, fast_body, jnp.zeros((V_AUG, MOBA_BLOCK), F32))

    den = out_scr[:, HEAD_DIM:HEAD_DIM + 1, :]
    in_range = jnp.logical_and(jnp.min(den) > DEN_MIN, jnp.max(den) < DEN_MAX)

    def item(n):
        return tabr_ref[0, n], tabr_ref[1, n], tabr_ref[2, n], tabr_ref[3, n]

    def diag_scores(i, s_ref, mb_ref):
        i = jnp.minimum(i, nb - 1)
        r0 = pl.multiple_of(i * MOBA_BLOCK, MOBA_BLOCK)
        s = _dot(kall_ref[pl.ds(r0, MOBA_BLOCK), :], qa_ref[i])
        s = jnp.where(kk <= qq, s, NEG_BIG)
        s_ref[0:MOBA_BLOCK, :] = s
        mb_ref[0] = jnp.max(s, axis=0, keepdims=True)

    def diag_softmax(i, s_ref, mb_ref, p_ref):
        i = jnp.minimum(i, nb - 1)
        m0 = mb_ref[0]
        m_scr[i] = jnp.broadcast_to(m0, m_scr.shape[1:])
        p_ref[0:MOBA_BLOCK, :] = jnp.exp2(s_ref[0:MOBA_BLOCK, :] - m0).astype(BF16)

    def diag_accumulate(t, k, p_ref):
        vown = vall_ref[t][:, k * MOBA_BLOCK:(k + 1) * MOBA_BLOCK]
        acc_scr[MOBA_GROUP * t + k] = _dot(vown, p_ref[0:MOBA_BLOCK, :])

    def scores(n, s_ref, mb_ref):
        i, g, _, _ = item(n)
        for part in range(2):
            r0 = pl.multiple_of(g * gk + part * half, half)
            s = _dot(kall_ref[pl.ds(r0, half), :], qa_ref[i])
            s_ref[part * half:(part + 1) * half, :] = s
            for u in range(per_half):
                blk = s[u * MOBA_BLOCK:(u + 1) * MOBA_BLOCK]
                mb_ref[part * per_half + u] = jnp.max(blk, axis=0, keepdims=True)

    def softmax(n, s_ref, mb_ref, p_ref, m):
        i, g, first, valid = item(n)
        m = jnp.where(first == 1, m_scr[i][0:1], m)
        m_new = m
        members = []
        for u in range(MOBA_GROUP):
            j = g * MOBA_GROUP + u
            picked = jnp.where(valid == 1, sel_ref[i, pl.ds(j, 1), :], 0.0) > 0.5
            cj = c1 * (MOBA_BLOCK * (j - i)).astype(F32)
            m_new = jnp.where(picked, jnp.maximum(m_new, mb_ref[u] + cj), m_new)
            members.append((picked, cj))
        for u, (picked, cj) in enumerate(members):
            shift = jnp.where(picked, m_new - cj, POS_BIG)
            rows = slice(u * MOBA_BLOCK, (u + 1) * MOBA_BLOCK)
            p_ref[rows, :] = jnp.exp2(s_ref[rows, :] - shift).astype(BF16)
        return m_new, jnp.exp2(m - m_new)

    def accumulate(n, p_ref, alpha, acc):
        i, g, first, _ = item(n)
        acc = jnp.where(first == 1, acc_scr[i], acc) * alpha + _dot(vall_ref[g], p_ref[...])
        acc_scr[i] = acc
        return acc

    @pl.when(jnp.logical_not(in_range))
    def _():
        diag_scores(0, s0_ref, mb0_ref)
        diag_scores(1, s1_ref, mb1_ref)
        diag_softmax(0, s0_ref, mb0_ref, p0_ref)

        def diag_body(t, carry):
            for k in range(MOBA_GROUP):
                i = MOBA_GROUP * t + k
                s_a, mb_a, p_a = bufs[k % 2]
                s_b, mb_b, p_b = bufs[(k + 1) % 2]
                diag_scores(i + 2, s_a, mb_a)
                diag_softmax(i + 1, s_b, mb_b, p_b)
                diag_accumulate(t, k, p_a)
            return carry

        lax.fori_loop(0, nb // MOBA_GROUP, diag_body, 0)

        scores(0, s0_ref, mb0_ref)
        scores(1, s1_ref, mb1_ref)
        m_init = jnp.full((1, MOBA_BLOCK), NEG_BIG, F32)
        m, alpha = softmax(0, s0_ref, mb0_ref, p0_ref, m_init)
        acc = jnp.zeros((V_AUG, MOBA_BLOCK), F32)

        def body(t, carry):
            m, alpha, acc = carry
            for k in range(MOBA_UNROLL):
                n = MOBA_UNROLL * t + k
                s_a, mb_a, p_a = bufs[k % 2]
                s_b, mb_b, p_b = bufs[(k + 1) % 2]
                acc = accumulate(n, p_a, alpha, acc)
                m, alpha = softmax(n + 1, s_b, mb_b, p_b, m)
                scores(n + 2, s_a, mb_a)
            return m, alpha, acc

        lax.fori_loop(0, n_robust // MOBA_UNROLL, body, (m, alpha, acc))
        out_scr[...] = acc_scr[...]

    def finish(i, carry):
        acc = out_scr[i]
        o = acc[:HEAD_DIM] / acc[HEAD_DIM:HEAD_DIM + 1]
        o_ref[i] = (o * sga_ref[i].astype(F32)).astype(BF16)
        return carry

    lax.fori_loop(0, nb, finish, 0, unroll=4)


def _moba_call(c1, ub, qa, kaug, vT, sel, sga):
    bsz, _, nb = qa.shape[:3]
    s = nb * MOBA_BLOCK
    gk = MOBA_GROUP * MOBA_BLOCK
    assert nb % (MOBA_GROUP * MOBA_OWN_GROUPS) == 0 and MOBA_GROUP % 2 == 0 and MOBA_UNROLL % 2 == 0
    assert MOBA_FAST_UNROLL % MOBA_UNROLL == 0
    (tabf, n_fast), (tabr, n_robust) = _moba_items(nb)
    per_bh = lambda *tail: pl.BlockSpec((None, None) + tail, lambda b, h: (b, h) + (0,) * len(tail))
    return pl.pallas_call(
        functools.partial(_moba_kernel, n_fast=n_fast, n_robust=n_robust),
        grid=(bsz, A_HEADS),
        in_specs=[
            pl.BlockSpec(memory_space=pltpu.SMEM),
            pl.BlockSpec(memory_space=pltpu.SMEM),
            pl.BlockSpec(memory_space=pltpu.SMEM),
            pl.BlockSpec(memory_space=pltpu.SMEM),
            per_bh(nb, K_AUG, MOBA_BLOCK),
            per_bh(s, K_AUG),
            per_bh(nb // MOBA_GROUP, V_AUG, gk),
            per_bh(nb, nb, MOBA_BLOCK),
            per_bh(nb, HEAD_DIM, MOBA_BLOCK),
        ],
        out_specs=per_bh(nb, HEAD_DIM, MOBA_BLOCK),
        out_shape=jax.ShapeDtypeStruct((bsz, A_HEADS, nb, HEAD_DIM, MOBA_BLOCK), BF16),
        scratch_shapes=[pltpu.VMEM((nb, 8, MOBA_BLOCK), F32), pltpu.VMEM((nb, V_AUG, MOBA_BLOCK), F32),
                        pltpu.VMEM((nb, V_AUG, MOBA_BLOCK), F32),
                        pltpu.VMEM((gk, MOBA_BLOCK), F32), pltpu.VMEM((gk, MOBA_BLOCK), F32),
                        pltpu.VMEM((MOBA_GROUP, 1, MOBA_BLOCK), F32), pltpu.VMEM((MOBA_GROUP, 1, MOBA_BLOCK), F32),
                        pltpu.VMEM((gk, MOBA_BLOCK), BF16), pltpu.VMEM((gk, MOBA_BLOCK), BF16)],
        compiler_params=pltpu.CompilerParams(dimension_semantics=("arbitrary", "arbitrary"),
                                             vmem_limit_bytes=MOBA_VMEM_LIMIT),
        name="moba",
    )(jnp.asarray(tabf), jnp.asarray(tabr), c1, ub, qa, kaug, vT, sel, sga)


def _gla_consts():
    c, sb = GLA_CHUNK, GLA_SUB
    r = np.arange(c)
    blk = r // sb
    tri = (r[None, :] <= r[:, None])
    same = blk[None, :] == blk[:, None]
    lall = np.concatenate([
        tri,
        tri & same,
        same,
        blk[None, :] == blk[:, None] - 1,
        blk[None, :] == blk[:, None] - 2,
    ], axis=0).astype(np.float32)
    diff = blk[:, None] - blk[None, :]
    band = np.stack([diff == 1, diff == 2, diff == 3, same & tri]).astype(np.float32)
    dk_head = np.arange(G_KW) // G_DK
    dv_head = np.arange(G_VW) // G_DV
    bdt = (dv_head[:, None] == dk_head[None, :]).astype(np.float32)
    lex, mex = [tri], [np.eye(c, dtype=bool)]
    for s in GLA_LEVELS:
        same_s = (r[None, :] // s) == (r[:, None] // s)
        lex += [tri & same_s, same_s]
        pair = (r[None, :] // (2 * s)) == (r[:, None] // (2 * s))
        mex.append(pair & ((r[:, None] // s) % 2 == 1) & ((r[None, :] // s) % 2 == 0))
    lex = np.concatenate(lex, axis=0).astype(np.float32)
    mex = np.stack(mex).astype(np.float32)
    return lall, band, bdt, lex, mex


def _gla_kernel(q_ref, k_ref, v_ref, g_ref, sgb_ref, lall_ref, band_ref, bdt_ref, bones_ref, gout_ref,
                lex_ref, mex_ref, o_ref, st_ref):
    c = GLA_CHUNK
    nbatch = q_ref.shape[0]

    @pl.when(pl.program_id(0) == 0)
    def _():
        st_ref[...] = jnp.zeros_like(st_ref)

    lane_k = lax.broadcasted_iota(jnp.int32, (1, G_KW), 1) // G_DK
    lane_v = lax.broadcasted_iota(jnp.int32, (1, G_VW), 1) // G_DV
    n_chunks = q_ref.shape[1] // c

    def heads(t):
        return jnp.concatenate([jnp.where(lane_k == hd, t, 0.0) for hd in range(G_HEADS)],
                               axis=0).astype(BF16)

    def chunk(ci, states):
        r0 = pl.multiple_of(ci * c, c)
        nbs = range(nbatch)
        q = [q_ref[bi, pl.ds(r0, c), :] for bi in nbs]
        k = [k_ref[bi, pl.ds(r0, c), :] for bi in nbs]
        v = [v_ref[bi, pl.ds(r0, c), :] for bi in nbs]

        r = [_dot(lall_ref[...], jnp.concatenate(_split3(g_ref[bi, pl.ds(r0, c), :]), axis=1)) for bi in nbs]
        r = [x[:, :G_KW] + x[:, G_KW:2 * G_KW] + x[:, 2 * G_KW:] for x in r]
        b, cc, tt, p1, p2 = ([x[n * c:(n + 1) * c] for x in r] for n in range(5))
        b_last = [x[c - 1:c] for x in b]

        qt = [q[bi] * jnp.exp(cc[bi]) for bi in nbs]
        q2 = [qt[bi] * jnp.exp(p1[bi]) for bi in nbs]
        q3 = [q2[bi] * jnp.exp(p2[bi]) for bi in nbs]
        kt = [(k[bi] * jnp.exp(tt[bi] - cc[bi])).astype(BF16) for bi in nbs]
        kd = [(k[bi] * jnp.exp(-cc[bi])).astype(BF16) for bi in nbs]
        qi = [(q[bi] * jnp.exp(b[bi])).astype(BF16) for bi in nbs]
        kl = [(k[bi] * jnp.exp(b_last[bi] - b[bi])).astype(BF16) for bi in nbs]

        xs = [[_dg(heads(t[bi]), kt[bi], _NT) for bi in nbs] for t in (qt, q2, q3)]
        yd = [_dg(heads(qt[bi]), kd[bi], _NT) for bi in nbs]
        o = [_dg(qi[bi], states[bi].astype(BF16), _NT) for bi in nbs]
        st_new = tuple(states[bi] * jnp.exp(b_last[bi]) + _dg(v[bi], kl[bi], _TN) * bdt_ref[...]
                       for bi in nbs)
        for hd in range(G_HEADS):
            sl = slice(hd * c, (hd + 1) * c)
            for bi in nbs:
                att = (band_ref[0] * xs[0][bi][sl] + band_ref[1] * xs[1][bi][sl]
                       + band_ref[2] * xs[2][bi][sl] + band_ref[3] * yd[bi][sl])
                vh = jnp.where(lane_v == hd, v[bi], jnp.zeros_like(v[bi]))
                o[bi] = o[bi] + _dot(att.astype(BF16), vh)

        ss = [_group_sumsq(o[bi], bones_ref[...]) for bi in nbs]
        for bi in nbs:
            y = o[bi] * lax.rsqrt(ss[bi] * (1.0 / G_DV) + EPS) * gout_ref[...]
            o_ref[bi, pl.ds(r0, c), :] = (y * sgb_ref[bi, pl.ds(r0, c), :].astype(F32)).astype(BF16)
        return st_new

    def chunk_stable(ci, states):
        r0 = pl.multiple_of(ci * c, c)
        out = []
        for bi in range(nbatch):
            q = q_ref[bi, pl.ds(r0, c), :]
            k = k_ref[bi, pl.ds(r0, c), :]
            v = v_ref[bi, pl.ds(r0, c), :]
            r = _dot(lex_ref[...], jnp.concatenate(_split3(g_ref[bi, pl.ds(r0, c), :]), axis=1))
            r = r[:, :G_KW] + r[:, G_KW:2 * G_KW] + r[:, 2 * G_KW:]
            b = r[0:c]
            b_last = b[c - 1:c]
            x = [_dg(heads(q), k.astype(BF16), _NT)]
            for lv in range(len(GLA_LEVELS)):
                cs = r[(1 + 2 * lv) * c:(2 + 2 * lv) * c]
                ts = r[(2 + 2 * lv) * c:(3 + 2 * lv) * c]
                x.append(_dg(heads(q * jnp.exp(cs)), (k * jnp.exp(ts - cs)).astype(BF16), _NT))
            o = _dg((q * jnp.exp(b)).astype(BF16), states[bi].astype(BF16), _NT)
            for hd in range(G_HEADS):
                sl = slice(hd * c, (hd + 1) * c)
                att = mex_ref[0] * x[0][sl]
                for lv in range(len(GLA_LEVELS)):
                    att = att + mex_ref[lv + 1] * x[lv + 1][sl]
                o = o + _dot(att.astype(BF16), jnp.where(lane_v == hd, v, jnp.zeros_like(v)))
            kl = (k * jnp.exp(b_last - b)).astype(BF16)
            out.append(states[bi] * jnp.exp(b_last) + _dg(v, kl, _TN) * bdt_ref[...])
            ss = _group_sumsq(o, bones_ref[...])
            y = o * lax.rsqrt(ss * (1.0 / G_DV) + EPS) * gout_ref[...]
            o_ref[bi, pl.ds(r0, c), :] = (y * sgb_ref[bi, pl.ds(r0, c), :].astype(F32)).astype(BF16)
        return tuple(out)

    def run(body, unroll):
        states = lax.fori_loop(0, n_chunks, body, tuple(st_ref[bi] for bi in range(nbatch)), unroll=unroll)
        for bi in range(nbatch):
            st_ref[bi] = states[bi]

    risky = jnp.max(-g_ref[...]) * GLA_SUB > GLA_SAFE_EXP
    pl.when(jnp.logical_not(risky))(lambda: run(chunk, GLA_UNROLL))
    pl.when(risky)(lambda: run(chunk_stable, 1))


def _gla_call(gq, gk, gv, gg, sgbm, lall, band, bdt, bones, gout, lex, mex):
    bsz, s, _ = gq.shape
    tm = ROW_TILE
    row = lambda w: pl.BlockSpec((bsz, tm, w), lambda t: (0, t, 0))
    return pl.pallas_call(
        _gla_kernel,
        grid=(s // tm,),
        in_specs=[row(G_KW), row(G_KW), row(G_VW), row(G_KW), row(G_VW),
                  _full(lall.shape), _full(band.shape), _full(bdt.shape), _full(bones.shape), _full((1, G_VW)),
                  _full(lex.shape), _full(mex.shape)],
        out_specs=row(G_VW),
        out_shape=jax.ShapeDtypeStruct((bsz, s, G_VW), BF16),
        scratch_shapes=[pltpu.VMEM((bsz, G_VW, G_KW), F32)],
        compiler_params=_params(("arbitrary",)),
        name="gla",
    )(gq, gk, gv, gg, sgbm, lall, band, bdt, bones, gout, lex, mex)


def _mem_kernel(qm_ref, sgm_ref, mk_ref, mvx_ref, gq_ref, bones_ref, o_ref):
    qm = qm_ref[...]
    ss = _group_sumsq(qm, bones_ref[...])
    qn = qm * lax.rsqrt(ss * (1.0 / HEAD_DIM) + EPS) * gq_ref[...]
    qs = qn * (LOG2E * HEAD_DIM ** -0.5)
    lane = lax.broadcasted_iota(jnp.int32, (1, M_WIDTH), 1) // HEAD_DIM
    mk = mk_ref[...]
    s = [_dg(jnp.where(lane == h, qs, 0.0).astype(BF16), mk, _NT) for h in range(M_HEADS)]
    p = [jnp.exp2(x - jnp.max(x, axis=-1, keepdims=True)).astype(BF16) for x in s]
    acc = _dot(p[0], mvx_ref[0])
    for h in range(1, M_HEADS):
        acc = acc + _dot(p[h], mvx_ref[h])
    om = acc[:, :M_WIDTH] / acc[:, M_WIDTH:]
    o_ref[...] = (om * sgm_ref[...].astype(F32)).astype(BF16)


def _mem_call(qm, sgbm, mk, mvx, gq, bones):
    bsz, s, _ = qm.shape
    tm = ROW_TILE
    mlen = mk.shape[1]
    return pl.pallas_call(
        _mem_kernel,
        grid=(bsz, s // tm),
        in_specs=[
            pl.BlockSpec((None, tm, M_WIDTH), lambda b, t: (b, t, 0)),
            pl.BlockSpec((None, tm, M_WIDTH), lambda b, t: (b, t, 1)),
            pl.BlockSpec((None, mlen, M_WIDTH), lambda b, t: (b, 0, 0)),
            pl.BlockSpec((None, M_HEADS, mlen, 2 * M_WIDTH), lambda b, t: (b, 0, 0, 0)),
            _full((1, M_WIDTH)), _full((M_WIDTH, M_WIDTH)),
        ],
        out_specs=pl.BlockSpec((None, tm, M_WIDTH), lambda b, t: (b, t, 0)),
        out_shape=jax.ShapeDtypeStruct((bsz, s, M_WIDTH), BF16),
        compiler_params=_params(("arbitrary", "arbitrary")),
        name="memattn",
    )(qm, sgbm, mk, mvx, gq, bones)


def _out_kernel(x_ref, oaT_ref, ob_ref, om_ref, wa_ref, wb_ref, o_ref):
    y = _dot(jnp.concatenate([ob_ref[...], om_ref[...]], axis=-1), wb_ref[...])
    for t in range(oaT_ref.shape[1]):
        rows = slice(t * MOBA_BLOCK, (t + 1) * MOBA_BLOCK)
        oaT = oaT_ref[:, t].reshape(A_WIDTH, MOBA_BLOCK)
        o_ref[rows, :] = x_ref[rows, :] + y[rows] + _dg(oaT, wa_ref[...], _TN)


def _out_call(x, oaT, ob, om, wa, wb):
    bsz, s, d = x.shape
    tm = OUT_TILE
    assert s % tm == 0 and tm % MOBA_BLOCK == 0
    row = lambda w: pl.BlockSpec((None, tm, w), lambda b, t: (b, t, 0))
    return pl.pallas_call(
        _out_kernel,
        grid=(bsz, s // tm),
        in_specs=[row(d),
                  pl.BlockSpec((None, A_HEADS, tm // MOBA_BLOCK, HEAD_DIM, MOBA_BLOCK), lambda b, t: (b, 0, t, 0, 0)),
                  row(G_VW), row(M_WIDTH),
                  _full(wa.shape), _full(wb.shape)],
        out_specs=row(d),
        out_shape=jax.ShapeDtypeStruct((bsz, s, d), F32),
        compiler_params=_params(("arbitrary", "arbitrary")),
        name="outproj",
    )(x, oaT, ob, om, wa, wb)


def _alibi_consts():
    slopes = np.asarray([2.0 ** (-8.0 * (i + 1) / A_HEADS) for i in range(A_HEADS)], np.float32)
    c1 = (slopes * np.float32(LOG2E)).astype(np.float32)
    c1j = jnp.asarray(c1)
    pieces = list(_split3(c1j * 16.0)) + list(_split3(c1j))
    qx = jnp.zeros((A_HEADS, K_AUG - HEAD_DIM, MOBA_BLOCK), BF16)
    for n, pc in enumerate(pieces):
        qx = qx.at[:, n, :].set(jnp.broadcast_to(pc[:, None], (A_HEADS, MOBA_BLOCK)))
    pos = np.arange(MOBA_BLOCK)
    e = np.zeros((MOBA_BLOCK, K_AUG - HEAD_DIM), np.float32)
    e[:, 0:3] = (pos // 16)[:, None]
    e[:, 3:6] = (pos % 16)[:, None]
    return c1j, qx, jnp.asarray(e)


def kernel(x, mem, g_pre, w_in, g_q_moba, g_k_moba, w_gate_up, b_gate_up, g_gla_out,
           g_mem, w_mem_kv, g_q_mem, g_k_mem, w_out):
    depth = g_pre.shape[0]
    d = x.shape[-1]
    c1, qx, e = _alibi_consts()
    lall_np, band_np, bdt_np, lex_np, mex_np = _gla_consts()
    lall = jnp.asarray(lall_np, BF16)
    band = jnp.asarray(band_np)
    bdt = jnp.asarray(bdt_np)
    lex = jnp.asarray(lex_np, BF16)
    mex = jnp.asarray(mex_np)
    bones_a = jnp.asarray(_block_ones(A_WIDTH, HEAD_DIM), BF16)
    bones_m = jnp.asarray(_block_ones(M_WIDTH, HEAD_DIM), BF16)

    o_qa, o_ka, o_va, o_ga = 0, 512, 1024, 1536
    o_qb, o_kb, o_vb, o_gb, o_rb = 2048, 2176, 2304, 2560, 2816
    o_qm, o_gm = 2832, 3088

    mk_all, mvx_all = _memkv_call(mem, g_mem, w_mem_kv, g_k_mem)

    for l in range(depth):
        w = w_in[l]
        col = lambda o, n: w[:, o:o + n]
        rb_pad = jnp.pad(col(o_rb, GATE_RANK), ((0, 0), (0, G_KW - GATE_RANK)))
        wn = jnp.concatenate([col(o_ka, 512), col(o_gb, 256), col(o_gm, 256), col(o_qb, 128), col(o_kb, 128),
                              col(o_vb, 256), rb_pad, col(o_qm, 256)], axis=1).astype(BF16)
        wt = jnp.concatenate([col(o_qa, 512), col(o_va, 512), col(o_ga, 512)], axis=1).T.astype(BF16)
        wgu = jnp.pad(w_gate_up[l], ((0, G_KW - GATE_RANK), (0, 0)))
        gk = jnp.tile(g_k_moba[l], A_HEADS).reshape(1, A_WIDTH)

        (qT, qa, kaug, vT, kmean_nat, sga, sgbm, gq, gk2, gv, gg, qm) = _proj_call(
            x, g_pre[l].reshape(1, d), wn, wt, gk, g_q_moba[l].reshape(HEAD_DIM, 1), bones_a, e, qx,
            wgu, b_gate_up[l].reshape(1, G_KW))

        bsz, s = x.shape[0], x.shape[1]
        nb = s // MOBA_BLOCK
        kmean = kmean_nat.reshape(bsz, nb, A_HEADS, HEAD_DIM).transpose(0, 2, 1, 3)
        ub = (UB_SLACK * LOG2E * HEAD_DIM ** 0.5) * jnp.max(jnp.abs(g_q_moba[l])) * jnp.max(jnp.abs(g_k_moba[l]))
        oaT = _moba_call(c1, ub.reshape(1), qa, kaug, vT, _sel_call(qT, kmean), sga)
        ob = _gla_call(gq, gk2, gv, gg, sgbm, lall, band, bdt, bones_m,
                       jnp.tile(g_gla_out[l], G_HEADS).reshape(1, G_VW), lex, mex)
        om = _mem_call(qm, sgbm, mk_all[l], mvx_all[l],
                       jnp.tile(g_q_mem[l], M_HEADS).reshape(1, M_WIDTH), bones_m)
        wo = w_out[l].astype(BF16)
        x = _out_call(x, oaT, ob, om, wo[:A_WIDTH], wo[A_WIDTH:])
    return x
```

```python
import functools

import numpy as np
import jax
import jax.numpy as jnp
from jax import lax
from jax.experimental import pallas as pl
from jax.experimental.pallas import tpu as pltpu

F32 = jnp.float32
BF16 = jnp.bfloat16
HIGHEST = lax.Precision.HIGHEST

EPS = 1e-6
LOG2E = 1.4426950408889634
HEAD_DIM = 64
A_HEADS = 8
A_WIDTH = A_HEADS * HEAD_DIM
MOBA_BLOCK = 256
MOBA_TOPK = 3
MOBA_GROUP = 4
MOBA_UNROLL = 4
MOBA_FAST_UNROLL = 48
MOBA_OWN_GROUPS = 2
SEL_WIDTH = 4096
G_HEADS = 4
G_DK = 32
G_DV = 64
G_KW = G_HEADS * G_DK
G_VW = G_HEADS * G_DV
GATE_RANK = 16
GATE_TEMP = 16.0
GLA_CHUNK = 64
GLA_SUB = 16
GLA_UNROLL = 4
GLA_LEVELS = (32, 16, 8, 4, 2, 1)
GLA_SAFE_EXP = 80.0
M_HEADS = 4
M_WIDTH = M_HEADS * HEAD_DIM
K_AUG = 128
V_AUG = 80
ROW_TILE = 512
OUT_TILE = 1024
V7X_VMEM_BYTES = 64 * 1024 * 1024
VMEM_LIMIT = 3 * V7X_VMEM_BYTES // 4
MOBA_VMEM_LIMIT = 29 * V7X_VMEM_BYTES // 32

NEG_BIG = -1e30
POS_BIG = 1e30
DEN_MIN = 1e-18
DEN_MAX = 1e30
UB_SLACK = 1.02

_NT = (((1,), (1,)), ((), ()))
_TN = (((0,), (0,)), ((), ()))


def _dot(a, b, **kw):
    return jnp.dot(a, b, preferred_element_type=F32, **kw)


def _dg(a, b, dims, **kw):
    return lax.dot_general(a, b, dims, preferred_element_type=F32, **kw)


def _split2(v):
    hi = v.astype(BF16)
    lo = (v - hi.astype(F32)).astype(BF16)
    return hi, lo


def _split3(v):
    hi = v.astype(BF16)
    r = v - hi.astype(F32)
    mid = r.astype(BF16)
    lo = (r - mid.astype(F32)).astype(BF16)
    return hi, mid, lo


def _group_sumsq(v, bones):
    hi, lo = _split2(v * v)
    return _dot(hi, bones) + _dot(lo, bones)


def _silu(v):
    return v / (1.0 + jnp.exp(-v))


def _block_ones(n, g):
    i = np.arange(n) // g
    return (i[:, None] == i[None, :]).astype(np.float32)


def _params(sem):
    return pltpu.CompilerParams(dimension_semantics=sem, vmem_limit_bytes=VMEM_LIMIT)


def _full(shape):
    return pl.BlockSpec(shape, lambda *_: (0,) * len(shape))


def _memkv_kernel(mem_ref, gmem_ref, w_ref, gk_ref, bones_ref, mk_ref, mvx_ref):
    m = mem_ref[...]
    ms = jnp.mean(m * m, axis=-1, keepdims=True)
    hn = (m * lax.rsqrt(ms + EPS) * gmem_ref[...]).astype(BF16)
    kv = _dot(hn, w_ref[...])
    mk = kv[:, :M_WIDTH]
    mv = kv[:, M_WIDTH:]
    ss = _group_sumsq(mk, bones_ref[...])
    mk_ref[...] = (mk * lax.rsqrt(ss * (1.0 / HEAD_DIM) + EPS) * gk_ref[...]).astype(BF16)
    lane = lax.broadcasted_iota(jnp.int32, mv.shape, 1) // HEAD_DIM
    for h in range(M_HEADS):
        hm = lane == h
        mvx_ref[h] = jnp.concatenate(
            [jnp.where(hm, mv, 0.0), jnp.where(hm, 1.0, 0.0)], axis=-1).astype(BF16)


def _memkv_call(mem, g_mem, w_mem_kv, g_k_mem):
    depth = g_mem.shape[0]
    bsz, mlen, d = mem.shape
    bones = jnp.asarray(_block_ones(M_WIDTH, HEAD_DIM), BF16)
    gk = jnp.tile(g_k_mem, (1, M_HEADS)).reshape(depth, 1, M_WIDTH)
    return pl.pallas_call(
        _memkv_kernel,
        grid=(depth, bsz),
        in_specs=[
            pl.BlockSpec((None, mlen, d), lambda l, b: (b, 0, 0)),
            pl.BlockSpec((None, 1, d), lambda l, b: (l, 0, 0)),
            pl.BlockSpec((None, d, 2 * M_WIDTH), lambda l, b: (l, 0, 0)),
            pl.BlockSpec((None, 1, M_WIDTH), lambda l, b: (l, 0, 0)),
            _full((M_WIDTH, M_WIDTH)),
        ],
        out_specs=[
            pl.BlockSpec((None, None, mlen, M_WIDTH), lambda l, b: (l, b, 0, 0)),
            pl.BlockSpec((None, None, M_HEADS, mlen, 2 * M_WIDTH), lambda l, b: (l, b, 0, 0, 0)),
        ],
        out_shape=[
            jax.ShapeDtypeStruct((depth, bsz, mlen, M_WIDTH), BF16),
            jax.ShapeDtypeStruct((depth, bsz, M_HEADS, mlen, 2 * M_WIDTH), BF16),
        ],
        compiler_params=_params(("arbitrary", "arbitrary")),
        name="memkv",
    )(mem, g_mem.reshape(depth, 1, d), w_mem_kv.astype(BF16), gk, bones)


_N_K = (0, 512)
_N_GBM = (512, 1024)
_N_QB = (1024, 1152)
_N_KB = (1152, 1280)
_N_VB = (1280, 1536)
_N_RB = (1536, 1664)
_N_QM = (1664, 1920)


def _proj_kernel(x_ref, gpre_ref, wn_ref, wt_ref, gk_ref, gqcol_ref, bones_ref, e_ref, qx_ref, wgu_ref, bgu_ref,
                 qT_ref, qa_ref, kaug_ref, vT_ref, kmean_ref, sga_ref, sgbm_ref, gq_ref, gk2_ref, gv_ref, gg_ref,
                 qm_ref):
    tm = x_ref.shape[0]
    nblk = tm // MOBA_BLOCK
    x = x_ref[...]
    ms = jnp.mean(x * x, axis=-1, keepdims=True)
    h = (x * lax.rsqrt(ms + EPS) * gpre_ref[...]).astype(BF16)

    def nat(cols):
        return _dot(h, wn_ref[:, cols[0]:cols[1]])

    k = nat(_N_K)
    qT = _dg(wt_ref[0:A_WIDTH, :], h, _NT)

    ss = _group_sumsq(k, bones_ref[...])
    kn = k * lax.rsqrt(ss * (1.0 / HEAD_DIM) + EPS) * gk_ref[...]
    kmean_ref[...] = jnp.mean(kn.reshape(nblk, MOBA_BLOCK, A_WIDTH), axis=1)
    e = jnp.concatenate([e_ref[...]] * nblk, axis=0)
    for hh in range(A_HEADS):
        rows = kn[:, hh * HEAD_DIM:(hh + 1) * HEAD_DIM]
        kaug_ref[hh] = jnp.concatenate([rows, e], axis=-1).astype(BF16)

    q3 = qT.reshape(A_HEADS, HEAD_DIM, tm)
    msq = jnp.mean(q3 * q3, axis=1, keepdims=True)
    qn = q3 * lax.rsqrt(msq + EPS) * gqcol_ref[...].reshape(1, HEAD_DIM, 1)
    qT_ref[...] = qn.reshape(A_WIDTH, tm)
    qs = (qn * (LOG2E * HEAD_DIM ** -0.5)).astype(BF16)
    blocks = [slice(t * MOBA_BLOCK, (t + 1) * MOBA_BLOCK) for t in range(nblk)]
    for hh in range(A_HEADS):
        for t, cols in enumerate(blocks):
            qa_ref[hh, t] = jnp.concatenate([qs[hh][:, cols], qx_ref[hh]], axis=0)

    vT = _dg(wt_ref[A_WIDTH:2 * A_WIDTH, :], h, _NT)
    ones_rows = jnp.where(lax.broadcasted_iota(jnp.int32, (V_AUG - HEAD_DIM, tm), 0) == 0, 1.0, 0.0)
    sga = _silu(_dg(wt_ref[2 * A_WIDTH:3 * A_WIDTH, :], h, _NT)).astype(BF16)
    for hh in range(A_HEADS):
        rows = slice(hh * HEAD_DIM, (hh + 1) * HEAD_DIM)
        vT_ref[hh] = jnp.concatenate([vT[rows], ones_rows], axis=0).astype(BF16)
        for t, cols in enumerate(blocks):
            sga_ref[hh, t] = sga[rows, cols]
    sgbm_ref[...] = _silu(nat(_N_GBM)).astype(BF16)

    qkb = nat((_N_QB[0], _N_KB[1]))
    gq_ref[...] = qkb[:, :G_KW] * (G_DK ** -0.5)
    gk2_ref[...] = qkb[:, G_KW:]
    rb = nat(_N_RB)
    gv_ref[...] = nat(_N_VB).astype(BF16)
    qm_ref[...] = nat(_N_QM)
    z = _dot(rb, wgu_ref[...], precision=HIGHEST) + bgu_ref[...]
    gg_ref[...] = (jnp.minimum(z, 0.0) - jnp.log(1.0 + jnp.exp(-jnp.abs(z)))) * (1.0 / GATE_TEMP)


def _proj_call(x, g_pre, wn, wt, gk, gqcol, bones, e, qx, wgu, bgu):
    bsz, s, d = x.shape
    tm = ROW_TILE
    nt = s // tm
    nblk = tm // MOBA_BLOCK
    nb = s // MOBA_BLOCK
    row = lambda w: pl.BlockSpec((None, tm, w), lambda b, t: (b, t, 0))
    in_specs = [
        row(d), _full((1, d)), _full(wn.shape), _full(wt.shape), _full((1, A_WIDTH)),
        _full((HEAD_DIM, 1)), _full((A_WIDTH, A_WIDTH)), _full(e.shape), _full(qx.shape), _full(wgu.shape),
        _full((1, G_KW)),
    ]
    group = MOBA_GROUP * MOBA_BLOCK
    assert d == wn.shape[0] and s % group == 0 and group % tm == 0 and tm % MOBA_BLOCK == 0
    tpg = group // tm
    blocked = lambda r: pl.BlockSpec((None, A_HEADS, nblk, r, MOBA_BLOCK), lambda b, t: (b, 0, t, 0, 0))
    out_specs = [
        pl.BlockSpec((None, A_WIDTH, tm), lambda b, t: (b, 0, t)),
        blocked(K_AUG),
        pl.BlockSpec((None, A_HEADS, tm, K_AUG), lambda b, t: (b, 0, t, 0)),
        pl.BlockSpec((None, A_HEADS, None, V_AUG, tm), lambda b, t: (b, 0, t // tpg, 0, t % tpg)),
        pl.BlockSpec((None, None, nblk, A_WIDTH), lambda b, t: (b, t, 0, 0)),
        blocked(HEAD_DIM),
        row(2 * M_WIDTH), row(G_KW), row(G_KW), row(G_VW), row(G_KW), row(M_WIDTH),
    ]
    out_shape = [
        jax.ShapeDtypeStruct((bsz, A_WIDTH, s), F32),
        jax.ShapeDtypeStruct((bsz, A_HEADS, nb, K_AUG, MOBA_BLOCK), BF16),
        jax.ShapeDtypeStruct((bsz, A_HEADS, s, K_AUG), BF16),
        jax.ShapeDtypeStruct((bsz, A_HEADS, s // group, V_AUG, group), BF16),
        jax.ShapeDtypeStruct((bsz, nt, nblk, A_WIDTH), F32),
        jax.ShapeDtypeStruct((bsz, A_HEADS, nb, HEAD_DIM, MOBA_BLOCK), BF16),
        jax.ShapeDtypeStruct((bsz, s, 2 * M_WIDTH), BF16),
        jax.ShapeDtypeStruct((bsz, s, G_KW), F32),
        jax.ShapeDtypeStruct((bsz, s, G_KW), F32),
        jax.ShapeDtypeStruct((bsz, s, G_VW), BF16),
        jax.ShapeDtypeStruct((bsz, s, G_KW), F32),
        jax.ShapeDtypeStruct((bsz, s, M_WIDTH), F32),
    ]
    return pl.pallas_call(
        _proj_kernel, grid=(bsz, nt), in_specs=in_specs, out_specs=out_specs, out_shape=out_shape,
        compiler_params=_params(("arbitrary", "arbitrary")), name="proj",
    )(x, g_pre, wn, wt, gk, gqcol, bones, e, qx, wgu, bgu)


def _sel_kernel(qT_ref, kmean_ref, o_ref):
    width = qT_ref.shape[1]
    nb = kmean_ref.shape[0]
    per = width // MOBA_BLOCK

    def choose(first):
        rows = min(nb, first + per)
        gate = _dot(kmean_ref[0:rows, :], qT_ref[...], precision=HIGHEST)
        rowf = lax.broadcasted_iota(jnp.int32, gate.shape, 0).astype(F32)
        qblk = (first + lax.broadcasted_iota(jnp.int32, gate.shape, 1) // MOBA_BLOCK).astype(F32)
        g = jnp.where(rowf < qblk, gate, -jnp.inf)
        sel = jnp.zeros(gate.shape, F32)
        for _ in range(MOBA_TOPK):
            m = jnp.max(g, axis=0, keepdims=True)
            idx = jnp.min(jnp.where(g == m, rowf, float(rows)), axis=0, keepdims=True)
            hit = rowf == idx
            sel = jnp.where(hit, 1.0, sel)
            g = jnp.where(hit, -jnp.inf, g)
        sel = jnp.where(rowf < qblk, sel, 0.0)
        for u in range(per):
            o_ref[u, 0:rows, :] = sel[:, u * MOBA_BLOCK:(u + 1) * MOBA_BLOCK]
            if rows < nb:
                o_ref[u, rows:nb, :] = jnp.zeros((nb - rows, MOBA_BLOCK), F32)

    for step in range(nb // per):
        pl.when(pl.program_id(2) == step)(functools.partial(choose, step * per))


def _sel_call(qT, kmean):
    bsz, _, s = qT.shape
    nb = s // MOBA_BLOCK
    width = min(SEL_WIDTH, s)
    return pl.pallas_call(
        _sel_kernel,
        grid=(bsz, A_HEADS, s // width),
        in_specs=[
            pl.BlockSpec((None, HEAD_DIM, width), lambda b, h, c: (b, h, c)),
            pl.BlockSpec((None, None, nb, HEAD_DIM), lambda b, h, c: (b, h, 0, 0)),
        ],
        out_specs=pl.BlockSpec((None, None, width // MOBA_BLOCK, nb, MOBA_BLOCK), lambda b, h, c: (b, h, c, 0, 0)),
        out_shape=jax.ShapeDtypeStruct((bsz, A_HEADS, nb, nb, MOBA_BLOCK), F32),
        compiler_params=_params(("arbitrary", "arbitrary", "arbitrary")),
        name="mobasel",
    )(qT, kmean)


def _moba_items(nb):
    def table(rows, multiple):
        n_items = -(-len(rows) // multiple) * multiple
        rows = rows + [(rows[-1][0], 0, 0, 0)] * (n_items + 2 - len(rows))
        return np.asarray(rows, np.int32).T, n_items

    fast = [(i, g, int(g == 0), 1) for i in range(MOBA_GROUP, nb) for g in range(i // MOBA_GROUP)]
    robust = [(i, g, int(g == 0), 1) for i in range(1, nb) for g in range((i - 1) // MOBA_GROUP + 1)]
    return table(fast, MOBA_FAST_UNROLL), table(robust, MOBA_UNROLL)


def _moba_kernel(tabf_ref, tabr_ref, c1_ref, ub_ref, qa_ref, kall_ref, vall_ref, sel_ref, sga_ref, o_ref,
                 m_scr, acc_scr, out_scr, s0_ref, s1_ref, mb0_ref, mb1_ref, p0_ref, p1_ref,
                 *, n_fast, n_robust):
    hh = pl.program_id(1)
    c1 = c1_ref[hh]
    ub = ub_ref[0]
    nb = qa_ref.shape[0]
    gk = MOBA_GROUP * MOBA_BLOCK
    half = gk // 2
    per_half = MOBA_GROUP // 2
    bufs = ((s0_ref, mb0_ref, p0_ref), (s1_ref, mb1_ref, p1_ref))
    kk = lax.broadcasted_iota(jnp.int32, (MOBA_BLOCK, MOBA_BLOCK), 0)
    qq = lax.broadcasted_iota(jnp.int32, (MOBA_BLOCK, MOBA_BLOCK), 1)

    sigma = ub + c1 * lax.broadcasted_iota(jnp.int32, (1, MOBA_BLOCK), 1).astype(F32)

    def past_probs(s_blk, i, j, valid):
        picked = jnp.where(valid == 1, sel_ref[i, pl.ds(j, 1), :], 0.0) > 0.5
        cj = c1 * (MOBA_BLOCK * (j - i)).astype(F32)
        shift = jnp.where(picked, sigma - cj, POS_BIG)
        return jnp.exp2(s_blk - shift).astype(BF16)

    def own_probs(t, k, p_ref):
        t = jnp.minimum(t, nb // MOBA_GROUP - 1)
        i = MOBA_GROUP * t + k
        for part in range(k // per_half + 1):
            members = min(per_half, k + 1 - per_half * part)
            r0 = pl.multiple_of(t * gk + part * half, half)
            s = _dot(kall_ref[pl.ds(r0, members * MOBA_BLOCK), :], qa_ref[i])
            for u2 in range(members):
                u = per_half * part + u2
                blk = s[u2 * MOBA_BLOCK:(u2 + 1) * MOBA_BLOCK]
                if u == k:
                    p = jnp.exp2(jnp.where(kk <= qq, blk, NEG_BIG) - sigma).astype(BF16)
                else:
                    p = past_probs(blk, i, MOBA_GROUP * t + u, 1)
                p_ref[u * MOBA_BLOCK:(u + 1) * MOBA_BLOCK, :] = p

    def own_accumulate(t, k, p_ref):
        n = (k + 1) * MOBA_BLOCK
        acc = _dot(vall_ref[t][:, 0:n], p_ref[0:n, :])
        acc_scr[MOBA_GROUP * t + k] = acc
        out_scr[MOBA_GROUP * t + k] = acc

    own_probs(0, 0, p0_ref)

    def own_body(tt, carry):
        for step in range(MOBA_OWN_GROUPS * MOBA_GROUP):
            t, k = MOBA_OWN_GROUPS * tt + step // MOBA_GROUP, step % MOBA_GROUP
            nxt = (t, k + 1) if k + 1 < MOBA_GROUP else (t + 1, 0)
            own_probs(nxt[0], nxt[1], bufs[(step + 1) % 2][2])
            own_accumulate(t, k, bufs[step % 2][2])
        return carry

    lax.fori_loop(0, nb // (MOBA_GROUP * MOBA_OWN_GROUPS), own_body, 0)

    def fast_probs(n, p_ref):
        i, g, valid = tabf_ref[0, n], tabf_ref[1, n], tabf_ref[3, n]
        for part in range(2):
            r0 = pl.multiple_of(g * gk + part * half, half)
            s = _dot(kall_ref[pl.ds(r0, half), :], qa_ref[i])
            for u2 in range(per_half):
                u = per_half * part + u2
                p_ref[u * MOBA_BLOCK:(u + 1) * MOBA_BLOCK, :] = past_probs(
                    s[u2 * MOBA_BLOCK:(u2 + 1) * MOBA_BLOCK], i, g * MOBA_GROUP + u, valid)

    def fast_accumulate(n, p_ref, acc):
        i, g, first = tabf_ref[0, n], tabf_ref[1, n], tabf_ref[2, n]
        acc = jnp.where(first == 1, acc_scr[i], acc) + _dot(vall_ref[g], p_ref[...])
        out_scr[i] = acc
        return acc

    fast_probs(0, p0_ref)

    def fast_body(t, acc):
        for k in range(MOBA_FAST_UNROLL):
            n = MOBA_FAST_UNROLL * t + k
            fast_probs(n + 1, bufs[(k + 1) % 2][2])
            acc = fast_accumulate(n, bufs[k % 2][2], acc)
        return acc

    lax.fori_loop(0, n_fast // MOBA_FAST_UNROLL, fast_body, jnp.zeros((V_AUG, MOBA_BLOCK), F32))

    den = out_scr[:, HEAD_DIM:HEAD_DIM + 1, :]
    in_range = jnp.logical_and(jnp.min(den) > DEN_MIN, jnp.max(den) < DEN_MAX)

    def item(n):
        return tabr_ref[0, n], tabr_ref[1, n], tabr_ref[2, n], tabr_ref[3, n]

    def diag_scores(i, s_ref, mb_ref):
        i = jnp.minimum(i, nb - 1)
        r0 = pl.multiple_of(i * MOBA_BLOCK, MOBA_BLOCK)
        s = _dot(kall_ref[pl.ds(r0, MOBA_BLOCK), :], qa_ref[i])
        s = jnp.where(kk <= qq, s, NEG_BIG)
        s_ref[0:MOBA_BLOCK, :] = s
        mb_ref[0] = jnp.max(s, axis=0, keepdims=True)

    def diag_softmax(i, s_ref, mb_ref, p_ref):
        i = jnp.minimum(i, nb - 1)
        m0 = mb_ref[0]
        m_scr[i] = jnp.broadcast_to(m0, m_scr.shape[1:])
        p_ref[0:MOBA_BLOCK, :] = jnp.exp2(s_ref[0:MOBA_BLOCK, :] - m0).astype(BF16)

    def diag_accumulate(t, k, p_ref):
        vown = vall_ref[t][:, k * MOBA_BLOCK:(k + 1) * MOBA_BLOCK]
        acc_scr[MOBA_GROUP * t + k] = _dot(vown, p_ref[0:MOBA_BLOCK, :])

    def scores(n, s_ref, mb_ref):
        i, g, _, _ = item(n)
        for part in range(2):
            r0 = pl.multiple_of(g * gk + part * half, half)
            s = _dot(kall_ref[pl.ds(r0, half), :], qa_ref[i])
            s_ref[part * half:(part + 1) * half, :] = s
            for u in range(per_half):
                blk = s[u * MOBA_BLOCK:(u + 1) * MOBA_BLOCK]
                mb_ref[part * per_half + u] = jnp.max(blk, axis=0, keepdims=True)

    def softmax(n, s_ref, mb_ref, p_ref, m):
        i, g, first, valid = item(n)
        m = jnp.where(first == 1, m_scr[i][0:1], m)
        m_new = m
        members = []
        for u in range(MOBA_GROUP):
            j = g * MOBA_GROUP + u
            picked = jnp.where(valid == 1, sel_ref[i, pl.ds(j, 1), :], 0.0) > 0.5
            cj = c1 * (MOBA_BLOCK * (j - i)).astype(F32)
            m_new = jnp.where(picked, jnp.maximum(m_new, mb_ref[u] + cj), m_new)
            members.append((picked, cj))
        for u, (picked, cj) in enumerate(members):
            shift = jnp.where(picked, m_new - cj, POS_BIG)
            rows = slice(u * MOBA_BLOCK, (u + 1) * MOBA_BLOCK)
            p_ref[rows, :] = jnp.exp2(s_ref[rows, :] - shift).astype(BF16)
        return m_new, jnp.exp2(m - m_new)

    def accumulate(n, p_ref, alpha, acc):
        i, g, first, _ = item(n)
        acc = jnp.where(first == 1, acc_scr[i], acc) * alpha + _dot(vall_ref[g], p_ref[...])
        acc_scr[i] = acc
        return acc

    @pl.when(jnp.logical_not(in_range))
    def _():
        diag_scores(0, s0_ref, mb0_ref)
        diag_scores(1, s1_ref, mb1_ref)
        diag_softmax(0, s0_ref, mb0_ref, p0_ref)

        def diag_body(t, carry):
            for k in range(MOBA_GROUP):
                i = MOBA_GROUP * t + k
                s_a, mb_a, p_a = bufs[k % 2]
                s_b, mb_b, p_b = bufs[(k + 1) % 2]
                diag_scores(i + 2, s_a, mb_a)
                diag_softmax(i + 1, s_b, mb_b, p_b)
                diag_accumulate(t, k, p_a)
            return carry

        lax.fori_loop(0, nb // MOBA_GROUP, diag_body, 0)

        scores(0, s0_ref, mb0_ref)
        scores(1, s1_ref, mb1_ref)
        m_init = jnp.full((1, MOBA_BLOCK), NEG_BIG, F32)
        m, alpha = softmax(0, s0_ref, mb0_ref, p0_ref, m_init)
        acc = jnp.zeros((V_AUG, MOBA_BLOCK), F32)

        def body(t, carry):
            m, alpha, acc = carry
            for k in range(MOBA_UNROLL):
                n = MOBA_UNROLL * t + k
                s_a, mb_a, p_a = bufs[k % 2]
                s_b, mb_b, p_b = bufs[(k + 1) % 2]
                acc = accumulate(n, p_a, alpha, acc)
                m, alpha = softmax(n + 1, s_b, mb_b, p_b, m)
                scores(n + 2, s_a, mb_a)
            return m, alpha, acc

        lax.fori_loop(0, n_robust // MOBA_UNROLL, body, (m, alpha, acc))
        out_scr[...] = acc_scr[...]

    def finish(i, carry):
        acc = out_scr[i]
        o = acc[:HEAD_DIM] / acc[HEAD_DIM:HEAD_DIM + 1]
        o_ref[i] = (o * sga_ref[i].astype(F32)).astype(BF16)
        return carry

    lax.fori_loop(0, nb, finish, 0, unroll=4)


def _moba_call(c1, ub, qa, kaug, vT, sel, sga):
    bsz, _, nb = qa.shape[:3]
    s = nb * MOBA_BLOCK
    gk = MOBA_GROUP * MOBA_BLOCK
    assert nb % (MOBA_GROUP * MOBA_OWN_GROUPS) == 0 and MOBA_GROUP % 2 == 0 and MOBA_UNROLL % 2 == 0
    assert MOBA_FAST_UNROLL % MOBA_UNROLL == 0
    (tabf, n_fast), (tabr, n_robust) = _moba_items(nb)
    per_bh = lambda *tail: pl.BlockSpec((None, None) + tail, lambda b, h: (b, h) + (0,) * len(tail))
    return pl.pallas_call(
        functools.partial(_moba_kernel, n_fast=n_fast, n_robust=n_robust),
        grid=(bsz, A_HEADS),
        in_specs=[
            pl.BlockSpec(memory_space=pltpu.SMEM),
            pl.BlockSpec(memory_space=pltpu.SMEM),
            pl.BlockSpec(memory_space=pltpu.SMEM),
            pl.BlockSpec(memory_space=pltpu.SMEM),
            per_bh(nb, K_AUG, MOBA_BLOCK),
            per_bh(s, K_AUG),
            per_bh(nb // MOBA_GROUP, V_AUG, gk),
            per_bh(nb, nb, MOBA_BLOCK),
            per_bh(nb, HEAD_DIM, MOBA_BLOCK),
        ],
        out_specs=per_bh(nb, HEAD_DIM, MOBA_BLOCK),
        out_shape=jax.ShapeDtypeStruct((bsz, A_HEADS, nb, HEAD_DIM, MOBA_BLOCK), BF16),
        scratch_shapes=[pltpu.VMEM((nb, 8, MOBA_BLOCK), F32), pltpu.VMEM((nb, V_AUG, MOBA_BLOCK), F32),
                        pltpu.VMEM((nb, V_AUG, MOBA_BLOCK), F32),
                        pltpu.VMEM((gk, MOBA_BLOCK), F32), pltpu.VMEM((gk, MOBA_BLOCK), F32),
                        pltpu.VMEM((MOBA_GROUP, 1, MOBA_BLOCK), F32), pltpu.VMEM((MOBA_GROUP, 1, MOBA_BLOCK), F32),
                        pltpu.VMEM((gk, MOBA_BLOCK), BF16), pltpu.VMEM((gk, MOBA_BLOCK), BF16)],
        compiler_params=pltpu.CompilerParams(dimension_semantics=("arbitrary", "arbitrary"),
                                             vmem_limit_bytes=MOBA_VMEM_LIMIT),
        name="moba",
    )(jnp.asarray(tabf), jnp.asarray(tabr), c1, ub, qa, kaug, vT, sel, sga)


def _gla_consts():
    c, sb = GLA_CHUNK, GLA_SUB
    r = np.arange(c)
    blk = r // sb
    tri = (r[None, :] <= r[:, None])
    same = blk[None, :] == blk[:, None]
    lall = np.concatenate([
        tri,
        tri & same,
        same,
        blk[None, :] == blk[:, None] - 1,
        blk[None, :] == blk[:, None] - 2,
    ], axis=0).astype(np.float32)
    diff = blk[:, None] - blk[None, :]
    band = np.stack([diff == 1, diff == 2, diff == 3, same & tri]).astype(np.float32)
    dk_head = np.arange(G_KW) // G_DK
    dv_head = np.arange(G_VW) // G_DV
    bdt = (dv_head[:, None] == dk_head[None, :]).astype(np.float32)
    lex, mex = [tri], [np.eye(c, dtype=bool)]
    for s in GLA_LEVELS:
        same_s = (r[None, :] // s) == (r[:, None] // s)
        lex += [tri & same_s, same_s]
        pair = (r[None, :] // (2 * s)) == (r[:, None] // (2 * s))
        mex.append(pair & ((r[:, None] // s) % 2 == 1) & ((r[None, :] // s) % 2 == 0))
    lex = np.concatenate(lex, axis=0).astype(np.float32)
    mex = np.stack(mex).astype(np.float32)
    return lall, band, bdt, lex, mex


def _gla_kernel(q_ref, k_ref, v_ref, g_ref, sgb_ref, lall_ref, band_ref, bdt_ref, bones_ref, gout_ref,
                lex_ref, mex_ref, o_ref, st_ref):
    c = GLA_CHUNK
    nbatch = q_ref.shape[0]

    @pl.when(pl.program_id(0) == 0)
    def _():
        st_ref[...] = jnp.zeros_like(st_ref)

    lane_k = lax.broadcasted_iota(jnp.int32, (1, G_KW), 1) // G_DK
    lane_v = lax.broadcasted_iota(jnp.int32, (1, G_VW), 1) // G_DV
    n_chunks = q_ref.shape[1] // c

    def heads(t):
        return jnp.concatenate([jnp.where(lane_k == hd, t, 0.0) for hd in range(G_HEADS)],
                               axis=0).astype(BF16)

    def chunk(ci, states):
        r0 = pl.multiple_of(ci * c, c)
        nbs = range(nbatch)
        q = [q_ref[bi, pl.ds(r0, c), :] for bi in nbs]
        k = [k_ref[bi, pl.ds(r0, c), :] for bi in nbs]
        v = [v_ref[bi, pl.ds(r0, c), :] for bi in nbs]

        r = [_dot(lall_ref[...], jnp.concatenate(_split3(g_ref[bi, pl.ds(r0, c), :]), axis=1)) for bi in nbs]
        r = [x[:, :G_KW] + x[:, G_KW:2 * G_KW] + x[:, 2 * G_KW:] for x in r]
        b, cc, tt, p1, p2 = ([x[n * c:(n + 1) * c] for x in r] for n in range(5))
        b_last = [x[c - 1:c] for x in b]

        qt = [q[bi] * jnp.exp(cc[bi]) for bi in nbs]
        q2 = [qt[bi] * jnp.exp(p1[bi]) for bi in nbs]
        q3 = [q2[bi] * jnp.exp(p2[bi]) for bi in nbs]
        kt = [(k[bi] * jnp.exp(tt[bi] - cc[bi])).astype(BF16) for bi in nbs]
        kd = [(k[bi] * jnp.exp(-cc[bi])).astype(BF16) for bi in nbs]
        qi = [(q[bi] * jnp.exp(b[bi])).astype(BF16) for bi in nbs]
        kl = [(k[bi] * jnp.exp(b_last[bi] - b[bi])).astype(BF16) for bi in nbs]

        xs = [[_dg(heads(t[bi]), kt[bi], _NT) for bi in nbs] for t in (qt, q2, q3)]
        yd = [_dg(heads(qt[bi]), kd[bi], _NT) for bi in nbs]
        o = [_dg(qi[bi], states[bi].astype(BF16), _NT) for bi in nbs]
        st_new = tuple(states[bi] * jnp.exp(b_last[bi]) + _dg(v[bi], kl[bi], _TN) * bdt_ref[...]
                       for bi in nbs)
        for hd in range(G_HEADS):
            sl = slice(hd * c, (hd + 1) * c)
            for bi in nbs:
                att = (band_ref[0] * xs[0][bi][sl] + band_ref[1] * xs[1][bi][sl]
                       + band_ref[2] * xs[2][bi][sl] + band_ref[3] * yd[bi][sl])
                vh = jnp.where(lane_v == hd, v[bi], jnp.zeros_like(v[bi]))
                o[bi] = o[bi] + _dot(att.astype(BF16), vh)

        ss = [_group_sumsq(o[bi], bones_ref[...]) for bi in nbs]
        for bi in nbs:
            y = o[bi] * lax.rsqrt(ss[bi] * (1.0 / G_DV) + EPS) * gout_ref[...]
            o_ref[bi, pl.ds(r0, c), :] = (y * sgb_ref[bi, pl.ds(r0, c), :].astype(F32)).astype(BF16)
        return st_new

    def chunk_stable(ci, states):
        r0 = pl.multiple_of(ci * c, c)
        out = []
        for bi in range(nbatch):
            q = q_ref[bi, pl.ds(r0, c), :]
            k = k_ref[bi, pl.ds(r0, c), :]
            v = v_ref[bi, pl.ds(r0, c), :]
            r = _dot(lex_ref[...], jnp.concatenate(_split3(g_ref[bi, pl.ds(r0, c), :]), axis=1))
            r = r[:, :G_KW] + r[:, G_KW:2 * G_KW] + r[:, 2 * G_KW:]
            b = r[0:c]
            b_last = b[c - 1:c]
            x = [_dg(heads(q), k.astype(BF16), _NT)]
            for lv in range(len(GLA_LEVELS)):
                cs = r[(1 + 2 * lv) * c:(2 + 2 * lv) * c]
                ts = r[(2 + 2 * lv) * c:(3 + 2 * lv) * c]
                x.append(_dg(heads(q * jnp.exp(cs)), (k * jnp.exp(ts - cs)).astype(BF16), _NT))
            o = _dg((q * jnp.exp(b)).astype(BF16), states[bi].astype(BF16), _NT)
            for hd in range(G_HEADS):
                sl = slice(hd * c, (hd + 1) * c)
                att = mex_ref[0] * x[0][sl]
                for lv in range(len(GLA_LEVELS)):
                    att = att + mex_ref[lv + 1] * x[lv + 1][sl]
                o = o + _dot(att.astype(BF16), jnp.where(lane_v == hd, v, jnp.zeros_like(v)))
            kl = (k * jnp.exp(b_last - b)).astype(BF16)
            out.append(states[bi] * jnp.exp(b_last) + _dg(v, kl, _TN) * bdt_ref[...])
            ss = _group_sumsq(o, bones_ref[...])
            y = o * lax.rsqrt(ss * (1.0 / G_DV) + EPS) * gout_ref[...]
            o_ref[bi, pl.ds(r0, c), :] = (y * sgb_ref[bi, pl.ds(r0, c), :].astype(F32)).astype(BF16)
        return tuple(out)

    def run(body, unroll):
        states = lax.fori_loop(0, n_chunks, body, tuple(st_ref[bi] for bi in range(nbatch)), unroll=unroll)
        for bi in range(nbatch):
            st_ref[bi] = states[bi]

    risky = jnp.max(-g_ref[...]) * GLA_SUB > GLA_SAFE_EXP
    pl.when(jnp.logical_not(risky))(lambda: run(chunk, GLA_UNROLL))
    pl.when(risky)(lambda: run(chunk_stable, 1))


def _gla_call(gq, gk, gv, gg, sgbm, lall, band, bdt, bones, gout, lex, mex):
    bsz, s, _ = gq.shape
    tm = OUT_TILE
    row = lambda w: pl.BlockSpec((bsz, tm, w), lambda t: (0, t, 0))
    return pl.pallas_call(
        _gla_kernel,
        grid=(s // tm,),
        in_specs=[row(G_KW), row(G_KW), row(G_VW), row(G_KW), row(G_VW),
                  _full(lall.shape), _full(band.shape), _full(bdt.shape), _full(bones.shape), _full((1, G_VW)),
                  _full(lex.shape), _full(mex.shape)],
        out_specs=row(G_VW),
        out_shape=jax.ShapeDtypeStruct((bsz, s, G_VW), BF16),
        scratch_shapes=[pltpu.VMEM((bsz, G_VW, G_KW), F32)],
        compiler_params=_params(("arbitrary",)),
        name="gla",
    )(gq, gk, gv, gg, sgbm, lall, band, bdt, bones, gout, lex, mex)


def _mem_kernel(qm_ref, sgm_ref, mk_ref, mvx_ref, gq_ref, bones_ref, o_ref):
    qm = qm_ref[...]
    ss = _group_sumsq(qm, bones_ref[...])
    qn = qm * lax.rsqrt(ss * (1.0 / HEAD_DIM) + EPS) * gq_ref[...]
    qs = qn * (LOG2E * HEAD_DIM ** -0.5)
    lane = lax.broadcasted_iota(jnp.int32, (1, M_WIDTH), 1) // HEAD_DIM
    mk = mk_ref[...]
    s = [_dg(jnp.where(lane == h, qs, 0.0).astype(BF16), mk, _NT) for h in range(M_HEADS)]
    p = [jnp.exp2(x - jnp.max(x, axis=-1, keepdims=True)).astype(BF16) for x in s]
    acc = _dot(p[0], mvx_ref[0])
    for h in range(1, M_HEADS):
        acc = acc + _dot(p[h], mvx_ref[h])
    om = acc[:, :M_WIDTH] / acc[:, M_WIDTH:]
    o_ref[...] = (om * sgm_ref[...].astype(F32)).astype(BF16)


def _mem_call(qm, sgbm, mk, mvx, gq, bones):
    bsz, s, _ = qm.shape
    tm = OUT_TILE
    mlen = mk.shape[1]
    return pl.pallas_call(
        _mem_kernel,
        grid=(bsz, s // tm),
        in_specs=[
            pl.BlockSpec((None, tm, M_WIDTH), lambda b, t: (b, t, 0)),
            pl.BlockSpec((None, tm, M_WIDTH), lambda b, t: (b, t, 1)),
            pl.BlockSpec((None, mlen, M_WIDTH), lambda b, t: (b, 0, 0)),
            pl.BlockSpec((None, M_HEADS, mlen, 2 * M_WIDTH), lambda b, t: (b, 0, 0, 0)),
            _full((1, M_WIDTH)), _full((M_WIDTH, M_WIDTH)),
        ],
        out_specs=pl.BlockSpec((None, tm, M_WIDTH), lambda b, t: (b, t, 0)),
        out_shape=jax.ShapeDtypeStruct((bsz, s, M_WIDTH), BF16),
        compiler_params=_params(("arbitrary", "arbitrary")),
        name="memattn",
    )(qm, sgbm, mk, mvx, gq, bones)


def _out_kernel(x_ref, oaT_ref, ob_ref, om_ref, wa_ref, wb_ref, o_ref):
    y = _dot(jnp.concatenate([ob_ref[...], om_ref[...]], axis=-1), wb_ref[...])
    for t in range(oaT_ref.shape[1]):
        rows = slice(t * MOBA_BLOCK, (t + 1) * MOBA_BLOCK)
        oaT = oaT_ref[:, t].reshape(A_WIDTH, MOBA_BLOCK)
        o_ref[rows, :] = x_ref[rows, :] + y[rows] + _dg(oaT, wa_ref[...], _TN)


def _out_call(x, oaT, ob, om, wa, wb):
    bsz, s, d = x.shape
    tm = OUT_TILE
    assert s % tm == 0 and tm % MOBA_BLOCK == 0
    row = lambda w: pl.BlockSpec((None, tm, w), lambda b, t: (b, t, 0))
    return pl.pallas_call(
        _out_kernel,
        grid=(bsz, s // tm),
        in_specs=[row(d),
                  pl.BlockSpec((None, A_HEADS, tm // MOBA_BLOCK, HEAD_DIM, MOBA_BLOCK), lambda b, t: (b, 0, t, 0, 0)),
                  row(G_VW), row(M_WIDTH),
                  _full(wa.shape), _full(wb.shape)],
        out_specs=row(d),
        out_shape=jax.ShapeDtypeStruct((bsz, s, d), F32),
        compiler_params=_params(("arbitrary", "arbitrary")),
        name="outproj",
    )(x, oaT, ob, om, wa, wb)


def _alibi_consts():
    slopes = np.asarray([2.0 ** (-8.0 * (i + 1) / A_HEADS) for i in range(A_HEADS)], np.float32)
    c1 = (slopes * np.float32(LOG2E)).astype(np.float32)
    c1j = jnp.asarray(c1)
    pieces = list(_split3(c1j * 16.0)) + list(_split3(c1j))
    qx = jnp.zeros((A_HEADS, K_AUG - HEAD_DIM, MOBA_BLOCK), BF16)
    for n, pc in enumerate(pieces):
        qx = qx.at[:, n, :].set(jnp.broadcast_to(pc[:, None], (A_HEADS, MOBA_BLOCK)))
    pos = np.arange(MOBA_BLOCK)
    e = np.zeros((MOBA_BLOCK, K_AUG - HEAD_DIM), np.float32)
    e[:, 0:3] = (pos // 16)[:, None]
    e[:, 3:6] = (pos % 16)[:, None]
    return c1j, qx, jnp.asarray(e)


def kernel(x, mem, g_pre, w_in, g_q_moba, g_k_moba, w_gate_up, b_gate_up, g_gla_out,
           g_mem, w_mem_kv, g_q_mem, g_k_mem, w_out):
    depth = g_pre.shape[0]
    d = x.shape[-1]
    c1, qx, e = _alibi_consts()
    lall_np, band_np, bdt_np, lex_np, mex_np = _gla_consts()
    lall = jnp.asarray(lall_np, BF16)
    band = jnp.asarray(band_np)
    bdt = jnp.asarray(bdt_np)
    lex = jnp.asarray(lex_np, BF16)
    mex = jnp.asarray(mex_np)
    bones_a = jnp.asarray(_block_ones(A_WIDTH, HEAD_DIM), BF16)
    bones_m = jnp.asarray(_block_ones(M_WIDTH, HEAD_DIM), BF16)

    o_qa, o_ka, o_va, o_ga = 0, 512, 1024, 1536
    o_qb, o_kb, o_vb, o_gb, o_rb = 2048, 2176, 2304, 2560, 2816
    o_qm, o_gm = 2832, 3088

    mk_all, mvx_all = _memkv_call(mem, g_mem, w_mem_kv, g_k_mem)

    for l in range(depth):
        w = w_in[l]
        col = lambda o, n: w[:, o:o + n]
        rb_pad = jnp.pad(col(o_rb, GATE_RANK), ((0, 0), (0, G_KW - GATE_RANK)))
        wn = jnp.concatenate([col(o_ka, 512), col(o_gb, 256), col(o_gm, 256), col(o_qb, 128), col(o_kb, 128),
                              col(o_vb, 256), rb_pad, col(o_qm, 256)], axis=1).astype(BF16)
        wt = jnp.concatenate([col(o_qa, 512), col(o_va, 512), col(o_ga, 512)], axis=1).T.astype(BF16)
        wgu = jnp.pad(w_gate_up[l], ((0, G_KW - GATE_RANK), (0, 0)))
        gk = jnp.tile(g_k_moba[l], A_HEADS).reshape(1, A_WIDTH)

        (qT, qa, kaug, vT, kmean_nat, sga, sgbm, gq, gk2, gv, gg, qm) = _proj_call(
            x, g_pre[l].reshape(1, d), wn, wt, gk, g_q_moba[l].reshape(HEAD_DIM, 1), bones_a, e, qx,
            wgu, b_gate_up[l].reshape(1, G_KW))

        bsz, s = x.shape[0], x.shape[1]
        nb = s // MOBA_BLOCK
        kmean = kmean_nat.reshape(bsz, nb, A_HEADS, HEAD_DIM).transpose(0, 2, 1, 3)
        ub = (UB_SLACK * LOG2E * HEAD_DIM ** 0.5) * jnp.max(jnp.abs(g_q_moba[l])) * jnp.max(jnp.abs(g_k_moba[l]))
        oaT = _moba_call(c1, ub.reshape(1), qa, kaug, vT, _sel_call(qT, kmean), sga)
        ob = _gla_call(gq, gk2, gv, gg, sgbm, lall, band, bdt, bones_m,
                       jnp.tile(g_gla_out[l], G_HEADS).reshape(1, G_VW), lex, mex)
        om = _mem_call(qm, sgbm, mk_all[l], mvx_all[l],
                       jnp.tile(g_q_mem[l], M_HEADS).reshape(1, M_WIDTH), bones_m)
        wo = w_out[l].astype(BF16)
        x = _out_call(x, oaT, ob, om, wo[:A_WIDTH], wo[A_WIDTH:])
    return x
```

```python
import functools

import numpy as np
import jax
import jax.numpy as jnp
from jax import lax
from jax.experimental import pallas as pl
from jax.experimental.pallas import tpu as pltpu

F32 = jnp.float32
BF16 = jnp.bfloat16
HIGHEST = lax.Precision.HIGHEST

EPS = 1e-6
LOG2E = 1.4426950408889634
HEAD_DIM = 64
A_HEADS = 8
A_WIDTH = A_HEADS * HEAD_DIM
MOBA_BLOCK = 256
MOBA_TOPK = 3
MOBA_GROUP = 4
MOBA_UNROLL = 4
MOBA_FAST_UNROLL = 48
MOBA_OWN_GROUPS = 2
SEL_WIDTH = 4096
G_HEADS = 4
G_DK = 32
G_DV = 64
G_KW = G_HEADS * G_DK
G_VW = G_HEADS * G_DV
GATE_RANK = 16
GATE_TEMP = 16.0
GLA_CHUNK = 64
GLA_SUB = 16
GLA_UNROLL = 4
GLA_LEVELS = (32, 16, 8, 4, 2, 1)
GLA_SAFE_EXP = 80.0
M_HEADS = 4
M_WIDTH = M_HEADS * HEAD_DIM
K_AUG = 128
V_AUG = 80
ROW_TILE = 512
OUT_TILE = 1024
V7X_VMEM_BYTES = 64 * 1024 * 1024
VMEM_LIMIT = 3 * V7X_VMEM_BYTES // 4
MOBA_VMEM_LIMIT = 29 * V7X_VMEM_BYTES // 32

NEG_BIG = -1e30
POS_BIG = 1e30
DEN_MIN = 1e-18
DEN_MAX = 1e30
UB_SLACK = 1.02

_NT = (((1,), (1,)), ((), ()))
_TN = (((0,), (0,)), ((), ()))


def _dot(a, b, **kw):
    return jnp.dot(a, b, preferred_element_type=F32, **kw)


def _dg(a, b, dims, **kw):
    return lax.dot_general(a, b, dims, preferred_element_type=F32, **kw)


def _split2(v):
    hi = v.astype(BF16)
    lo = (v - hi.astype(F32)).astype(BF16)
    return hi, lo


def _split3(v):
    hi = v.astype(BF16)
    r = v - hi.astype(F32)
    mid = r.astype(BF16)
    lo = (r - mid.astype(F32)).astype(BF16)
    return hi, mid, lo


def _group_sumsq(v, bones):
    hi, lo = _split2(v * v)
    return _dot(hi, bones) + _dot(lo, bones)


def _silu(v):
    return v / (1.0 + jnp.exp(-v))


def _block_ones(n, g):
    i = np.arange(n) // g
    return (i[:, None] == i[None, :]).astype(np.float32)


def _params(sem):
    return pltpu.CompilerParams(dimension_semantics=sem, vmem_limit_bytes=VMEM_LIMIT)


def _full(shape):
    return pl.BlockSpec(shape, lambda *_: (0,) * len(shape))


def _memkv_kernel(mem_ref, gmem_ref, w_ref, gk_ref, bones_ref, mk_ref, mvx_ref):
    m = mem_ref[...]
    ms = jnp.mean(m * m, axis=-1, keepdims=True)
    hn = (m * lax.rsqrt(ms + EPS) * gmem_ref[...]).astype(BF16)
    kv = _dot(hn, w_ref[...])
    mk = kv[:, :M_WIDTH]
    mv = kv[:, M_WIDTH:]
    ss = _group_sumsq(mk, bones_ref[...])
    mk_ref[...] = (mk * lax.rsqrt(ss * (1.0 / HEAD_DIM) + EPS) * gk_ref[...]).astype(BF16)
    lane = lax.broadcasted_iota(jnp.int32, mv.shape, 1) // HEAD_DIM
    for h in range(M_HEADS):
        hm = lane == h
        mvx_ref[h] = jnp.concatenate(
            [jnp.where(hm, mv, 0.0), jnp.where(hm, 1.0, 0.0)], axis=-1).astype(BF16)


def _memkv_call(mem, g_mem, w_mem_kv, g_k_mem):
    depth = g_mem.shape[0]
    bsz, mlen, d = mem.shape
    bones = jnp.asarray(_block_ones(M_WIDTH, HEAD_DIM), BF16)
    gk = jnp.tile(g_k_mem, (1, M_HEADS)).reshape(depth, 1, M_WIDTH)
    return pl.pallas_call(
        _memkv_kernel,
        grid=(depth, bsz),
        in_specs=[
            pl.BlockSpec((None, mlen, d), lambda l, b: (b, 0, 0)),
            pl.BlockSpec((None, 1, d), lambda l, b: (l, 0, 0)),
            pl.BlockSpec((None, d, 2 * M_WIDTH), lambda l, b: (l, 0, 0)),
            pl.BlockSpec((None, 1, M_WIDTH), lambda l, b: (l, 0, 0)),
            _full((M_WIDTH, M_WIDTH)),
        ],
        out_specs=[
            pl.BlockSpec((None, None, mlen, M_WIDTH), lambda l, b: (l, b, 0, 0)),
            pl.BlockSpec((None, None, M_HEADS, mlen, 2 * M_WIDTH), lambda l, b: (l, b, 0, 0, 0)),
        ],
        out_shape=[
            jax.ShapeDtypeStruct((depth, bsz, mlen, M_WIDTH), BF16),
            jax.ShapeDtypeStruct((depth, bsz, M_HEADS, mlen, 2 * M_WIDTH), BF16),
        ],
        compiler_params=_params(("arbitrary", "arbitrary")),
        name="memkv",
    )(mem, g_mem.reshape(depth, 1, d), w_mem_kv.astype(BF16), gk, bones)


_N_K = (0, 512)
_N_GBM = (512, 1024)
_N_QB = (1024, 1152)
_N_KB = (1152, 1280)
_N_VB = (1280, 1536)
_N_RB = (1536, 1664)
_N_QM = (1664, 1920)


def _proj_kernel(x_ref, gpre_ref, wn_ref, wt_ref, gk_ref, gqcol_ref, bones_ref, e_ref, qx_ref, wgu_ref, bgu_ref,
                 qT_ref, qa_ref, kaug_ref, vT_ref, kmean_ref, sga_ref, sgbm_ref, gq_ref, gk2_ref, gv_ref, gg_ref,
                 qm_ref):
    tm = x_ref.shape[0]
    nblk = tm // MOBA_BLOCK
    x = x_ref[...]
    ms = jnp.mean(x * x, axis=-1, keepdims=True)
    h = (x * lax.rsqrt(ms + EPS) * gpre_ref[...]).astype(BF16)

    def nat(cols):
        return _dot(h, wn_ref[:, cols[0]:cols[1]])

    k = nat(_N_K)
    qT = _dg(wt_ref[0:A_WIDTH, :], h, _NT)

    ss = _group_sumsq(k, bones_ref[...])
    kn = k * lax.rsqrt(ss * (1.0 / HEAD_DIM) + EPS) * gk_ref[...]
    kmean_ref[...] = jnp.mean(kn.reshape(nblk, MOBA_BLOCK, A_WIDTH), axis=1)
    e = jnp.concatenate([e_ref[...]] * nblk, axis=0)
    for hh in range(A_HEADS):
        rows = kn[:, hh * HEAD_DIM:(hh + 1) * HEAD_DIM]
        kaug_ref[hh] = jnp.concatenate([rows, e], axis=-1).astype(BF16)

    q3 = qT.reshape(A_HEADS, HEAD_DIM, tm)
    msq = jnp.mean(q3 * q3, axis=1, keepdims=True)
    qn = q3 * lax.rsqrt(msq + EPS) * gqcol_ref[...].reshape(1, HEAD_DIM, 1)
    qT_ref[...] = qn.reshape(A_WIDTH, tm)
    qs = (qn * (LOG2E * HEAD_DIM ** -0.5)).astype(BF16)
    blocks = [slice(t * MOBA_BLOCK, (t + 1) * MOBA_BLOCK) for t in range(nblk)]
    for hh in range(A_HEADS):
        for t, cols in enumerate(blocks):
            qa_ref[hh, t] = jnp.concatenate([qs[hh][:, cols], qx_ref[hh]], axis=0)

    vT = _dg(wt_ref[A_WIDTH:2 * A_WIDTH, :], h, _NT)
    ones_rows = jnp.where(lax.broadcasted_iota(jnp.int32, (V_AUG - HEAD_DIM, tm), 0) == 0, 1.0, 0.0)
    sga = _silu(_dg(wt_ref[2 * A_WIDTH:3 * A_WIDTH, :], h, _NT)).astype(BF16)
    for hh in range(A_HEADS):
        rows = slice(hh * HEAD_DIM, (hh + 1) * HEAD_DIM)
        vT_ref[hh] = jnp.concatenate([vT[rows], ones_rows], axis=0).astype(BF16)
        for t, cols in enumerate(blocks):
            sga_ref[hh, t] = sga[rows, cols]
    sgbm_ref[...] = _silu(nat(_N_GBM)).astype(BF16)

    qkb = nat((_N_QB[0], _N_KB[1]))
    gq_ref[...] = qkb[:, :G_KW] * (G_DK ** -0.5)
    gk2_ref[...] = qkb[:, G_KW:]
    rb = nat(_N_RB)
    gv_ref[...] = nat(_N_VB).astype(BF16)
    qm_ref[...] = nat(_N_QM)
    rb_hi, rb_lo = _split2(rb)
    w_hi, w_lo = _split2(wgu_ref[...])
    zz = _dot(rb_hi, jnp.concatenate([w_hi, w_lo], axis=1))
    z = zz[:, :G_KW] + zz[:, G_KW:] + _dot(rb_lo, w_hi) + bgu_ref[...]
    gg_ref[...] = (jnp.minimum(z, 0.0) - jnp.log(1.0 + jnp.exp(-jnp.abs(z)))) * (1.0 / GATE_TEMP)


def _proj_call(x, g_pre, wn, wt, gk, gqcol, bones, e, qx, wgu, bgu):
    bsz, s, d = x.shape
    tm = ROW_TILE
    nt = s // tm
    nblk = tm // MOBA_BLOCK
    nb = s // MOBA_BLOCK
    row = lambda w: pl.BlockSpec((None, tm, w), lambda b, t: (b, t, 0))
    in_specs = [
        row(d), _full((1, d)), _full(wn.shape), _full(wt.shape), _full((1, A_WIDTH)),
        _full((HEAD_DIM, 1)), _full((A_WIDTH, A_WIDTH)), _full(e.shape), _full(qx.shape), _full(wgu.shape),
        _full((1, G_KW)),
    ]
    group = MOBA_GROUP * MOBA_BLOCK
    assert d == wn.shape[0] and s % group == 0 and group % tm == 0 and tm % MOBA_BLOCK == 0
    tpg = group // tm
    blocked = lambda r: pl.BlockSpec((None, A_HEADS, nblk, r, MOBA_BLOCK), lambda b, t: (b, 0, t, 0, 0))
    out_specs = [
        pl.BlockSpec((None, A_WIDTH, tm), lambda b, t: (b, 0, t)),
        blocked(K_AUG),
        pl.BlockSpec((None, A_HEADS, tm, K_AUG), lambda b, t: (b, 0, t, 0)),
        pl.BlockSpec((None, A_HEADS, None, V_AUG, tm), lambda b, t: (b, 0, t // tpg, 0, t % tpg)),
        pl.BlockSpec((None, None, nblk, A_WIDTH), lambda b, t: (b, t, 0, 0)),
        blocked(HEAD_DIM),
        row(2 * M_WIDTH), row(G_KW), row(G_KW), row(G_VW), row(G_KW), row(M_WIDTH),
    ]
    out_shape = [
        jax.ShapeDtypeStruct((bsz, A_WIDTH, s), F32),
        jax.ShapeDtypeStruct((bsz, A_HEADS, nb, K_AUG, MOBA_BLOCK), BF16),
        jax.ShapeDtypeStruct((bsz, A_HEADS, s, K_AUG), BF16),
        jax.ShapeDtypeStruct((bsz, A_HEADS, s // group, V_AUG, group), BF16),
        jax.ShapeDtypeStruct((bsz, nt, nblk, A_WIDTH), F32),
        jax.ShapeDtypeStruct((bsz, A_HEADS, nb, HEAD_DIM, MOBA_BLOCK), BF16),
        jax.ShapeDtypeStruct((bsz, s, 2 * M_WIDTH), BF16),
        jax.ShapeDtypeStruct((bsz, s, G_KW), F32),
        jax.ShapeDtypeStruct((bsz, s, G_KW), F32),
        jax.ShapeDtypeStruct((bsz, s, G_VW), BF16),
        jax.ShapeDtypeStruct((bsz, s, G_KW), F32),
        jax.ShapeDtypeStruct((bsz, s, M_WIDTH), F32),
    ]
    return pl.pallas_call(
        _proj_kernel, grid=(bsz, nt), in_specs=in_specs, out_specs=out_specs, out_shape=out_shape,
        compiler_params=_params(("arbitrary", "arbitrary")), name="proj",
    )(x, g_pre, wn, wt, gk, gqcol, bones, e, qx, wgu, bgu)


def _sel_kernel(qT_ref, kmean_ref, o_ref):
    width = qT_ref.shape[1]
    nb = kmean_ref.shape[0]
    per = width // MOBA_BLOCK

    def choose(first):
        rows = min(nb, first + per)
        gate = _dot(kmean_ref[0:rows, :], qT_ref[...], precision=HIGHEST)
        rowf = lax.broadcasted_iota(jnp.int32, gate.shape, 0).astype(F32)
        qblk = (first + lax.broadcasted_iota(jnp.int32, gate.shape, 1) // MOBA_BLOCK).astype(F32)
        g = jnp.where(rowf < qblk, gate, -jnp.inf)
        sel = jnp.zeros(gate.shape, F32)
        for _ in range(MOBA_TOPK):
            m = jnp.max(g, axis=0, keepdims=True)
            idx = jnp.min(jnp.where(g == m, rowf, float(rows)), axis=0, keepdims=True)
            hit = rowf == idx
            sel = jnp.where(hit, 1.0, sel)
            g = jnp.where(hit, -jnp.inf, g)
        sel = jnp.where(rowf < qblk, sel, 0.0)
        for u in range(per):
            o_ref[u, 0:rows, :] = sel[:, u * MOBA_BLOCK:(u + 1) * MOBA_BLOCK]
            if rows < nb:
                o_ref[u, rows:nb, :] = jnp.zeros((nb - rows, MOBA_BLOCK), F32)

    for step in range(nb // per):
        pl.when(pl.program_id(2) == step)(functools.partial(choose, step * per))


def _sel_call(qT, kmean):
    bsz, _, s = qT.shape
    nb = s // MOBA_BLOCK
    width = min(SEL_WIDTH, s)
    return pl.pallas_call(
        _sel_kernel,
        grid=(bsz, A_HEADS, s // width),
        in_specs=[
            pl.BlockSpec((None, HEAD_DIM, width), lambda b, h, c: (b, h, c)),
            pl.BlockSpec((None, None, nb, HEAD_DIM), lambda b, h, c: (b, h, 0, 0)),
        ],
        out_specs=pl.BlockSpec((None, None, width // MOBA_BLOCK, nb, MOBA_BLOCK), lambda b, h, c: (b, h, c, 0, 0)),
        out_shape=jax.ShapeDtypeStruct((bsz, A_HEADS, nb, nb, MOBA_BLOCK), F32),
        compiler_params=_params(("arbitrary", "arbitrary", "arbitrary")),
        name="mobasel",
    )(qT, kmean)


def _moba_items(nb):
    def table(rows, multiple):
        n_items = -(-len(rows) // multiple) * multiple
        rows = rows + [(rows[-1][0], 0, 0, 0)] * (n_items + 2 - len(rows))
        return np.asarray(rows, np.int32).T, n_items

    fast = [(i, g, int(g == 0), 1) for i in range(MOBA_GROUP, nb) for g in range(i // MOBA_GROUP)]
    robust = [(i, g, int(g == 0), 1) for i in range(1, nb) for g in range((i - 1) // MOBA_GROUP + 1)]
    return table(fast, MOBA_FAST_UNROLL), table(robust, MOBA_UNROLL)


def _moba_kernel(tabf_ref, tabr_ref, c1_ref, ub_ref, qa_ref, kall_ref, vall_ref, sel_ref, sga_ref, o_ref,
                 m_scr, acc_scr, out_scr, s0_ref, s1_ref, mb0_ref, mb1_ref, p0_ref, p1_ref,
                 *, n_fast, n_robust):
    hh = pl.program_id(1)
    c1 = c1_ref[hh]
    ub = ub_ref[0]
    nb = qa_ref.shape[0]
    gk = MOBA_GROUP * MOBA_BLOCK
    half = gk // 2
    per_half = MOBA_GROUP // 2
    bufs = ((s0_ref, mb0_ref, p0_ref), (s1_ref, mb1_ref, p1_ref))
    kk = lax.broadcasted_iota(jnp.int32, (MOBA_BLOCK, MOBA_BLOCK), 0)
    qq = lax.broadcasted_iota(jnp.int32, (MOBA_BLOCK, MOBA_BLOCK), 1)

    sigma = ub + c1 * lax.broadcasted_iota(jnp.int32, (1, MOBA_BLOCK), 1).astype(F32)

    def past_probs(s_blk, i, j, valid):
        picked = jnp.where(valid == 1, sel_ref[i, pl.ds(j, 1), :], 0.0) > 0.5
        cj = c1 * (MOBA_BLOCK * (j - i)).astype(F32)
        shift = jnp.where(picked, sigma - cj, POS_BIG)
        return jnp.exp2(s_blk - shift).astype(BF16)

    def own_probs(t, k, p_ref):
        t = jnp.minimum(t, nb // MOBA_GROUP - 1)
        i = MOBA_GROUP * t + k
        for part in range(k // per_half + 1):
            members = min(per_half, k + 1 - per_half * part)
            r0 = pl.multiple_of(t * gk + part * half, half)
            s = _dot(kall_ref[pl.ds(r0, members * MOBA_BLOCK), :], qa_ref[i])
            for u2 in range(members):
                u = per_half * part + u2
                blk = s[u2 * MOBA_BLOCK:(u2 + 1) * MOBA_BLOCK]
                if u == k:
                    p = jnp.exp2(jnp.where(kk <= qq, blk, NEG_BIG) - sigma).astype(BF16)
                else:
                    p = past_probs(blk, i, MOBA_GROUP * t + u, 1)
                p_ref[u * MOBA_BLOCK:(u + 1) * MOBA_BLOCK, :] = p

    def own_accumulate(t, k, p_ref):
        n = (k + 1) * MOBA_BLOCK
        acc = _dot(vall_ref[t][:, 0:n], p_ref[0:n, :])
        acc_scr[MOBA_GROUP * t + k] = acc
        out_scr[MOBA_GROUP * t + k] = acc

    own_probs(0, 0, p0_ref)

    def own_body(tt, carry):
        for step in range(MOBA_OWN_GROUPS * MOBA_GROUP):
            t, k = MOBA_OWN_GROUPS * tt + step // MOBA_GROUP, step % MOBA_GROUP
            nxt = (t, k + 1) if k + 1 < MOBA_GROUP else (t + 1, 0)
            own_probs(nxt[0], nxt[1], bufs[(step + 1) % 2][2])
            own_accumulate(t, k, bufs[step % 2][2])
        return carry

    lax.fori_loop(0, nb // (MOBA_GROUP * MOBA_OWN_GROUPS), own_body, 0)

    def fast_probs(n, p_ref):
        i, g, valid = tabf_ref[0, n], tabf_ref[1, n], tabf_ref[3, n]
        for part in range(2):
            r0 = pl.multiple_of(g * gk + part * half, half)
            s = _dot(kall_ref[pl.ds(r0, half), :], qa_ref[i])
            for u2 in range(per_half):
                u = per_half * part + u2
                p_ref[u * MOBA_BLOCK:(u + 1) * MOBA_BLOCK, :] = past_probs(
                    s[u2 * MOBA_BLOCK:(u2 + 1) * MOBA_BLOCK], i, g * MOBA_GROUP + u, valid)

    def fast_accumulate(n, p_ref, acc):
        i, g, first = tabf_ref[0, n], tabf_ref[1, n], tabf_ref[2, n]
        acc = jnp.where(first == 1, acc_scr[i], acc) + _dot(vall_ref[g], p_ref[...])
        out_scr[i] = acc
        return acc

    fast_probs(0, p0_ref)

    def fast_body(t, acc):
        for k in range(MOBA_FAST_UNROLL):
            n = MOBA_FAST_UNROLL * t + k
            fast_probs(n + 1, bufs[(k + 1) % 2][2])
            acc = fast_accumulate(n, bufs[k % 2][2], acc)
        return acc

    lax.fori_loop(0, n_fast // MOBA_FAST_UNROLL, fast_body, jnp.zeros((V_AUG, MOBA_BLOCK), F32))

    den = out_scr[:, HEAD_DIM:HEAD_DIM + 1, :]
    in_range = jnp.logical_and(jnp.min(den) > DEN_MIN, jnp.max(den) < DEN_MAX)

    def item(n):
        return tabr_ref[0, n], tabr_ref[1, n], tabr_ref[2, n], tabr_ref[3, n]

    def diag_scores(i, s_ref, mb_ref):
        i = jnp.minimum(i, nb - 1)
        r0 = pl.multiple_of(i * MOBA_BLOCK, MOBA_BLOCK)
        s = _dot(kall_ref[pl.ds(r0, MOBA_BLOCK), :], qa_ref[i])
        s = jnp.where(kk <= qq, s, NEG_BIG)
        s_ref[0:MOBA_BLOCK, :] = s
        mb_ref[0] = jnp.max(s, axis=0, keepdims=True)

    def diag_softmax(i, s_ref, mb_ref, p_ref):
        i = jnp.minimum(i, nb - 1)
        m0 = mb_ref[0]
        m_scr[i] = jnp.broadcast_to(m0, m_scr.shape[1:])
        p_ref[0:MOBA_BLOCK, :] = jnp.exp2(s_ref[0:MOBA_BLOCK, :] - m0).astype(BF16)

    def diag_accumulate(t, k, p_ref):
        vown = vall_ref[t][:, k * MOBA_BLOCK:(k + 1) * MOBA_BLOCK]
        acc_scr[MOBA_GROUP * t + k] = _dot(vown, p_ref[0:MOBA_BLOCK, :])

    def scores(n, s_ref, mb_ref):
        i, g, _, _ = item(n)
        for part in range(2):
            r0 = pl.multiple_of(g * gk + part * half, half)
            s = _dot(kall_ref[pl.ds(r0, half), :], qa_ref[i])
            s_ref[part * half:(part + 1) * half, :] = s
            for u in range(per_half):
                blk = s[u * MOBA_BLOCK:(u + 1) * MOBA_BLOCK]
                mb_ref[part * per_half + u] = jnp.max(blk, axis=0, keepdims=True)

    def softmax(n, s_ref, mb_ref, p_ref, m):
        i, g, first, valid = item(n)
        m = jnp.where(first == 1, m_scr[i][0:1], m)
        m_new = m
        members = []
        for u in range(MOBA_GROUP):
            j = g * MOBA_GROUP + u
            picked = jnp.where(valid == 1, sel_ref[i, pl.ds(j, 1), :], 0.0) > 0.5
            cj = c1 * (MOBA_BLOCK * (j - i)).astype(F32)
            m_new = jnp.where(picked, jnp.maximum(m_new, mb_ref[u] + cj), m_new)
            members.append((picked, cj))
        for u, (picked, cj) in enumerate(members):
            shift = jnp.where(picked, m_new - cj, POS_BIG)
            rows = slice(u * MOBA_BLOCK, (u + 1) * MOBA_BLOCK)
            p_ref[rows, :] = jnp.exp2(s_ref[rows, :] - shift).astype(BF16)
        return m_new, jnp.exp2(m - m_new)

    def accumulate(n, p_ref, alpha, acc):
        i, g, first, _ = item(n)
        acc = jnp.where(first == 1, acc_scr[i], acc) * alpha + _dot(vall_ref[g], p_ref[...])
        acc_scr[i] = acc
        return acc

    @pl.when(jnp.logical_not(in_range))
    def _():
        diag_scores(0, s0_ref, mb0_ref)
        diag_scores(1, s1_ref, mb1_ref)
        diag_softmax(0, s0_ref, mb0_ref, p0_ref)

        def diag_body(t, carry):
            for k in range(MOBA_GROUP):
                i = MOBA_GROUP * t + k
                s_a, mb_a, p_a = bufs[k % 2]
                s_b, mb_b, p_b = bufs[(k + 1) % 2]
                diag_scores(i + 2, s_a, mb_a)
                diag_softmax(i + 1, s_b, mb_b, p_b)
                diag_accumulate(t, k, p_a)
            return carry

        lax.fori_loop(0, nb // MOBA_GROUP, diag_body, 0)

        scores(0, s0_ref, mb0_ref)
        scores(1, s1_ref, mb1_ref)
        m_init = jnp.full((1, MOBA_BLOCK), NEG_BIG, F32)
        m, alpha = softmax(0, s0_ref, mb0_ref, p0_ref, m_init)
        acc = jnp.zeros((V_AUG, MOBA_BLOCK), F32)

        def body(t, carry):
            m, alpha, acc = carry
            for k in range(MOBA_UNROLL):
                n = MOBA_UNROLL * t + k
                s_a, mb_a, p_a = bufs[k % 2]
                s_b, mb_b, p_b = bufs[(k + 1) % 2]
                acc = accumulate(n, p_a, alpha, acc)
                m, alpha = softmax(n + 1, s_b, mb_b, p_b, m)
                scores(n + 2, s_a, mb_a)
            return m, alpha, acc

        lax.fori_loop(0, n_robust // MOBA_UNROLL, body, (m, alpha, acc))
        out_scr[...] = acc_scr[...]

    def finish(i, carry):
        acc = out_scr[i]
        o = acc[:HEAD_DIM] / acc[HEAD_DIM:HEAD_DIM + 1]
        o_ref[i] = (o * sga_ref[i].astype(F32)).astype(BF16)
        return carry

    lax.fori_loop(0, nb, finish, 0, unroll=4)


def _moba_call(c1, ub, qa, kaug, vT, sel, sga):
    bsz, _, nb = qa.shape[:3]
    s = nb * MOBA_BLOCK
    gk = MOBA_GROUP * MOBA_BLOCK
    assert nb % (MOBA_GROUP * MOBA_OWN_GROUPS) == 0 and MOBA_GROUP % 2 == 0 and MOBA_UNROLL % 2 == 0
    assert MOBA_FAST_UNROLL % MOBA_UNROLL == 0
    (tabf, n_fast), (tabr, n_robust) = _moba_items(nb)
    per_bh = lambda *tail: pl.BlockSpec((None, None) + tail, lambda b, h: (b, h) + (0,) * len(tail))
    return pl.pallas_call(
        functools.partial(_moba_kernel, n_fast=n_fast, n_robust=n_robust),
        grid=(bsz, A_HEADS),
        in_specs=[
            pl.BlockSpec(memory_space=pltpu.SMEM),
            pl.BlockSpec(memory_space=pltpu.SMEM),
            pl.BlockSpec(memory_space=pltpu.SMEM),
            pl.BlockSpec(memory_space=pltpu.SMEM),
            per_bh(nb, K_AUG, MOBA_BLOCK),
            per_bh(s, K_AUG),
            per_bh(nb // MOBA_GROUP, V_AUG, gk),
            per_bh(nb, nb, MOBA_BLOCK),
            per_bh(nb, HEAD_DIM, MOBA_BLOCK),
        ],
        out_specs=per_bh(nb, HEAD_DIM, MOBA_BLOCK),
        out_shape=jax.ShapeDtypeStruct((bsz, A_HEADS, nb, HEAD_DIM, MOBA_BLOCK), BF16),
        scratch_shapes=[pltpu.VMEM((nb, 8, MOBA_BLOCK), F32), pltpu.VMEM((nb, V_AUG, MOBA_BLOCK), F32),
                        pltpu.VMEM((nb, V_AUG, MOBA_BLOCK), F32),
                        pltpu.VMEM((gk, MOBA_BLOCK), F32), pltpu.VMEM((gk, MOBA_BLOCK), F32),
                        pltpu.VMEM((MOBA_GROUP, 1, MOBA_BLOCK), F32), pltpu.VMEM((MOBA_GROUP, 1, MOBA_BLOCK), F32),
                        pltpu.VMEM((gk, MOBA_BLOCK), BF16), pltpu.VMEM((gk, MOBA_BLOCK), BF16)],
        compiler_params=pltpu.CompilerParams(dimension_semantics=("arbitrary", "arbitrary"),
                                             vmem_limit_bytes=MOBA_VMEM_LIMIT),
        name="moba",
    )(jnp.asarray(tabf), jnp.asarray(tabr), c1, ub, qa, kaug, vT, sel, sga)


def _gla_consts():
    c, sb = GLA_CHUNK, GLA_SUB
    r = np.arange(c)
    blk = r // sb
    tri = (r[None, :] <= r[:, None])
    same = blk[None, :] == blk[:, None]
    lall = np.concatenate([
        tri,
        tri & same,
        same,
        blk[None, :] == blk[:, None] - 1,
        blk[None, :] == blk[:, None] - 2,
    ], axis=0).astype(np.float32)
    diff = blk[:, None] - blk[None, :]
    band = np.stack([diff == 1, diff == 2, diff == 3, same & tri]).astype(np.float32)
    dk_head = np.arange(G_KW) // G_DK
    dv_head = np.arange(G_VW) // G_DV
    bdt = (dv_head[:, None] == dk_head[None, :]).astype(np.float32)
    lex, mex = [tri], [np.eye(c, dtype=bool)]
    for s in GLA_LEVELS:
        same_s = (r[None, :] // s) == (r[:, None] // s)
        lex += [tri & same_s, same_s]
        pair = (r[None, :] // (2 * s)) == (r[:, None] // (2 * s))
        mex.append(pair & ((r[:, None] // s) % 2 == 1) & ((r[None, :] // s) % 2 == 0))
    lex = np.concatenate(lex, axis=0).astype(np.float32)
    mex = np.stack(mex).astype(np.float32)
    return lall, band, bdt, lex, mex


def _gla_kernel(q_ref, k_ref, v_ref, g_ref, sgb_ref, lall_ref, band_ref, bdt_ref, bones_ref, gout_ref,
                lex_ref, mex_ref, o_ref, st_ref):
    c = GLA_CHUNK
    nbatch = q_ref.shape[0]

    @pl.when(pl.program_id(0) == 0)
    def _():
        st_ref[...] = jnp.zeros_like(st_ref)

    lane_k = lax.broadcasted_iota(jnp.int32, (1, G_KW), 1) // G_DK
    lane_v = lax.broadcasted_iota(jnp.int32, (1, G_VW), 1) // G_DV
    n_chunks = q_ref.shape[1] // c

    def heads(t):
        return jnp.concatenate([jnp.where(lane_k == hd, t, 0.0) for hd in range(G_HEADS)],
                               axis=0).astype(BF16)

    def chunk(ci, states):
        r0 = pl.multiple_of(ci * c, c)
        nbs = range(nbatch)
        q = [q_ref[bi, pl.ds(r0, c), :] for bi in nbs]
        k = [k_ref[bi, pl.ds(r0, c), :] for bi in nbs]
        v = [v_ref[bi, pl.ds(r0, c), :] for bi in nbs]

        r = [_dot(lall_ref[...], jnp.concatenate(_split3(g_ref[bi, pl.ds(r0, c), :]), axis=1)) for bi in nbs]
        r = [x[:, :G_KW] + x[:, G_KW:2 * G_KW] + x[:, 2 * G_KW:] for x in r]
        b, cc, tt, p1, p2 = ([x[n * c:(n + 1) * c] for x in r] for n in range(5))
        b_last = [x[c - 1:c] for x in b]

        qt = [q[bi] * jnp.exp(cc[bi]) for bi in nbs]
        q2 = [qt[bi] * jnp.exp(p1[bi]) for bi in nbs]
        q3 = [q2[bi] * jnp.exp(p2[bi]) for bi in nbs]
        kt = [(k[bi] * jnp.exp(tt[bi] - cc[bi])).astype(BF16) for bi in nbs]
        kd = [(k[bi] * jnp.exp(-cc[bi])).astype(BF16) for bi in nbs]
        qi = [(q[bi] * jnp.exp(b[bi])).astype(BF16) for bi in nbs]
        kl = [(k[bi] * jnp.exp(b_last[bi] - b[bi])).astype(BF16) for bi in nbs]

        xs = [[_dg(heads(t[bi]), kt[bi], _NT) for bi in nbs] for t in (qt, q2, q3)]
        yd = [_dg(heads(qt[bi]), kd[bi], _NT) for bi in nbs]
        o = [_dg(qi[bi], states[bi].astype(BF16), _NT) for bi in nbs]
        st_new = tuple(states[bi] * jnp.exp(b_last[bi]) + _dg(v[bi], kl[bi], _TN) * bdt_ref[...]
                       for bi in nbs)
        for hd in range(G_HEADS):
            sl = slice(hd * c, (hd + 1) * c)
            for bi in nbs:
                att = (band_ref[0] * xs[0][bi][sl] + band_ref[1] * xs[1][bi][sl]
                       + band_ref[2] * xs[2][bi][sl] + band_ref[3] * yd[bi][sl])
                vh = jnp.where(lane_v == hd, v[bi], jnp.zeros_like(v[bi]))
                o[bi] = o[bi] + _dot(att.astype(BF16), vh)

        ss = [_group_sumsq(o[bi], bones_ref[...]) for bi in nbs]
        for bi in nbs:
            y = o[bi] * lax.rsqrt(ss[bi] * (1.0 / G_DV) + EPS) * gout_ref[...]
            o_ref[bi, pl.ds(r0, c), :] = (y * sgb_ref[bi, pl.ds(r0, c), :].astype(F32)).astype(BF16)
        return st_new

    def chunk_stable(ci, states):
        r0 = pl.multiple_of(ci * c, c)
        out = []
        for bi in range(nbatch):
            q = q_ref[bi, pl.ds(r0, c), :]
            k = k_ref[bi, pl.ds(r0, c), :]
            v = v_ref[bi, pl.ds(r0, c), :]
            r = _dot(lex_ref[...], jnp.concatenate(_split3(g_ref[bi, pl.ds(r0, c), :]), axis=1))
            r = r[:, :G_KW] + r[:, G_KW:2 * G_KW] + r[:, 2 * G_KW:]
            b = r[0:c]
            b_last = b[c - 1:c]
            x = [_dg(heads(q), k.astype(BF16), _NT)]
            for lv in range(len(GLA_LEVELS)):
                cs = r[(1 + 2 * lv) * c:(2 + 2 * lv) * c]
                ts = r[(2 + 2 * lv) * c:(3 + 2 * lv) * c]
                x.append(_dg(heads(q * jnp.exp(cs)), (k * jnp.exp(ts - cs)).astype(BF16), _NT))
            o = _dg((q * jnp.exp(b)).astype(BF16), states[bi].astype(BF16), _NT)
            for hd in range(G_HEADS):
                sl = slice(hd * c, (hd + 1) * c)
                att = mex_ref[0] * x[0][sl]
                for lv in range(len(GLA_LEVELS)):
                    att = att + mex_ref[lv + 1] * x[lv + 1][sl]
                o = o + _dot(att.astype(BF16), jnp.where(lane_v == hd, v, jnp.zeros_like(v)))
            kl = (k * jnp.exp(b_last - b)).astype(BF16)
            out.append(states[bi] * jnp.exp(b_last) + _dg(v, kl, _TN) * bdt_ref[...])
            ss = _group_sumsq(o, bones_ref[...])
            y = o * lax.rsqrt(ss * (1.0 / G_DV) + EPS) * gout_ref[...]
            o_ref[bi, pl.ds(r0, c), :] = (y * sgb_ref[bi, pl.ds(r0, c), :].astype(F32)).astype(BF16)
        return tuple(out)

    def run(body, unroll):
        states = lax.fori_loop(0, n_chunks, body, tuple(st_ref[bi] for bi in range(nbatch)), unroll=unroll)
        for bi in range(nbatch):
            st_ref[bi] = states[bi]

    risky = jnp.max(-g_ref[...]) * GLA_SUB > GLA_SAFE_EXP
    pl.when(jnp.logical_not(risky))(lambda: run(chunk, GLA_UNROLL))
    pl.when(risky)(lambda: run(chunk_stable, 1))


def _gla_call(gq, gk, gv, gg, sgbm, lall, band, bdt, bones, gout, lex, mex):
    bsz, s, _ = gq.shape
    tm = OUT_TILE
    row = lambda w: pl.BlockSpec((bsz, tm, w), lambda t: (0, t, 0))
    return pl.pallas_call(
        _gla_kernel,
        grid=(s // tm,),
        in_specs=[row(G_KW), row(G_KW), row(G_VW), row(G_KW), row(G_VW),
                  _full(lall.shape), _full(band.shape), _full(bdt.shape), _full(bones.shape), _full((1, G_VW)),
                  _full(lex.shape), _full(mex.shape)],
        out_specs=row(G_VW),
        out_shape=jax.ShapeDtypeStruct((bsz, s, G_VW), BF16),
        scratch_shapes=[pltpu.VMEM((bsz, G_VW, G_KW), F32)],
        compiler_params=_params(("arbitrary",)),
        name="gla",
    )(gq, gk, gv, gg, sgbm, lall, band, bdt, bones, gout, lex, mex)


def _mem_kernel(qm_ref, sgm_ref, mk_ref, mvx_ref, gq_ref, bones_ref, o_ref):
    qm = qm_ref[...]
    ss = _group_sumsq(qm, bones_ref[...])
    qn = qm * lax.rsqrt(ss * (1.0 / HEAD_DIM) + EPS) * gq_ref[...]
    qs = qn * (LOG2E * HEAD_DIM ** -0.5)
    lane = lax.broadcasted_iota(jnp.int32, (1, M_WIDTH), 1) // HEAD_DIM
    mk = mk_ref[...]
    s = [_dg(jnp.where(lane == h, qs, 0.0).astype(BF16), mk, _NT) for h in range(M_HEADS)]
    p = [jnp.exp2(x - jnp.max(x, axis=-1, keepdims=True)).astype(BF16) for x in s]
    acc = _dot(p[0], mvx_ref[0])
    for h in range(1, M_HEADS):
        acc = acc + _dot(p[h], mvx_ref[h])
    om = acc[:, :M_WIDTH] / acc[:, M_WIDTH:]
    o_ref[...] = (om * sgm_ref[...].astype(F32)).astype(BF16)


def _mem_call(qm, sgbm, mk, mvx, gq, bones):
    bsz, s, _ = qm.shape
    tm = OUT_TILE
    mlen = mk.shape[1]
    return pl.pallas_call(
        _mem_kernel,
        grid=(bsz, s // tm),
        in_specs=[
            pl.BlockSpec((None, tm, M_WIDTH), lambda b, t: (b, t, 0)),
            pl.BlockSpec((None, tm, M_WIDTH), lambda b, t: (b, t, 1)),
            pl.BlockSpec((None, mlen, M_WIDTH), lambda b, t: (b, 0, 0)),
            pl.BlockSpec((None, M_HEADS, mlen, 2 * M_WIDTH), lambda b, t: (b, 0, 0, 0)),
            _full((1, M_WIDTH)), _full((M_WIDTH, M_WIDTH)),
        ],
        out_specs=pl.BlockSpec((None, tm, M_WIDTH), lambda b, t: (b, t, 0)),
        out_shape=jax.ShapeDtypeStruct((bsz, s, M_WIDTH), BF16),
        compiler_params=_params(("arbitrary", "arbitrary")),
        name="memattn",
    )(qm, sgbm, mk, mvx, gq, bones)


def _out_kernel(x_ref, oaT_ref, ob_ref, om_ref, wa_ref, wb_ref, o_ref):
    y = _dot(jnp.concatenate([ob_ref[...], om_ref[...]], axis=-1), wb_ref[...])
    for t in range(oaT_ref.shape[1]):
        rows = slice(t * MOBA_BLOCK, (t + 1) * MOBA_BLOCK)
        oaT = oaT_ref[:, t].reshape(A_WIDTH, MOBA_BLOCK)
        o_ref[rows, :] = x_ref[rows, :] + y[rows] + _dg(oaT, wa_ref[...], _TN)


def _out_call(x, oaT, ob, om, wa, wb):
    bsz, s, d = x.shape
    tm = OUT_TILE
    assert s % tm == 0 and tm % MOBA_BLOCK == 0
    row = lambda w: pl.BlockSpec((None, tm, w), lambda b, t: (b, t, 0))
    return pl.pallas_call(
        _out_kernel,
        grid=(bsz, s // tm),
        in_specs=[row(d),
                  pl.BlockSpec((None, A_HEADS, tm // MOBA_BLOCK, HEAD_DIM, MOBA_BLOCK), lambda b, t: (b, 0, t, 0, 0)),
                  row(G_VW), row(M_WIDTH),
                  _full(wa.shape), _full(wb.shape)],
        out_specs=row(d),
        out_shape=jax.ShapeDtypeStruct((bsz, s, d), F32),
        compiler_params=_params(("arbitrary", "arbitrary")),
        name="outproj",
    )(x, oaT, ob, om, wa, wb)


def _alibi_consts():
    slopes = np.asarray([2.0 ** (-8.0 * (i + 1) / A_HEADS) for i in range(A_HEADS)], np.float32)
    c1 = (slopes * np.float32(LOG2E)).astype(np.float32)
    c1j = jnp.asarray(c1)
    pieces = list(_split3(c1j * 16.0)) + list(_split3(c1j))
    qx = jnp.zeros((A_HEADS, K_AUG - HEAD_DIM, MOBA_BLOCK), BF16)
    for n, pc in enumerate(pieces):
        qx = qx.at[:, n, :].set(jnp.broadcast_to(pc[:, None], (A_HEADS, MOBA_BLOCK)))
    pos = np.arange(MOBA_BLOCK)
    e = np.zeros((MOBA_BLOCK, K_AUG - HEAD_DIM), np.float32)
    e[:, 0:3] = (pos // 16)[:, None]
    e[:, 3:6] = (pos % 16)[:, None]
    return c1j, qx, jnp.asarray(e)


def kernel(x, mem, g_pre, w_in, g_q_moba, g_k_moba, w_gate_up, b_gate_up, g_gla_out,
           g_mem, w_mem_kv, g_q_mem, g_k_mem, w_out):
    depth = g_pre.shape[0]
    d = x.shape[-1]
    c1, qx, e = _alibi_consts()
    lall_np, band_np, bdt_np, lex_np, mex_np = _gla_consts()
    lall = jnp.asarray(lall_np, BF16)
    band = jnp.asarray(band_np)
    bdt = jnp.asarray(bdt_np)
    lex = jnp.asarray(lex_np, BF16)
    mex = jnp.asarray(mex_np)
    bones_a = jnp.asarray(_block_ones(A_WIDTH, HEAD_DIM), BF16)
    bones_m = jnp.asarray(_block_ones(M_WIDTH, HEAD_DIM), BF16)

    o_qa, o_ka, o_va, o_ga = 0, 512, 1024, 1536
    o_qb, o_kb, o_vb, o_gb, o_rb = 2048, 2176, 2304, 2560, 2816
    o_qm, o_gm = 2832, 3088

    mk_all, mvx_all = _memkv_call(mem, g_mem, w_mem_kv, g_k_mem)

    for l in range(depth):
        w = w_in[l]
        col = lambda o, n: w[:, o:o + n]
        rb_pad = jnp.pad(col(o_rb, GATE_RANK), ((0, 0), (0, G_KW - GATE_RANK)))
        wn = jnp.concatenate([col(o_ka, 512), col(o_gb, 256), col(o_gm, 256), col(o_qb, 128), col(o_kb, 128),
                              col(o_vb, 256), rb_pad, col(o_qm, 256)], axis=1).astype(BF16)
        wt = jnp.concatenate([col(o_qa, 512), col(o_va, 512), col(o_ga, 512)], axis=1).T.astype(BF16)
        wgu = jnp.pad(w_gate_up[l], ((0, G_KW - GATE_RANK), (0, 0)))
        gk = jnp.tile(g_k_moba[l], A_HEADS).reshape(1, A_WIDTH)

        (qT, qa, kaug, vT, kmean_nat, sga, sgbm, gq, gk2, gv, gg, qm) = _proj_call(
            x, g_pre[l].reshape(1, d), wn, wt, gk, g_q_moba[l].reshape(HEAD_DIM, 1), bones_a, e, qx,
            wgu, b_gate_up[l].reshape(1, G_KW))

        bsz, s = x.shape[0], x.shape[1]
        nb = s // MOBA_BLOCK
        kmean = kmean_nat.reshape(bsz, nb, A_HEADS, HEAD_DIM).transpose(0, 2, 1, 3)
        ub = (UB_SLACK * LOG2E * HEAD_DIM ** 0.5) * jnp.max(jnp.abs(g_q_moba[l])) * jnp.max(jnp.abs(g_k_moba[l]))
        oaT = _moba_call(c1, ub.reshape(1), qa, kaug, vT, _sel_call(qT, kmean), sga)
        ob = _gla_call(gq, gk2, gv, gg, sgbm, lall, band, bdt, bones_m,
                       jnp.tile(g_gla_out[l], G_HEADS).reshape(1, G_VW), lex, mex)
        om = _mem_call(qm, sgbm, mk_all[l], mvx_all[l],
                       jnp.tile(g_q_mem[l], M_HEADS).reshape(1, M_WIDTH), bones_m)
        wo = w_out[l].astype(BF16)
        x = _out_call(x, oaT, ob, om, wo[:A_WIDTH], wo[A_WIDTH:])
    return x
```
